```python
import numpy as np
import jax, jax.numpy as jnp
from jax import lax

D_MODEL = 1024
BATCH = 8
SEQ = 2048
DEPTH = 1
DEC_BATCH = 128
DEC_SEQ = 8
PAST_LEN = 16384
PAGE_SIZE = 128

MIX_WIDTH = D_MODEL
RET_WIDTH = MIX_WIDTH // 2
RET_HEADS = 4
RET_HEAD_DIM = RET_WIDTH // RET_HEADS
MLSTM_WIDTH = MIX_WIDTH - RET_WIDTH
MLSTM_HEADS = 4
MLSTM_HEAD_DIM = MLSTM_WIDTH // MLSTM_HEADS
CONV_WIDTH = 4
CHUNK = 128
ROPE_BASE = 10000.0
N_EXPERTS = 32
TOP_K = 4
D_FF = D_MODEL
SWIGLU_LIMIT = 7.0
SWIGLU_ALPHA = 1.702
EXPERT_BLOCK = 128
N_MOD = 6
EPS = 1e-6
PROJ_SIZES = (RET_WIDTH, RET_WIDTH, RET_WIDTH, RET_WIDTH, 2 * MLSTM_WIDTH, MLSTM_WIDTH, MLSTM_WIDTH, MLSTM_HEADS, MLSTM_HEADS)
PROJ_COLS = sum(PROJ_SIZES)

kernel_name = 'hybrid_retention_mlstm_moe_step'


def rms_norm(x, g):
    xf = x.astype(jnp.float32)
    y = xf * lax.rsqrt(jnp.mean(xf * xf, axis=-1, keepdims=True) + EPS)
    return (y * g.astype(jnp.float32)).astype(x.dtype)


def head_norm(x, g):
    mu = jnp.mean(x, axis=-1, keepdims=True)
    xc = x - mu
    var = jnp.mean(xc * xc, axis=-1, keepdims=True)
    return xc * lax.rsqrt(var + EPS) * g.astype(jnp.float32)


def rotary(x, pos):
    half = x.shape[-1] // 2
    inv_freq = jnp.power(ROPE_BASE, -jnp.arange(half, dtype=jnp.float32) / half)
    ang = pos.astype(jnp.float32)[:, None] * inv_freq[None, :]
    cos = jnp.cos(ang)[None, :, None, :]
    sin = jnp.sin(ang)[None, :, None, :]
    x1, x2 = x[..., :half], x[..., half:]
    return jnp.concatenate([x1 * cos - x2 * sin, x2 * cos + x1 * sin], axis=-1)


def to_chunks(x, L):
    B, T, H = x.shape[:3]
    x = x.reshape((B, T // L, L, H) + x.shape[3:])
    return jnp.moveaxis(x, (1, 2), (0, 3))


def from_chunks(x):
    x = jnp.moveaxis(x, (0, 3), (1, 2))
    B, NC, L, H = x.shape[:4]
    return x.reshape((B, NC * L, H) + x.shape[4:])


def retention(q, k, v, S0, log_gamma):
    T = q.shape[1]
    L = min(CHUNK, T)
    idx = jnp.arange(L, dtype=jnp.float32)
    rel = idx[:, None] - idx[None, :]
    decay = jnp.where(rel >= 0, jnp.exp(log_gamma[:, None, None] * jnp.maximum(rel, 0.0)), 0.0)
    q_decay = jnp.exp(log_gamma[:, None] * (idx + 1.0))[..., None]
    k_decay = jnp.exp(log_gamma[:, None] * (L - 1.0 - idx))[..., None]
    chunk_decay = jnp.exp(log_gamma * L)[:, None, None]

    def step(S, inp):
        qc, kc, vc = inp
        scores = jnp.einsum('bhid,bhjd->bhij', qc, kc) * decay
        out = jnp.einsum('bhij,bhje->bhie', scores, vc) + jnp.einsum('bhid,bhde->bhie', qc * q_decay, S)
        S = chunk_decay * S + jnp.einsum('bhjd,bhje->bhde', kc * k_decay, vc)
        return S, out

    S, out = lax.scan(step, S0, (to_chunks(q, L), to_chunks(k, L), to_chunks(v, L)))
    return S, from_chunks(out)


def mlstm(q, k, v, i_pre, log_f, C0, n0, m0):
    T = q.shape[1]
    L = min(CHUNK, T)
    causal = jnp.tril(jnp.ones((L, L), dtype=bool))

    def step(carry, inp):
        C, n, m = carry
        qc, kc, vc, ic, fc = inp
        b = jnp.cumsum(fc, axis=-1)
        d_log = jnp.where(causal, b[..., :, None] - b[..., None, :] + ic[..., None, :], -jnp.inf)
        inter_log = b + m[..., None]
        m_t = jnp.maximum(inter_log, jnp.max(d_log, axis=-1))
        w_intra = jnp.exp(d_log - m_t[..., None])
        w_inter = jnp.exp(inter_log - m_t)
        s = jnp.einsum('bhid,bhjd->bhij', qc, kc) * w_intra
        num = jnp.einsum('bhij,bhje->bhie', s, vc) + w_inter[..., None] * jnp.einsum('bhid,bhde->bhie', qc, C)
        den = jnp.sum(s, axis=-1) + w_inter * jnp.einsum('bhid,bhd->bhi', qc, n)
        h = num / jnp.maximum(jnp.abs(den), jnp.exp(-m_t))[..., None]
        b_last = b[..., -1]
        w_log = b_last[..., None] - b + ic
        m_new = jnp.maximum(b_last + m, jnp.max(w_log, axis=-1))
        w_k = jnp.exp(w_log - m_new[..., None])[..., None] * kc
        carry_decay = jnp.exp(b_last + m - m_new)
        C = carry_decay[..., None, None] * C + jnp.einsum('bhjd,bhje->bhde', w_k, vc)
        n = carry_decay[..., None] * n + jnp.sum(w_k, axis=2)
        return (C, n, m_new), h

    inputs = (to_chunks(q, L), to_chunks(k, L), to_chunks(v, L), to_chunks(i_pre, L), to_chunks(log_f, L))
    (C, n, m), h = lax.scan(step, (C0, n0, m0), inputs)
    return C, n, m, from_chunks(h)


def causal_conv(x, buf, w, b):
    T = x.shape[1]
    xp = jnp.concatenate([buf.astype(x.dtype), x], axis=1)
    y = b
    for j in range(CONV_WIDTH):
        y = y + w[j] * xp[:, j:j + T]
    return jax.nn.silu(y), xp[:, xp.shape[1] - (CONV_WIDTH - 1):]


def moe(x, w_router, b_router, w_up, b_up, w_down, b_down):
    T, D = x.shape
    logits = (x @ w_router + b_router).astype(jnp.float32)
    top_vals, top_idx = lax.top_k(logits, TOP_K)
    gates = jax.nn.softmax(top_vals, axis=-1)
    flat_e = top_idx.reshape(-1)
    order = jnp.argsort(flat_e)
    sorted_e = flat_e[order]
    tok = order // TOP_K
    counts = jnp.bincount(flat_e, length=N_EXPERTS)
    padded = (counts + EXPERT_BLOCK - 1) // EXPERT_BLOCK * EXPERT_BLOCK
    pad_end = jnp.cumsum(padded)
    pad_start = pad_end - padded
    start = jnp.cumsum(counts) - counts
    dest = pad_start[sorted_e] + (jnp.arange(T * TOP_K) - start[sorted_e])
    n_blocks = -(-(T * TOP_K) // EXPERT_BLOCK) + N_EXPERTS
    cap = n_blocks * EXPERT_BLOCK
    slot_tok = jnp.full((cap,), T, dtype=jnp.int32).at[dest].set(tok.astype(jnp.int32))
    block_e = jnp.minimum(jnp.searchsorted(pad_end, jnp.arange(n_blocks) * EXPERT_BLOCK, side='right'), N_EXPERTS - 1)
    x_pad = jnp.concatenate([x, jnp.zeros((1, D), x.dtype)], axis=0)
    xb = x_pad[slot_tok].reshape(n_blocks, EXPERT_BLOCK, D)

    def expert(args):
        xe, e = args
        hu = xe @ w_up[e] + b_up[e]
        gate = jnp.minimum(hu[:, :D_FF], SWIGLU_LIMIT)
        lin = jnp.clip(hu[:, D_FF:], -SWIGLU_LIMIT, SWIGLU_LIMIT)
        glu = gate * jax.nn.sigmoid(SWIGLU_ALPHA * gate)
        return ((lin + 1.0) * glu) @ w_down[e] + b_down[e]

    yb = lax.map(expert, (xb, block_e))
    y_sorted = yb.reshape(cap, D)[dest]
    weights = gates.reshape(-1)[order]
    out = jax.ops.segment_sum(y_sorted.astype(jnp.float32) * weights[:, None], tok, num_segments=T)
    return out.astype(x.dtype)


def layer(x, pos, c, S_ret, C, n, m, conv_buf, w_mod, b_mod, g_mix, g_ffn, w_in, b_igate, b_fgate, w_conv, b_conv, g_ret, g_mlstm, w_out, w_router, b_router, w_up, b_up, w_down, b_down):
    B, T, D = x.shape
    f32 = jnp.float32
    mod = (jax.nn.silu(c) @ w_mod + b_mod)[:, None, :]
    sh_a, sc_a, gt_a, sh_f, sc_f, gt_f = jnp.split(mod, N_MOD, axis=-1)
    h = rms_norm(x, g_mix) * (1.0 + sc_a) + sh_a
    p = h @ w_in
    cuts = [int(s) for s in np.cumsum(PROJ_SIZES)[:-1]]
    rq, rk, rv, rg, mqk, mv, mo, mi, mf = jnp.split(p, cuts, axis=-1)

    def heads(t, H):
        return t.astype(f32).reshape(B, T, H, -1)

    log_gamma = jnp.log1p(-jnp.exp2(-5.0 - jnp.arange(RET_HEADS, dtype=f32)))
    q_r = rotary(heads(rq, RET_HEADS), pos)
    k_r = rotary(heads(rk, RET_HEADS), pos) * RET_HEAD_DIM ** -0.5
    S_new, y_r = retention(q_r, k_r, heads(rv, RET_HEADS), S_ret.astype(f32), log_gamma)
    y_r = jax.nn.silu(rg.astype(f32)) * head_norm(y_r, g_ret).reshape(B, T, RET_WIDTH)

    qk, conv_new = causal_conv(mqk, conv_buf, w_conv, b_conv)
    mq, mk = jnp.split(qk, 2, axis=-1)
    i_pre = mi.astype(f32) + b_igate.astype(f32)
    log_f = jax.nn.log_sigmoid(mf.astype(f32) + b_fgate.astype(f32))
    C_new, n_new, m_new, y_m = mlstm(heads(mq, MLSTM_HEADS), heads(mk, MLSTM_HEADS) * MLSTM_HEAD_DIM ** -0.5, heads(mv, MLSTM_HEADS), i_pre, log_f, C.astype(f32), n.astype(f32), m.astype(f32))
    y_m = jax.nn.sigmoid(mo.astype(f32)) * head_norm(y_m, g_mlstm).reshape(B, T, MLSTM_WIDTH)

    mixed = jnp.concatenate([y_r, y_m], axis=-1).astype(x.dtype) @ w_out
    x = x + gt_a * mixed
    h2 = rms_norm(x, g_ffn) * (1.0 + sc_f) + sh_f
    x = x + gt_f * moe(h2.reshape(B * T, D), w_router, b_router, w_up, b_up, w_down, b_down).reshape(B, T, D)
    return x, S_new, C_new, n_new, m_new, conv_new


def setup_inputs(seed: int = 0) -> dict:
    key = jax.random.key(seed)
    ks = iter(jax.random.split(key, 32))
    f32 = jnp.float32

    def nrm(shape, scale):
        return jax.random.normal(next(ks), shape, f32) * scale

    return {
        'x_prompt': nrm((BATCH, SEQ, D_MODEL), 1.0),
        'x_sample': nrm((DEC_BATCH, DEC_SEQ, D_MODEL), 1.0),
        'c_prompt': nrm((BATCH, D_MODEL), 1.0),
        'c_sample': nrm((DEC_BATCH, D_MODEL), 1.0),
        'state_ret': nrm((DEPTH, DEC_BATCH, RET_HEADS, RET_HEAD_DIM, RET_HEAD_DIM), 0.5),
        'state_mlstm_c': nrm((DEPTH, DEC_BATCH, MLSTM_HEADS, MLSTM_HEAD_DIM, MLSTM_HEAD_DIM), 0.5),
        'state_mlstm_n': nrm((DEPTH, DEC_BATCH, MLSTM_HEADS, MLSTM_HEAD_DIM), 0.5),
        'state_mlstm_m': nrm((DEPTH, DEC_BATCH, MLSTM_HEADS), 1.0),
        'state_conv': nrm((DEPTH, DEC_BATCH, CONV_WIDTH - 1, 2 * MLSTM_WIDTH), 1.0),
        'w_mod': nrm((DEPTH, D_MODEL, N_MOD * D_MODEL), 0.5 * D_MODEL ** -0.5),
        'b_mod': nrm((DEPTH, N_MOD * D_MODEL), 0.02),
        'g_mix': 1.0 + nrm((DEPTH, D_MODEL), 0.02),
        'g_ffn': 1.0 + nrm((DEPTH, D_MODEL), 0.02),
        'w_in': nrm((DEPTH, D_MODEL, PROJ_COLS), D_MODEL ** -0.5),
        'b_igate': nrm((DEPTH, MLSTM_HEADS), 0.1),
        'b_fgate': jnp.linspace(3.0, 6.0, MLSTM_HEADS, dtype=f32)[None, :] + nrm((DEPTH, MLSTM_HEADS), 0.1),
        'w_conv': nrm((DEPTH, CONV_WIDTH, 2 * MLSTM_WIDTH), CONV_WIDTH ** -0.5),
        'b_conv': nrm((DEPTH, 2 * MLSTM_WIDTH), 0.02),
        'g_ret': 1.0 + nrm((DEPTH, RET_HEADS, RET_HEAD_DIM), 0.02),
        'g_mlstm': 1.0 + nrm((DEPTH, MLSTM_HEADS, MLSTM_HEAD_DIM), 0.02),
        'w_out': nrm((DEPTH, MIX_WIDTH, D_MODEL), MIX_WIDTH ** -0.5),
        'w_router': nrm((DEPTH, D_MODEL, N_EXPERTS), D_MODEL ** -0.5),
        'b_router': nrm((DEPTH, N_EXPERTS), 0.01),
        'w_up': nrm((DEPTH, N_EXPERTS, D_MODEL, 2 * D_FF), D_MODEL ** -0.5),
        'b_up': nrm((DEPTH, N_EXPERTS, 2 * D_FF), 0.02),
        'w_down': nrm((DEPTH, N_EXPERTS, D_FF, D_MODEL), D_FF ** -0.5),
        'b_down': nrm((DEPTH, N_EXPERTS, D_MODEL), 0.02),
        'g_final': 1.0 + nrm((D_MODEL,), 0.02),
    }


def reference(x_prompt, x_sample, c_prompt, c_sample, state_ret, state_mlstm_c, state_mlstm_n, state_mlstm_m, state_conv, w_mod, b_mod, g_mix, g_ffn, w_in, b_igate, b_fgate, w_conv, b_conv, g_ret, g_mlstm, w_out, w_router, b_router, w_up, b_up, w_down, b_down, g_final):
    f32 = jnp.float32
    bp, tp = x_prompt.shape[0], x_prompt.shape[1]
    ts = x_sample.shape[1]
    pos_p = jnp.arange(tp, dtype=jnp.int32)
    pos_s = PAST_LEN + jnp.arange(ts, dtype=jnp.int32)
    zero_ret = jnp.zeros((bp, RET_HEADS, RET_HEAD_DIM, RET_HEAD_DIM), f32)
    zero_c = jnp.zeros((bp, MLSTM_HEADS, MLSTM_HEAD_DIM, MLSTM_HEAD_DIM), f32)
    zero_n = jnp.zeros((bp, MLSTM_HEADS, MLSTM_HEAD_DIM), f32)
    zero_m = jnp.zeros((bp, MLSTM_HEADS), f32)
    zero_conv = jnp.zeros((bp, CONV_WIDTH - 1, 2 * MLSTM_WIDTH), x_prompt.dtype)
    xp, xs = x_prompt, x_sample
    new_p, new_s = [], []
    for l in range(DEPTH):
        lp = (w_mod[l], b_mod[l], g_mix[l], g_ffn[l], w_in[l], b_igate[l], b_fgate[l], w_conv[l], b_conv[l], g_ret[l], g_mlstm[l], w_out[l], w_router[l], b_router[l], w_up[l], b_up[l], w_down[l], b_down[l])
        xp, *st_p = layer(xp, pos_p, c_prompt, zero_ret, zero_c, zero_n, zero_m, zero_conv, *lp)
        xs, *st_s = layer(xs, pos_s, c_sample, state_ret[l], state_mlstm_c[l], state_mlstm_n[l], state_mlstm_m[l], state_conv[l], *lp)
        new_p.append(st_p)
        new_s.append(st_s)

    def stack(states, i):
        return jnp.stack([s[i] for s in states], axis=0)

    y_prompt = rms_norm(xp, g_final)
    y_sample = rms_norm(xs, g_final)
    return (y_prompt, y_sample, stack(new_p, 0), stack(new_p, 1), stack(new_p, 2), stack(new_p, 3), stack(new_p, 4), stack(new_s, 0), stack(new_s, 1), stack(new_s, 2), stack(new_s, 3), stack(new_s, 4))
```

```python
import functools

import numpy as np
import jax
import jax.numpy as jnp
from jax import lax
from jax.experimental import pallas as pl
from jax.experimental.pallas import tpu as pltpu

F32 = jnp.float32
BF16 = jnp.bfloat16
I32 = jnp.int32

D_MODEL = 1024
PAST_LEN = 16384
RET_HEADS = 4
MLSTM_HEADS = 4
HEAD_DIM = 128
RET_WIDTH = RET_HEADS * HEAD_DIM
MLSTM_WIDTH = MLSTM_HEADS * HEAD_DIM
CONV_WIDTH = 4
CHUNK = 128
ROPE_BASE = 10000.0
N_EXPERTS = 32
TOP_K = 4
D_FF = D_MODEL
SWIGLU_LIMIT = 7.0
SWIGLU_ALPHA = 1.702
N_MOD = 6
EPS = 1e-6
MAIN_COLS = 4 * RET_WIDTH + 2 * MLSTM_WIDTH + 2 * MLSTM_WIDTH
N_GATES = 2 * MLSTM_HEADS
OFF_RQ, OFF_RK, OFF_RV, OFF_RG = 0, RET_WIDTH, 2 * RET_WIDTH, 3 * RET_WIDTH
OFF_MQK = 4 * RET_WIDTH
OFF_MV = OFF_MQK + 2 * MLSTM_WIDTH
OFF_MO = OFF_MV + MLSTM_WIDTH

VMEM_LIMIT_BYTES = 56 * 1024 * 1024
TOKEN_TILE = 256
ROUTE_TILE = 512
EXPERT_BLOCK = 256
CONV_TAIL_ROWS = 8


def _params(n_axes=1):
    return pltpu.CompilerParams(dimension_semantics=("arbitrary",) * n_axes, vmem_limit_bytes=VMEM_LIMIT_BYTES)


def _dot(a, b):
    return jnp.dot(a, b, preferred_element_type=F32)


def _dot_nt(a, b):
    return lax.dot_general(a, b, (((1,), (1,)), ((), ())), preferred_element_type=F32)


def _dot_tn(a, b):
    return lax.dot_general(a, b, (((0,), (0,)), ((), ())), preferred_element_type=F32)


def _split(a):
    hi = a.astype(BF16)
    lo = (a - hi.astype(F32)).astype(BF16)
    return hi, lo


def _dot3(a, b, dot=_dot):
    ah, al = _split(a)
    bh, bl = _split(b)
    return dot(ah, bh) + (dot(al, bh) + dot(ah, bl))


def _sigmoid(x):
    return 1.0 / (1.0 + jnp.exp(-x))


def _log_sigmoid(x):
    return jnp.minimum(x, 0.0) - jnp.log1p(jnp.exp(-jnp.abs(x)))


def _rms(x, g):
    ms = jnp.mean(x * x, axis=-1, keepdims=True)
    return (x * lax.rsqrt(ms + EPS)) * g


def _layer_norm(x, g):
    mu = jnp.mean(x, axis=-1, keepdims=True)
    xc = x - mu
    var = jnp.mean(xc * xc, axis=-1, keepdims=True)
    return xc * lax.rsqrt(var + EPS) * g


def _mod_kernel(c_ref, w_ref, b_ref, o_ref):
    c = c_ref[...]
    o_ref[...] = _dot3(c * _sigmoid(c), w_ref[...]) + b_ref[...]


def _mod_call(c_all, w_mod, b_mod):
    rows, d = c_all.shape
    cols = w_mod.shape[1]
    tn = 1024
    return pl.pallas_call(
        _mod_kernel,
        grid=(cols // tn,),
        in_specs=[pl.BlockSpec((rows, d), lambda j: (0, 0)),
                  pl.BlockSpec((d, tn), lambda j: (0, j)),
                  pl.BlockSpec((1, tn), lambda j: (0, j))],
        out_specs=pl.BlockSpec((rows, tn), lambda j: (0, j)),
        out_shape=jax.ShapeDtypeStruct((rows, cols), F32),
        compiler_params=_params(1), name="mod",
    )(c_all, w_mod, b_mod.reshape(1, cols))


def _inproj_kernel(x_ref, sh_ref, sc_ref, g_ref, w_ref, wg_ref, wgt_ref, p_ref, gc_ref, gr_ref):
    x = x_ref[...]
    gb, tt, d = x.shape
    h = _rms(x, g_ref[...]) * (1.0 + sc_ref[...]) + sh_ref[...]
    h = h.reshape(gb * tt, d)
    hb = h.astype(BF16)
    for j in range(MAIN_COLS // 1024):
        p_ref[:, j * 1024:(j + 1) * 1024] = _dot(hb, w_ref[:, j * 1024:(j + 1) * 1024])
    gc_ref[...] = _dot3(h, wg_ref[...])[:, :N_GATES]
    gr_ref[...] = _dot3(wgt_ref[...], h, dot=_dot_nt)


def _inproj_call(x3, mod3, g_mix, w_main_bf, w_gate, w_gate_t, gb, tt):
    g, t, d = x3.shape
    n = g * t
    tm = gb * tt
    tpg = t // tt
    grid = (g // gb, tpg)
    row = lambda i, j: (i * tpg + j, 0)
    return pl.pallas_call(
        _inproj_kernel,
        grid=grid,
        in_specs=[pl.BlockSpec((gb, tt, d), lambda i, j: (i, j, 0)),
                  pl.BlockSpec((gb, 1, d), lambda i, j: (i, 0, 0)),
                  pl.BlockSpec((gb, 1, d), lambda i, j: (i, 0, 1)),
                  pl.BlockSpec((1, 1, d), lambda i, j: (0, 0, 0)),
                  pl.BlockSpec((d, MAIN_COLS), lambda i, j: (0, 0)),
                  pl.BlockSpec((d, 128), lambda i, j: (0, 0)),
                  pl.BlockSpec((N_GATES, d), lambda i, j: (0, 0))],
        out_specs=[pl.BlockSpec((tm, MAIN_COLS), row),
                   pl.BlockSpec((tm, N_GATES), row),
                   pl.BlockSpec((N_GATES, tm), lambda i, j: (0, i * tpg + j))],
        out_shape=[jax.ShapeDtypeStruct((n, MAIN_COLS), F32),
                   jax.ShapeDtypeStruct((n, N_GATES), F32),
                   jax.ShapeDtypeStruct((N_GATES, n), F32)],
        compiler_params=_params(2), name="inproj",
    )(x3, mod3, mod3, g_mix.reshape(1, 1, d), w_main_bf, w_gate, w_gate_t)


def _mixer_kernel(p_ref, gc_ref, gr_ref, cos_ref, sin_ref, dec_ref, qd_ref, kd_ref, cd_ref,
                  tril_ref, triu_ref, wconv_ref, bconv_ref, gret_ref, gml_ref, bgc_ref, bgr_ref,
                  s0_ref, c0_ref, n0_ref, m0_ref, conv0_ref,
                  y_ref, s_ref, c_ref, n_ref, m_ref, tail_ref, xp_ref):
    L = p_ref.shape[0]
    chunk = pl.program_id(1)

    @pl.when(chunk == 0)
    def _():
        s_ref[...] = s0_ref[...]
        c_ref[...] = c0_ref[...]
        n_ref[...] = n0_ref[...]
        m_ref[...] = m0_ref[...]
        tail_ref[...] = conv0_ref[...]

    cos = cos_ref[...]
    sin = sin_ref[...]
    scale = HEAD_DIM ** -0.5

    def rot(x):
        return x * cos + pltpu.roll(x, HEAD_DIM // 2, axis=1) * sin

    for h in range(RET_HEADS):
        lo = h * HEAD_DIM
        q = rot(p_ref[:, OFF_RQ + lo:OFF_RQ + lo + HEAD_DIM])
        k = rot(p_ref[:, OFF_RK + lo:OFF_RK + lo + HEAD_DIM]) * scale
        v = p_ref[:, OFF_RV + lo:OFF_RV + lo + HEAD_DIM].astype(BF16)
        g = p_ref[:, OFF_RG + lo:OFF_RG + lo + HEAD_DIM]
        s_old = s_ref[0, h]
        scores = _dot_nt(q.astype(BF16), k.astype(BF16)) * dec_ref[h]
        out = _dot(scores.astype(BF16), v) + _dot((q * qd_ref[h]).astype(BF16), s_old.astype(BF16))
        s_ref[0, h] = cd_ref[h] * s_old + _dot_tn((k * kd_ref[h]).astype(BF16), v)
        y_ref[:, lo:lo + HEAD_DIM] = ((g * _sigmoid(g)) * _layer_norm(out, gret_ref[:, lo:lo + HEAD_DIM])).astype(y_ref.dtype)

    xp_ref[0:CONV_TAIL_ROWS, :] = tail_ref[0]
    xp_ref[CONV_TAIL_ROWS:CONV_TAIL_ROWS + L, :] = p_ref[:, OFF_MQK:OFF_MQK + 2 * MLSTM_WIDTH]
    acc = bconv_ref[...] + wconv_ref[0:1, :] * xp_ref[CONV_TAIL_ROWS - 3:CONV_TAIL_ROWS - 3 + L, :]
    for j in range(1, CONV_WIDTH):
        acc = acc + wconv_ref[j:j + 1, :] * xp_ref[CONV_TAIL_ROWS - 3 + j:CONV_TAIL_ROWS - 3 + j + L, :]
    tail_ref[0] = xp_ref[L:L + CONV_TAIL_ROWS, :]
    xp_ref[CONV_TAIL_ROWS:CONV_TAIL_ROWS + L, :] = acc * _sigmoid(acc)

    gcol = gc_ref[...] + bgc_ref[...]
    is_f_col = lax.broadcasted_iota(I32, gcol.shape, 1) >= MLSTM_HEADS
    gcol = jnp.where(is_f_col, _log_sigmoid(gcol), gcol)
    grow = gr_ref[0] + bgr_ref[...]
    is_f_row = lax.broadcasted_iota(I32, grow.shape, 0) >= MLSTM_HEADS
    grow = jnp.where(is_f_row, _log_sigmoid(grow), grow)
    bcol_all = _dot3(tril_ref[...], gcol)
    brow_all = _dot3(grow, triu_ref[...])
    causal = lax.broadcasted_iota(I32, (L, L), 0) >= lax.broadcasted_iota(I32, (L, L), 1)

    for h in range(MLSTM_HEADS):
        lo = h * HEAD_DIM
        q = xp_ref[CONV_TAIL_ROWS:CONV_TAIL_ROWS + L, lo:lo + HEAD_DIM]
        k = xp_ref[CONV_TAIL_ROWS:CONV_TAIL_ROWS + L, MLSTM_WIDTH + lo:MLSTM_WIDTH + lo + HEAD_DIM] * scale
        v = p_ref[:, OFF_MV + lo:OFF_MV + lo + HEAD_DIM].astype(BF16)
        o = p_ref[:, OFF_MO + lo:OFF_MO + lo + HEAD_DIM]
        ic_col = gcol[:, h:h + 1]
        ic_row = grow[h:h + 1, :]
        b_col = bcol_all[:, MLSTM_HEADS + h:MLSTM_HEADS + h + 1]
        b_row = brow_all[MLSTM_HEADS + h:MLSTM_HEADS + h + 1, :]
        c_old = c_ref[0, h]
        n_old = n_ref[0, h]
        m_old = m_ref[0, h][:, 0:1]

        d_log = jnp.where(causal, b_col - b_row + ic_row, -jnp.inf)
        inter = b_col + m_old
        m_t = jnp.maximum(inter, jnp.max(d_log, axis=1, keepdims=True))
        w_intra = jnp.exp(d_log - m_t)
        w_inter = jnp.exp(inter - m_t)
        qb = q.astype(BF16)
        s = _dot_nt(qb, k.astype(BF16)) * w_intra
        num = _dot(s.astype(BF16), v) + w_inter * _dot(qb, c_old.astype(BF16))
        den = jnp.sum(s, axis=1, keepdims=True) + w_inter * jnp.sum(q * n_old, axis=1, keepdims=True)
        hh = num / jnp.maximum(jnp.abs(den), jnp.exp(-m_t))

        b_last = b_col[L - 1:L, :]
        w_log_col = b_last - b_col + ic_col
        m_new = jnp.maximum(b_last + m_old, jnp.max(w_log_col, axis=0, keepdims=True))
        wk = jnp.exp(w_log_col - m_new) * k
        cdec = jnp.exp(b_last + m_old - m_new)
        c_ref[0, h] = cdec * c_old + _dot_tn(wk.astype(BF16), v)
        n_ref[0, h] = cdec * n_old + jnp.sum(wk, axis=0, keepdims=True)
        m_ref[0, h] = jnp.broadcast_to(m_new, (1, HEAD_DIM))
        y_ref[:, RET_WIDTH + lo:RET_WIDTH + lo + HEAD_DIM] = (
            _sigmoid(o) * _layer_norm(hh, gml_ref[:, lo:lo + HEAD_DIM])).astype(y_ref.dtype)


def _mixer_consts(L, pos):
    half = HEAD_DIM // 2
    inv_freq = jnp.power(ROPE_BASE, -jnp.arange(half, dtype=F32) / half)
    ang = pos.astype(F32)[:, None] * inv_freq[None, :]
    cos = jnp.concatenate([jnp.cos(ang), jnp.cos(ang)], axis=-1)
    sin = jnp.concatenate([-jnp.sin(ang), jnp.sin(ang)], axis=-1)
    log_gamma = jnp.log1p(-jnp.exp2(-5.0 - jnp.arange(RET_HEADS, dtype=F32)))
    idx = jnp.arange(L, dtype=F32)
    rel = idx[:, None] - idx[None, :]
    dec = jnp.where(rel >= 0, jnp.exp(log_gamma[:, None, None] * jnp.maximum(rel, 0.0)), 0.0)
    qd = jnp.broadcast_to(jnp.exp(log_gamma[:, None] * (idx + 1.0))[..., None], (RET_HEADS, L, HEAD_DIM))
    kd = jnp.broadcast_to(jnp.exp(log_gamma[:, None] * (L - 1.0 - idx))[..., None], (RET_HEADS, L, HEAD_DIM))
    cd = jnp.broadcast_to(jnp.exp(log_gamma * L)[:, None, None], (RET_HEADS, 1, HEAD_DIM))
    tril = (rel >= 0).astype(F32)
    triu = (rel <= 0).astype(F32)
    return cos, sin, dec, qd, kd, cd, tril, triu


def _mixer_call(p, gc, gr3, consts, w_conv, b_conv, g_ret, g_mlstm, bg_col, bg_row,
                s0, c0, n0, m0, conv0, groups, seq, L, y_dtype):
    cos, sin, dec, qd, kd, cd, tril, triu = consts
    nc = seq // L
    n = groups * seq
    hd = HEAD_DIM
    full = lambda *shape: pl.BlockSpec(shape, lambda g, c: (0,) * len(shape))
    state4 = pl.BlockSpec((1, RET_HEADS, hd, hd), lambda g, c: (g, 0, 0, 0))
    vec4 = pl.BlockSpec((1, MLSTM_HEADS, 1, hd), lambda g, c: (g, 0, 0, 0))
    tail3 = pl.BlockSpec((1, CONV_TAIL_ROWS, 2 * MLSTM_WIDTH), lambda g, c: (g, 0, 0))
    row = lambda g, c: (g * nc + c, 0)
    return pl.pallas_call(
        _mixer_kernel,
        grid=(groups, nc),
        in_specs=[pl.BlockSpec((L, MAIN_COLS), row),
                  pl.BlockSpec((L, N_GATES), row),
                  pl.BlockSpec((1, N_GATES, L), lambda g, c: (g * nc + c, 0, 0)),
                  pl.BlockSpec((L, hd), lambda g, c: (c, 0)),
                  pl.BlockSpec((L, hd), lambda g, c: (c, 0)),
                  full(RET_HEADS, L, L), full(RET_HEADS, L, hd), full(RET_HEADS, L, hd), full(RET_HEADS, 1, hd),
                  full(L, L), full(L, L),
                  full(CONV_WIDTH, 2 * MLSTM_WIDTH), full(1, 2 * MLSTM_WIDTH),
                  full(1, RET_WIDTH), full(1, MLSTM_WIDTH), full(1, N_GATES), full(N_GATES, 1),
                  state4, state4, vec4, vec4, tail3],
        out_specs=[pl.BlockSpec((L, RET_WIDTH + MLSTM_WIDTH), row),
                   state4, state4, vec4, vec4, tail3],
        out_shape=[jax.ShapeDtypeStruct((n, RET_WIDTH + MLSTM_WIDTH), y_dtype),
                   jax.ShapeDtypeStruct((groups, RET_HEADS, hd, hd), F32),
                   jax.ShapeDtypeStruct((groups, MLSTM_HEADS, hd, hd), F32),
                   jax.ShapeDtypeStruct((groups, MLSTM_HEADS, 1, hd), F32),
                   jax.ShapeDtypeStruct((groups, MLSTM_HEADS, 1, hd), F32),
                   jax.ShapeDtypeStruct((groups, CONV_TAIL_ROWS, 2 * MLSTM_WIDTH), F32)],
        scratch_shapes=[pltpu.VMEM((CONV_TAIL_ROWS + L, 2 * MLSTM_WIDTH), F32)],
        compiler_params=_params(2), name="mixer",
    )(p, gc, gr3, cos, sin, dec, qd, kd, cd, tril, triu, w_conv, b_conv.reshape(1, -1),
      g_ret.reshape(1, -1), g_mlstm.reshape(1, -1), bg_col, bg_row, s0, c0, n0, m0, conv0)


def _outproj_kernel(y_ref, x_ref, gt_ref, sh_ref, sc_ref, g_ref, w_ref, wrt_ref, br_ref, ustrict_ref, cnt0_ref,
                    x1_ref, h2_ref, idx_ref, gate_ref, rank_ref, cnt_ref):
    first = (pl.program_id(0) == 0) & (pl.program_id(1) == 0)

    @pl.when(first)
    def _():
        cnt_ref[...] = cnt0_ref[...]

    x = x_ref[...]
    gb, tt, d = x.shape
    tm = gb * tt
    mixed = _dot(y_ref[...].astype(BF16), w_ref[...])
    x1 = x + gt_ref[...] * mixed.reshape(gb, tt, d)
    x1_ref[...] = x1
    h2 = (_rms(x1, g_ref[...]) * (1.0 + sc_ref[...]) + sh_ref[...]).reshape(tm, d)
    h2_ref[...] = h2

    work = _dot3(wrt_ref[...], h2, dot=_dot_nt) + br_ref[...]
    e_iota = lax.broadcasted_iota(I32, work.shape, 0).astype(F32)
    vals, idxs, sels = [], [], []
    for _ in range(TOP_K):
        mx = jnp.max(work, axis=0, keepdims=True)
        ik = jnp.min(jnp.where(work == mx, e_iota, float(N_EXPERTS)), axis=0, keepdims=True)
        sel = e_iota == ik
        vals.append(mx)
        idxs.append(ik)
        sels.append(sel)
        work = jnp.where(sel, -jnp.inf, work)
    exps = [jnp.exp(v - vals[0]) for v in vals]
    denom = exps[0] + exps[1] + exps[2] + exps[3]
    gate_ref[...] = jnp.concatenate([e / denom for e in exps], axis=0)
    idx_ref[...] = jnp.concatenate(idxs, axis=0).astype(I32)

    mask = (sels[0] | sels[1] | sels[2] | sels[3]).astype(F32)
    cnt = cnt_ref[...][:, 0:1]
    base = _dot(mask.astype(BF16), ustrict_ref[...]) + cnt
    ranks = [jnp.sum(jnp.where(sel, base, 0.0), axis=0, keepdims=True) for sel in sels]
    rank_ref[...] = jnp.concatenate(ranks, axis=0).astype(I32)
    cnt_ref[...] = jnp.broadcast_to(cnt + jnp.sum(mask, axis=1, keepdims=True), cnt_ref.shape)


def _outproj_call(y, x3, mod3, g_ffn, w_out_bf, w_router_t, b_router, ustrict, cnt0, gb, tt):
    g, t, d = x3.shape
    n = g * t
    tm = gb * tt
    tpg = t // tt
    row = lambda i, j: (i * tpg + j, 0)
    col = lambda i, j: (0, i * tpg + j)
    mod_spec = lambda k: pl.BlockSpec((gb, 1, d), lambda i, j: (i, 0, k))
    const = lambda *shape: pl.BlockSpec(shape, lambda i, j: (0,) * len(shape))
    return pl.pallas_call(
        _outproj_kernel,
        grid=(g // gb, tpg),
        in_specs=[pl.BlockSpec((tm, d), row),
                  pl.BlockSpec((gb, tt, d), lambda i, j: (i, j, 0)),
                  mod_spec(2), mod_spec(3), mod_spec(4),
                  const(1, 1, d), const(d, d), const(N_EXPERTS, d), const(N_EXPERTS, 1),
                  const(tm, tm), const(N_EXPERTS, 128)],
        out_specs=[pl.BlockSpec((gb, tt, d), lambda i, j: (i, j, 0)),
                   pl.BlockSpec((tm, d), row),
                   pl.BlockSpec((TOP_K, tm), col),
                   pl.BlockSpec((TOP_K, tm), col),
                   pl.BlockSpec((TOP_K, tm), col),
                   const(N_EXPERTS, 128)],
        out_shape=[jax.ShapeDtypeStruct((g, t, d), F32),
                   jax.ShapeDtypeStruct((n, d), F32),
                   jax.ShapeDtypeStruct((TOP_K, n), I32),
                   jax.ShapeDtypeStruct((TOP_K, n), F32),
                   jax.ShapeDtypeStruct((TOP_K, n), I32),
                   jax.ShapeDtypeStruct((N_EXPERTS, 128), F32)],
        compiler_params=_params(2), name="outproj_router",
    )(y, x3, mod3, mod3, mod3, g_ffn.reshape(1, 1, d), w_out_bf, w_router_t, b_router.reshape(N_EXPERTS, 1),
      ustrict, cnt0)


def _row_copy(src_ref, src_row, dst_ref, dst_row, sem):
    return pltpu.make_async_copy(src_ref.at[pl.ds(src_row, 1)], dst_ref.at[pl.ds(dst_row, 1)], sem)


def _dispatch_kernel(meta_ref, dest_ref, h2p_ref, h2s_ref, xs_ref, zero_ref, sem, *, prompt_tiles):
    i = pl.program_id(0)
    tm = h2p_ref.shape[0]
    bm = zero_ref.shape[0]
    n_blocks = xs_ref.shape[0] // bm

    @pl.when(i == 0)
    def _():
        zero_ref[...] = jnp.zeros(zero_ref.shape, zero_ref.dtype)

        def zero_copy(row):
            return pltpu.make_async_copy(zero_ref, xs_ref.at[pl.ds(pl.multiple_of(row, bm), bm)], sem)

        def tails(fn):
            def body(e, c):
                @pl.when(meta_ref[e] >= 0)
                def _():
                    fn(zero_copy(meta_ref[e]))
                return c
            lax.fori_loop(0, N_EXPERTS, body, 0)

        def unused(fn):
            def body(b, c):
                fn(zero_copy(b * bm))
                return c
            lax.fori_loop(meta_ref[N_EXPERTS], n_blocks, body, 0)

        tails(lambda cp: cp.start())
        unused(lambda cp: cp.start())
        tails(lambda cp: cp.wait())
        unused(lambda cp: cp.wait())

    def scatter(h2_ref):
        def start_rows(t, c):
            for k in range(TOP_K):
                _row_copy(h2_ref, t, xs_ref, dest_ref[k, t], sem).start()
            return c

        def wait_rows(t, c):
            for k in range(TOP_K):
                _row_copy(h2_ref, t, xs_ref, dest_ref[k, t], sem).wait()
            return c

        lax.fori_loop(0, tm, start_rows, 0)
        lax.fori_loop(0, tm, wait_rows, 0)

    @pl.when(i < prompt_tiles)
    def _():
        scatter(h2p_ref)

    @pl.when(i >= prompt_tiles)
    def _():
        scatter(h2s_ref)


def _dispatch_call(meta, dest, h2_p, h2_s, cap):
    d = h2_p.shape[1]
    tm = TOKEN_TILE
    pt, st = h2_p.shape[0] // tm, h2_s.shape[0] // tm
    return pl.pallas_call(
        functools.partial(_dispatch_kernel, prompt_tiles=pt),
        grid_spec=pltpu.PrefetchScalarGridSpec(
            num_scalar_prefetch=1,
            grid=(pt + st,),
            in_specs=[pl.BlockSpec((TOP_K, tm), lambda i, m: (0, i), memory_space=pltpu.SMEM),
                      pl.BlockSpec((tm, d), lambda i, m: (jnp.minimum(i, pt - 1), 0)),
                      pl.BlockSpec((tm, d), lambda i, m: (jnp.maximum(i - pt, 0), 0))],
            out_specs=pl.BlockSpec(memory_space=pl.ANY),
            scratch_shapes=[pltpu.VMEM((EXPERT_BLOCK, d), F32), pltpu.SemaphoreType.DMA(())]),
        out_shape=jax.ShapeDtypeStruct((cap, d), F32),
        compiler_params=_params(1), name="dispatch",
    )(meta, dest, h2_p, h2_s)


def _expert_kernel(be_ref, nu_ref, xs_ref, wup_ref, bup_ref, wdn_ref, bdn_ref, ys_ref, wup_bf, wdn_bf):
    i = pl.program_id(0)
    prev = be_ref[jnp.maximum(i - 1, 0)]

    @pl.when((i == 0) | (be_ref[i] != prev))
    def _():
        def cast(r, c):
            rows = pl.ds(pl.multiple_of(r * 64, 64), 64)
            wup_bf[rows, :] = wup_ref[0, rows, :].astype(BF16)
            wdn_bf[rows, :] = wdn_ref[0, rows, :].astype(BF16)
            return c

        lax.fori_loop(0, D_MODEL // 64, cast, 0)

    @pl.when(i < nu_ref[0])
    def _():
        hu = _dot(xs_ref[...].astype(BF16), wup_bf[...]) + bup_ref[0]
        gate = jnp.minimum(hu[:, :D_FF], SWIGLU_LIMIT)
        lin = jnp.clip(hu[:, D_FF:], -SWIGLU_LIMIT, SWIGLU_LIMIT)
        glu = gate * _sigmoid(SWIGLU_ALPHA * gate)
        ys_ref[...] = _dot(((lin + 1.0) * glu).astype(BF16), wdn_bf[...]) + bdn_ref[0]

    @pl.when(i >= nu_ref[0])
    def _():
        ys_ref[...] = jnp.zeros(ys_ref.shape, ys_ref.dtype)


def _expert_call(block_e, n_used, xs, w_up, b_up, w_down, b_down):
    cap, d = xs.shape
    bm = EXPERT_BLOCK
    blk = lambda i, be, nu: (jnp.minimum(i, nu[0] - 1), 0)
    per_e = lambda i, be, nu: (be[i], 0, 0)
    return pl.pallas_call(
        _expert_kernel,
        grid_spec=pltpu.PrefetchScalarGridSpec(
            num_scalar_prefetch=2,
            grid=(cap // bm,),
            in_specs=[pl.BlockSpec((bm, d), blk),
                      pl.BlockSpec((1, d, 2 * D_FF), per_e),
                      pl.BlockSpec((1, 1, 2 * D_FF), per_e),
                      pl.BlockSpec((1, D_FF, d), per_e),
                      pl.BlockSpec((1, 1, d), per_e)],
            out_specs=pl.BlockSpec((bm, d), lambda i, be, nu: (i, 0)),
            scratch_shapes=[pltpu.VMEM((d, 2 * D_FF), BF16), pltpu.VMEM((D_FF, d), BF16)]),
        out_shape=jax.ShapeDtypeStruct((cap, d), F32),
        compiler_params=_params(1), name="experts",
    )(block_e, n_used, xs, w_up, b_up.reshape(N_EXPERTS, 1, -1), w_down, b_down.reshape(N_EXPERTS, 1, -1))


def _combine_kernel(dest_ref, x1_ref, gt_ref, gate_ref, gfin_ref, ys_ref, o_ref, buf, sem):
    x1 = x1_ref[...]
    gb, tt, d = x1.shape
    tm = gb * tt

    def start_rows(t, c):
        for k in range(TOP_K):
            _row_copy(ys_ref, dest_ref[k, t], buf.at[k], t, sem).start()
        return c

    def wait_rows(t, c):
        for k in range(TOP_K):
            _row_copy(ys_ref, dest_ref[k, t], buf.at[k], t, sem).wait()
        return c

    lax.fori_loop(0, tm, start_rows, 0)
    lax.fori_loop(0, tm, wait_rows, 0)
    gates = gate_ref[...]
    moe = gates[:, 0:1] * buf[0]
    for k in range(1, TOP_K):
        moe = moe + gates[:, k:k + 1] * buf[k]
    xo = x1 + gt_ref[...] * moe.reshape(gb, tt, d)
    o_ref[...] = _rms(xo, gfin_ref[...])


def _combine_call(dest, x1, mod3, gates_t, g_final, ys, tok0, gb, tt):
    g, t, d = x1.shape
    tm = gb * tt
    tpg = t // tt
    t0 = tok0 // tm
    return pl.pallas_call(
        _combine_kernel,
        grid=(g // gb, tpg),
        in_specs=[pl.BlockSpec((TOP_K, tm), lambda i, j: (0, t0 + i * tpg + j), memory_space=pltpu.SMEM),
                  pl.BlockSpec((gb, tt, d), lambda i, j: (i, j, 0)),
                  pl.BlockSpec((gb, 1, d), lambda i, j: (i, 0, 5)),
                  pl.BlockSpec((tm, TOP_K), lambda i, j: (t0 + i * tpg + j, 0)),
                  pl.BlockSpec((1, 1, d), lambda i, j: (0, 0, 0)),
                  pl.BlockSpec(memory_space=pl.ANY)],
        out_specs=pl.BlockSpec((gb, tt, d), lambda i, j: (i, j, 0)),
        out_shape=jax.ShapeDtypeStruct((g, t, d), F32),
        scratch_shapes=[pltpu.VMEM((TOP_K, tm, d), F32), pltpu.SemaphoreType.DMA(())],
        compiler_params=_params(2), name="combine",
    )(dest, x1, mod3, gates_t, g_final.reshape(1, 1, d), ys)


def _group_blocking(groups, seq, tile):
    if seq >= tile:
        return 1, tile
    return tile // seq, seq


def _gates_by_chunk(gr, groups, seq, L):
    return gr.reshape(N_GATES, groups * seq // L, L).transpose(1, 0, 2)


def kernel(x_prompt, x_sample, c_prompt, c_sample, state_ret, state_mlstm_c, state_mlstm_n, state_mlstm_m, state_conv, w_mod, b_mod, g_mix, g_ffn, w_in, b_igate, b_fgate, w_conv, b_conv, g_ret, g_mlstm, w_out, w_router, b_router, w_up, b_up, w_down, b_down, g_final):
    depth = w_mod.shape[0]
    assert depth == 1, "single-layer trunk"
    bp, tp, d = x_prompt.shape
    bs, ts, _ = x_sample.shape
    n_p, n_s = bp * tp, bs * ts
    hd = HEAD_DIM
    l = 0

    mod = _mod_call(jnp.concatenate([c_prompt, c_sample], axis=0), w_mod[l], b_mod[l])
    mod_p = mod[:bp].reshape(bp, 1, N_MOD * d)
    mod_s = mod[bp:].reshape(bs, 1, N_MOD * d)

    w_main_bf = w_in[l][:, :MAIN_COLS].astype(BF16)
    w_gate = jnp.pad(w_in[l][:, MAIN_COLS:], ((0, 0), (0, 128 - N_GATES)))
    w_gate_t = w_in[l][:, MAIN_COLS:].T
    w_out_bf = w_out[l].astype(BF16)
    bg_col = jnp.concatenate([b_igate[l], b_fgate[l]]).reshape(1, N_GATES)
    bg_row = bg_col.reshape(N_GATES, 1)

    groups = (
        (x_prompt, mod_p, min(CHUNK, tp), jnp.arange(tp, dtype=I32),
         jnp.zeros((bp, RET_HEADS, hd, hd), F32), jnp.zeros((bp, MLSTM_HEADS, hd, hd), F32),
         jnp.zeros((bp, MLSTM_HEADS, hd), F32), jnp.zeros((bp, MLSTM_HEADS), F32),
         jnp.zeros((bp, CONV_WIDTH - 1, 2 * MLSTM_WIDTH), F32)),
        (x_sample, mod_s, min(CHUNK, ts), PAST_LEN + jnp.arange(ts, dtype=I32),
         state_ret[l], state_mlstm_c[l], state_mlstm_n[l], state_mlstm_m[l], state_conv[l]),
    )

    ustrict = (jnp.arange(ROUTE_TILE)[:, None] < jnp.arange(ROUTE_TILE)[None, :]).astype(BF16)
    cnt = jnp.zeros((N_EXPERTS, 128), F32)
    staged = []
    for x3, mod3, L, pos, s0, c0, n0, m0, conv0 in groups:
        g, t, _ = x3.shape
        gb, tt = _group_blocking(g, t, TOKEN_TILE)
        p, gc, gr = _inproj_call(x3, mod3, g_mix[l], w_main_bf, w_gate, w_gate_t, gb, tt)
        conv0p = jnp.pad(conv0.astype(F32), ((0, 0), (CONV_TAIL_ROWS - (CONV_WIDTH - 1), 0), (0, 0)))
        y, s_new, c_new, n_new, m_new, tail = _mixer_call(
            p, gc, _gates_by_chunk(gr, g, t, L), _mixer_consts(L, pos), w_conv[l], b_conv[l], g_ret[l], g_mlstm[l],
            bg_col, bg_row, s0.astype(F32), c0.astype(F32), n0.astype(F32).reshape(g, MLSTM_HEADS, 1, hd),
            jnp.broadcast_to(m0.astype(F32)[:, :, None, None], (g, MLSTM_HEADS, 1, hd)), conv0p,
            g, t, L, F32)
        states = (s_new, c_new, n_new.reshape(g, MLSTM_HEADS, hd), m_new[:, :, 0, 0],
                  tail[:, CONV_TAIL_ROWS - (CONV_WIDTH - 1):, :])
        gb, tt = _group_blocking(g, t, ROUTE_TILE)
        x1, h2, idx, gates, rank, cnt = _outproj_call(
            y, x3, mod3, g_ffn[l], w_out_bf, w_router[l].T, b_router[l], ustrict, cnt, gb, tt)
        staged.append((x1, mod3, h2, idx, gates, rank, states))

    n_tok = n_p + n_s
    bm = EXPERT_BLOCK
    n_blocks = -(-(n_tok * TOP_K) // bm) + N_EXPERTS
    cap = n_blocks * bm
    counts = cnt[:, 0].astype(I32)
    padded = (counts + bm - 1) // bm * bm
    pad_end = jnp.cumsum(padded)
    pad_start = pad_end - padded
    idx_all = jnp.concatenate([s[3] for s in staged], axis=1)
    rank_all = jnp.concatenate([s[5] for s in staged], axis=1)
    gates_t = jnp.concatenate([s[4] for s in staged], axis=1).T
    dest = pad_start[idx_all] + rank_all
    block_e = jnp.minimum(jnp.searchsorted(pad_end, jnp.arange(n_blocks, dtype=I32) * bm, side='right'),
                          N_EXPERTS - 1).astype(I32)
    n_used = (pad_end[-1:] // bm).astype(I32)
    meta = jnp.concatenate([jnp.where(counts > 0, pad_end - bm, -1), n_used]).astype(I32)

    xs = _dispatch_call(meta, dest, staged[0][2], staged[1][2], cap)
    ys = _expert_call(block_e, n_used, xs, w_up[l], b_up[l], w_down[l], b_down[l])

    outs = []
    tok0 = 0
    for x1, mod3, *_ in staged:
        g, t, _ = x1.shape
        gb, tt = _group_blocking(g, t, TOKEN_TILE)
        outs.append(_combine_call(dest, x1, mod3, gates_t, g_final, ys, tok0, gb, tt))
        tok0 += g * t

    st_p, st_s = staged[0][6], staged[1][6]
    return (outs[0], outs[1]) + tuple(a[None] for a in st_p) + tuple(a[None] for a in st_s)
```

```python
import functools

import numpy as np
import jax
import jax.numpy as jnp
from jax import lax
from jax.experimental import pallas as pl
from jax.experimental.pallas import tpu as pltpu

F32 = jnp.float32
BF16 = jnp.bfloat16
I32 = jnp.int32

D_MODEL = 1024
PAST_LEN = 16384
RET_HEADS = 4
MLSTM_HEADS = 4
HEAD_DIM = 128
RET_WIDTH = RET_HEADS * HEAD_DIM
MLSTM_WIDTH = MLSTM_HEADS * HEAD_DIM
CONV_WIDTH = 4
CHUNK = 128
ROPE_BASE = 10000.0
N_EXPERTS = 32
TOP_K = 4
D_FF = D_MODEL
SWIGLU_LIMIT = 7.0
SWIGLU_ALPHA = 1.702
N_MOD = 6
EPS = 1e-6
MAIN_COLS = 4 * RET_WIDTH + 2 * MLSTM_WIDTH + 2 * MLSTM_WIDTH
N_GATES = 2 * MLSTM_HEADS
OFF_RQ, OFF_RK, OFF_RV, OFF_RG = 0, RET_WIDTH, 2 * RET_WIDTH, 3 * RET_WIDTH
OFF_MQK = 4 * RET_WIDTH
OFF_MV = OFF_MQK + 2 * MLSTM_WIDTH
OFF_MO = OFF_MV + MLSTM_WIDTH

VMEM_LIMIT_BYTES = 56 * 1024 * 1024
TOKEN_TILE = 256
ROUTE_TILE = 512
EXPERT_BLOCK = 256
CONV_TAIL_ROWS = 8


def _params(n_axes=1):
    return pltpu.CompilerParams(dimension_semantics=("arbitrary",) * n_axes, vmem_limit_bytes=VMEM_LIMIT_BYTES)


def _dot(a, b):
    return jnp.dot(a, b, preferred_element_type=F32)


def _dot_nt(a, b):
    return lax.dot_general(a, b, (((1,), (1,)), ((), ())), preferred_element_type=F32)


def _dot_tn(a, b):
    return lax.dot_general(a, b, (((0,), (0,)), ((), ())), preferred_element_type=F32)


def _split(a):
    hi = a.astype(BF16)
    lo = (a - hi.astype(F32)).astype(BF16)
    return hi, lo


def _dot3(a, b, dot=_dot):
    ah, al = _split(a)
    bh, bl = _split(b)
    return dot(ah, bh) + (dot(al, bh) + dot(ah, bl))


def _sigmoid(x):
    return 1.0 / (1.0 + jnp.exp(-x))


def _log_sigmoid(x):
    return jnp.minimum(x, 0.0) - jnp.log1p(jnp.exp(-jnp.abs(x)))


def _rms(x, g):
    ms = jnp.mean(x * x, axis=-1, keepdims=True)
    return (x * lax.rsqrt(ms + EPS)) * g


def _layer_norm(x, g):
    mu = jnp.mean(x, axis=-1, keepdims=True)
    xc = x - mu
    var = jnp.mean(xc * xc, axis=-1, keepdims=True)
    return xc * lax.rsqrt(var + EPS) * g


def _mod_kernel(c_ref, w_ref, b_ref, o_ref):
    c = c_ref[...]
    o_ref[...] = _dot3(c * _sigmoid(c), w_ref[...]) + b_ref[...]


def _mod_call(c_all, w_mod, b_mod):
    rows, d = c_all.shape
    cols = w_mod.shape[1]
    tn = 1024
    return pl.pallas_call(
        _mod_kernel,
        grid=(cols // tn,),
        in_specs=[pl.BlockSpec((rows, d), lambda j: (0, 0)),
                  pl.BlockSpec((d, tn), lambda j: (0, j)),
                  pl.BlockSpec((1, tn), lambda j: (0, j))],
        out_specs=pl.BlockSpec((rows, tn), lambda j: (0, j)),
        out_shape=jax.ShapeDtypeStruct((rows, cols), F32),
        compiler_params=_params(1), name="mod",
    )(c_all, w_mod, b_mod.reshape(1, cols))


def _inproj_kernel(x_ref, sh_ref, sc_ref, g_ref, w_ref, wg_ref, wgt_ref, p_ref, gc_ref, gr_ref):
    x = x_ref[...]
    gb, tt, d = x.shape
    h = _rms(x, g_ref[...]) * (1.0 + sc_ref[...]) + sh_ref[...]
    h = h.reshape(gb * tt, d)
    hb = h.astype(BF16)
    for j in range(MAIN_COLS // 1024):
        p_ref[:, j * 1024:(j + 1) * 1024] = _dot(hb, w_ref[:, j * 1024:(j + 1) * 1024])
    gc_ref[...] = _dot3(h, wg_ref[...])[:, :N_GATES]
    gr_ref[...] = _dot3(wgt_ref[...], h, dot=_dot_nt)


def _inproj_call(x3, mod3, g_mix, w_main_bf, w_gate, w_gate_t, gb, tt):
    g, t, d = x3.shape
    n = g * t
    tm = gb * tt
    tpg = t // tt
    grid = (g // gb, tpg)
    row = lambda i, j: (i * tpg + j, 0)
    return pl.pallas_call(
        _inproj_kernel,
        grid=grid,
        in_specs=[pl.BlockSpec((gb, tt, d), lambda i, j: (i, j, 0)),
                  pl.BlockSpec((gb, 1, d), lambda i, j: (i, 0, 0)),
                  pl.BlockSpec((gb, 1, d), lambda i, j: (i, 0, 1)),
                  pl.BlockSpec((1, 1, d), lambda i, j: (0, 0, 0)),
                  pl.BlockSpec((d, MAIN_COLS), lambda i, j: (0, 0)),
                  pl.BlockSpec((d, 128), lambda i, j: (0, 0)),
                  pl.BlockSpec((N_GATES, d), lambda i, j: (0, 0))],
        out_specs=[pl.BlockSpec((tm, MAIN_COLS), row),
                   pl.BlockSpec((tm, N_GATES), row),
                   pl.BlockSpec((N_GATES, tm), lambda i, j: (0, i * tpg + j))],
        out_shape=[jax.ShapeDtypeStruct((n, MAIN_COLS), F32),
                   jax.ShapeDtypeStruct((n, N_GATES), F32),
                   jax.ShapeDtypeStruct((N_GATES, n), F32)],
        compiler_params=_params(2), name="inproj",
    )(x3, mod3, mod3, g_mix.reshape(1, 1, d), w_main_bf, w_gate, w_gate_t)


def _mixer_kernel(p_ref, gc_ref, gr_ref, cos_ref, sin_ref, dec_ref, qd_ref, kd_ref, cd_ref,
                  tril_ref, triu_ref, wconv_ref, bconv_ref, gret_ref, gml_ref, bgc_ref, bgr_ref,
                  s0_ref, c0_ref, n0_ref, m0_ref, conv0_ref,
                  y_ref, s_ref, c_ref, n_ref, m_ref, tail_ref, xp_ref):
    L = p_ref.shape[0]
    chunk = pl.program_id(1)

    @pl.when(chunk == 0)
    def _():
        s_ref[...] = s0_ref[...]
        c_ref[...] = c0_ref[...]
        n_ref[...] = n0_ref[...]
        m_ref[...] = m0_ref[...]
        tail_ref[...] = conv0_ref[...]

    cos = cos_ref[...]
    sin = sin_ref[...]
    scale = HEAD_DIM ** -0.5

    def rot(x):
        return x * cos + pltpu.roll(x, HEAD_DIM // 2, axis=1) * sin

    for h in range(RET_HEADS):
        lo = h * HEAD_DIM
        q = rot(p_ref[:, OFF_RQ + lo:OFF_RQ + lo + HEAD_DIM])
        k = rot(p_ref[:, OFF_RK + lo:OFF_RK + lo + HEAD_DIM]) * scale
        v = p_ref[:, OFF_RV + lo:OFF_RV + lo + HEAD_DIM].astype(BF16)
        g = p_ref[:, OFF_RG + lo:OFF_RG + lo + HEAD_DIM]
        s_old = s_ref[0, h]
        scores = _dot_nt(q.astype(BF16), k.astype(BF16)) * dec_ref[h]
        out = _dot(scores.astype(BF16), v) + _dot((q * qd_ref[h]).astype(BF16), s_old.astype(BF16))
        s_ref[0, h] = cd_ref[h] * s_old + _dot_tn((k * kd_ref[h]).astype(BF16), v)
        y_ref[:, lo:lo + HEAD_DIM] = ((g * _sigmoid(g)) * _layer_norm(out, gret_ref[:, lo:lo + HEAD_DIM])).astype(y_ref.dtype)

    xp_ref[0:CONV_TAIL_ROWS, :] = tail_ref[0]
    xp_ref[CONV_TAIL_ROWS:CONV_TAIL_ROWS + L, :] = p_ref[:, OFF_MQK:OFF_MQK + 2 * MLSTM_WIDTH]
    acc = bconv_ref[...] + wconv_ref[0:1, :] * xp_ref[CONV_TAIL_ROWS - 3:CONV_TAIL_ROWS - 3 + L, :]
    for j in range(1, CONV_WIDTH):
        acc = acc + wconv_ref[j:j + 1, :] * xp_ref[CONV_TAIL_ROWS - 3 + j:CONV_TAIL_ROWS - 3 + j + L, :]
    tail_ref[0] = xp_ref[L:L + CONV_TAIL_ROWS, :]
    xp_ref[CONV_TAIL_ROWS:CONV_TAIL_ROWS + L, :] = acc * _sigmoid(acc)

    gcol = gc_ref[...] + bgc_ref[...]
    is_f_col = lax.broadcasted_iota(I32, gcol.shape, 1) >= MLSTM_HEADS
    gcol = jnp.where(is_f_col, _log_sigmoid(gcol), gcol)
    grow = gr_ref[0] + bgr_ref[...]
    is_f_row = lax.broadcasted_iota(I32, grow.shape, 0) >= MLSTM_HEADS
    grow = jnp.where(is_f_row, _log_sigmoid(grow), grow)
    bcol_all = _dot3(tril_ref[...], gcol)
    brow_all = _dot3(grow, triu_ref[...])
    causal = lax.broadcasted_iota(I32, (L, L), 0) >= lax.broadcasted_iota(I32, (L, L), 1)

    for h in range(MLSTM_HEADS):
        lo = h * HEAD_DIM
        q = xp_ref[CONV_TAIL_ROWS:CONV_TAIL_ROWS + L, lo:lo + HEAD_DIM]
        k = xp_ref[CONV_TAIL_ROWS:CONV_TAIL_ROWS + L, MLSTM_WIDTH + lo:MLSTM_WIDTH + lo + HEAD_DIM] * scale
        v = p_ref[:, OFF_MV + lo:OFF_MV + lo + HEAD_DIM].astype(BF16)
        o = p_ref[:, OFF_MO + lo:OFF_MO + lo + HEAD_DIM]
        ic_col = gcol[:, h:h + 1]
        ic_row = grow[h:h + 1, :]
        b_col = bcol_all[:, MLSTM_HEADS + h:MLSTM_HEADS + h + 1]
        b_row = brow_all[MLSTM_HEADS + h:MLSTM_HEADS + h + 1, :]
        c_old = c_ref[0, h]
        n_old = n_ref[0, h]
        m_old = m_ref[0, h][:, 0:1]

        d_log = jnp.where(causal, b_col - b_row + ic_row, -jnp.inf)
        inter = b_col + m_old
        m_t = jnp.maximum(inter, jnp.max(d_log, axis=1, keepdims=True))
        w_intra = jnp.exp(d_log - m_t)
        w_inter = jnp.exp(inter - m_t)
        qb = q.astype(BF16)
        s = _dot_nt(qb, k.astype(BF16)) * w_intra
        num = _dot(s.astype(BF16), v) + w_inter * _dot(qb, c_old.astype(BF16))
        den = jnp.sum(s, axis=1, keepdims=True) + w_inter * jnp.sum(q * n_old, axis=1, keepdims=True)
        hh = num / jnp.maximum(jnp.abs(den), jnp.exp(-m_t))

        b_last = b_col[L - 1:L, :]
        w_log_col = b_last - b_col + ic_col
        m_new = jnp.maximum(b_last + m_old, jnp.max(w_log_col, axis=0, keepdims=True))
        wk = jnp.exp(w_log_col - m_new) * k
        cdec = jnp.exp(b_last + m_old - m_new)
        c_ref[0, h] = cdec * c_old + _dot_tn(wk.astype(BF16), v)
        n_ref[0, h] = cdec * n_old + jnp.sum(wk, axis=0, keepdims=True)
        m_ref[0, h] = jnp.broadcast_to(m_new, (1, HEAD_DIM))
        y_ref[:, RET_WIDTH + lo:RET_WIDTH + lo + HEAD_DIM] = (
            _sigmoid(o) * _layer_norm(hh, gml_ref[:, lo:lo + HEAD_DIM])).astype(y_ref.dtype)


def _mixer_consts(L, pos):
    f32 = np.float32
    half = HEAD_DIM // 2
    inv_freq = np.power(f32(ROPE_BASE), -np.arange(half, dtype=f32) / f32(half)).astype(f32)
    ang = (pos.astype(f32)[:, None] * inv_freq[None, :]).astype(f32)
    cos = np.concatenate([np.cos(ang), np.cos(ang)], axis=-1).astype(f32)
    sin = np.concatenate([-np.sin(ang), np.sin(ang)], axis=-1).astype(f32)
    log_gamma = np.log1p(-np.exp2(-5.0 - np.arange(RET_HEADS, dtype=np.float64)))
    idx = np.arange(L, dtype=np.float64)
    rel = idx[:, None] - idx[None, :]
    dec = np.where(rel >= 0, np.exp(log_gamma[:, None, None] * np.maximum(rel, 0.0)), 0.0)
    qd = np.broadcast_to(np.exp(log_gamma[:, None] * (idx + 1.0))[..., None], (RET_HEADS, L, HEAD_DIM))
    kd = np.broadcast_to(np.exp(log_gamma[:, None] * (L - 1.0 - idx))[..., None], (RET_HEADS, L, HEAD_DIM))
    cd = np.broadcast_to(np.exp(log_gamma * L)[:, None, None], (RET_HEADS, 1, HEAD_DIM))
    tril = rel >= 0
    triu = rel <= 0
    return tuple(jnp.asarray(a, F32) for a in (cos, sin, dec, qd, kd, cd, tril, triu))


def _mixer_call(p, gc, gr3, consts, w_conv, b_conv, g_ret, g_mlstm, bg_col, bg_row,
                s0, c0, n0, m0, conv0, groups, seq, L, y_dtype):
    cos, sin, dec, qd, kd, cd, tril, triu = consts
    nc = seq // L
    n = groups * seq
    hd = HEAD_DIM
    full = lambda *shape: pl.BlockSpec(shape, lambda g, c: (0,) * len(shape))
    state4 = pl.BlockSpec((1, RET_HEADS, hd, hd), lambda g, c: (g, 0, 0, 0))
    vec4 = pl.BlockSpec((1, MLSTM_HEADS, 1, hd), lambda g, c: (g, 0, 0, 0))
    tail3 = pl.BlockSpec((1, CONV_TAIL_ROWS, 2 * MLSTM_WIDTH), lambda g, c: (g, 0, 0))
    row = lambda g, c: (g * nc + c, 0)
    return pl.pallas_call(
        _mixer_kernel,
        grid=(groups, nc),
        in_specs=[pl.BlockSpec((L, MAIN_COLS), row),
                  pl.BlockSpec((L, N_GATES), row),
                  pl.BlockSpec((1, N_GATES, L), lambda g, c: (g * nc + c, 0, 0)),
                  pl.BlockSpec((L, hd), lambda g, c: (c, 0)),
                  pl.BlockSpec((L, hd), lambda g, c: (c, 0)),
                  full(RET_HEADS, L, L), full(RET_HEADS, L, hd), full(RET_HEADS, L, hd), full(RET_HEADS, 1, hd),
                  full(L, L), full(L, L),
                  full(CONV_WIDTH, 2 * MLSTM_WIDTH), full(1, 2 * MLSTM_WIDTH),
                  full(1, RET_WIDTH), full(1, MLSTM_WIDTH), full(1, N_GATES), full(N_GATES, 1),
                  state4, state4, vec4, vec4, tail3],
        out_specs=[pl.BlockSpec((L, RET_WIDTH + MLSTM_WIDTH), row),
                   state4, state4, vec4, vec4, tail3],
        out_shape=[jax.ShapeDtypeStruct((n, RET_WIDTH + MLSTM_WIDTH), y_dtype),
                   jax.ShapeDtypeStruct((groups, RET_HEADS, hd, hd), F32),
                   jax.ShapeDtypeStruct((groups, MLSTM_HEADS, hd, hd), F32),
                   jax.ShapeDtypeStruct((groups, MLSTM_HEADS, 1, hd), F32),
                   jax.ShapeDtypeStruct((groups, MLSTM_HEADS, 1, hd), F32),
                   jax.ShapeDtypeStruct((groups, CONV_TAIL_ROWS, 2 * MLSTM_WIDTH), F32)],
        scratch_shapes=[pltpu.VMEM((CONV_TAIL_ROWS + L, 2 * MLSTM_WIDTH), F32)],
        compiler_params=_params(2), name="mixer",
    )(p, gc, gr3, cos, sin, dec, qd, kd, cd, tril, triu, w_conv, b_conv.reshape(1, -1),
      g_ret.reshape(1, -1), g_mlstm.reshape(1, -1), bg_col, bg_row, s0, c0, n0, m0, conv0)


def _outproj_kernel(y_ref, x_ref, gt_ref, sh_ref, sc_ref, g_ref, w_ref, wrt_ref, br_ref, ustrict_ref, cnt0_ref,
                    x1_ref, h2_ref, idx_ref, gate_ref, rank_ref, cnt_ref):
    first = (pl.program_id(0) == 0) & (pl.program_id(1) == 0)

    @pl.when(first)
    def _():
        cnt_ref[...] = cnt0_ref[...]

    x = x_ref[...]
    gb, tt, d = x.shape
    tm = gb * tt
    mixed = _dot(y_ref[...].astype(BF16), w_ref[...])
    x1 = x + gt_ref[...] * mixed.reshape(gb, tt, d)
    x1_ref[...] = x1
    h2 = (_rms(x1, g_ref[...]) * (1.0 + sc_ref[...]) + sh_ref[...]).reshape(tm, d)
    h2_ref[...] = h2

    work = _dot3(wrt_ref[...], h2, dot=_dot_nt) + br_ref[...]
    e_iota = lax.broadcasted_iota(I32, work.shape, 0).astype(F32)
    vals, idxs, sels = [], [], []
    for _ in range(TOP_K):
        mx = jnp.max(work, axis=0, keepdims=True)
        ik = jnp.min(jnp.where(work == mx, e_iota, float(N_EXPERTS)), axis=0, keepdims=True)
        sel = e_iota == ik
        vals.append(mx)
        idxs.append(ik)
        sels.append(sel)
        work = jnp.where(sel, -jnp.inf, work)
    exps = [jnp.exp(v - vals[0]) for v in vals]
    denom = exps[0] + exps[1] + exps[2] + exps[3]
    gate_ref[...] = jnp.concatenate([e / denom for e in exps], axis=0)
    idx_ref[...] = jnp.concatenate(idxs, axis=0).astype(I32)

    mask = (sels[0] | sels[1] | sels[2] | sels[3]).astype(F32)
    cnt = cnt_ref[...][:, 0:1]
    base = _dot(mask.astype(BF16), ustrict_ref[...]) + cnt
    ranks = [jnp.sum(jnp.where(sel, base, 0.0), axis=0, keepdims=True) for sel in sels]
    rank_ref[...] = jnp.concatenate(ranks, axis=0).astype(I32)
    cnt_ref[...] = jnp.broadcast_to(cnt + jnp.sum(mask, axis=1, keepdims=True), cnt_ref.shape)


def _outproj_call(y, x3, mod3, g_ffn, w_out_bf, w_router_t, b_router, ustrict, cnt0, gb, tt):
    g, t, d = x3.shape
    n = g * t
    tm = gb * tt
    tpg = t // tt
    row = lambda i, j: (i * tpg + j, 0)
    col = lambda i, j: (0, i * tpg + j)
    mod_spec = lambda k: pl.BlockSpec((gb, 1, d), lambda i, j: (i, 0, k))
    const = lambda *shape: pl.BlockSpec(shape, lambda i, j: (0,) * len(shape))
    return pl.pallas_call(
        _outproj_kernel,
        grid=(g // gb, tpg),
        in_specs=[pl.BlockSpec((tm, d), row),
                  pl.BlockSpec((gb, tt, d), lambda i, j: (i, j, 0)),
                  mod_spec(2), mod_spec(3), mod_spec(4),
                  const(1, 1, d), const(d, d), const(N_EXPERTS, d), const(N_EXPERTS, 1),
                  const(tm, tm), const(N_EXPERTS, 128)],
        out_specs=[pl.BlockSpec((gb, tt, d), lambda i, j: (i, j, 0)),
                   pl.BlockSpec((tm, d), row),
                   pl.BlockSpec((TOP_K, tm), col),
                   pl.BlockSpec((TOP_K, tm), col),
                   pl.BlockSpec((TOP_K, tm), col),
                   const(N_EXPERTS, 128)],
        out_shape=[jax.ShapeDtypeStruct((g, t, d), F32),
                   jax.ShapeDtypeStruct((n, d), F32),
                   jax.ShapeDtypeStruct((TOP_K, n), I32),
                   jax.ShapeDtypeStruct((TOP_K, n), F32),
                   jax.ShapeDtypeStruct((TOP_K, n), I32),
                   jax.ShapeDtypeStruct((N_EXPERTS, 128), F32)],
        compiler_params=_params(2), name="outproj_router",
    )(y, x3, mod3, mod3, mod3, g_ffn.reshape(1, 1, d), w_out_bf, w_router_t, b_router.reshape(N_EXPERTS, 1),
      ustrict, cnt0)


def _row_copy(src_ref, src_row, dst_ref, dst_row, sem):
    return pltpu.make_async_copy(src_ref.at[pl.ds(src_row, 1)], dst_ref.at[pl.ds(dst_row, 1)], sem)


def _dispatch_kernel(meta_ref, dest_ref, h2p_ref, h2s_ref, xs_ref, zero_ref, sem, *, prompt_tiles):
    i = pl.program_id(0)
    tm = h2p_ref.shape[0]
    bm = zero_ref.shape[0]
    n_blocks = xs_ref.shape[0] // bm

    @pl.when(i == 0)
    def _():
        zero_ref[...] = jnp.zeros(zero_ref.shape, zero_ref.dtype)

        def zero_copy(row):
            return pltpu.make_async_copy(zero_ref, xs_ref.at[pl.ds(pl.multiple_of(row, bm), bm)], sem)

        def tails(fn):
            def body(e, c):
                @pl.when(meta_ref[e] >= 0)
                def _():
                    fn(zero_copy(meta_ref[e]))
                return c
            lax.fori_loop(0, N_EXPERTS, body, 0)

        def unused(fn):
            def body(b, c):
                fn(zero_copy(b * bm))
                return c
            lax.fori_loop(meta_ref[N_EXPERTS], n_blocks, body, 0)

        tails(lambda cp: cp.start())
        unused(lambda cp: cp.start())
        tails(lambda cp: cp.wait())
        unused(lambda cp: cp.wait())

    def scatter(h2_ref):
        def start_rows(t, c):
            for k in range(TOP_K):
                _row_copy(h2_ref, t, xs_ref, dest_ref[k, t], sem).start()
            return c

        def wait_rows(t, c):
            for k in range(TOP_K):
                _row_copy(h2_ref, t, xs_ref, dest_ref[k, t], sem).wait()
            return c

        lax.fori_loop(0, tm, start_rows, 0)
        lax.fori_loop(0, tm, wait_rows, 0)

    @pl.when(i < prompt_tiles)
    def _():
        scatter(h2p_ref)

    @pl.when(i >= prompt_tiles)
    def _():
        scatter(h2s_ref)


def _dispatch_call(meta, dest, h2_p, h2_s, cap):
    d = h2_p.shape[1]
    tm = TOKEN_TILE
    pt, st = h2_p.shape[0] // tm, h2_s.shape[0] // tm
    return pl.pallas_call(
        functools.partial(_dispatch_kernel, prompt_tiles=pt),
        grid_spec=pltpu.PrefetchScalarGridSpec(
            num_scalar_prefetch=1,
            grid=(pt + st,),
            in_specs=[pl.BlockSpec((TOP_K, tm), lambda i, m: (0, i), memory_space=pltpu.SMEM),
                      pl.BlockSpec((tm, d), lambda i, m: (jnp.minimum(i, pt - 1), 0)),
                      pl.BlockSpec((tm, d), lambda i, m: (jnp.maximum(i - pt, 0), 0))],
            out_specs=pl.BlockSpec(memory_space=pl.ANY),
            scratch_shapes=[pltpu.VMEM((EXPERT_BLOCK, d), F32), pltpu.SemaphoreType.DMA(())]),
        out_shape=jax.ShapeDtypeStruct((cap, d), F32),
        compiler_params=_params(1), name="dispatch",
    )(meta, dest, h2_p, h2_s)


def _expert_kernel(be_ref, nu_ref, xs_ref, wup_ref, bup_ref, wdn_ref, bdn_ref, ys_ref, wup_bf, wdn_bf):
    i = pl.program_id(0)
    prev = be_ref[jnp.maximum(i - 1, 0)]

    @pl.when((i == 0) | (be_ref[i] != prev))
    def _():
        def cast(r, c):
            rows = pl.ds(pl.multiple_of(r * 64, 64), 64)
            wup_bf[rows, :] = wup_ref[0, rows, :].astype(BF16)
            wdn_bf[rows, :] = wdn_ref[0, rows, :].astype(BF16)
            return c

        lax.fori_loop(0, D_MODEL // 64, cast, 0)

    @pl.when(i < nu_ref[0])
    def _():
        hu = _dot(xs_ref[...].astype(BF16), wup_bf[...]) + bup_ref[0]
        gate = jnp.minimum(hu[:, :D_FF], SWIGLU_LIMIT)
        lin = jnp.clip(hu[:, D_FF:], -SWIGLU_LIMIT, SWIGLU_LIMIT)
        glu = gate * _sigmoid(SWIGLU_ALPHA * gate)
        ys_ref[...] = _dot(((lin + 1.0) * glu).astype(BF16), wdn_bf[...]) + bdn_ref[0]

    @pl.when(i >= nu_ref[0])
    def _():
        ys_ref[...] = jnp.zeros(ys_ref.shape, ys_ref.dtype)


def _expert_call(block_e, n_used, xs, w_up, b_up, w_down, b_down):
    cap, d = xs.shape
    bm = EXPERT_BLOCK
    blk = lambda i, be, nu: (jnp.minimum(i, nu[0] - 1), 0)
    per_e = lambda i, be, nu: (be[i], 0, 0)
    return pl.pallas_call(
        _expert_kernel,
        grid_spec=pltpu.PrefetchScalarGridSpec(
            num_scalar_prefetch=2,
            grid=(cap // bm,),
            in_specs=[pl.BlockSpec((bm, d), blk),
                      pl.BlockSpec((1, d, 2 * D_FF), per_e),
                      pl.BlockSpec((1, 1, 2 * D_FF), per_e),
                      pl.BlockSpec((1, D_FF, d), per_e),
                      pl.BlockSpec((1, 1, d), per_e)],
            out_specs=pl.BlockSpec((bm, d), lambda i, be, nu: (i, 0)),
            scratch_shapes=[pltpu.VMEM((d, 2 * D_FF), BF16), pltpu.VMEM((D_FF, d), BF16)]),
        out_shape=jax.ShapeDtypeStruct((cap, d), F32),
        compiler_params=_params(1), name="experts",
    )(block_e, n_used, xs, w_up, b_up.reshape(N_EXPERTS, 1, -1), w_down, b_down.reshape(N_EXPERTS, 1, -1))


def _combine_kernel(dest_ref, x1_ref, gt_ref, gate_ref, gfin_ref, ys_ref, o_ref, buf, sem):
    x1 = x1_ref[...]
    gb, tt, d = x1.shape
    tm = gb * tt

    def start_rows(t, c):
        for k in range(TOP_K):
            _row_copy(ys_ref, dest_ref[k, t], buf.at[k], t, sem).start()
        return c

    def wait_rows(t, c):
        for k in range(TOP_K):
            _row_copy(ys_ref, dest_ref[k, t], buf.at[k], t, sem).wait()
        return c

    lax.fori_loop(0, tm, start_rows, 0)
    lax.fori_loop(0, tm, wait_rows, 0)
    gates = gate_ref[...]
    moe = gates[:, 0:1] * buf[0]
    for k in range(1, TOP_K):
        moe = moe + gates[:, k:k + 1] * buf[k]
    xo = x1 + gt_ref[...] * moe.reshape(gb, tt, d)
    o_ref[...] = _rms(xo, gfin_ref[...])


def _combine_call(dest, x1, mod3, gates_t, g_final, ys, tok0, gb, tt):
    g, t, d = x1.shape
    tm = gb * tt
    tpg = t // tt
    t0 = tok0 // tm
    return pl.pallas_call(
        _combine_kernel,
        grid=(g // gb, tpg),
        in_specs=[pl.BlockSpec((TOP_K, tm), lambda i, j: (0, t0 + i * tpg + j), memory_space=pltpu.SMEM),
                  pl.BlockSpec((gb, tt, d), lambda i, j: (i, j, 0)),
                  pl.BlockSpec((gb, 1, d), lambda i, j: (i, 0, 5)),
                  pl.BlockSpec((tm, TOP_K), lambda i, j: (t0 + i * tpg + j, 0)),
                  pl.BlockSpec((1, 1, d), lambda i, j: (0, 0, 0)),
                  pl.BlockSpec(memory_space=pl.ANY)],
        out_specs=pl.BlockSpec((gb, tt, d), lambda i, j: (i, j, 0)),
        out_shape=jax.ShapeDtypeStruct((g, t, d), F32),
        scratch_shapes=[pltpu.VMEM((TOP_K, tm, d), F32), pltpu.SemaphoreType.DMA(())],
        compiler_params=_params(2), name="combine",
    )(dest, x1, mod3, gates_t, g_final.reshape(1, 1, d), ys)


def _group_blocking(groups, seq, tile):
    if seq >= tile:
        return 1, tile
    return tile // seq, seq


def _gates_by_chunk(gr, groups, seq, L):
    return gr.reshape(N_GATES, groups * seq // L, L).transpose(1, 0, 2)


def kernel(x_prompt, x_sample, c_prompt, c_sample, state_ret, state_mlstm_c, state_mlstm_n, state_mlstm_m, state_conv, w_mod, b_mod, g_mix, g_ffn, w_in, b_igate, b_fgate, w_conv, b_conv, g_ret, g_mlstm, w_out, w_router, b_router, w_up, b_up, w_down, b_down, g_final):
    depth = w_mod.shape[0]
    assert depth == 1, "single-layer trunk"
    bp, tp, d = x_prompt.shape
    bs, ts, _ = x_sample.shape
    n_p, n_s = bp * tp, bs * ts
    hd = HEAD_DIM
    l = 0

    mod = _mod_call(jnp.concatenate([c_prompt, c_sample], axis=0), w_mod[l], b_mod[l])
    mod_p = mod[:bp].reshape(bp, 1, N_MOD * d)
    mod_s = mod[bp:].reshape(bs, 1, N_MOD * d)

    w_main_bf = w_in[l][:, :MAIN_COLS].astype(BF16)
    w_gate = jnp.pad(w_in[l][:, MAIN_COLS:], ((0, 0), (0, 128 - N_GATES)))
    w_gate_t = w_in[l][:, MAIN_COLS:].T
    w_out_bf = w_out[l].astype(BF16)
    bg_col = jnp.concatenate([b_igate[l], b_fgate[l]]).reshape(1, N_GATES)
    bg_row = bg_col.reshape(N_GATES, 1)

    groups = (
        (x_prompt, mod_p, min(CHUNK, tp), np.arange(tp),
         jnp.zeros((bp, RET_HEADS, hd, hd), F32), jnp.zeros((bp, MLSTM_HEADS, hd, hd), F32),
         jnp.zeros((bp, MLSTM_HEADS, hd), F32), jnp.zeros((bp, MLSTM_HEADS), F32),
         jnp.zeros((bp, CONV_WIDTH - 1, 2 * MLSTM_WIDTH), F32)),
        (x_sample, mod_s, min(CHUNK, ts), PAST_LEN + np.arange(ts),
         state_ret[l], state_mlstm_c[l], state_mlstm_n[l], state_mlstm_m[l], state_conv[l]),
    )

    ustrict = jnp.asarray(np.arange(ROUTE_TILE)[:, None] < np.arange(ROUTE_TILE)[None, :], BF16)
    cnt = jnp.zeros((N_EXPERTS, 128), F32)
    staged = []
    for x3, mod3, L, pos, s0, c0, n0, m0, conv0 in groups:
        g, t, _ = x3.shape
        gb, tt = _group_blocking(g, t, TOKEN_TILE)
        p, gc, gr = _inproj_call(x3, mod3, g_mix[l], w_main_bf, w_gate, w_gate_t, gb, tt)
        conv0p = jnp.pad(conv0.astype(F32), ((0, 0), (CONV_TAIL_ROWS - (CONV_WIDTH - 1), 0), (0, 0)))
        y, s_new, c_new, n_new, m_new, tail = _mixer_call(
            p, gc, _gates_by_chunk(gr, g, t, L), _mixer_consts(L, pos), w_conv[l], b_conv[l], g_ret[l], g_mlstm[l],
            bg_col, bg_row, s0.astype(F32), c0.astype(F32), n0.astype(F32).reshape(g, MLSTM_HEADS, 1, hd),
            jnp.broadcast_to(m0.astype(F32)[:, :, None, None], (g, MLSTM_HEADS, 1, hd)), conv0p,
            g, t, L, F32)
        states = (s_new, c_new, n_new.reshape(g, MLSTM_HEADS, hd), m_new[:, :, 0, 0],
                  tail[:, CONV_TAIL_ROWS - (CONV_WIDTH - 1):, :])
        gb, tt = _group_blocking(g, t, ROUTE_TILE)
        x1, h2, idx, gates, rank, cnt = _outproj_call(
            y, x3, mod3, g_ffn[l], w_out_bf, w_router[l].T, b_router[l], ustrict, cnt, gb, tt)
        staged.append((x1, mod3, h2, idx, gates, rank, states))

    n_tok = n_p + n_s
    bm = EXPERT_BLOCK
    n_blocks = -(-(n_tok * TOP_K) // bm) + N_EXPERTS
    cap = n_blocks * bm
    counts = cnt[:, 0].astype(I32)
    padded = (counts + bm - 1) // bm * bm
    pad_end = jnp.cumsum(padded)
    pad_start = pad_end - padded
    idx_all = jnp.concatenate([s[3] for s in staged], axis=1)
    rank_all = jnp.concatenate([s[5] for s in staged], axis=1)
    gates_t = jnp.concatenate([s[4] for s in staged], axis=1).T
    e_ids = jnp.arange(N_EXPERTS, dtype=I32)[:, None, None]
    dest = jnp.sum(jnp.where(idx_all[None] == e_ids, pad_start[:, None, None], 0), axis=0) + rank_all
    block_row = jnp.arange(n_blocks, dtype=I32) * bm
    block_e = jnp.minimum(jnp.sum((pad_end[None, :] <= block_row[:, None]).astype(I32), axis=1), N_EXPERTS - 1)
    n_used = (pad_end[-1:] // bm).astype(I32)
    meta = jnp.concatenate([jnp.where(counts > 0, pad_end - bm, -1), n_used]).astype(I32)

    xs = _dispatch_call(meta, dest, staged[0][2], staged[1][2], cap)
    ys = _expert_call(block_e, n_used, xs, w_up[l], b_up[l], w_down[l], b_down[l])

    outs = []
    tok0 = 0
    for x1, mod3, *_ in staged:
        g, t, _ = x1.shape
        gb, tt = _group_blocking(g, t, TOKEN_TILE)
        outs.append(_combine_call(dest, x1, mod3, gates_t, g_final, ys, tok0, gb, tt))
        tok0 += g * t

    st_p, st_s = staged[0][6], staged[1][6]
    return (outs[0], outs[1]) + tuple(a[None] for a in st_p) + tuple(a[None] for a in st_s)
```

```python
import functools

import numpy as np
import jax
import jax.numpy as jnp
from jax import lax
from jax.experimental import pallas as pl
from jax.experimental.pallas import tpu as pltpu

F32 = jnp.float32
BF16 = jnp.bfloat16
I32 = jnp.int32

D_MODEL = 1024
PAST_LEN = 16384
RET_HEADS = 4
MLSTM_HEADS = 4
HEAD_DIM = 128
RET_WIDTH = RET_HEADS * HEAD_DIM
MLSTM_WIDTH = MLSTM_HEADS * HEAD_DIM
CONV_WIDTH = 4
CHUNK = 128
ROPE_BASE = 10000.0
N_EXPERTS = 32
TOP_K = 4
D_FF = D_MODEL
SWIGLU_LIMIT = 7.0
SWIGLU_ALPHA = 1.702
N_MOD = 6
EPS = 1e-6
MAIN_COLS = 4 * RET_WIDTH + 2 * MLSTM_WIDTH + 2 * MLSTM_WIDTH
N_GATES = 2 * MLSTM_HEADS
OFF_RQ, OFF_RK, OFF_RV, OFF_RG = 0, RET_WIDTH, 2 * RET_WIDTH, 3 * RET_WIDTH
OFF_MQK = 4 * RET_WIDTH
OFF_MV = OFF_MQK + 2 * MLSTM_WIDTH
OFF_MO = OFF_MV + MLSTM_WIDTH

VMEM_LIMIT_BYTES = 56 * 1024 * 1024
TOKEN_TILE = 256
MOE_TILE = 512
EXPERT_BLOCK = 256
RUN_ALIGN = 16
TILE_BUF_ROWS = MOE_TILE * TOP_K + N_EXPERTS * RUN_ALIGN
PERM_ROWS = 256
PERM_COLS = 512
CONV_TAIL_ROWS = 8


def _params(n_axes=1):
    return pltpu.CompilerParams(dimension_semantics=("arbitrary",) * n_axes, vmem_limit_bytes=VMEM_LIMIT_BYTES)


def _dot(a, b):
    return jnp.dot(a, b, preferred_element_type=F32)


def _dot_nt(a, b):
    return lax.dot_general(a, b, (((1,), (1,)), ((), ())), preferred_element_type=F32)


def _dot_tn(a, b):
    return lax.dot_general(a, b, (((0,), (0,)), ((), ())), preferred_element_type=F32)


def _split(a):
    hi = a.astype(BF16)
    lo = (a - hi.astype(F32)).astype(BF16)
    return hi, lo


def _dot3(a, b, dot=_dot):
    ah, al = _split(a)
    bh, bl = _split(b)
    return dot(ah, bh) + (dot(al, bh) + dot(ah, bl))


def _sigmoid(x):
    return 1.0 / (1.0 + jnp.exp(-x))


def _log_sigmoid(x):
    return jnp.minimum(x, 0.0) - jnp.log1p(jnp.exp(-jnp.abs(x)))


def _rms(x, g):
    ms = jnp.mean(x * x, axis=-1, keepdims=True)
    return (x * lax.rsqrt(ms + EPS)) * g


def _layer_norm(x, g):
    mu = jnp.mean(x, axis=-1, keepdims=True)
    xc = x - mu
    var = jnp.mean(xc * xc, axis=-1, keepdims=True)
    return xc * lax.rsqrt(var + EPS) * g


def _mod_kernel(c_ref, w_ref, b_ref, o_ref):
    c = c_ref[...]
    o_ref[...] = _dot3(c * _sigmoid(c), w_ref[...]) + b_ref[...]


def _mod_call(c_all, w_mod, b_mod):
    rows, d = c_all.shape
    cols = w_mod.shape[1]
    tn = 1024
    return pl.pallas_call(
        _mod_kernel,
        grid=(cols // tn,),
        in_specs=[pl.BlockSpec((rows, d), lambda j: (0, 0)),
                  pl.BlockSpec((d, tn), lambda j: (0, j)),
                  pl.BlockSpec((1, tn), lambda j: (0, j))],
        out_specs=pl.BlockSpec((rows, tn), lambda j: (0, j)),
        out_shape=jax.ShapeDtypeStruct((rows, cols), F32),
        compiler_params=_params(1), name="mod",
    )(c_all, w_mod, b_mod.reshape(1, cols))


def _inproj_kernel(x_ref, sh_ref, sc_ref, g_ref, w_ref, wg_ref, wgt_ref, p_ref, gc_ref, gr_ref):
    x = x_ref[...]
    gb, tt, d = x.shape
    h = _rms(x, g_ref[...]) * (1.0 + sc_ref[...]) + sh_ref[...]
    h = h.reshape(gb * tt, d)
    hb = h.astype(BF16)
    for j in range(MAIN_COLS // 1024):
        p_ref[:, j * 1024:(j + 1) * 1024] = _dot(hb, w_ref[:, j * 1024:(j + 1) * 1024])
    gc_ref[...] = _dot3(h, wg_ref[...])[:, :N_GATES]
    gr_ref[...] = _dot3(wgt_ref[...], h, dot=_dot_nt)


def _inproj_call(x3, mod3, g_mix, w_main_bf, w_gate, w_gate_t, gb, tt):
    g, t, d = x3.shape
    n = g * t
    tm = gb * tt
    tpg = t // tt
    grid = (g // gb, tpg)
    row = lambda i, j: (i * tpg + j, 0)
    return pl.pallas_call(
        _inproj_kernel,
        grid=grid,
        in_specs=[pl.BlockSpec((gb, tt, d), lambda i, j: (i, j, 0)),
                  pl.BlockSpec((gb, 1, d), lambda i, j: (i, 0, 0)),
                  pl.BlockSpec((gb, 1, d), lambda i, j: (i, 0, 1)),
                  pl.BlockSpec((1, 1, d), lambda i, j: (0, 0, 0)),
                  pl.BlockSpec((d, MAIN_COLS), lambda i, j: (0, 0)),
                  pl.BlockSpec((d, 128), lambda i, j: (0, 0)),
                  pl.BlockSpec((N_GATES, d), lambda i, j: (0, 0))],
        out_specs=[pl.BlockSpec((tm, MAIN_COLS), row),
                   pl.BlockSpec((tm, N_GATES), row),
                   pl.BlockSpec((N_GATES, tm), lambda i, j: (0, i * tpg + j))],
        out_shape=[jax.ShapeDtypeStruct((n, MAIN_COLS), F32),
                   jax.ShapeDtypeStruct((n, N_GATES), F32),
                   jax.ShapeDtypeStruct((N_GATES, n), F32)],
        compiler_params=_params(2), name="inproj",
    )(x3, mod3, mod3, g_mix.reshape(1, 1, d), w_main_bf, w_gate, w_gate_t)


def _mixer_kernel(p_ref, gc_ref, gr_ref, cos_ref, sin_ref, dec_ref, qd_ref, kd_ref, cd_ref,
                  tril_ref, triu_ref, wconv_ref, bconv_ref, gret_ref, gml_ref, bgc_ref, bgr_ref,
                  s0_ref, c0_ref, n0_ref, m0_ref, conv0_ref,
                  y_ref, s_ref, c_ref, n_ref, m_ref, tail_ref, xp_ref):
    L = p_ref.shape[0]
    chunk = pl.program_id(1)

    @pl.when(chunk == 0)
    def _():
        s_ref[...] = s0_ref[...]
        c_ref[...] = c0_ref[...]
        n_ref[...] = n0_ref[...]
        m_ref[...] = m0_ref[...]
        tail_ref[...] = conv0_ref[...]

    cos = cos_ref[...]
    sin = sin_ref[...]
    scale = HEAD_DIM ** -0.5

    def rot(x):
        return x * cos + pltpu.roll(x, HEAD_DIM // 2, axis=1) * sin

    for h in range(RET_HEADS):
        lo = h * HEAD_DIM
        q = rot(p_ref[:, OFF_RQ + lo:OFF_RQ + lo + HEAD_DIM])
        k = rot(p_ref[:, OFF_RK + lo:OFF_RK + lo + HEAD_DIM]) * scale
        v = p_ref[:, OFF_RV + lo:OFF_RV + lo + HEAD_DIM].astype(BF16)
        g = p_ref[:, OFF_RG + lo:OFF_RG + lo + HEAD_DIM]
        s_old = s_ref[0, h]
        scores = _dot_nt(q.astype(BF16), k.astype(BF16)) * dec_ref[h]
        out = _dot(scores.astype(BF16), v) + _dot((q * qd_ref[h]).astype(BF16), s_old.astype(BF16))
        s_ref[0, h] = cd_ref[h] * s_old + _dot_tn((k * kd_ref[h]).astype(BF16), v)
        y_ref[:, lo:lo + HEAD_DIM] = ((g * _sigmoid(g)) * _layer_norm(out, gret_ref[:, lo:lo + HEAD_DIM])).astype(y_ref.dtype)

    xp_ref[0:CONV_TAIL_ROWS, :] = tail_ref[0]
    xp_ref[CONV_TAIL_ROWS:CONV_TAIL_ROWS + L, :] = p_ref[:, OFF_MQK:OFF_MQK + 2 * MLSTM_WIDTH]
    acc = bconv_ref[...] + wconv_ref[0:1, :] * xp_ref[CONV_TAIL_ROWS - 3:CONV_TAIL_ROWS - 3 + L, :]
    for j in range(1, CONV_WIDTH):
        acc = acc + wconv_ref[j:j + 1, :] * xp_ref[CONV_TAIL_ROWS - 3 + j:CONV_TAIL_ROWS - 3 + j + L, :]
    tail_ref[0] = xp_ref[L:L + CONV_TAIL_ROWS, :]
    xp_ref[CONV_TAIL_ROWS:CONV_TAIL_ROWS + L, :] = acc * _sigmoid(acc)

    gcol = gc_ref[...] + bgc_ref[...]
    is_f_col = lax.broadcasted_iota(I32, gcol.shape, 1) >= MLSTM_HEADS
    gcol = jnp.where(is_f_col, _log_sigmoid(gcol), gcol)
    grow = gr_ref[0] + bgr_ref[...]
    is_f_row = lax.broadcasted_iota(I32, grow.shape, 0) >= MLSTM_HEADS
    grow = jnp.where(is_f_row, _log_sigmoid(grow), grow)
    bcol_all = _dot3(tril_ref[...], gcol)
    brow_all = _dot3(grow, triu_ref[...])
    causal = lax.broadcasted_iota(I32, (L, L), 0) >= lax.broadcasted_iota(I32, (L, L), 1)

    for h in range(MLSTM_HEADS):
        lo = h * HEAD_DIM
        q = xp_ref[CONV_TAIL_ROWS:CONV_TAIL_ROWS + L, lo:lo + HEAD_DIM]
        k = xp_ref[CONV_TAIL_ROWS:CONV_TAIL_ROWS + L, MLSTM_WIDTH + lo:MLSTM_WIDTH + lo + HEAD_DIM] * scale
        v = p_ref[:, OFF_MV + lo:OFF_MV + lo + HEAD_DIM].astype(BF16)
        o = p_ref[:, OFF_MO + lo:OFF_MO + lo + HEAD_DIM]
        ic_col = gcol[:, h:h + 1]
        ic_row = grow[h:h + 1, :]
        b_col = bcol_all[:, MLSTM_HEADS + h:MLSTM_HEADS + h + 1]
        b_row = brow_all[MLSTM_HEADS + h:MLSTM_HEADS + h + 1, :]
        c_old = c_ref[0, h]
        n_old = n_ref[0, h]
        m_old = m_ref[0, h][:, 0:1]

        d_log = jnp.where(causal, b_col - b_row + ic_row, -jnp.inf)
        inter = b_col + m_old
        m_t = jnp.maximum(inter, jnp.max(d_log, axis=1, keepdims=True))
        w_intra = jnp.exp(d_log - m_t)
        w_inter = jnp.exp(inter - m_t)
        qb = q.astype(BF16)
        s = _dot_nt(qb, k.astype(BF16)) * w_intra
        num = _dot(s.astype(BF16), v) + w_inter * _dot(qb, c_old.astype(BF16))
        den = jnp.sum(s, axis=1, keepdims=True) + w_inter * jnp.sum(q * n_old, axis=1, keepdims=True)
        hh = num / jnp.maximum(jnp.abs(den), jnp.exp(-m_t))

        b_last = b_col[L - 1:L, :]
        w_log_col = b_last - b_col + ic_col
        m_new = jnp.maximum(b_last + m_old, jnp.max(w_log_col, axis=0, keepdims=True))
        wk = jnp.exp(w_log_col - m_new) * k
        cdec = jnp.exp(b_last + m_old - m_new)
        c_ref[0, h] = cdec * c_old + _dot_tn(wk.astype(BF16), v)
        n_ref[0, h] = cdec * n_old + jnp.sum(wk, axis=0, keepdims=True)
        m_ref[0, h] = jnp.broadcast_to(m_new, (1, HEAD_DIM))
        y_ref[:, RET_WIDTH + lo:RET_WIDTH + lo + HEAD_DIM] = (
            _sigmoid(o) * _layer_norm(hh, gml_ref[:, lo:lo + HEAD_DIM])).astype(y_ref.dtype)


def _mixer_consts(L, pos):
    f32 = np.float32
    half = HEAD_DIM // 2
    inv_freq = np.power(f32(ROPE_BASE), -np.arange(half, dtype=f32) / f32(half)).astype(f32)
    ang = (pos.astype(f32)[:, None] * inv_freq[None, :]).astype(f32)
    cos = np.concatenate([np.cos(ang), np.cos(ang)], axis=-1).astype(f32)
    sin = np.concatenate([-np.sin(ang), np.sin(ang)], axis=-1).astype(f32)
    log_gamma = np.log1p(-np.exp2(-5.0 - np.arange(RET_HEADS, dtype=np.float64)))
    idx = np.arange(L, dtype=np.float64)
    rel = idx[:, None] - idx[None, :]
    dec = np.where(rel >= 0, np.exp(log_gamma[:, None, None] * np.maximum(rel, 0.0)), 0.0)
    qd = np.broadcast_to(np.exp(log_gamma[:, None] * (idx + 1.0))[..., None], (RET_HEADS, L, HEAD_DIM))
    kd = np.broadcast_to(np.exp(log_gamma[:, None] * (L - 1.0 - idx))[..., None], (RET_HEADS, L, HEAD_DIM))
    cd = np.broadcast_to(np.exp(log_gamma * L)[:, None, None], (RET_HEADS, 1, HEAD_DIM))
    tril = rel >= 0
    triu = rel <= 0
    return tuple(jnp.asarray(a, F32) for a in (cos, sin, dec, qd, kd, cd, tril, triu))


def _mixer_call(p, gc, gr3, consts, w_conv, b_conv, g_ret, g_mlstm, bg_col, bg_row,
                s0, c0, n0, m0, conv0, groups, seq, L, y_dtype):
    cos, sin, dec, qd, kd, cd, tril, triu = consts
    nc = seq // L
    n = groups * seq
    hd = HEAD_DIM
    full = lambda *shape: pl.BlockSpec(shape, lambda g, c: (0,) * len(shape))
    state4 = pl.BlockSpec((1, RET_HEADS, hd, hd), lambda g, c: (g, 0, 0, 0))
    vec4 = pl.BlockSpec((1, MLSTM_HEADS, 1, hd), lambda g, c: (g, 0, 0, 0))
    tail3 = pl.BlockSpec((1, CONV_TAIL_ROWS, 2 * MLSTM_WIDTH), lambda g, c: (g, 0, 0))
    row = lambda g, c: (g * nc + c, 0)
    return pl.pallas_call(
        _mixer_kernel,
        grid=(groups, nc),
        in_specs=[pl.BlockSpec((L, MAIN_COLS), row),
                  pl.BlockSpec((L, N_GATES), row),
                  pl.BlockSpec((1, N_GATES, L), lambda g, c: (g * nc + c, 0, 0)),
                  pl.BlockSpec((L, hd), lambda g, c: (c, 0)),
                  pl.BlockSpec((L, hd), lambda g, c: (c, 0)),
                  full(RET_HEADS, L, L), full(RET_HEADS, L, hd), full(RET_HEADS, L, hd), full(RET_HEADS, 1, hd),
                  full(L, L), full(L, L),
                  full(CONV_WIDTH, 2 * MLSTM_WIDTH), full(1, 2 * MLSTM_WIDTH),
                  full(1, RET_WIDTH), full(1, MLSTM_WIDTH), full(1, N_GATES), full(N_GATES, 1),
                  state4, state4, vec4, vec4, tail3],
        out_specs=[pl.BlockSpec((L, RET_WIDTH + MLSTM_WIDTH), row),
                   state4, state4, vec4, vec4, tail3],
        out_shape=[jax.ShapeDtypeStruct((n, RET_WIDTH + MLSTM_WIDTH), y_dtype),
                   jax.ShapeDtypeStruct((groups, RET_HEADS, hd, hd), F32),
                   jax.ShapeDtypeStruct((groups, MLSTM_HEADS, hd, hd), F32),
                   jax.ShapeDtypeStruct((groups, MLSTM_HEADS, 1, hd), F32),
                   jax.ShapeDtypeStruct((groups, MLSTM_HEADS, 1, hd), F32),
                   jax.ShapeDtypeStruct((groups, CONV_TAIL_ROWS, 2 * MLSTM_WIDTH), F32)],
        scratch_shapes=[pltpu.VMEM((CONV_TAIL_ROWS + L, 2 * MLSTM_WIDTH), F32)],
        compiler_params=_params(2), name="mixer",
    )(p, gc, gr3, cos, sin, dec, qd, kd, cd, tril, triu, w_conv, b_conv.reshape(1, -1),
      g_ret.reshape(1, -1), g_mlstm.reshape(1, -1), bg_col, bg_row, s0, c0, n0, m0, conv0)


def _outproj_kernel(y_ref, x_ref, gt_ref, sh_ref, sc_ref, g_ref, w_ref, wrt_ref, br_ref, ustrict_ref,
                    x1_ref, h2_ref, idx_ref, gate_ref, rank_ref, cnt_ref):
    x = x_ref[...]
    gb, tt, d = x.shape
    tm = gb * tt
    mixed = _dot(y_ref[...].astype(BF16), w_ref[...])
    x1 = x + gt_ref[...] * mixed.reshape(gb, tt, d)
    x1_ref[...] = x1
    h2 = (_rms(x1, g_ref[...]) * (1.0 + sc_ref[...]) + sh_ref[...]).reshape(tm, d)
    h2_ref[...] = h2.astype(BF16)

    work = _dot3(wrt_ref[...], h2, dot=_dot_nt) + br_ref[...]
    e_iota = lax.broadcasted_iota(I32, work.shape, 0).astype(F32)
    vals, idxs, sels = [], [], []
    for _ in range(TOP_K):
        mx = jnp.max(work, axis=0, keepdims=True)
        ik = jnp.min(jnp.where(work == mx, e_iota, float(N_EXPERTS)), axis=0, keepdims=True)
        sel = e_iota == ik
        vals.append(mx)
        idxs.append(ik)
        sels.append(sel)
        work = jnp.where(sel, -jnp.inf, work)
    exps = [jnp.exp(v - vals[0]) for v in vals]
    denom = exps[0] + exps[1] + exps[2] + exps[3]
    gate_ref[...] = jnp.concatenate([e / denom for e in exps], axis=0)
    idx_ref[...] = jnp.concatenate(idxs, axis=0).astype(I32)

    mask = (sels[0] | sels[1] | sels[2] | sels[3]).astype(F32)
    before = _dot(mask.astype(BF16), ustrict_ref[...])
    ranks = [jnp.sum(jnp.where(sel, before, 0.0), axis=0, keepdims=True) for sel in sels]
    rank_ref[...] = jnp.concatenate(ranks, axis=0).astype(I32)
    cnt_ref[0] = jnp.broadcast_to(jnp.sum(mask, axis=1, keepdims=True), cnt_ref.shape[1:])


def _outproj_call(y, x3, mod3, g_ffn, w_out_bf, w_router_t, b_router, ustrict, gb, tt):
    g, t, d = x3.shape
    n = g * t
    tm = gb * tt
    tpg = t // tt
    row = lambda i, j: (i * tpg + j, 0)
    col = lambda i, j: (0, i * tpg + j)
    mod_spec = lambda k: pl.BlockSpec((gb, 1, d), lambda i, j: (i, 0, k))
    const = lambda *shape: pl.BlockSpec(shape, lambda i, j: (0,) * len(shape))
    return pl.pallas_call(
        _outproj_kernel,
        grid=(g // gb, tpg),
        in_specs=[pl.BlockSpec((tm, d), row),
                  pl.BlockSpec((gb, tt, d), lambda i, j: (i, j, 0)),
                  mod_spec(2), mod_spec(3), mod_spec(4),
                  const(1, 1, d), const(d, d), const(N_EXPERTS, d), const(N_EXPERTS, 1),
                  const(tm, tm)],
        out_specs=[pl.BlockSpec((gb, tt, d), lambda i, j: (i, j, 0)),
                   pl.BlockSpec((tm, d), row),
                   pl.BlockSpec((TOP_K, tm), col),
                   pl.BlockSpec((TOP_K, tm), col),
                   pl.BlockSpec((TOP_K, tm), col),
                   pl.BlockSpec((1, N_EXPERTS, 128), lambda i, j: (i * tpg + j, 0, 0))],
        out_shape=[jax.ShapeDtypeStruct((g, t, d), F32),
                   jax.ShapeDtypeStruct((n, d), BF16),
                   jax.ShapeDtypeStruct((TOP_K, n), I32),
                   jax.ShapeDtypeStruct((TOP_K, n), F32),
                   jax.ShapeDtypeStruct((TOP_K, n), I32),
                   jax.ShapeDtypeStruct((n // tm, N_EXPERTS, 128), F32)],
        compiler_params=_params(2), name="outproj_router",
    )(y, x3, mod3, mod3, mod3, g_ffn.reshape(1, 1, d), w_out_bf, w_router_t, b_router.reshape(N_EXPERTS, 1),
      ustrict)


def _chunk_copy(hbm_ref, hbm_row, buf_ref, chunk, sem, to_hbm):
    hbm = hbm_ref.at[pl.ds(pl.multiple_of(hbm_row, RUN_ALIGN), RUN_ALIGN)]
    buf = buf_ref.at[pl.ds(pl.multiple_of(chunk * RUN_ALIGN, RUN_ALIGN), RUN_ALIGN)]
    return pltpu.make_async_copy(buf, hbm, sem) if to_hbm else pltpu.make_async_copy(hbm, buf, sem)


def _move_chunks(hbm_ref, buf_ref, starts_ref, first, n, sem, to_hbm):
    def start(q, c):
        _chunk_copy(hbm_ref, starts_ref[first + q], buf_ref, q, sem, to_hbm).start()
        return c

    def wait(q, c):
        _chunk_copy(hbm_ref, starts_ref[first + q], buf_ref, q, sem, to_hbm).wait()
        return c

    lax.fori_loop(0, n, start, 0)
    lax.fori_loop(0, n, wait, 0)


def _dispatch_kernel(starts_ref, nq_ref, meta_ref, rbuf_ref, h2p_ref, h2s_ref, xs_ref, buf_ref, zero_ref, sem,
                     *, prompt_tiles):
    i = pl.program_id(0)
    tm = h2p_ref.shape[0]
    bm = zero_ref.shape[0]
    n_blocks = xs_ref.shape[0] // bm

    @pl.when(i == 0)
    def _():
        zero_ref[...] = jnp.zeros(zero_ref.shape, zero_ref.dtype)

        def zero_copy(row):
            return pltpu.make_async_copy(zero_ref, xs_ref.at[pl.ds(pl.multiple_of(row, bm), bm)], sem)

        def tails(fn):
            def body(e, c):
                @pl.when(meta_ref[e] >= 0)
                def _():
                    fn(zero_copy(meta_ref[e]))
                return c
            lax.fori_loop(0, N_EXPERTS, body, 0)

        def unused(fn):
            def body(b, c):
                fn(zero_copy(b * bm))
                return c
            lax.fori_loop(meta_ref[N_EXPERTS], n_blocks, body, 0)

        tails(lambda cp: cp.start())
        unused(lambda cp: cp.start())
        tails(lambda cp: cp.wait())
        unused(lambda cp: cp.wait())

    def build(h2_ref):
        h2 = h2_ref[...]
        rb = rbuf_ref[...]
        for c in range(buf_ref.shape[0] // PERM_ROWS):
            r = lax.broadcasted_iota(I32, (PERM_ROWS, tm), 0) + c * PERM_ROWS
            hit = (r == rb[0:1, :]) | (r == rb[1:2, :]) | (r == rb[2:3, :]) | (r == rb[3:4, :])
            onehot = jnp.where(hit, 1.0, 0.0).astype(BF16)
            buf_ref[c * PERM_ROWS:(c + 1) * PERM_ROWS, :] = _dot(onehot, h2).astype(BF16)

    @pl.when(i < prompt_tiles)
    def _():
        build(h2p_ref)

    @pl.when(i >= prompt_tiles)
    def _():
        build(h2s_ref)

    _move_chunks(xs_ref, buf_ref, starts_ref, i * (buf_ref.shape[0] // RUN_ALIGN), nq_ref[i], sem, to_hbm=True)


def _dispatch_call(starts, nq, meta, rbuf, h2_p, h2_s, cap):
    d = h2_p.shape[1]
    tm = MOE_TILE
    pt, st = h2_p.shape[0] // tm, h2_s.shape[0] // tm
    return pl.pallas_call(
        functools.partial(_dispatch_kernel, prompt_tiles=pt),
        grid_spec=pltpu.PrefetchScalarGridSpec(
            num_scalar_prefetch=3,
            grid=(pt + st,),
            in_specs=[pl.BlockSpec((TOP_K, tm), lambda i, *_: (0, i)),
                      pl.BlockSpec((tm, d), lambda i, *_: (jnp.minimum(i, pt - 1), 0)),
                      pl.BlockSpec((tm, d), lambda i, *_: (jnp.maximum(i - pt, 0), 0))],
            out_specs=pl.BlockSpec(memory_space=pl.ANY),
            scratch_shapes=[pltpu.VMEM((TILE_BUF_ROWS, d), BF16), pltpu.VMEM((EXPERT_BLOCK, d), BF16),
                            pltpu.SemaphoreType.DMA(())]),
        out_shape=jax.ShapeDtypeStruct((cap, d), BF16),
        compiler_params=_params(1), name="dispatch",
    )(starts, nq, meta, rbuf, h2_p, h2_s)


def _expert_kernel(be_ref, nu_ref, xs_ref, wup_ref, bup_ref, wdn_ref, bdn_ref, ys_ref, wup_bf, wdn_bf):
    i = pl.program_id(0)
    prev = be_ref[jnp.maximum(i - 1, 0)]

    @pl.when((i == 0) | (be_ref[i] != prev))
    def _():
        def cast(r, c):
            rows = pl.ds(pl.multiple_of(r * 64, 64), 64)
            wup_bf[rows, :] = wup_ref[0, rows, :].astype(BF16)
            wdn_bf[rows, :] = wdn_ref[0, rows, :].astype(BF16)
            return c

        lax.fori_loop(0, D_MODEL // 64, cast, 0)

    @pl.when(i < nu_ref[0])
    def _():
        hu = _dot(xs_ref[...], wup_bf[...]) + bup_ref[0]
        gate = jnp.minimum(hu[:, :D_FF], SWIGLU_LIMIT)
        lin = jnp.clip(hu[:, D_FF:], -SWIGLU_LIMIT, SWIGLU_LIMIT)
        glu = gate * _sigmoid(SWIGLU_ALPHA * gate)
        ys_ref[...] = (_dot(((lin + 1.0) * glu).astype(BF16), wdn_bf[...]) + bdn_ref[0]).astype(ys_ref.dtype)

    @pl.when(i >= nu_ref[0])
    def _():
        ys_ref[...] = jnp.zeros(ys_ref.shape, ys_ref.dtype)


def _expert_call(block_e, n_used, xs, w_up, b_up, w_down, b_down):
    cap, d = xs.shape
    bm = EXPERT_BLOCK
    blk = lambda i, be, nu: (jnp.minimum(i, nu[0] - 1), 0)
    per_e = lambda i, be, nu: (be[i], 0, 0)
    return pl.pallas_call(
        _expert_kernel,
        grid_spec=pltpu.PrefetchScalarGridSpec(
            num_scalar_prefetch=2,
            grid=(cap // bm,),
            in_specs=[pl.BlockSpec((bm, d), blk),
                      pl.BlockSpec((1, d, 2 * D_FF), per_e),
                      pl.BlockSpec((1, 1, 2 * D_FF), per_e),
                      pl.BlockSpec((1, D_FF, d), per_e),
                      pl.BlockSpec((1, 1, d), per_e)],
            out_specs=pl.BlockSpec((bm, d), lambda i, be, nu: (i, 0)),
            scratch_shapes=[pltpu.VMEM((d, 2 * D_FF), BF16), pltpu.VMEM((D_FF, d), BF16)]),
        out_shape=jax.ShapeDtypeStruct((cap, d), BF16),
        compiler_params=_params(1), name="experts",
    )(block_e, n_used, xs, w_up, b_up.reshape(N_EXPERTS, 1, -1), w_down, b_down.reshape(N_EXPERTS, 1, -1))


def _combine_kernel(starts_ref, nq_ref, x1_ref, gt_ref, gate_ref, rcol_ref, gfin_ref, ys_ref, o_ref, buf_ref, sem,
                    *, tile0, tiles_per_group):
    x1 = x1_ref[...]
    gb, tt, d = x1.shape
    tm = gb * tt
    tile = tile0 + pl.program_id(0) * tiles_per_group + pl.program_id(1)

    @pl.when((pl.program_id(0) == 0) & (pl.program_id(1) == 0))
    def _():
        buf_ref[...] = jnp.zeros(buf_ref.shape, buf_ref.dtype)

    _move_chunks(ys_ref, buf_ref, starts_ref, tile * (buf_ref.shape[0] // RUN_ALIGN), nq_ref[tile], sem,
                 to_hbm=False)

    gates = gate_ref[...]
    rows = rcol_ref[...]
    moe = jnp.zeros((tm, d), F32)
    for c in range(buf_ref.shape[0] // PERM_COLS):
        r = lax.broadcasted_iota(I32, (tm, PERM_COLS), 1) + c * PERM_COLS
        g = jnp.where(r == rows[:, 0:1], gates[:, 0:1], 0.0)
        for k in range(1, TOP_K):
            g = g + jnp.where(r == rows[:, k:k + 1], gates[:, k:k + 1], 0.0)
        g_hi, g_lo = _split(g)
        part = buf_ref[c * PERM_COLS:(c + 1) * PERM_COLS, :]
        moe = moe + (_dot(g_hi, part) + _dot(g_lo, part))
    xo = x1 + gt_ref[...] * moe.reshape(gb, tt, d)
    o_ref[...] = _rms(xo, gfin_ref[...])


def _combine_call(starts, nq, x1, mod3, gates_t, rcol, g_final, ys, tok0, gb, tt):
    g, t, d = x1.shape
    tm = gb * tt
    tpg = t // tt
    t0 = tok0 // tm
    tok = lambda i, j, *_: (t0 + i * tpg + j, 0)
    return pl.pallas_call(
        functools.partial(_combine_kernel, tile0=t0, tiles_per_group=tpg),
        grid_spec=pltpu.PrefetchScalarGridSpec(
            num_scalar_prefetch=2,
            grid=(g // gb, tpg),
            in_specs=[pl.BlockSpec((gb, tt, d), lambda i, j, *_: (i, j, 0)),
                      pl.BlockSpec((gb, 1, d), lambda i, j, *_: (i, 0, 5)),
                      pl.BlockSpec((tm, TOP_K), tok),
                      pl.BlockSpec((tm, TOP_K), tok),
                      pl.BlockSpec((1, 1, d), lambda i, j, *_: (0, 0, 0)),
                      pl.BlockSpec(memory_space=pl.ANY)],
            out_specs=pl.BlockSpec((gb, tt, d), lambda i, j, *_: (i, j, 0)),
            scratch_shapes=[pltpu.VMEM((TILE_BUF_ROWS, d), BF16), pltpu.SemaphoreType.DMA(())]),
        out_shape=jax.ShapeDtypeStruct((g, t, d), F32),
        compiler_params=_params(2), name="combine",
    )(starts, nq, x1, mod3, gates_t, rcol, g_final.reshape(1, 1, d), ys)


def _group_blocking(groups, seq, tile):
    if seq >= tile:
        return 1, tile
    return tile // seq, seq


def _gates_by_chunk(gr, groups, seq, L):
    return gr.reshape(N_GATES, groups * seq // L, L).transpose(1, 0, 2)


def kernel(x_prompt, x_sample, c_prompt, c_sample, state_ret, state_mlstm_c, state_mlstm_n, state_mlstm_m, state_conv, w_mod, b_mod, g_mix, g_ffn, w_in, b_igate, b_fgate, w_conv, b_conv, g_ret, g_mlstm, w_out, w_router, b_router, w_up, b_up, w_down, b_down, g_final):
    depth = w_mod.shape[0]
    assert depth == 1, "single-layer trunk"
    bp, tp, d = x_prompt.shape
    bs, ts, _ = x_sample.shape
    n_p, n_s = bp * tp, bs * ts
    hd = HEAD_DIM
    l = 0

    mod = _mod_call(jnp.concatenate([c_prompt, c_sample], axis=0), w_mod[l], b_mod[l])
    mod_p = mod[:bp].reshape(bp, 1, N_MOD * d)
    mod_s = mod[bp:].reshape(bs, 1, N_MOD * d)

    w_main_bf = w_in[l][:, :MAIN_COLS].astype(BF16)
    w_gate = jnp.pad(w_in[l][:, MAIN_COLS:], ((0, 0), (0, 128 - N_GATES)))
    w_gate_t = w_in[l][:, MAIN_COLS:].T
    w_out_bf = w_out[l].astype(BF16)
    bg_col = jnp.concatenate([b_igate[l], b_fgate[l]]).reshape(1, N_GATES)
    bg_row = bg_col.reshape(N_GATES, 1)

    groups = (
        (x_prompt, mod_p, min(CHUNK, tp), np.arange(tp),
         jnp.zeros((bp, RET_HEADS, hd, hd), F32), jnp.zeros((bp, MLSTM_HEADS, hd, hd), F32),
         jnp.zeros((bp, MLSTM_HEADS, hd), F32), jnp.zeros((bp, MLSTM_HEADS), F32),
         jnp.zeros((bp, CONV_WIDTH - 1, 2 * MLSTM_WIDTH), F32)),
        (x_sample, mod_s, min(CHUNK, ts), PAST_LEN + np.arange(ts),
         state_ret[l], state_mlstm_c[l], state_mlstm_n[l], state_mlstm_m[l], state_conv[l]),
    )

    ustrict = jnp.asarray(np.arange(MOE_TILE)[:, None] < np.arange(MOE_TILE)[None, :], BF16)
    staged = []
    for x3, mod3, L, pos, s0, c0, n0, m0, conv0 in groups:
        g, t, _ = x3.shape
        gb, tt = _group_blocking(g, t, TOKEN_TILE)
        p, gc, gr = _inproj_call(x3, mod3, g_mix[l], w_main_bf, w_gate, w_gate_t, gb, tt)
        conv0p = jnp.pad(conv0.astype(F32), ((0, 0), (CONV_TAIL_ROWS - (CONV_WIDTH - 1), 0), (0, 0)))
        y, s_new, c_new, n_new, m_new, tail = _mixer_call(
            p, gc, _gates_by_chunk(gr, g, t, L), _mixer_consts(L, pos), w_conv[l], b_conv[l], g_ret[l], g_mlstm[l],
            bg_col, bg_row, s0.astype(F32), c0.astype(F32), n0.astype(F32).reshape(g, MLSTM_HEADS, 1, hd),
            jnp.broadcast_to(m0.astype(F32)[:, :, None, None], (g, MLSTM_HEADS, 1, hd)), conv0p,
            g, t, L, F32)
        states = (s_new, c_new, n_new.reshape(g, MLSTM_HEADS, hd), m_new[:, :, 0, 0],
                  tail[:, CONV_TAIL_ROWS - (CONV_WIDTH - 1):, :])
        gb, tt = _group_blocking(g, t, MOE_TILE)
        x1, h2, idx, gates, rank, cnt = _outproj_call(
            y, x3, mod3, g_ffn[l], w_out_bf, w_router[l].T, b_router[l], ustrict, gb, tt)
        staged.append((x1, mod3, h2, idx, gates, rank, states, cnt))

    n_tok = n_p + n_s
    tm, bm, ra = MOE_TILE, EXPERT_BLOCK, RUN_ALIGN
    n_tiles = n_tok // tm
    q_max = TILE_BUF_ROWS // ra
    n_blocks = -(-(n_tok * TOP_K + n_tiles * N_EXPERTS * (ra - 1)) // bm) + N_EXPERTS
    cap = n_blocks * bm
    counts = jnp.concatenate([s[7][:, :, 0] for s in staged], axis=0).astype(I32)
    run = (counts + ra - 1) // ra * ra
    region = jnp.sum(run, axis=0)
    padded = (region + bm - 1) // bm * bm
    pad_end = jnp.cumsum(padded)
    pad_start = pad_end - padded
    run_start = pad_start[None, :] + jnp.cumsum(run, axis=0) - run
    buf_end = jnp.cumsum(run, axis=1)
    buf_start = buf_end - run
    nq = (buf_end[:, -1] // ra).astype(I32)
    chunk_row = jnp.arange(q_max, dtype=I32) * ra
    chunk_e = jnp.minimum(jnp.sum((buf_end[:, None, :] <= chunk_row[None, :, None]).astype(I32), axis=2),
                          N_EXPERTS - 1)
    e_ids = jnp.arange(N_EXPERTS, dtype=I32)
    shift = run_start - buf_start
    starts = jnp.sum(jnp.where(chunk_e[:, :, None] == e_ids, shift[:, None, :], 0), axis=2) + chunk_row[None, :]
    starts = jnp.where(chunk_row[None, :] < buf_end[:, -1:], starts, 0).reshape(-1).astype(I32)

    idx_all = jnp.concatenate([s[3] for s in staged], axis=1)
    rank_all = jnp.concatenate([s[5] for s in staged], axis=1)
    gates_t = jnp.concatenate([s[4] for s in staged], axis=1).T
    buf_start_tok = jnp.repeat(buf_start, tm, axis=0).T
    rbuf = jnp.sum(jnp.where(idx_all[None] == e_ids[:, None, None], buf_start_tok[:, None, :], 0), axis=0) + rank_all
    rcol = rbuf.T
    block_row = jnp.arange(n_blocks, dtype=I32) * bm
    block_e = jnp.minimum(jnp.sum((pad_end[None, :] <= block_row[:, None]).astype(I32), axis=1), N_EXPERTS - 1)
    n_used = (pad_end[-1:] // bm).astype(I32)
    meta = jnp.concatenate([jnp.where(region > 0, pad_end - bm, -1), n_used]).astype(I32)

    xs = _dispatch_call(starts, nq, meta, rbuf, staged[0][2], staged[1][2], cap)
    ys = _expert_call(block_e, n_used, xs, w_up[l], b_up[l], w_down[l], b_down[l])

    outs = []
    tok0 = 0
    for x1, mod3, *_ in staged:
        g, t, _ = x1.shape
        gb, tt = _group_blocking(g, t, tm)
        outs.append(_combine_call(starts, nq, x1, mod3, gates_t, rcol, g_final, ys, tok0, gb, tt))
        tok0 += g * t

    st_p, st_s = staged[0][6], staged[1][6]
    return (outs[0], outs[1]) + tuple(a[None] for a in st_p) + tuple(a[None] for a in st_s)
```

```python
import functools

import numpy as np
import jax
import jax.numpy as jnp
from jax import lax
from jax.experimental import pallas as pl
from jax.experimental.pallas import tpu as pltpu

F32 = jnp.float32
BF16 = jnp.bfloat16
I32 = jnp.int32

D_MODEL = 1024
PAST_LEN = 16384
RET_HEADS = 4
MLSTM_HEADS = 4
HEAD_DIM = 128
RET_WIDTH = RET_HEADS * HEAD_DIM
MLSTM_WIDTH = MLSTM_HEADS * HEAD_DIM
CONV_WIDTH = 4
CHUNK = 128
ROPE_BASE = 10000.0
N_EXPERTS = 32
TOP_K = 4
D_FF = D_MODEL
SWIGLU_LIMIT = 7.0
SWIGLU_ALPHA = 1.702
N_MOD = 6
EPS = 1e-6
MAIN_COLS = 4 * RET_WIDTH + 2 * MLSTM_WIDTH + 2 * MLSTM_WIDTH
N_GATES = 2 * MLSTM_HEADS
OFF_RQ, OFF_RK, OFF_RV, OFF_RG = 0, RET_WIDTH, 2 * RET_WIDTH, 3 * RET_WIDTH
OFF_MQK = 4 * RET_WIDTH
OFF_MV = OFF_MQK + 2 * MLSTM_WIDTH
OFF_MO = OFF_MV + MLSTM_WIDTH

VMEM_LIMIT_BYTES = 56 * 1024 * 1024
TOKEN_TILE = 512
MOE_TILE = 512
EXPERT_BLOCK = 512
MIXER_SEQS_CHUNKED = 2
MIXER_SEQS_SHORT = 4
RUN_ALIGN = 16
TILE_BUF_ROWS = MOE_TILE * TOP_K + N_EXPERTS * RUN_ALIGN
PERM_ROWS = 256
PERM_COLS = 512
CONV_TAIL_ROWS = 8


def _params(n_axes=1):
    return pltpu.CompilerParams(dimension_semantics=("arbitrary",) * n_axes, vmem_limit_bytes=VMEM_LIMIT_BYTES)


def _dot(a, b):
    return jnp.dot(a, b, preferred_element_type=F32)


def _dot_nt(a, b):
    return lax.dot_general(a, b, (((1,), (1,)), ((), ())), preferred_element_type=F32)


def _dot_tn(a, b):
    return lax.dot_general(a, b, (((0,), (0,)), ((), ())), preferred_element_type=F32)


def _split(a):
    hi = a.astype(BF16)
    lo = (a - hi.astype(F32)).astype(BF16)
    return hi, lo


def _dot3(a, b, dot=_dot):
    ah, al = _split(a)
    bh, bl = _split(b)
    return dot(ah, bh) + (dot(al, bh) + dot(ah, bl))


def _sigmoid(x):
    return 0.5 * (jnp.tanh(0.5 * x) + 1.0)


def _log_sigmoid(x):
    return jnp.minimum(x, 0.0) - jnp.log1p(jnp.exp(-jnp.abs(x)))


def _rms(x, g):
    ms = jnp.mean(x * x, axis=-1, keepdims=True)
    return (x * lax.rsqrt(ms + EPS)) * g


def _layer_norm(x, g):
    mu = jnp.mean(x, axis=-1, keepdims=True)
    xc = x - mu
    var = jnp.mean(xc * xc, axis=-1, keepdims=True)
    return xc * lax.rsqrt(var + EPS) * g


def _mod_kernel(c_ref, w_ref, b_ref, o_ref):
    c = c_ref[...]
    o_ref[...] = _dot3(c * _sigmoid(c), w_ref[...]) + b_ref[...]


def _mod_call(c_all, w_mod, b_mod):
    rows, d = c_all.shape
    cols = w_mod.shape[1]
    tn = 1024
    return pl.pallas_call(
        _mod_kernel,
        grid=(cols // tn,),
        in_specs=[pl.BlockSpec((rows, d), lambda j: (0, 0)),
                  pl.BlockSpec((d, tn), lambda j: (0, j)),
                  pl.BlockSpec((1, tn), lambda j: (0, j))],
        out_specs=pl.BlockSpec((rows, tn), lambda j: (0, j)),
        out_shape=jax.ShapeDtypeStruct((rows, cols), F32),
        compiler_params=_params(1), name="mod",
    )(c_all, w_mod, b_mod.reshape(1, cols))


def _inproj_kernel(x_ref, sh_ref, sc_ref, g_ref, w_ref, wg_ref, wgt_ref, p_ref, gc_ref, gr_ref):
    x = x_ref[...]
    gb, tt, d = x.shape
    h = _rms(x, g_ref[...]) * (1.0 + sc_ref[...]) + sh_ref[...]
    h = h.reshape(gb * tt, d)
    hb = h.astype(BF16)
    for j in range(MAIN_COLS // 1024):
        p_ref[:, j * 1024:(j + 1) * 1024] = _dot(hb, w_ref[:, j * 1024:(j + 1) * 1024])
    gc_ref[...] = _dot3(h, wg_ref[...])[:, :N_GATES]
    gr_ref[...] = _dot3(wgt_ref[...], h, dot=_dot_nt)


def _inproj_call(x3, mod3, g_mix, w_main_bf, w_gate, w_gate_t, gb, tt):
    g, t, d = x3.shape
    n = g * t
    tm = gb * tt
    tpg = t // tt
    grid = (g // gb, tpg)
    row = lambda i, j: (i * tpg + j, 0)
    return pl.pallas_call(
        _inproj_kernel,
        grid=grid,
        in_specs=[pl.BlockSpec((gb, tt, d), lambda i, j: (i, j, 0)),
                  pl.BlockSpec((gb, 1, d), lambda i, j: (i, 0, 0)),
                  pl.BlockSpec((gb, 1, d), lambda i, j: (i, 0, 1)),
                  pl.BlockSpec((1, 1, d), lambda i, j: (0, 0, 0)),
                  pl.BlockSpec((d, MAIN_COLS), lambda i, j: (0, 0)),
                  pl.BlockSpec((d, 128), lambda i, j: (0, 0)),
                  pl.BlockSpec((N_GATES, d), lambda i, j: (0, 0))],
        out_specs=[pl.BlockSpec((tm, MAIN_COLS), row),
                   pl.BlockSpec((tm, N_GATES), row),
                   pl.BlockSpec((N_GATES, tm), lambda i, j: (0, i * tpg + j))],
        out_shape=[jax.ShapeDtypeStruct((n, MAIN_COLS), F32),
                   jax.ShapeDtypeStruct((n, N_GATES), F32),
                   jax.ShapeDtypeStruct((N_GATES, n), F32)],
        compiler_params=_params(2), name="inproj",
    )(x3, mod3, mod3, g_mix.reshape(1, 1, d), w_main_bf, w_gate, w_gate_t)


def _mixer_kernel(p_ref, gc_ref, gr_ref, cos_ref, sin_ref, dec_ref, qd_ref, kd_ref, cd_ref,
                  tril_ref, triu_ref, wconv_ref, bconv_ref, gret_ref, gml_ref, bgc_ref, bgr_ref,
                  s0_ref, c0_ref, n0_ref, m0_ref, conv0_ref,
                  y_ref, s_ref, c_ref, n_ref, m_ref, tail_ref, xp_ref):
    @pl.when(pl.program_id(1) == 0)
    def _():
        s_ref[...] = s0_ref[...]
        c_ref[...] = c0_ref[...]
        n_ref[...] = n0_ref[...]
        m_ref[...] = m0_ref[...]
        tail_ref[...] = conv0_ref[...]

    for sq in range(p_ref.shape[0]):
        _mixer_sequence(p_ref.at[sq], gc_ref.at[sq], gr_ref.at[sq], cos_ref, sin_ref, dec_ref, qd_ref, kd_ref, cd_ref,
                        tril_ref, triu_ref, wconv_ref, bconv_ref, gret_ref, gml_ref, bgc_ref, bgr_ref,
                        y_ref.at[sq], s_ref.at[sq], c_ref.at[sq], n_ref.at[sq], m_ref.at[sq], tail_ref.at[sq],
                        xp_ref.at[sq])


def _mixer_sequence(p_ref, gc_ref, gr_ref, cos_ref, sin_ref, dec_ref, qd_ref, kd_ref, cd_ref,
                    tril_ref, triu_ref, wconv_ref, bconv_ref, gret_ref, gml_ref, bgc_ref, bgr_ref,
                    y_ref, s_ref, c_ref, n_ref, m_ref, tail_ref, xp_ref):
    L = p_ref.shape[0]
    cos = cos_ref[...]
    sin = sin_ref[...]
    scale = HEAD_DIM ** -0.5

    def rot(x):
        return x * cos + pltpu.roll(x, HEAD_DIM // 2, axis=1) * sin

    for h in range(RET_HEADS):
        lo = h * HEAD_DIM
        q = rot(p_ref[:, OFF_RQ + lo:OFF_RQ + lo + HEAD_DIM])
        k = rot(p_ref[:, OFF_RK + lo:OFF_RK + lo + HEAD_DIM]) * scale
        v = p_ref[:, OFF_RV + lo:OFF_RV + lo + HEAD_DIM].astype(BF16)
        g = p_ref[:, OFF_RG + lo:OFF_RG + lo + HEAD_DIM]
        s_old = s_ref[h]
        scores = _dot_nt(q.astype(BF16), k.astype(BF16)) * dec_ref[h]
        out = _dot(scores.astype(BF16), v) + _dot((q * qd_ref[h]).astype(BF16), s_old.astype(BF16))
        s_ref[h] = cd_ref[h] * s_old + _dot_tn((k * kd_ref[h]).astype(BF16), v)
        y_ref[:, lo:lo + HEAD_DIM] = ((g * _sigmoid(g)) * _layer_norm(out, gret_ref[:, lo:lo + HEAD_DIM])).astype(y_ref.dtype)

    xp_ref[0:CONV_TAIL_ROWS, :] = tail_ref[...]
    xp_ref[CONV_TAIL_ROWS:CONV_TAIL_ROWS + L, :] = p_ref[:, OFF_MQK:OFF_MQK + 2 * MLSTM_WIDTH]
    acc = bconv_ref[...] + wconv_ref[0:1, :] * xp_ref[CONV_TAIL_ROWS - 3:CONV_TAIL_ROWS - 3 + L, :]
    for j in range(1, CONV_WIDTH):
        acc = acc + wconv_ref[j:j + 1, :] * xp_ref[CONV_TAIL_ROWS - 3 + j:CONV_TAIL_ROWS - 3 + j + L, :]
    tail_ref[...] = xp_ref[L:L + CONV_TAIL_ROWS, :]
    xp_ref[CONV_TAIL_ROWS:CONV_TAIL_ROWS + L, :] = acc * _sigmoid(acc)

    gcol = gc_ref[...] + bgc_ref[...]
    is_f_col = lax.broadcasted_iota(I32, gcol.shape, 1) >= MLSTM_HEADS
    gcol = jnp.where(is_f_col, _log_sigmoid(gcol), gcol)
    grow = gr_ref[0] + bgr_ref[...]
    is_f_row = lax.broadcasted_iota(I32, grow.shape, 0) >= MLSTM_HEADS
    grow = jnp.where(is_f_row, _log_sigmoid(grow), grow)
    bcol_all = _dot3(tril_ref[...], gcol)
    brow_all = _dot3(grow, triu_ref[...])
    causal = lax.broadcasted_iota(I32, (L, L), 0) >= lax.broadcasted_iota(I32, (L, L), 1)

    for h in range(MLSTM_HEADS):
        lo = h * HEAD_DIM
        q = xp_ref[CONV_TAIL_ROWS:CONV_TAIL_ROWS + L, lo:lo + HEAD_DIM]
        k = xp_ref[CONV_TAIL_ROWS:CONV_TAIL_ROWS + L, MLSTM_WIDTH + lo:MLSTM_WIDTH + lo + HEAD_DIM] * scale
        v = p_ref[:, OFF_MV + lo:OFF_MV + lo + HEAD_DIM].astype(BF16)
        o = p_ref[:, OFF_MO + lo:OFF_MO + lo + HEAD_DIM]
        ic_col = gcol[:, h:h + 1]
        ic_row = grow[h:h + 1, :]
        b_col = bcol_all[:, MLSTM_HEADS + h:MLSTM_HEADS + h + 1]
        b_row = brow_all[MLSTM_HEADS + h:MLSTM_HEADS + h + 1, :]
        c_old = c_ref[h]
        n_old = n_ref[h]
        m_old = m_ref[h][:, 0:1]

        d_log = jnp.where(causal, b_col - b_row + ic_row, -jnp.inf)
        inter = b_col + m_old
        m_t = jnp.maximum(inter, jnp.max(d_log, axis=1, keepdims=True))
        w_intra = jnp.exp(d_log - m_t)
        w_inter = jnp.exp(inter - m_t)
        qb = q.astype(BF16)
        s = _dot_nt(qb, k.astype(BF16)) * w_intra
        num = _dot(s.astype(BF16), v) + w_inter * _dot(qb, c_old.astype(BF16))
        den = jnp.sum(s, axis=1, keepdims=True) + w_inter * jnp.sum(q * n_old, axis=1, keepdims=True)
        hh = num / jnp.maximum(jnp.abs(den), jnp.exp(-m_t))

        b_last = b_col[L - 1:L, :]
        w_log_col = b_last - b_col + ic_col
        m_new = jnp.maximum(b_last + m_old, jnp.max(w_log_col, axis=0, keepdims=True))
        wk = jnp.exp(w_log_col - m_new) * k
        cdec = jnp.exp(b_last + m_old - m_new)
        c_ref[h] = cdec * c_old + _dot_tn(wk.astype(BF16), v)
        n_ref[h] = cdec * n_old + jnp.sum(wk, axis=0, keepdims=True)
        m_ref[h] = jnp.broadcast_to(m_new, (1, HEAD_DIM))
        y_ref[:, RET_WIDTH + lo:RET_WIDTH + lo + HEAD_DIM] = (
            _sigmoid(o) * _layer_norm(hh, gml_ref[:, lo:lo + HEAD_DIM])).astype(y_ref.dtype)


def _mixer_consts(L, pos):
    f32 = np.float32
    half = HEAD_DIM // 2
    inv_freq = np.power(f32(ROPE_BASE), -np.arange(half, dtype=f32) / f32(half)).astype(f32)
    ang = (pos.astype(f32)[:, None] * inv_freq[None, :]).astype(f32)
    cos = np.concatenate([np.cos(ang), np.cos(ang)], axis=-1).astype(f32)
    sin = np.concatenate([-np.sin(ang), np.sin(ang)], axis=-1).astype(f32)
    log_gamma = np.log1p(-np.exp2(-5.0 - np.arange(RET_HEADS, dtype=np.float64)))
    idx = np.arange(L, dtype=np.float64)
    rel = idx[:, None] - idx[None, :]
    dec = np.where(rel >= 0, np.exp(log_gamma[:, None, None] * np.maximum(rel, 0.0)), 0.0)
    qd = np.broadcast_to(np.exp(log_gamma[:, None] * (idx + 1.0))[..., None], (RET_HEADS, L, HEAD_DIM))
    kd = np.broadcast_to(np.exp(log_gamma[:, None] * (L - 1.0 - idx))[..., None], (RET_HEADS, L, HEAD_DIM))
    cd = np.broadcast_to(np.exp(log_gamma * L)[:, None, None], (RET_HEADS, 1, HEAD_DIM))
    tril = rel >= 0
    triu = rel <= 0
    return tuple(jnp.asarray(a, F32) for a in (cos, sin, dec, qd, kd, cd, tril, triu))


def _mixer_call(p, gc, gr4, consts, w_conv, b_conv, g_ret, g_mlstm, bg_col, bg_row,
                s0, c0, n0, m0, conv0, groups, seq, L, gs, y_dtype):
    cos, sin, dec, qd, kd, cd, tril, triu = consts
    nc = seq // L
    hd = HEAD_DIM
    full = lambda *shape: pl.BlockSpec(shape, lambda g, c: (0,) * len(shape))
    state4 = pl.BlockSpec((gs, RET_HEADS, hd, hd), lambda g, c: (g, 0, 0, 0))
    vec4 = pl.BlockSpec((gs, MLSTM_HEADS, 1, hd), lambda g, c: (g, 0, 0, 0))
    tail3 = pl.BlockSpec((gs, CONV_TAIL_ROWS, 2 * MLSTM_WIDTH), lambda g, c: (g, 0, 0))
    row = lambda g, c: (g, c, 0)
    return pl.pallas_call(
        _mixer_kernel,
        grid=(groups // gs, nc),
        in_specs=[pl.BlockSpec((gs, L, MAIN_COLS), row),
                  pl.BlockSpec((gs, L, N_GATES), row),
                  pl.BlockSpec((gs, 1, N_GATES, L), lambda g, c: (g, c, 0, 0)),
                  pl.BlockSpec((L, hd), lambda g, c: (c, 0)),
                  pl.BlockSpec((L, hd), lambda g, c: (c, 0)),
                  full(RET_HEADS, L, L), full(RET_HEADS, L, hd), full(RET_HEADS, L, hd), full(RET_HEADS, 1, hd),
                  full(L, L), full(L, L),
                  full(CONV_WIDTH, 2 * MLSTM_WIDTH), full(1, 2 * MLSTM_WIDTH),
                  full(1, RET_WIDTH), full(1, MLSTM_WIDTH), full(1, N_GATES), full(N_GATES, 1),
                  state4, state4, vec4, vec4, tail3],
        out_specs=[pl.BlockSpec((gs, L, RET_WIDTH + MLSTM_WIDTH), row),
                   state4, state4, vec4, vec4, tail3],
        out_shape=[jax.ShapeDtypeStruct((groups, seq, RET_WIDTH + MLSTM_WIDTH), y_dtype),
                   jax.ShapeDtypeStruct((groups, RET_HEADS, hd, hd), F32),
                   jax.ShapeDtypeStruct((groups, MLSTM_HEADS, hd, hd), F32),
                   jax.ShapeDtypeStruct((groups, MLSTM_HEADS, 1, hd), F32),
                   jax.ShapeDtypeStruct((groups, MLSTM_HEADS, 1, hd), F32),
                   jax.ShapeDtypeStruct((groups, CONV_TAIL_ROWS, 2 * MLSTM_WIDTH), F32)],
        scratch_shapes=[pltpu.VMEM((gs, CONV_TAIL_ROWS + L, 2 * MLSTM_WIDTH), F32)],
        compiler_params=_params(2), name="mixer",
    )(p, gc, gr4, cos, sin, dec, qd, kd, cd, tril, triu, w_conv, b_conv.reshape(1, -1),
      g_ret.reshape(1, -1), g_mlstm.reshape(1, -1), bg_col, bg_row, s0, c0, n0, m0, conv0)


def _outproj_kernel(y_ref, x_ref, gt_ref, sh_ref, sc_ref, g_ref, w_ref, wrt_ref, br_ref, ustrict_ref,
                    x1_ref, h2_ref, idx_ref, gate_ref, rank_ref, cnt_ref):
    x = x_ref[...]
    gb, tt, d = x.shape
    tm = gb * tt
    mixed = _dot(y_ref[...].astype(BF16), w_ref[...])
    x1 = x + gt_ref[...] * mixed.reshape(gb, tt, d)
    x1_ref[...] = x1
    h2 = (_rms(x1, g_ref[...]) * (1.0 + sc_ref[...]) + sh_ref[...]).reshape(tm, d)
    h2_ref[...] = h2.astype(BF16)

    work = _dot3(wrt_ref[...], h2, dot=_dot_nt) + br_ref[...]
    e_iota = lax.broadcasted_iota(I32, work.shape, 0).astype(F32)
    vals, idxs, sels = [], [], []
    for _ in range(TOP_K):
        mx = jnp.max(work, axis=0, keepdims=True)
        ik = jnp.min(jnp.where(work == mx, e_iota, float(N_EXPERTS)), axis=0, keepdims=True)
        sel = e_iota == ik
        vals.append(mx)
        idxs.append(ik)
        sels.append(sel)
        work = jnp.where(sel, -jnp.inf, work)
    exps = [jnp.exp(v - vals[0]) for v in vals]
    denom = exps[0] + exps[1] + exps[2] + exps[3]
    gate_ref[...] = jnp.concatenate([e / denom for e in exps], axis=0)
    idx_ref[...] = jnp.concatenate(idxs, axis=0).astype(I32)

    mask = (sels[0] | sels[1] | sels[2] | sels[3]).astype(F32)
    before = _dot(mask.astype(BF16), ustrict_ref[...])
    ranks = [jnp.sum(jnp.where(sel, before, 0.0), axis=0, keepdims=True) for sel in sels]
    rank_ref[...] = jnp.concatenate(ranks, axis=0).astype(I32)
    cnt_ref[0] = jnp.broadcast_to(jnp.sum(mask, axis=1, keepdims=True), cnt_ref.shape[1:])


def _outproj_call(y, x3, mod3, g_ffn, w_out_bf, w_router_t, b_router, ustrict, gb, tt):
    g, t, d = x3.shape
    n = g * t
    tm = gb * tt
    tpg = t // tt
    row = lambda i, j: (i * tpg + j, 0)
    col = lambda i, j: (0, i * tpg + j)
    mod_spec = lambda k: pl.BlockSpec((gb, 1, d), lambda i, j: (i, 0, k))
    const = lambda *shape: pl.BlockSpec(shape, lambda i, j: (0,) * len(shape))
    return pl.pallas_call(
        _outproj_kernel,
        grid=(g // gb, tpg),
        in_specs=[pl.BlockSpec((tm, d), row),
                  pl.BlockSpec((gb, tt, d), lambda i, j: (i, j, 0)),
                  mod_spec(2), mod_spec(3), mod_spec(4),
                  const(1, 1, d), const(d, d), const(N_EXPERTS, d), const(N_EXPERTS, 1),
                  const(tm, tm)],
        out_specs=[pl.BlockSpec((gb, tt, d), lambda i, j: (i, j, 0)),
                   pl.BlockSpec((tm, d), row),
                   pl.BlockSpec((TOP_K, tm), col),
                   pl.BlockSpec((TOP_K, tm), col),
                   pl.BlockSpec((TOP_K, tm), col),
                   pl.BlockSpec((1, N_EXPERTS, 128), lambda i, j: (i * tpg + j, 0, 0))],
        out_shape=[jax.ShapeDtypeStruct((g, t, d), F32),
                   jax.ShapeDtypeStruct((n, d), BF16),
                   jax.ShapeDtypeStruct((TOP_K, n), I32),
                   jax.ShapeDtypeStruct((TOP_K, n), F32),
                   jax.ShapeDtypeStruct((TOP_K, n), I32),
                   jax.ShapeDtypeStruct((n // tm, N_EXPERTS, 128), F32)],
        compiler_params=_params(2), name="outproj_router",
    )(y, x3, mod3, mod3, mod3, g_ffn.reshape(1, 1, d), w_out_bf, w_router_t, b_router.reshape(N_EXPERTS, 1),
      ustrict)


def _chunk_copy(hbm_ref, hbm_row, buf_ref, chunk, sem, to_hbm):
    hbm = hbm_ref.at[pl.ds(pl.multiple_of(hbm_row, RUN_ALIGN), RUN_ALIGN)]
    buf = buf_ref.at[pl.ds(pl.multiple_of(chunk * RUN_ALIGN, RUN_ALIGN), RUN_ALIGN)]
    return pltpu.make_async_copy(buf, hbm, sem) if to_hbm else pltpu.make_async_copy(hbm, buf, sem)


def _move_chunks(hbm_ref, buf_ref, starts_ref, first, n, sem, to_hbm):
    def start(q, c):
        _chunk_copy(hbm_ref, starts_ref[first + q], buf_ref, q, sem, to_hbm).start()
        return c

    def wait(q, c):
        _chunk_copy(hbm_ref, starts_ref[first + q], buf_ref, q, sem, to_hbm).wait()
        return c

    lax.fori_loop(0, n, start, 0)
    lax.fori_loop(0, n, wait, 0)


def _dispatch_kernel(starts_ref, nq_ref, meta_ref, rbuf_ref, h2p_ref, h2s_ref, xs_ref, buf_ref, zero_ref, sem,
                     *, prompt_tiles):
    i = pl.program_id(0)
    tm = h2p_ref.shape[0]
    bm = zero_ref.shape[0]
    n_blocks = xs_ref.shape[0] // bm

    @pl.when(i == 0)
    def _():
        zero_ref[...] = jnp.zeros(zero_ref.shape, zero_ref.dtype)

        def zero_copy(row):
            return pltpu.make_async_copy(zero_ref, xs_ref.at[pl.ds(pl.multiple_of(row, bm), bm)], sem)

        def tails(fn):
            def body(e, c):
                @pl.when(meta_ref[e] >= 0)
                def _():
                    fn(zero_copy(meta_ref[e]))
                return c
            lax.fori_loop(0, N_EXPERTS, body, 0)

        def unused(fn):
            def body(b, c):
                fn(zero_copy(b * bm))
                return c
            lax.fori_loop(meta_ref[N_EXPERTS], n_blocks, body, 0)

        tails(lambda cp: cp.start())
        unused(lambda cp: cp.start())
        tails(lambda cp: cp.wait())
        unused(lambda cp: cp.wait())

    def build(h2_ref):
        h2 = h2_ref[...]
        rb = rbuf_ref[...]
        for c in range(buf_ref.shape[0] // PERM_ROWS):
            r = lax.broadcasted_iota(I32, (PERM_ROWS, tm), 0) + c * PERM_ROWS
            hit = (r == rb[0:1, :]) | (r == rb[1:2, :]) | (r == rb[2:3, :]) | (r == rb[3:4, :])
            onehot = jnp.where(hit, 1.0, 0.0).astype(BF16)
            buf_ref[c * PERM_ROWS:(c + 1) * PERM_ROWS, :] = _dot(onehot, h2).astype(BF16)

    @pl.when(i < prompt_tiles)
    def _():
        build(h2p_ref)

    @pl.when(i >= prompt_tiles)
    def _():
        build(h2s_ref)

    _move_chunks(xs_ref, buf_ref, starts_ref, i * (buf_ref.shape[0] // RUN_ALIGN), nq_ref[i], sem, to_hbm=True)


def _dispatch_call(starts, nq, meta, rbuf, h2_p, h2_s, cap):
    d = h2_p.shape[1]
    tm = MOE_TILE
    pt, st = h2_p.shape[0] // tm, h2_s.shape[0] // tm
    return pl.pallas_call(
        functools.partial(_dispatch_kernel, prompt_tiles=pt),
        grid_spec=pltpu.PrefetchScalarGridSpec(
            num_scalar_prefetch=3,
            grid=(pt + st,),
            in_specs=[pl.BlockSpec((TOP_K, tm), lambda i, *_: (0, i)),
                      pl.BlockSpec((tm, d), lambda i, *_: (jnp.minimum(i, pt - 1), 0)),
                      pl.BlockSpec((tm, d), lambda i, *_: (jnp.maximum(i - pt, 0), 0))],
            out_specs=pl.BlockSpec(memory_space=pl.ANY),
            scratch_shapes=[pltpu.VMEM((TILE_BUF_ROWS, d), BF16), pltpu.VMEM((EXPERT_BLOCK, d), BF16),
                            pltpu.SemaphoreType.DMA(())]),
        out_shape=jax.ShapeDtypeStruct((cap, d), BF16),
        compiler_params=_params(1), name="dispatch",
    )(starts, nq, meta, rbuf, h2_p, h2_s)


def _expert_kernel(be_ref, nu_ref, xs_ref, wup_ref, bup_ref, wdn_ref, bdn_ref, ys_ref, wup_bf, wdn_bf):
    i = pl.program_id(0)
    prev = be_ref[jnp.maximum(i - 1, 0)]

    @pl.when((i == 0) | (be_ref[i] != prev))
    def _():
        def cast(r, c):
            rows = pl.ds(pl.multiple_of(r * 64, 64), 64)
            wup_bf[rows, :] = wup_ref[0, rows, :].astype(BF16)
            wdn_bf[rows, :] = wdn_ref[0, rows, :].astype(BF16)
            return c

        lax.fori_loop(0, D_MODEL // 64, cast, 0)

    @pl.when(i < nu_ref[0])
    def _():
        hu = _dot(xs_ref[...], wup_bf[...]) + bup_ref[0]
        gate = jnp.minimum(hu[:, :D_FF], SWIGLU_LIMIT)
        lin = jnp.clip(hu[:, D_FF:], -SWIGLU_LIMIT, SWIGLU_LIMIT)
        glu = gate * _sigmoid(SWIGLU_ALPHA * gate)
        ys_ref[...] = (_dot(((lin + 1.0) * glu).astype(BF16), wdn_bf[...]) + bdn_ref[0]).astype(ys_ref.dtype)

    @pl.when(i >= nu_ref[0])
    def _():
        ys_ref[...] = jnp.zeros(ys_ref.shape, ys_ref.dtype)


def _expert_call(block_e, n_used, xs, w_up, b_up, w_down, b_down):
    cap, d = xs.shape
    bm = EXPERT_BLOCK
    blk = lambda i, be, nu: (jnp.minimum(i, nu[0] - 1), 0)
    per_e = lambda i, be, nu: (be[i], 0, 0)
    return pl.pallas_call(
        _expert_kernel,
        grid_spec=pltpu.PrefetchScalarGridSpec(
            num_scalar_prefetch=2,
            grid=(cap // bm,),
            in_specs=[pl.BlockSpec((bm, d), blk),
                      pl.BlockSpec((1, d, 2 * D_FF), per_e),
                      pl.BlockSpec((1, 1, 2 * D_FF), per_e),
                      pl.BlockSpec((1, D_FF, d), per_e),
                      pl.BlockSpec((1, 1, d), per_e)],
            out_specs=pl.BlockSpec((bm, d), lambda i, be, nu: (i, 0)),
            scratch_shapes=[pltpu.VMEM((d, 2 * D_FF), BF16), pltpu.VMEM((D_FF, d), BF16)]),
        out_shape=jax.ShapeDtypeStruct((cap, d), BF16),
        compiler_params=_params(1), name="experts",
    )(block_e, n_used, xs, w_up, b_up.reshape(N_EXPERTS, 1, -1), w_down, b_down.reshape(N_EXPERTS, 1, -1))


def _combine_kernel(starts_ref, nq_ref, x1_ref, gt_ref, gate_ref, rcol_ref, gfin_ref, ys_ref, o_ref, buf_ref, sem,
                    *, tile0, tiles_per_group):
    x1 = x1_ref[...]
    gb, tt, d = x1.shape
    tm = gb * tt
    tile = tile0 + pl.program_id(0) * tiles_per_group + pl.program_id(1)

    @pl.when((pl.program_id(0) == 0) & (pl.program_id(1) == 0))
    def _():
        buf_ref[...] = jnp.zeros(buf_ref.shape, buf_ref.dtype)

    _move_chunks(ys_ref, buf_ref, starts_ref, tile * (buf_ref.shape[0] // RUN_ALIGN), nq_ref[tile], sem,
                 to_hbm=False)

    gates = gate_ref[...]
    rows = rcol_ref[...]
    moe = jnp.zeros((tm, d), F32)
    for c in range(buf_ref.shape[0] // PERM_COLS):
        r = lax.broadcasted_iota(I32, (tm, PERM_COLS), 1) + c * PERM_COLS
        g = jnp.where(r == rows[:, 0:1], gates[:, 0:1], 0.0)
        for k in range(1, TOP_K):
            g = g + jnp.where(r == rows[:, k:k + 1], gates[:, k:k + 1], 0.0)
        g_hi, g_lo = _split(g)
        part = buf_ref[c * PERM_COLS:(c + 1) * PERM_COLS, :]
        moe = moe + (_dot(g_hi, part) + _dot(g_lo, part))
    xo = x1 + gt_ref[...] * moe.reshape(gb, tt, d)
    o_ref[...] = _rms(xo, gfin_ref[...])


def _combine_call(starts, nq, x1, mod3, gates_t, rcol, g_final, ys, tok0, gb, tt):
    g, t, d = x1.shape
    tm = gb * tt
    tpg = t // tt
    t0 = tok0 // tm
    tok = lambda i, j, *_: (t0 + i * tpg + j, 0)
    return pl.pallas_call(
        functools.partial(_combine_kernel, tile0=t0, tiles_per_group=tpg),
        grid_spec=pltpu.PrefetchScalarGridSpec(
            num_scalar_prefetch=2,
            grid=(g // gb, tpg),
            in_specs=[pl.BlockSpec((gb, tt, d), lambda i, j, *_: (i, j, 0)),
                      pl.BlockSpec((gb, 1, d), lambda i, j, *_: (i, 0, 5)),
                      pl.BlockSpec((tm, TOP_K), tok),
                      pl.BlockSpec((tm, TOP_K), tok),
                      pl.BlockSpec((1, 1, d), lambda i, j, *_: (0, 0, 0)),
                      pl.BlockSpec(memory_space=pl.ANY)],
            out_specs=pl.BlockSpec((gb, tt, d), lambda i, j, *_: (i, j, 0)),
            scratch_shapes=[pltpu.VMEM((TILE_BUF_ROWS, d), BF16), pltpu.SemaphoreType.DMA(())]),
        out_shape=jax.ShapeDtypeStruct((g, t, d), F32),
        compiler_params=_params(2), name="combine",
    )(starts, nq, x1, mod3, gates_t, rcol, g_final.reshape(1, 1, d), ys)


def _group_blocking(groups, seq, tile):
    if seq >= tile:
        return 1, tile
    return tile // seq, seq


def _gates_by_chunk(gr, groups, seq, L):
    return gr.reshape(N_GATES, groups, seq // L, L).transpose(1, 2, 0, 3)


def kernel(x_prompt, x_sample, c_prompt, c_sample, state_ret, state_mlstm_c, state_mlstm_n, state_mlstm_m, state_conv, w_mod, b_mod, g_mix, g_ffn, w_in, b_igate, b_fgate, w_conv, b_conv, g_ret, g_mlstm, w_out, w_router, b_router, w_up, b_up, w_down, b_down, g_final):
    depth = w_mod.shape[0]
    assert depth == 1, "single-layer trunk"
    bp, tp, d = x_prompt.shape
    bs, ts, _ = x_sample.shape
    n_p, n_s = bp * tp, bs * ts
    hd = HEAD_DIM
    l = 0

    mod = _mod_call(jnp.concatenate([c_prompt, c_sample], axis=0), w_mod[l], b_mod[l])
    mod_p = mod[:bp].reshape(bp, 1, N_MOD * d)
    mod_s = mod[bp:].reshape(bs, 1, N_MOD * d)

    w_main_bf = w_in[l][:, :MAIN_COLS].astype(BF16)
    w_gate = jnp.pad(w_in[l][:, MAIN_COLS:], ((0, 0), (0, 128 - N_GATES)))
    w_gate_t = w_in[l][:, MAIN_COLS:].T
    w_out_bf = w_out[l].astype(BF16)
    bg_col = jnp.concatenate([b_igate[l], b_fgate[l]]).reshape(1, N_GATES)
    bg_row = bg_col.reshape(N_GATES, 1)

    groups = (
        (x_prompt, mod_p, min(CHUNK, tp), np.arange(tp),
         jnp.zeros((bp, RET_HEADS, hd, hd), F32), jnp.zeros((bp, MLSTM_HEADS, hd, hd), F32),
         jnp.zeros((bp, MLSTM_HEADS, hd), F32), jnp.zeros((bp, MLSTM_HEADS), F32),
         jnp.zeros((bp, CONV_WIDTH - 1, 2 * MLSTM_WIDTH), F32)),
        (x_sample, mod_s, min(CHUNK, ts), PAST_LEN + np.arange(ts),
         state_ret[l], state_mlstm_c[l], state_mlstm_n[l], state_mlstm_m[l], state_conv[l]),
    )

    ustrict = jnp.asarray(np.arange(MOE_TILE)[:, None] < np.arange(MOE_TILE)[None, :], BF16)
    staged = []
    for x3, mod3, L, pos, s0, c0, n0, m0, conv0 in groups:
        g, t, _ = x3.shape
        gb, tt = _group_blocking(g, t, TOKEN_TILE)
        p, gc, gr = _inproj_call(x3, mod3, g_mix[l], w_main_bf, w_gate, w_gate_t, gb, tt)
        conv0p = jnp.pad(conv0.astype(F32), ((0, 0), (CONV_TAIL_ROWS - (CONV_WIDTH - 1), 0), (0, 0)))
        y, s_new, c_new, n_new, m_new, tail = _mixer_call(
            p.reshape(g, t, MAIN_COLS), gc.reshape(g, t, N_GATES), _gates_by_chunk(gr, g, t, L), _mixer_consts(L, pos),
            w_conv[l], b_conv[l], g_ret[l], g_mlstm[l],
            bg_col, bg_row, s0.astype(F32), c0.astype(F32), n0.astype(F32).reshape(g, MLSTM_HEADS, 1, hd),
            jnp.broadcast_to(m0.astype(F32)[:, :, None, None], (g, MLSTM_HEADS, 1, hd)), conv0p,
            g, t, L, MIXER_SEQS_CHUNKED if t > L else MIXER_SEQS_SHORT, F32)
        states = (s_new, c_new, n_new.reshape(g, MLSTM_HEADS, hd), m_new[:, :, 0, 0],
                  tail[:, CONV_TAIL_ROWS - (CONV_WIDTH - 1):, :])
        gb, tt = _group_blocking(g, t, MOE_TILE)
        x1, h2, idx, gates, rank, cnt = _outproj_call(
            y.reshape(g * t, d), x3, mod3, g_ffn[l], w_out_bf, w_router[l].T, b_router[l], ustrict, gb, tt)
        staged.append((x1, mod3, h2, idx, gates, rank, states, cnt))

    n_tok = n_p + n_s
    tm, bm, ra = MOE_TILE, EXPERT_BLOCK, RUN_ALIGN
    n_tiles = n_tok // tm
    q_max = TILE_BUF_ROWS // ra
    n_blocks = -(-(n_tok * TOP_K + n_tiles * N_EXPERTS * (ra - 1)) // bm) + N_EXPERTS
    cap = n_blocks * bm
    counts = jnp.concatenate([s[7][:, :, 0] for s in staged], axis=0).astype(I32)
    run = (counts + ra - 1) // ra * ra
    region = jnp.sum(run, axis=0)
    padded = (region + bm - 1) // bm * bm
    pad_end = jnp.cumsum(padded)
    pad_start = pad_end - padded
    run_start = pad_start[None, :] + jnp.cumsum(run, axis=0) - run
    buf_end = jnp.cumsum(run, axis=1)
    buf_start = buf_end - run
    nq = (buf_end[:, -1] // ra).astype(I32)
    chunk_row = jnp.arange(q_max, dtype=I32) * ra
    chunk_e = jnp.minimum(jnp.sum((buf_end[:, None, :] <= chunk_row[None, :, None]).astype(I32), axis=2),
                          N_EXPERTS - 1)
    e_ids = jnp.arange(N_EXPERTS, dtype=I32)
    shift = run_start - buf_start
    starts = jnp.sum(jnp.where(chunk_e[:, :, None] == e_ids, shift[:, None, :], 0), axis=2) + chunk_row[None, :]
    starts = jnp.where(chunk_row[None, :] < buf_end[:, -1:], starts, 0).reshape(-1).astype(I32)

    idx_all = jnp.concatenate([s[3] for s in staged], axis=1)
    rank_all = jnp.concatenate([s[5] for s in staged], axis=1)
    gates_t = jnp.concatenate([s[4] for s in staged], axis=1).T
    buf_start_tok = jnp.repeat(buf_start, tm, axis=0).T
    rbuf = jnp.sum(jnp.where(idx_all[None] == e_ids[:, None, None], buf_start_tok[:, None, :], 0), axis=0) + rank_all
    rcol = rbuf.T
    block_row = jnp.arange(n_blocks, dtype=I32) * bm
    block_e = jnp.minimum(jnp.sum((pad_end[None, :] <= block_row[:, None]).astype(I32), axis=1), N_EXPERTS - 1)
    n_used = (pad_end[-1:] // bm).astype(I32)
    meta = jnp.concatenate([jnp.where(region > 0, pad_end - bm, -1), n_used]).astype(I32)

    xs = _dispatch_call(starts, nq, meta, rbuf, staged[0][2], staged[1][2], cap)
    ys = _expert_call(block_e, n_used, xs, w_up[l], b_up[l], w_down[l], b_down[l])

    outs = []
    tok0 = 0
    for x1, mod3, *_ in staged:
        g, t, _ = x1.shape
        gb, tt = _group_blocking(g, t, tm)
        outs.append(_combine_call(starts, nq, x1, mod3, gates_t, rcol, g_final, ys, tok0, gb, tt))
        tok0 += g * t

    st_p, st_s = staged[0][6], staged[1][6]
    return (outs[0], outs[1]) + tuple(a[None] for a in st_p) + tuple(a[None] for a in st_s)
```

```python
import functools

import numpy as np
import jax
import jax.numpy as jnp
from jax import lax
from jax.experimental import pallas as pl
from jax.experimental.pallas import tpu as pltpu

F32 = jnp.float32
BF16 = jnp.bfloat16
I32 = jnp.int32

D_MODEL = 1024
PAST_LEN = 16384
RET_HEADS = 4
MLSTM_HEADS = 4
HEAD_DIM = 128
RET_WIDTH = RET_HEADS * HEAD_DIM
MLSTM_WIDTH = MLSTM_HEADS * HEAD_DIM
CONV_WIDTH = 4
CHUNK = 128
ROPE_BASE = 10000.0
N_EXPERTS = 32
TOP_K = 4
D_FF = D_MODEL
SWIGLU_LIMIT = 7.0
SWIGLU_ALPHA = 1.702
N_MOD = 6
EPS = 1e-6
MAIN_COLS = 4 * RET_WIDTH + 2 * MLSTM_WIDTH + 2 * MLSTM_WIDTH
N_GATES = 2 * MLSTM_HEADS
OFF_RQ, OFF_RK, OFF_RV, OFF_RG = 0, RET_WIDTH, 2 * RET_WIDTH, 3 * RET_WIDTH
OFF_MQK = 4 * RET_WIDTH
OFF_MV = OFF_MQK + 2 * MLSTM_WIDTH
OFF_MO = OFF_MV + MLSTM_WIDTH

VMEM_LIMIT_BYTES = 56 * 1024 * 1024
TOKEN_TILE = 512
MOE_TILE = 512
EXPERT_BLOCK = 512
MIXER_SEQS_CHUNKED = 2
MIXER_SEQS_SHORT = 8
RUN_ALIGN = 16
TILE_BUF_ROWS = MOE_TILE * TOP_K + N_EXPERTS * RUN_ALIGN
PERM_ROWS = 256
PERM_COLS = 512
CONV_TAIL_ROWS = 8


def _params(n_axes=1):
    return pltpu.CompilerParams(dimension_semantics=("arbitrary",) * n_axes, vmem_limit_bytes=VMEM_LIMIT_BYTES)


def _dot(a, b):
    return jnp.dot(a, b, preferred_element_type=F32)


def _dot_nt(a, b):
    return lax.dot_general(a, b, (((1,), (1,)), ((), ())), preferred_element_type=F32)


def _dot_tn(a, b):
    return lax.dot_general(a, b, (((0,), (0,)), ((), ())), preferred_element_type=F32)


def _split(a):
    hi = a.astype(BF16)
    lo = (a - hi.astype(F32)).astype(BF16)
    return hi, lo


def _dot3(a, b, dot=_dot):
    ah, al = _split(a)
    bh, bl = _split(b)
    return dot(ah, bh) + (dot(al, bh) + dot(ah, bl))


def _sigmoid(x):
    return 0.5 * (jnp.tanh(0.5 * x) + 1.0)


def _log_sigmoid(x):
    return jnp.minimum(x, 0.0) - jnp.log1p(jnp.exp(-jnp.abs(x)))


def _rms(x, g):
    ms = jnp.mean(x * x, axis=-1, keepdims=True)
    return (x * lax.rsqrt(ms + EPS)) * g


def _layer_norm(x, g):
    mu = jnp.mean(x, axis=-1, keepdims=True)
    xc = x - mu
    var = jnp.mean(xc * xc, axis=-1, keepdims=True)
    return xc * lax.rsqrt(var + EPS) * g


def _mod_kernel(c_ref, w_ref, b_ref, o_ref):
    c = c_ref[...]
    o_ref[...] = _dot3(c * _sigmoid(c), w_ref[...]) + b_ref[...]


def _mod_call(c_all, w_mod, b_mod):
    rows, d = c_all.shape
    cols = w_mod.shape[1]
    tn = 1024
    return pl.pallas_call(
        _mod_kernel,
        grid=(cols // tn,),
        in_specs=[pl.BlockSpec((rows, d), lambda j: (0, 0)),
                  pl.BlockSpec((d, tn), lambda j: (0, j)),
                  pl.BlockSpec((1, tn), lambda j: (0, j))],
        out_specs=pl.BlockSpec((rows, tn), lambda j: (0, j)),
        out_shape=jax.ShapeDtypeStruct((rows, cols), F32),
        compiler_params=_params(1), name="mod",
    )(c_all, w_mod, b_mod.reshape(1, cols))


def _inproj_kernel(x_ref, sh_ref, sc_ref, g_ref, w_ref, wg_ref, wgt_ref, p_ref, gc_ref, gr_ref):
    x = x_ref[...]
    gb, tt, d = x.shape
    h = _rms(x, g_ref[...]) * (1.0 + sc_ref[...]) + sh_ref[...]
    h = h.reshape(gb * tt, d)
    hb = h.astype(BF16)
    for j in range(MAIN_COLS // 1024):
        p_ref[:, j * 1024:(j + 1) * 1024] = _dot(hb, w_ref[:, j * 1024:(j + 1) * 1024])
    gc_ref[...] = _dot3(h, wg_ref[...])[:, :N_GATES]
    gr_ref[...] = _dot3(wgt_ref[...], h, dot=_dot_nt)


def _inproj_call(x3, mod3, g_mix, w_main_bf, w_gate, w_gate_t, gb, tt):
    g, t, d = x3.shape
    n = g * t
    tm = gb * tt
    tpg = t // tt
    grid = (g // gb, tpg)
    row = lambda i, j: (i * tpg + j, 0)
    return pl.pallas_call(
        _inproj_kernel,
        grid=grid,
        in_specs=[pl.BlockSpec((gb, tt, d), lambda i, j: (i, j, 0)),
                  pl.BlockSpec((gb, 1, d), lambda i, j: (i, 0, 0)),
                  pl.BlockSpec((gb, 1, d), lambda i, j: (i, 0, 1)),
                  pl.BlockSpec((1, 1, d), lambda i, j: (0, 0, 0)),
                  pl.BlockSpec((d, MAIN_COLS), lambda i, j: (0, 0)),
                  pl.BlockSpec((d, 128), lambda i, j: (0, 0)),
                  pl.BlockSpec((N_GATES, d), lambda i, j: (0, 0))],
        out_specs=[pl.BlockSpec((tm, MAIN_COLS), row),
                   pl.BlockSpec((tm, N_GATES), row),
                   pl.BlockSpec((N_GATES, tm), lambda i, j: (0, i * tpg + j))],
        out_shape=[jax.ShapeDtypeStruct((n, MAIN_COLS), F32),
                   jax.ShapeDtypeStruct((n, N_GATES), F32),
                   jax.ShapeDtypeStruct((N_GATES, n), F32)],
        compiler_params=_params(2), name="inproj",
    )(x3, mod3, mod3, g_mix.reshape(1, 1, d), w_main_bf, w_gate, w_gate_t)


def _mixer_kernel(p_ref, gc_ref, gr_ref, cos_ref, sin_ref, dec_ref, qd_ref, kd_ref, cd_ref,
                  tril_ref, triu_ref, wconv_ref, bconv_ref, gret_ref, gml_ref, bgc_ref, bgr_ref,
                  s0_ref, c0_ref, n0_ref, m0_ref, conv0_ref,
                  y_ref, s_ref, c_ref, n_ref, m_ref, tail_ref, xp_ref):
    @pl.when(pl.program_id(1) == 0)
    def _():
        s_ref[...] = s0_ref[...]
        c_ref[...] = c0_ref[...]
        n_ref[...] = n0_ref[...]
        m_ref[...] = m0_ref[...]
        tail_ref[...] = conv0_ref[...]

    chains = []
    for sq in range(p_ref.shape[0]):
        chains += _mixer_sequence(p_ref.at[sq], gc_ref.at[sq], gr_ref.at[sq], cos_ref, sin_ref, dec_ref, qd_ref, kd_ref,
                                  cd_ref, tril_ref, triu_ref, wconv_ref, bconv_ref, gret_ref, gml_ref, bgc_ref,
                                  bgr_ref, y_ref.at[sq], s_ref.at[sq], c_ref.at[sq], n_ref.at[sq], m_ref.at[sq],
                                  tail_ref.at[sq], xp_ref.at[sq])
    while chains:
        alive = []
        for chain in chains:
            if next(chain, None) is not None:
                alive.append(chain)
        chains = alive


def _mixer_sequence(p_ref, gc_ref, gr_ref, cos_ref, sin_ref, dec_ref, qd_ref, kd_ref, cd_ref,
                    tril_ref, triu_ref, wconv_ref, bconv_ref, gret_ref, gml_ref, bgc_ref, bgr_ref,
                    y_ref, s_ref, c_ref, n_ref, m_ref, tail_ref, xp_ref):
    L = p_ref.shape[0]
    cos = cos_ref[...]
    sin = sin_ref[...]
    scale = HEAD_DIM ** -0.5

    def rot(x):
        return x * cos + pltpu.roll(x, HEAD_DIM // 2, axis=1) * sin

    def retention_head(h):
        lo = h * HEAD_DIM
        q = rot(p_ref[:, OFF_RQ + lo:OFF_RQ + lo + HEAD_DIM])
        k = rot(p_ref[:, OFF_RK + lo:OFF_RK + lo + HEAD_DIM]) * scale
        v = p_ref[:, OFF_RV + lo:OFF_RV + lo + HEAD_DIM].astype(BF16)
        s_old = s_ref[h]
        yield True
        scores = _dot_nt(q.astype(BF16), k.astype(BF16)) * dec_ref[h]
        yield True
        out = _dot(scores.astype(BF16), v) + _dot((q * qd_ref[h]).astype(BF16), s_old.astype(BF16))
        s_ref[h] = cd_ref[h] * s_old + _dot_tn((k * kd_ref[h]).astype(BF16), v)
        yield True
        g = p_ref[:, OFF_RG + lo:OFF_RG + lo + HEAD_DIM]
        y_ref[:, lo:lo + HEAD_DIM] = ((g * _sigmoid(g)) * _layer_norm(out, gret_ref[:, lo:lo + HEAD_DIM])).astype(y_ref.dtype)

    xp_ref[0:CONV_TAIL_ROWS, :] = tail_ref[...]
    xp_ref[CONV_TAIL_ROWS:CONV_TAIL_ROWS + L, :] = p_ref[:, OFF_MQK:OFF_MQK + 2 * MLSTM_WIDTH]
    acc = bconv_ref[...] + wconv_ref[0:1, :] * xp_ref[CONV_TAIL_ROWS - 3:CONV_TAIL_ROWS - 3 + L, :]
    for j in range(1, CONV_WIDTH):
        acc = acc + wconv_ref[j:j + 1, :] * xp_ref[CONV_TAIL_ROWS - 3 + j:CONV_TAIL_ROWS - 3 + j + L, :]
    tail_ref[...] = xp_ref[L:L + CONV_TAIL_ROWS, :]
    xp_ref[CONV_TAIL_ROWS:CONV_TAIL_ROWS + L, :] = acc * _sigmoid(acc)

    gcol = gc_ref[...] + bgc_ref[...]
    is_f_col = lax.broadcasted_iota(I32, gcol.shape, 1) >= MLSTM_HEADS
    gcol = jnp.where(is_f_col, _log_sigmoid(gcol), gcol)
    grow = gr_ref[0] + bgr_ref[...]
    is_f_row = lax.broadcasted_iota(I32, grow.shape, 0) >= MLSTM_HEADS
    grow = jnp.where(is_f_row, _log_sigmoid(grow), grow)
    bcol_all = _dot3(tril_ref[...], gcol)
    brow_all = _dot3(grow, triu_ref[...])
    causal = lax.broadcasted_iota(I32, (L, L), 0) >= lax.broadcasted_iota(I32, (L, L), 1)

    def mlstm_head(h):
        lo = h * HEAD_DIM
        q = xp_ref[CONV_TAIL_ROWS:CONV_TAIL_ROWS + L, lo:lo + HEAD_DIM]
        k = xp_ref[CONV_TAIL_ROWS:CONV_TAIL_ROWS + L, MLSTM_WIDTH + lo:MLSTM_WIDTH + lo + HEAD_DIM] * scale
        v = p_ref[:, OFF_MV + lo:OFF_MV + lo + HEAD_DIM].astype(BF16)
        ic_col = gcol[:, h:h + 1]
        ic_row = grow[h:h + 1, :]
        b_col = bcol_all[:, MLSTM_HEADS + h:MLSTM_HEADS + h + 1]
        b_row = brow_all[MLSTM_HEADS + h:MLSTM_HEADS + h + 1, :]
        c_old = c_ref[h]
        n_old = n_ref[h]
        m_old = m_ref[h][:, 0:1]

        d_log = jnp.where(causal, b_col - b_row + ic_row, -jnp.inf)
        inter = b_col + m_old
        m_t = jnp.maximum(inter, jnp.max(d_log, axis=1, keepdims=True))
        yield True
        w_intra = jnp.exp(d_log - m_t)
        w_inter = jnp.exp(inter - m_t)
        qb = q.astype(BF16)
        s = _dot_nt(qb, k.astype(BF16)) * w_intra
        yield True
        num = _dot(s.astype(BF16), v) + w_inter * _dot(qb, c_old.astype(BF16))
        den = jnp.sum(s, axis=1, keepdims=True) + w_inter * jnp.sum(q * n_old, axis=1, keepdims=True)
        yield True
        hh = num / jnp.maximum(jnp.abs(den), jnp.exp(-m_t))

        b_last = b_col[L - 1:L, :]
        w_log_col = b_last - b_col + ic_col
        m_new = jnp.maximum(b_last + m_old, jnp.max(w_log_col, axis=0, keepdims=True))
        wk = jnp.exp(w_log_col - m_new) * k
        cdec = jnp.exp(b_last + m_old - m_new)
        yield True
        c_ref[h] = cdec * c_old + _dot_tn(wk.astype(BF16), v)
        n_ref[h] = cdec * n_old + jnp.sum(wk, axis=0, keepdims=True)
        m_ref[h] = jnp.broadcast_to(m_new, (1, HEAD_DIM))
        yield True
        o = p_ref[:, OFF_MO + lo:OFF_MO + lo + HEAD_DIM]
        y_ref[:, RET_WIDTH + lo:RET_WIDTH + lo + HEAD_DIM] = (
            _sigmoid(o) * _layer_norm(hh, gml_ref[:, lo:lo + HEAD_DIM])).astype(y_ref.dtype)

    return [retention_head(h) for h in range(RET_HEADS)] + [mlstm_head(h) for h in range(MLSTM_HEADS)]


def _mixer_consts(L, pos):
    f32 = np.float32
    half = HEAD_DIM // 2
    inv_freq = np.power(f32(ROPE_BASE), -np.arange(half, dtype=f32) / f32(half)).astype(f32)
    ang = (pos.astype(f32)[:, None] * inv_freq[None, :]).astype(f32)
    cos = np.concatenate([np.cos(ang), np.cos(ang)], axis=-1).astype(f32)
    sin = np.concatenate([-np.sin(ang), np.sin(ang)], axis=-1).astype(f32)
    log_gamma = np.log1p(-np.exp2(-5.0 - np.arange(RET_HEADS, dtype=np.float64)))
    idx = np.arange(L, dtype=np.float64)
    rel = idx[:, None] - idx[None, :]
    dec = np.where(rel >= 0, np.exp(log_gamma[:, None, None] * np.maximum(rel, 0.0)), 0.0)
    qd = np.broadcast_to(np.exp(log_gamma[:, None] * (idx + 1.0))[..., None], (RET_HEADS, L, HEAD_DIM))
    kd = np.broadcast_to(np.exp(log_gamma[:, None] * (L - 1.0 - idx))[..., None], (RET_HEADS, L, HEAD_DIM))
    cd = np.broadcast_to(np.exp(log_gamma * L)[:, None, None], (RET_HEADS, 1, HEAD_DIM))
    tril = rel >= 0
    triu = rel <= 0
    return tuple(jnp.asarray(a, F32) for a in (cos, sin, dec, qd, kd, cd, tril, triu))


def _mixer_call(p, gc, gr4, consts, w_conv, b_conv, g_ret, g_mlstm, bg_col, bg_row,
                s0, c0, n0, m0, conv0, groups, seq, L, gs, y_dtype):
    cos, sin, dec, qd, kd, cd, tril, triu = consts
    nc = seq // L
    hd = HEAD_DIM
    full = lambda *shape: pl.BlockSpec(shape, lambda g, c: (0,) * len(shape))
    state4 = pl.BlockSpec((gs, RET_HEADS, hd, hd), lambda g, c: (g, 0, 0, 0))
    vec4 = pl.BlockSpec((gs, MLSTM_HEADS, 1, hd), lambda g, c: (g, 0, 0, 0))
    tail3 = pl.BlockSpec((gs, CONV_TAIL_ROWS, 2 * MLSTM_WIDTH), lambda g, c: (g, 0, 0))
    row = lambda g, c: (g, c, 0)
    return pl.pallas_call(
        _mixer_kernel,
        grid=(groups // gs, nc),
        in_specs=[pl.BlockSpec((gs, L, MAIN_COLS), row),
                  pl.BlockSpec((gs, L, N_GATES), row),
                  pl.BlockSpec((gs, 1, N_GATES, L), lambda g, c: (g, c, 0, 0)),
                  pl.BlockSpec((L, hd), lambda g, c: (c, 0)),
                  pl.BlockSpec((L, hd), lambda g, c: (c, 0)),
                  full(RET_HEADS, L, L), full(RET_HEADS, L, hd), full(RET_HEADS, L, hd), full(RET_HEADS, 1, hd),
                  full(L, L), full(L, L),
                  full(CONV_WIDTH, 2 * MLSTM_WIDTH), full(1, 2 * MLSTM_WIDTH),
                  full(1, RET_WIDTH), full(1, MLSTM_WIDTH), full(1, N_GATES), full(N_GATES, 1),
                  state4, state4, vec4, vec4, tail3],
        out_specs=[pl.BlockSpec((gs, L, RET_WIDTH + MLSTM_WIDTH), row),
                   state4, state4, vec4, vec4, tail3],
        out_shape=[jax.ShapeDtypeStruct((groups, seq, RET_WIDTH + MLSTM_WIDTH), y_dtype),
                   jax.ShapeDtypeStruct((groups, RET_HEADS, hd, hd), F32),
                   jax.ShapeDtypeStruct((groups, MLSTM_HEADS, hd, hd), F32),
                   jax.ShapeDtypeStruct((groups, MLSTM_HEADS, 1, hd), F32),
                   jax.ShapeDtypeStruct((groups, MLSTM_HEADS, 1, hd), F32),
                   jax.ShapeDtypeStruct((groups, CONV_TAIL_ROWS, 2 * MLSTM_WIDTH), F32)],
        scratch_shapes=[pltpu.VMEM((gs, CONV_TAIL_ROWS + L, 2 * MLSTM_WIDTH), F32)],
        compiler_params=_params(2), name="mixer",
    )(p, gc, gr4, cos, sin, dec, qd, kd, cd, tril, triu, w_conv, b_conv.reshape(1, -1),
      g_ret.reshape(1, -1), g_mlstm.reshape(1, -1), bg_col, bg_row, s0, c0, n0, m0, conv0)


def _outproj_kernel(y_ref, x_ref, gt_ref, sh_ref, sc_ref, g_ref, w_ref, wrt_ref, br_ref, ustrict_ref,
                    x1_ref, h2_ref, idx_ref, gate_ref, rank_ref, cnt_ref):
    x = x_ref[...]
    gb, tt, d = x.shape
    tm = gb * tt
    mixed = _dot(y_ref[...].astype(BF16), w_ref[...])
    x1 = x + gt_ref[...] * mixed.reshape(gb, tt, d)
    x1_ref[...] = x1
    h2 = (_rms(x1, g_ref[...]) * (1.0 + sc_ref[...]) + sh_ref[...]).reshape(tm, d)
    h2_ref[...] = h2.astype(BF16)

    work = _dot3(wrt_ref[...], h2, dot=_dot_nt) + br_ref[...]
    e_iota = lax.broadcasted_iota(I32, work.shape, 0).astype(F32)
    vals, idxs, sels = [], [], []
    for _ in range(TOP_K):
        mx = jnp.max(work, axis=0, keepdims=True)
        ik = jnp.min(jnp.where(work == mx, e_iota, float(N_EXPERTS)), axis=0, keepdims=True)
        sel = e_iota == ik
        vals.append(mx)
        idxs.append(ik)
        sels.append(sel)
        work = jnp.where(sel, -jnp.inf, work)
    exps = [jnp.exp(v - vals[0]) for v in vals]
    denom = exps[0] + exps[1] + exps[2] + exps[3]
    gate_ref[...] = jnp.concatenate([e / denom for e in exps], axis=0)
    idx_ref[...] = jnp.concatenate(idxs, axis=0).astype(I32)

    mask = (sels[0] | sels[1] | sels[2] | sels[3]).astype(F32)
    before = _dot(mask.astype(BF16), ustrict_ref[...])
    ranks = [jnp.sum(jnp.where(sel, before, 0.0), axis=0, keepdims=True) for sel in sels]
    rank_ref[...] = jnp.concatenate(ranks, axis=0).astype(I32)
    cnt_ref[0] = jnp.broadcast_to(jnp.sum(mask, axis=1, keepdims=True), cnt_ref.shape[1:])


def _outproj_call(y, x3, mod3, g_ffn, w_out_bf, w_router_t, b_router, ustrict, gb, tt):
    g, t, d = x3.shape
    n = g * t
    tm = gb * tt
    tpg = t // tt
    row = lambda i, j: (i * tpg + j, 0)
    col = lambda i, j: (0, i * tpg + j)
    mod_spec = lambda k: pl.BlockSpec((gb, 1, d), lambda i, j: (i, 0, k))
    const = lambda *shape: pl.BlockSpec(shape, lambda i, j: (0,) * len(shape))
    return pl.pallas_call(
        _outproj_kernel,
        grid=(g // gb, tpg),
        in_specs=[pl.BlockSpec((tm, d), row),
                  pl.BlockSpec((gb, tt, d), lambda i, j: (i, j, 0)),
                  mod_spec(2), mod_spec(3), mod_spec(4),
                  const(1, 1, d), const(d, d), const(N_EXPERTS, d), const(N_EXPERTS, 1),
                  const(tm, tm)],
        out_specs=[pl.BlockSpec((gb, tt, d), lambda i, j: (i, j, 0)),
                   pl.BlockSpec((tm, d), row),
                   pl.BlockSpec((TOP_K, tm), col),
                   pl.BlockSpec((TOP_K, tm), col),
                   pl.BlockSpec((TOP_K, tm), col),
                   pl.BlockSpec((1, N_EXPERTS, 128), lambda i, j: (i * tpg + j, 0, 0))],
        out_shape=[jax.ShapeDtypeStruct((g, t, d), F32),
                   jax.ShapeDtypeStruct((n, d), BF16),
                   jax.ShapeDtypeStruct((TOP_K, n), I32),
                   jax.ShapeDtypeStruct((TOP_K, n), F32),
                   jax.ShapeDtypeStruct((TOP_K, n), I32),
                   jax.ShapeDtypeStruct((n // tm, N_EXPERTS, 128), F32)],
        compiler_params=_params(2), name="outproj_router",
    )(y, x3, mod3, mod3, mod3, g_ffn.reshape(1, 1, d), w_out_bf, w_router_t, b_router.reshape(N_EXPERTS, 1),
      ustrict)


def _chunk_copy(hbm_ref, hbm_row, buf_ref, chunk, sem, to_hbm):
    hbm = hbm_ref.at[pl.ds(pl.multiple_of(hbm_row, RUN_ALIGN), RUN_ALIGN)]
    buf = buf_ref.at[pl.ds(pl.multiple_of(chunk * RUN_ALIGN, RUN_ALIGN), RUN_ALIGN)]
    return pltpu.make_async_copy(buf, hbm, sem) if to_hbm else pltpu.make_async_copy(hbm, buf, sem)


def _tile_chunks(hbm_ref, bufs_ref, starts_ref, nq_ref, sems, tile, to_hbm, wait):
    slot = tile % 2
    first = tile * (bufs_ref.shape[1] // RUN_ALIGN)

    def body(q, c):
        cp = _chunk_copy(hbm_ref, starts_ref[first + q], bufs_ref.at[slot], q, sems.at[slot], to_hbm)
        cp.wait() if wait else cp.start()
        return c

    lax.fori_loop(0, nq_ref[tile], body, 0)


def _dispatch_kernel(starts_ref, nq_ref, meta_ref, rbuf_ref, h2p_ref, h2s_ref, xs_ref, bufs_ref, zero_ref, sems,
                     *, prompt_tiles):
    i = pl.program_id(0)
    tm = h2p_ref.shape[0]
    bm = zero_ref.shape[0]
    n_blocks = xs_ref.shape[0] // bm
    sem = sems.at[0]
    buf_ref = bufs_ref.at[i % 2]

    @pl.when(i == 0)
    def _():
        zero_ref[...] = jnp.zeros(zero_ref.shape, zero_ref.dtype)

        def zero_copy(row):
            return pltpu.make_async_copy(zero_ref, xs_ref.at[pl.ds(pl.multiple_of(row, bm), bm)], sem)

        def tails(fn):
            def body(e, c):
                @pl.when(meta_ref[e] >= 0)
                def _():
                    fn(zero_copy(meta_ref[e]))
                return c
            lax.fori_loop(0, N_EXPERTS, body, 0)

        def unused(fn):
            def body(b, c):
                fn(zero_copy(b * bm))
                return c
            lax.fori_loop(meta_ref[N_EXPERTS], n_blocks, body, 0)

        tails(lambda cp: cp.start())
        unused(lambda cp: cp.start())
        tails(lambda cp: cp.wait())
        unused(lambda cp: cp.wait())

    def build(h2_ref):
        h2 = h2_ref[...]
        rb = rbuf_ref[...]
        for c in range(bufs_ref.shape[1] // PERM_ROWS):
            r = lax.broadcasted_iota(I32, (PERM_ROWS, tm), 0) + c * PERM_ROWS
            hit = (r == rb[0:1, :]) | (r == rb[1:2, :]) | (r == rb[2:3, :]) | (r == rb[3:4, :])
            onehot = jnp.where(hit, 1.0, 0.0).astype(BF16)
            buf_ref[c * PERM_ROWS:(c + 1) * PERM_ROWS, :] = _dot(onehot, h2).astype(BF16)

    @pl.when(i < prompt_tiles)
    def _():
        build(h2p_ref)

    @pl.when(i >= prompt_tiles)
    def _():
        build(h2s_ref)

    chunks = functools.partial(_tile_chunks, xs_ref, bufs_ref, starts_ref, nq_ref, sems, to_hbm=True)
    chunks(i, wait=False)

    @pl.when(i > 0)
    def _():
        chunks(i - 1, wait=True)

    @pl.when(i == pl.num_programs(0) - 1)
    def _():
        chunks(i, wait=True)


def _dispatch_call(starts, nq, meta, rbuf, h2_p, h2_s, cap):
    d = h2_p.shape[1]
    tm = MOE_TILE
    pt, st = h2_p.shape[0] // tm, h2_s.shape[0] // tm
    return pl.pallas_call(
        functools.partial(_dispatch_kernel, prompt_tiles=pt),
        grid_spec=pltpu.PrefetchScalarGridSpec(
            num_scalar_prefetch=3,
            grid=(pt + st,),
            in_specs=[pl.BlockSpec((TOP_K, tm), lambda i, *_: (0, i)),
                      pl.BlockSpec((tm, d), lambda i, *_: (jnp.minimum(i, pt - 1), 0)),
                      pl.BlockSpec((tm, d), lambda i, *_: (jnp.maximum(i - pt, 0), 0))],
            out_specs=pl.BlockSpec(memory_space=pl.ANY),
            scratch_shapes=[pltpu.VMEM((2, TILE_BUF_ROWS, d), BF16), pltpu.VMEM((EXPERT_BLOCK, d), BF16),
                            pltpu.SemaphoreType.DMA((2,))]),
        out_shape=jax.ShapeDtypeStruct((cap, d), BF16),
        compiler_params=_params(1), name="dispatch",
    )(starts, nq, meta, rbuf, h2_p, h2_s)


def _expert_kernel(be_ref, nu_ref, xs_ref, wup_ref, bup_ref, wdn_ref, bdn_ref, ys_ref, wup_bf, wdn_bf):
    i = pl.program_id(0)
    prev = be_ref[jnp.maximum(i - 1, 0)]

    @pl.when((i == 0) | (be_ref[i] != prev))
    def _():
        def cast(r, c):
            rows = pl.ds(pl.multiple_of(r * 64, 64), 64)
            wup_bf[rows, :] = wup_ref[0, rows, :].astype(BF16)
            wdn_bf[rows, :] = wdn_ref[0, rows, :].astype(BF16)
            return c

        lax.fori_loop(0, D_MODEL // 64, cast, 0)

    @pl.when(i < nu_ref[0])
    def _():
        hu = _dot(xs_ref[...], wup_bf[...]) + bup_ref[0]
        gate = jnp.minimum(hu[:, :D_FF], SWIGLU_LIMIT)
        lin = jnp.clip(hu[:, D_FF:], -SWIGLU_LIMIT, SWIGLU_LIMIT)
        glu = gate * _sigmoid(SWIGLU_ALPHA * gate)
        ys_ref[...] = (_dot(((lin + 1.0) * glu).astype(BF16), wdn_bf[...]) + bdn_ref[0]).astype(ys_ref.dtype)

    @pl.when(i >= nu_ref[0])
    def _():
        ys_ref[...] = jnp.zeros(ys_ref.shape, ys_ref.dtype)


def _expert_call(block_e, n_used, xs, w_up, b_up, w_down, b_down):
    cap, d = xs.shape
    bm = EXPERT_BLOCK
    blk = lambda i, be, nu: (jnp.minimum(i, nu[0] - 1), 0)
    per_e = lambda i, be, nu: (be[i], 0, 0)
    return pl.pallas_call(
        _expert_kernel,
        grid_spec=pltpu.PrefetchScalarGridSpec(
            num_scalar_prefetch=2,
            grid=(cap // bm,),
            in_specs=[pl.BlockSpec((bm, d), blk),
                      pl.BlockSpec((1, d, 2 * D_FF), per_e),
                      pl.BlockSpec((1, 1, 2 * D_FF), per_e),
                      pl.BlockSpec((1, D_FF, d), per_e),
                      pl.BlockSpec((1, 1, d), per_e)],
            out_specs=pl.BlockSpec((bm, d), lambda i, be, nu: (i, 0)),
            scratch_shapes=[pltpu.VMEM((d, 2 * D_FF), BF16), pltpu.VMEM((D_FF, d), BF16)]),
        out_shape=jax.ShapeDtypeStruct((cap, d), BF16),
        compiler_params=_params(1), name="experts",
    )(block_e, n_used, xs, w_up, b_up.reshape(N_EXPERTS, 1, -1), w_down, b_down.reshape(N_EXPERTS, 1, -1))


def _combine_kernel(starts_ref, nq_ref, x1_ref, gt_ref, gate_ref, rcol_ref, gfin_ref, ys_ref, o_ref, bufs_ref, sems,
                    *, tile0, tiles_per_group):
    x1 = x1_ref[...]
    gb, tt, d = x1.shape
    tm = gb * tt
    step = pl.program_id(0) * tiles_per_group + pl.program_id(1)
    n_steps = pl.num_programs(0) * tiles_per_group
    tile = tile0 + step
    chunks = functools.partial(_tile_chunks, ys_ref, bufs_ref, starts_ref, nq_ref, sems, to_hbm=False)

    @pl.when(step == 0)
    def _():
        bufs_ref[...] = jnp.zeros(bufs_ref.shape, bufs_ref.dtype)
        chunks(tile, wait=False)

    @pl.when(step + 1 < n_steps)
    def _():
        chunks(tile + 1, wait=False)

    chunks(tile, wait=True)
    buf_ref = bufs_ref.at[tile % 2]

    gates = gate_ref[...]
    rows = rcol_ref[...]
    moe = jnp.zeros((tm, d), F32)
    for c in range(bufs_ref.shape[1] // PERM_COLS):
        r = lax.broadcasted_iota(I32, (tm, PERM_COLS), 1) + c * PERM_COLS
        g = jnp.where(r == rows[:, 0:1], gates[:, 0:1], 0.0)
        for k in range(1, TOP_K):
            g = g + jnp.where(r == rows[:, k:k + 1], gates[:, k:k + 1], 0.0)
        g_hi, g_lo = _split(g)
        part = buf_ref[c * PERM_COLS:(c + 1) * PERM_COLS, :]
        moe = moe + (_dot(g_hi, part) + _dot(g_lo, part))
    xo = x1 + gt_ref[...] * moe.reshape(gb, tt, d)
    o_ref[...] = _rms(xo, gfin_ref[...])


def _combine_call(starts, nq, x1, mod3, gates_t, rcol, g_final, ys, tok0, gb, tt):
    g, t, d = x1.shape
    tm = gb * tt
    tpg = t // tt
    t0 = tok0 // tm
    tok = lambda i, j, *_: (t0 + i * tpg + j, 0)
    return pl.pallas_call(
        functools.partial(_combine_kernel, tile0=t0, tiles_per_group=tpg),
        grid_spec=pltpu.PrefetchScalarGridSpec(
            num_scalar_prefetch=2,
            grid=(g // gb, tpg),
            in_specs=[pl.BlockSpec((gb, tt, d), lambda i, j, *_: (i, j, 0)),
                      pl.BlockSpec((gb, 1, d), lambda i, j, *_: (i, 0, 5)),
                      pl.BlockSpec((tm, TOP_K), tok),
                      pl.BlockSpec((tm, TOP_K), tok),
                      pl.BlockSpec((1, 1, d), lambda i, j, *_: (0, 0, 0)),
                      pl.BlockSpec(memory_space=pl.ANY)],
            out_specs=pl.BlockSpec((gb, tt, d), lambda i, j, *_: (i, j, 0)),
            scratch_shapes=[pltpu.VMEM((2, TILE_BUF_ROWS, d), BF16), pltpu.SemaphoreType.DMA((2,))]),
        out_shape=jax.ShapeDtypeStruct((g, t, d), F32),
        compiler_params=_params(2), name="combine",
    )(starts, nq, x1, mod3, gates_t, rcol, g_final.reshape(1, 1, d), ys)


def _group_blocking(groups, seq, tile):
    if seq >= tile:
        return 1, tile
    return tile // seq, seq


def _gates_by_chunk(gr, groups, seq, L):
    return gr.reshape(N_GATES, groups, seq // L, L).transpose(1, 2, 0, 3)


def kernel(x_prompt, x_sample, c_prompt, c_sample, state_ret, state_mlstm_c, state_mlstm_n, state_mlstm_m, state_conv, w_mod, b_mod, g_mix, g_ffn, w_in, b_igate, b_fgate, w_conv, b_conv, g_ret, g_mlstm, w_out, w_router, b_router, w_up, b_up, w_down, b_down, g_final):
    depth = w_mod.shape[0]
    assert depth == 1, "single-layer trunk"
    bp, tp, d = x_prompt.shape
    bs, ts, _ = x_sample.shape
    n_p, n_s = bp * tp, bs * ts
    hd = HEAD_DIM
    l = 0

    mod = _mod_call(jnp.concatenate([c_prompt, c_sample], axis=0), w_mod[l], b_mod[l])
    mod_p = mod[:bp].reshape(bp, 1, N_MOD * d)
    mod_s = mod[bp:].reshape(bs, 1, N_MOD * d)

    w_main_bf = w_in[l][:, :MAIN_COLS].astype(BF16)
    w_gate = jnp.pad(w_in[l][:, MAIN_COLS:], ((0, 0), (0, 128 - N_GATES)))
    w_gate_t = w_in[l][:, MAIN_COLS:].T
    w_out_bf = w_out[l].astype(BF16)
    bg_col = jnp.concatenate([b_igate[l], b_fgate[l]]).reshape(1, N_GATES)
    bg_row = bg_col.reshape(N_GATES, 1)

    groups = (
        (x_prompt, mod_p, min(CHUNK, tp), np.arange(tp),
         jnp.zeros((bp, RET_HEADS, hd, hd), F32), jnp.zeros((bp, MLSTM_HEADS, hd, hd), F32),
         jnp.zeros((bp, MLSTM_HEADS, hd), F32), jnp.zeros((bp, MLSTM_HEADS), F32),
         jnp.zeros((bp, CONV_WIDTH - 1, 2 * MLSTM_WIDTH), F32)),
        (x_sample, mod_s, min(CHUNK, ts), PAST_LEN + np.arange(ts),
         state_ret[l], state_mlstm_c[l], state_mlstm_n[l], state_mlstm_m[l], state_conv[l]),
    )

    ustrict = jnp.asarray(np.arange(MOE_TILE)[:, None] < np.arange(MOE_TILE)[None, :], BF16)
    staged = []
    for x3, mod3, L, pos, s0, c0, n0, m0, conv0 in groups:
        g, t, _ = x3.shape
        gb, tt = _group_blocking(g, t, TOKEN_TILE)
        p, gc, gr = _inproj_call(x3, mod3, g_mix[l], w_main_bf, w_gate, w_gate_t, gb, tt)
        conv0p = jnp.pad(conv0.astype(F32), ((0, 0), (CONV_TAIL_ROWS - (CONV_WIDTH - 1), 0), (0, 0)))
        y, s_new, c_new, n_new, m_new, tail = _mixer_call(
            p.reshape(g, t, MAIN_COLS), gc.reshape(g, t, N_GATES), _gates_by_chunk(gr, g, t, L), _mixer_consts(L, pos),
            w_conv[l], b_conv[l], g_ret[l], g_mlstm[l],
            bg_col, bg_row, s0.astype(F32), c0.astype(F32), n0.astype(F32).reshape(g, MLSTM_HEADS, 1, hd),
            jnp.broadcast_to(m0.astype(F32)[:, :, None, None], (g, MLSTM_HEADS, 1, hd)), conv0p,
            g, t, L, MIXER_SEQS_CHUNKED if t > L else MIXER_SEQS_SHORT, F32)
        states = (s_new, c_new, n_new.reshape(g, MLSTM_HEADS, hd), m_new[:, :, 0, 0],
                  tail[:, CONV_TAIL_ROWS - (CONV_WIDTH - 1):, :])
        gb, tt = _group_blocking(g, t, MOE_TILE)
        x1, h2, idx, gates, rank, cnt = _outproj_call(
            y.reshape(g * t, d), x3, mod3, g_ffn[l], w_out_bf, w_router[l].T, b_router[l], ustrict, gb, tt)
        staged.append((x1, mod3, h2, idx, gates, rank, states, cnt))

    n_tok = n_p + n_s
    tm, bm, ra = MOE_TILE, EXPERT_BLOCK, RUN_ALIGN
    n_tiles = n_tok // tm
    q_max = TILE_BUF_ROWS // ra
    n_blocks = -(-(n_tok * TOP_K + n_tiles * N_EXPERTS * (ra - 1)) // bm) + N_EXPERTS
    cap = n_blocks * bm
    counts = jnp.concatenate([s[7][:, :, 0] for s in staged], axis=0).astype(I32)
    run = (counts + ra - 1) // ra * ra
    region = jnp.sum(run, axis=0)
    padded = (region + bm - 1) // bm * bm
    pad_end = jnp.cumsum(padded)
    pad_start = pad_end - padded
    run_start = pad_start[None, :] + jnp.cumsum(run, axis=0) - run
    buf_end = jnp.cumsum(run, axis=1)
    buf_start = buf_end - run
    nq = (buf_end[:, -1] // ra).astype(I32)
    chunk_row = jnp.arange(q_max, dtype=I32) * ra
    chunk_e = jnp.minimum(jnp.sum((buf_end[:, None, :] <= chunk_row[None, :, None]).astype(I32), axis=2),
                          N_EXPERTS - 1)
    e_ids = jnp.arange(N_EXPERTS, dtype=I32)
    shift = run_start - buf_start
    starts = jnp.sum(jnp.where(chunk_e[:, :, None] == e_ids, shift[:, None, :], 0), axis=2) + chunk_row[None, :]
    starts = jnp.where(chunk_row[None, :] < buf_end[:, -1:], starts, 0).reshape(-1).astype(I32)

    idx_all = jnp.concatenate([s[3] for s in staged], axis=1)
    rank_all = jnp.concatenate([s[5] for s in staged], axis=1)
    gates_t = jnp.concatenate([s[4] for s in staged], axis=1).T
    buf_start_tok = jnp.repeat(buf_start, tm, axis=0).T
    rbuf = jnp.sum(jnp.where(idx_all[None] == e_ids[:, None, None], buf_start_tok[:, None, :], 0), axis=0) + rank_all
    rcol = rbuf.T
    block_row = jnp.arange(n_blocks, dtype=I32) * bm
    block_e = jnp.minimum(jnp.sum((pad_end[None, :] <= block_row[:, None]).astype(I32), axis=1), N_EXPERTS - 1)
    n_used = (pad_end[-1:] // bm).astype(I32)
    meta = jnp.concatenate([jnp.where(region > 0, pad_end - bm, -1), n_used]).astype(I32)

    xs = _dispatch_call(starts, nq, meta, rbuf, staged[0][2], staged[1][2], cap)
    ys = _expert_call(block_e, n_used, xs, w_up[l], b_up[l], w_down[l], b_down[l])

    outs = []
    tok0 = 0
    for x1, mod3, *_ in staged:
        g, t, _ = x1.shape
        gb, tt = _group_blocking(g, t, tm)
        outs.append(_combine_call(starts, nq, x1, mod3, gates_t, rcol, g_final, ys, tok0, gb, tt))
        tok0 += g * t

    st_p, st_s = staged[0][6], staged[1][6]
    return (outs[0], outs[1]) + tuple(a[None] for a in st_p) + tuple(a[None] for a in st_s)
```

```python
import functools

import numpy as np
import jax
import jax.numpy as jnp
from jax import lax
from jax.experimental import pallas as pl
from jax.experimental.pallas import tpu as pltpu

F32 = jnp.float32
BF16 = jnp.bfloat16
I32 = jnp.int32

D_MODEL = 1024
PAST_LEN = 16384
RET_HEADS = 4
MLSTM_HEADS = 4
HEAD_DIM = 128
RET_WIDTH = RET_HEADS * HEAD_DIM
MLSTM_WIDTH = MLSTM_HEADS * HEAD_DIM
CONV_WIDTH = 4
CHUNK = 128
ROPE_BASE = 10000.0
N_EXPERTS = 32
TOP_K = 4
D_FF = D_MODEL
SWIGLU_LIMIT = 7.0
SWIGLU_ALPHA = 1.702
N_MOD = 6
EPS = 1e-6
MAIN_COLS = 4 * RET_WIDTH + 2 * MLSTM_WIDTH + 2 * MLSTM_WIDTH
N_GATES = 2 * MLSTM_HEADS
OFF_RQ, OFF_RK, OFF_RV, OFF_RG = 0, RET_WIDTH, 2 * RET_WIDTH, 3 * RET_WIDTH
OFF_MQK = 4 * RET_WIDTH
OFF_MV = OFF_MQK + 2 * MLSTM_WIDTH
OFF_MO = OFF_MV + MLSTM_WIDTH

VMEM_LIMIT_BYTES = 56 * 1024 * 1024
TOKEN_TILE = 512
MOE_TILE = 512
EXPERT_BLOCK = 512
MIXER_SEQS_CHUNKED = 2
MIXER_SEQS_SHORT = 8
RUN_ALIGN = 16
TILE_BUF_ROWS = MOE_TILE * TOP_K + N_EXPERTS * RUN_ALIGN
PERM_ROWS = 1280
PERM_COLS = 512
CONV_TAIL_ROWS = 8


def _params(n_axes=1):
    return pltpu.CompilerParams(dimension_semantics=("arbitrary",) * n_axes, vmem_limit_bytes=VMEM_LIMIT_BYTES)


def _dot(a, b):
    return jnp.dot(a, b, preferred_element_type=F32)


def _dot_nt(a, b):
    return lax.dot_general(a, b, (((1,), (1,)), ((), ())), preferred_element_type=F32)


def _dot_tn(a, b):
    return lax.dot_general(a, b, (((0,), (0,)), ((), ())), preferred_element_type=F32)


def _split(a):
    hi = a.astype(BF16)
    lo = (a - hi.astype(F32)).astype(BF16)
    return hi, lo


def _dot3(a, b, dot=_dot):
    ah, al = _split(a)
    bh, bl = _split(b)
    return dot(ah, bh) + (dot(al, bh) + dot(ah, bl))


def _sigmoid(x):
    return 0.5 * (jnp.tanh(0.5 * x) + 1.0)


def _log_sigmoid(x):
    return jnp.minimum(x, 0.0) - jnp.log1p(jnp.exp(-jnp.abs(x)))


def _rms(x, g):
    ms = jnp.mean(x * x, axis=-1, keepdims=True)
    return (x * lax.rsqrt(ms + EPS)) * g


def _layer_norm(x, g):
    mu = jnp.mean(x, axis=-1, keepdims=True)
    xc = x - mu
    var = jnp.mean(xc * xc, axis=-1, keepdims=True)
    return xc * lax.rsqrt(var + EPS) * g


def _mod_kernel(c_ref, w_ref, b_ref, o_ref):
    c = c_ref[...]
    o_ref[...] = _dot3(c * _sigmoid(c), w_ref[...]) + b_ref[...]


def _mod_call(c_all, w_mod, b_mod):
    rows, d = c_all.shape
    cols = w_mod.shape[1]
    tn = 1024
    return pl.pallas_call(
        _mod_kernel,
        grid=(cols // tn,),
        in_specs=[pl.BlockSpec((rows, d), lambda j: (0, 0)),
                  pl.BlockSpec((d, tn), lambda j: (0, j)),
                  pl.BlockSpec((1, tn), lambda j: (0, j))],
        out_specs=pl.BlockSpec((rows, tn), lambda j: (0, j)),
        out_shape=jax.ShapeDtypeStruct((rows, cols), F32),
        compiler_params=_params(1), name="mod",
    )(c_all, w_mod, b_mod.reshape(1, cols))


def _inproj_kernel(x_ref, sh_ref, sc_ref, g_ref, w_ref, wg_ref, wgt_ref, p_ref, gc_ref, gr_ref):
    x = x_ref[...]
    gb, tt, d = x.shape
    h = _rms(x, g_ref[...]) * (1.0 + sc_ref[...]) + sh_ref[...]
    h = h.reshape(gb * tt, d)
    hb = h.astype(BF16)
    for j in range(MAIN_COLS // 1024):
        p_ref[:, j * 1024:(j + 1) * 1024] = _dot(hb, w_ref[:, j * 1024:(j + 1) * 1024])
    gc_ref[...] = _dot3(h, wg_ref[...])[:, :N_GATES]
    gr_ref[...] = _dot3(wgt_ref[...], h, dot=_dot_nt)


def _inproj_call(x3, mod3, g_mix, w_main_bf, w_gate, w_gate_t, gb, tt):
    g, t, d = x3.shape
    n = g * t
    tm = gb * tt
    tpg = t // tt
    grid = (g // gb, tpg)
    row = lambda i, j: (i * tpg + j, 0)
    return pl.pallas_call(
        _inproj_kernel,
        grid=grid,
        in_specs=[pl.BlockSpec((gb, tt, d), lambda i, j: (i, j, 0)),
                  pl.BlockSpec((gb, 1, d), lambda i, j: (i, 0, 0)),
                  pl.BlockSpec((gb, 1, d), lambda i, j: (i, 0, 1)),
                  pl.BlockSpec((1, 1, d), lambda i, j: (0, 0, 0)),
                  pl.BlockSpec((d, MAIN_COLS), lambda i, j: (0, 0)),
                  pl.BlockSpec((d, 128), lambda i, j: (0, 0)),
                  pl.BlockSpec((N_GATES, d), lambda i, j: (0, 0))],
        out_specs=[pl.BlockSpec((tm, MAIN_COLS), row),
                   pl.BlockSpec((tm, N_GATES), row),
                   pl.BlockSpec((N_GATES, tm), lambda i, j: (0, i * tpg + j))],
        out_shape=[jax.ShapeDtypeStruct((n, MAIN_COLS), F32),
                   jax.ShapeDtypeStruct((n, N_GATES), F32),
                   jax.ShapeDtypeStruct((N_GATES, n), F32)],
        compiler_params=_params(2), name="inproj",
    )(x3, mod3, mod3, g_mix.reshape(1, 1, d), w_main_bf, w_gate, w_gate_t)


def _mixer_kernel(p_ref, gc_ref, gr_ref, cos_ref, sin_ref, dec_ref, qd_ref, kd_ref, cd_ref,
                  tril_ref, triu_ref, wconv_ref, bconv_ref, gret_ref, gml_ref, bgc_ref, bgr_ref,
                  s0_ref, c0_ref, n0_ref, m0_ref, conv0_ref,
                  y_ref, s_ref, c_ref, n_ref, m_ref, tail_ref, xp_ref):
    @pl.when(pl.program_id(1) == 0)
    def _():
        s_ref[...] = s0_ref[...]
        c_ref[...] = c0_ref[...]
        n_ref[...] = n0_ref[...]
        m_ref[...] = m0_ref[...]
        tail_ref[...] = conv0_ref[...]

    chains = []
    for sq in range(p_ref.shape[0]):
        chains += _mixer_sequence(p_ref.at[sq], gc_ref.at[sq], gr_ref.at[sq], cos_ref, sin_ref, dec_ref, qd_ref, kd_ref,
                                  cd_ref, tril_ref, triu_ref, wconv_ref, bconv_ref, gret_ref, gml_ref, bgc_ref,
                                  bgr_ref, y_ref.at[sq], s_ref.at[sq], c_ref.at[sq], n_ref.at[sq], m_ref.at[sq],
                                  tail_ref.at[sq], xp_ref.at[sq])
    while chains:
        alive = []
        for chain in chains:
            if next(chain, None) is not None:
                alive.append(chain)
        chains = alive


def _mixer_sequence(p_ref, gc_ref, gr_ref, cos_ref, sin_ref, dec_ref, qd_ref, kd_ref, cd_ref,
                    tril_ref, triu_ref, wconv_ref, bconv_ref, gret_ref, gml_ref, bgc_ref, bgr_ref,
                    y_ref, s_ref, c_ref, n_ref, m_ref, tail_ref, xp_ref):
    L = p_ref.shape[0]
    cos = cos_ref[...]
    sin = sin_ref[...]
    scale = HEAD_DIM ** -0.5

    def rot(x):
        return x * cos + pltpu.roll(x, HEAD_DIM // 2, axis=1) * sin

    def retention_head(h):
        lo = h * HEAD_DIM
        q = rot(p_ref[:, OFF_RQ + lo:OFF_RQ + lo + HEAD_DIM])
        k = rot(p_ref[:, OFF_RK + lo:OFF_RK + lo + HEAD_DIM]) * scale
        v = p_ref[:, OFF_RV + lo:OFF_RV + lo + HEAD_DIM].astype(BF16)
        s_old = s_ref[h]
        yield True
        scores = _dot_nt(q.astype(BF16), k.astype(BF16)) * dec_ref[h]
        yield True
        out = _dot(scores.astype(BF16), v) + _dot((q * qd_ref[h]).astype(BF16), s_old.astype(BF16))
        s_ref[h] = cd_ref[h] * s_old + _dot_tn((k * kd_ref[h]).astype(BF16), v)
        yield True
        g = p_ref[:, OFF_RG + lo:OFF_RG + lo + HEAD_DIM]
        y_ref[:, lo:lo + HEAD_DIM] = ((g * _sigmoid(g)) * _layer_norm(out, gret_ref[:, lo:lo + HEAD_DIM])).astype(y_ref.dtype)

    xp_ref[0:CONV_TAIL_ROWS, :] = tail_ref[...]
    xp_ref[CONV_TAIL_ROWS:CONV_TAIL_ROWS + L, :] = p_ref[:, OFF_MQK:OFF_MQK + 2 * MLSTM_WIDTH]
    acc = bconv_ref[...] + wconv_ref[0:1, :] * xp_ref[CONV_TAIL_ROWS - 3:CONV_TAIL_ROWS - 3 + L, :]
    for j in range(1, CONV_WIDTH):
        acc = acc + wconv_ref[j:j + 1, :] * xp_ref[CONV_TAIL_ROWS - 3 + j:CONV_TAIL_ROWS - 3 + j + L, :]
    tail_ref[...] = xp_ref[L:L + CONV_TAIL_ROWS, :]
    xp_ref[CONV_TAIL_ROWS:CONV_TAIL_ROWS + L, :] = acc * _sigmoid(acc)

    gcol = gc_ref[...] + bgc_ref[...]
    is_f_col = lax.broadcasted_iota(I32, gcol.shape, 1) >= MLSTM_HEADS
    gcol = jnp.where(is_f_col, _log_sigmoid(gcol), gcol)
    grow = gr_ref[0] + bgr_ref[...]
    is_f_row = lax.broadcasted_iota(I32, grow.shape, 0) >= MLSTM_HEADS
    grow = jnp.where(is_f_row, _log_sigmoid(grow), grow)
    bcol_all = _dot3(tril_ref[...], gcol)
    brow_all = _dot3(grow, triu_ref[...])
    causal = lax.broadcasted_iota(I32, (L, L), 0) >= lax.broadcasted_iota(I32, (L, L), 1)

    def mlstm_head(h):
        lo = h * HEAD_DIM
        q = xp_ref[CONV_TAIL_ROWS:CONV_TAIL_ROWS + L, lo:lo + HEAD_DIM]
        k = xp_ref[CONV_TAIL_ROWS:CONV_TAIL_ROWS + L, MLSTM_WIDTH + lo:MLSTM_WIDTH + lo + HEAD_DIM] * scale
        v = p_ref[:, OFF_MV + lo:OFF_MV + lo + HEAD_DIM].astype(BF16)
        ic_col = gcol[:, h:h + 1]
        ic_row = grow[h:h + 1, :]
        b_col = bcol_all[:, MLSTM_HEADS + h:MLSTM_HEADS + h + 1]
        b_row = brow_all[MLSTM_HEADS + h:MLSTM_HEADS + h + 1, :]
        c_old = c_ref[h]
        n_old = n_ref[h]
        m_old = m_ref[h][:, 0:1]

        d_log = jnp.where(causal, b_col - b_row + ic_row, -jnp.inf)
        inter = b_col + m_old
        m_t = jnp.maximum(inter, jnp.max(d_log, axis=1, keepdims=True))
        yield True
        w_intra = jnp.exp(d_log - m_t)
        w_inter = jnp.exp(inter - m_t)
        qb = q.astype(BF16)
        s = _dot_nt(qb, k.astype(BF16)) * w_intra
        yield True
        num = _dot(s.astype(BF16), v) + w_inter * _dot(qb, c_old.astype(BF16))
        den = jnp.sum(s, axis=1, keepdims=True) + w_inter * jnp.sum(q * n_old, axis=1, keepdims=True)
        yield True
        hh = num / jnp.maximum(jnp.abs(den), jnp.exp(-m_t))

        b_last = b_col[L - 1:L, :]
        w_log_col = b_last - b_col + ic_col
        m_new = jnp.maximum(b_last + m_old, jnp.max(w_log_col, axis=0, keepdims=True))
        wk = jnp.exp(w_log_col - m_new) * k
        cdec = jnp.exp(b_last + m_old - m_new)
        yield True
        c_ref[h] = cdec * c_old + _dot_tn(wk.astype(BF16), v)
        n_ref[h] = cdec * n_old + jnp.sum(wk, axis=0, keepdims=True)
        m_ref[h] = jnp.broadcast_to(m_new, (1, HEAD_DIM))
        yield True
        o = p_ref[:, OFF_MO + lo:OFF_MO + lo + HEAD_DIM]
        y_ref[:, RET_WIDTH + lo:RET_WIDTH + lo + HEAD_DIM] = (
            _sigmoid(o) * _layer_norm(hh, gml_ref[:, lo:lo + HEAD_DIM])).astype(y_ref.dtype)

    return [retention_head(h) for h in range(RET_HEADS)] + [mlstm_head(h) for h in range(MLSTM_HEADS)]


def _mixer_consts(L, pos):
    f32 = np.float32
    half = HEAD_DIM // 2
    inv_freq = np.power(f32(ROPE_BASE), -np.arange(half, dtype=f32) / f32(half)).astype(f32)
    ang = (pos.astype(f32)[:, None] * inv_freq[None, :]).astype(f32)
    cos = np.concatenate([np.cos(ang), np.cos(ang)], axis=-1).astype(f32)
    sin = np.concatenate([-np.sin(ang), np.sin(ang)], axis=-1).astype(f32)
    log_gamma = np.log1p(-np.exp2(-5.0 - np.arange(RET_HEADS, dtype=np.float64)))
    idx = np.arange(L, dtype=np.float64)
    rel = idx[:, None] - idx[None, :]
    dec = np.where(rel >= 0, np.exp(log_gamma[:, None, None] * np.maximum(rel, 0.0)), 0.0)
    qd = np.broadcast_to(np.exp(log_gamma[:, None] * (idx + 1.0))[..., None], (RET_HEADS, L, HEAD_DIM))
    kd = np.broadcast_to(np.exp(log_gamma[:, None] * (L - 1.0 - idx))[..., None], (RET_HEADS, L, HEAD_DIM))
    cd = np.broadcast_to(np.exp(log_gamma * L)[:, None, None], (RET_HEADS, 1, HEAD_DIM))
    tril = rel >= 0
    triu = rel <= 0
    return tuple(jnp.asarray(a, F32) for a in (cos, sin, dec, qd, kd, cd, tril, triu))


def _mixer_call(p, gc, gr4, consts, w_conv, b_conv, g_ret, g_mlstm, bg_col, bg_row,
                s0, c0, n0, m0, conv0, groups, seq, L, gs, y_dtype):
    cos, sin, dec, qd, kd, cd, tril, triu = consts
    nc = seq // L
    hd = HEAD_DIM
    full = lambda *shape: pl.BlockSpec(shape, lambda g, c: (0,) * len(shape))
    state4 = pl.BlockSpec((gs, RET_HEADS, hd, hd), lambda g, c: (g, 0, 0, 0))
    vec4 = pl.BlockSpec((gs, MLSTM_HEADS, 1, hd), lambda g, c: (g, 0, 0, 0))
    tail3 = pl.BlockSpec((gs, CONV_TAIL_ROWS, 2 * MLSTM_WIDTH), lambda g, c: (g, 0, 0))
    row = lambda g, c: (g, c, 0)
    return pl.pallas_call(
        _mixer_kernel,
        grid=(groups // gs, nc),
        in_specs=[pl.BlockSpec((gs, L, MAIN_COLS), row),
                  pl.BlockSpec((gs, L, N_GATES), row),
                  pl.BlockSpec((gs, 1, N_GATES, L), lambda g, c: (g, c, 0, 0)),
                  pl.BlockSpec((L, hd), lambda g, c: (c, 0)),
                  pl.BlockSpec((L, hd), lambda g, c: (c, 0)),
                  full(RET_HEADS, L, L), full(RET_HEADS, L, hd), full(RET_HEADS, L, hd), full(RET_HEADS, 1, hd),
                  full(L, L), full(L, L),
                  full(CONV_WIDTH, 2 * MLSTM_WIDTH), full(1, 2 * MLSTM_WIDTH),
                  full(1, RET_WIDTH), full(1, MLSTM_WIDTH), full(1, N_GATES), full(N_GATES, 1),
                  state4, state4, vec4, vec4, tail3],
        out_specs=[pl.BlockSpec((gs, L, RET_WIDTH + MLSTM_WIDTH), row),
                   state4, state4, vec4, vec4, tail3],
        out_shape=[jax.ShapeDtypeStruct((groups, seq, RET_WIDTH + MLSTM_WIDTH), y_dtype),
                   jax.ShapeDtypeStruct((groups, RET_HEADS, hd, hd), F32),
                   jax.ShapeDtypeStruct((groups, MLSTM_HEADS, hd, hd), F32),
                   jax.ShapeDtypeStruct((groups, MLSTM_HEADS, 1, hd), F32),
                   jax.ShapeDtypeStruct((groups, MLSTM_HEADS, 1, hd), F32),
                   jax.ShapeDtypeStruct((groups, CONV_TAIL_ROWS, 2 * MLSTM_WIDTH), F32)],
        scratch_shapes=[pltpu.VMEM((gs, CONV_TAIL_ROWS + L, 2 * MLSTM_WIDTH), F32)],
        compiler_params=_params(2), name="mixer",
    )(p, gc, gr4, cos, sin, dec, qd, kd, cd, tril, triu, w_conv, b_conv.reshape(1, -1),
      g_ret.reshape(1, -1), g_mlstm.reshape(1, -1), bg_col, bg_row, s0, c0, n0, m0, conv0)


def _outproj_kernel(y_ref, x_ref, gt_ref, sh_ref, sc_ref, g_ref, w_ref, wrt_ref, br_ref, ustrict_ref,
                    x1_ref, h2_ref, idx_ref, gate_ref, rank_ref, cnt_ref):
    x = x_ref[...]
    gb, tt, d = x.shape
    tm = gb * tt
    mixed = _dot(y_ref[...].astype(BF16), w_ref[...])
    x1 = x + gt_ref[...] * mixed.reshape(gb, tt, d)
    x1_ref[...] = x1
    h2 = (_rms(x1, g_ref[...]) * (1.0 + sc_ref[...]) + sh_ref[...]).reshape(tm, d)
    h2_ref[...] = h2.astype(BF16)

    work = _dot3(wrt_ref[...], h2, dot=_dot_nt) + br_ref[...]
    e_iota = lax.broadcasted_iota(I32, work.shape, 0).astype(F32)
    vals, idxs, sels = [], [], []
    for _ in range(TOP_K):
        mx = jnp.max(work, axis=0, keepdims=True)
        ik = jnp.min(jnp.where(work == mx, e_iota, float(N_EXPERTS)), axis=0, keepdims=True)
        sel = e_iota == ik
        vals.append(mx)
        idxs.append(ik)
        sels.append(sel)
        work = jnp.where(sel, -jnp.inf, work)
    exps = [jnp.exp(v - vals[0]) for v in vals]
    denom = exps[0] + exps[1] + exps[2] + exps[3]
    gate_ref[...] = jnp.concatenate([e / denom for e in exps], axis=0)
    idx_ref[...] = jnp.concatenate(idxs, axis=0).astype(I32)

    mask = (sels[0] | sels[1] | sels[2] | sels[3]).astype(F32)
    before = _dot(mask.astype(BF16), ustrict_ref[...])
    ranks = [jnp.sum(jnp.where(sel, before, 0.0), axis=0, keepdims=True) for sel in sels]
    rank_ref[...] = jnp.concatenate(ranks, axis=0).astype(I32)
    cnt_ref[0] = jnp.broadcast_to(jnp.sum(mask, axis=1, keepdims=True), cnt_ref.shape[1:])


def _outproj_call(y, x3, mod3, g_ffn, w_out_bf, w_router_t, b_router, ustrict, gb, tt):
    g, t, d = x3.shape
    n = g * t
    tm = gb * tt
    tpg = t // tt
    row = lambda i, j: (i * tpg + j, 0)
    col = lambda i, j: (0, i * tpg + j)
    mod_spec = lambda k: pl.BlockSpec((gb, 1, d), lambda i, j: (i, 0, k))
    const = lambda *shape: pl.BlockSpec(shape, lambda i, j: (0,) * len(shape))
    return pl.pallas_call(
        _outproj_kernel,
        grid=(g // gb, tpg),
        in_specs=[pl.BlockSpec((tm, d), row),
                  pl.BlockSpec((gb, tt, d), lambda i, j: (i, j, 0)),
                  mod_spec(2), mod_spec(3), mod_spec(4),
                  const(1, 1, d), const(d, d), const(N_EXPERTS, d), const(N_EXPERTS, 1),
                  const(tm, tm)],
        out_specs=[pl.BlockSpec((gb, tt, d), lambda i, j: (i, j, 0)),
                   pl.BlockSpec((tm, d), row),
                   pl.BlockSpec((TOP_K, tm), col),
                   pl.BlockSpec((TOP_K, tm), col),
                   pl.BlockSpec((TOP_K, tm), col),
                   pl.BlockSpec((1, N_EXPERTS, 128), lambda i, j: (i * tpg + j, 0, 0))],
        out_shape=[jax.ShapeDtypeStruct((g, t, d), F32),
                   jax.ShapeDtypeStruct((n, d), BF16),
                   jax.ShapeDtypeStruct((TOP_K, n), I32),
                   jax.ShapeDtypeStruct((TOP_K, n), F32),
                   jax.ShapeDtypeStruct((TOP_K, n), I32),
                   jax.ShapeDtypeStruct((n // tm, N_EXPERTS, 128), F32)],
        compiler_params=_params(2), name="outproj_router",
    )(y, x3, mod3, mod3, mod3, g_ffn.reshape(1, 1, d), w_out_bf, w_router_t, b_router.reshape(N_EXPERTS, 1),
      ustrict)


def _chunk_copy(hbm_ref, hbm_row, buf_ref, chunk, sem, to_hbm):
    hbm = hbm_ref.at[pl.ds(pl.multiple_of(hbm_row, RUN_ALIGN), RUN_ALIGN)]
    buf = buf_ref.at[pl.ds(pl.multiple_of(chunk * RUN_ALIGN, RUN_ALIGN), RUN_ALIGN)]
    return pltpu.make_async_copy(buf, hbm, sem) if to_hbm else pltpu.make_async_copy(hbm, buf, sem)


def _tile_chunks(hbm_ref, bufs_ref, starts_ref, nq_ref, sems, tile, to_hbm, wait):
    slot = tile % 2
    first = tile * (bufs_ref.shape[1] // RUN_ALIGN)

    def body(q, c):
        cp = _chunk_copy(hbm_ref, starts_ref[first + q], bufs_ref.at[slot], q, sems.at[slot], to_hbm)
        cp.wait() if wait else cp.start()
        return c

    lax.fori_loop(0, nq_ref[tile], body, 0)


def _dispatch_kernel(starts_ref, nq_ref, meta_ref, rbuf_ref, h2p_ref, h2s_ref, xs_ref, bufs_ref, zero_ref, sems,
                     *, prompt_tiles):
    i = pl.program_id(0)
    tm = h2p_ref.shape[0]
    bm = zero_ref.shape[0]
    n_blocks = xs_ref.shape[0] // bm
    sem = sems.at[0]
    buf_ref = bufs_ref.at[i % 2]

    @pl.when(i == 0)
    def _():
        zero_ref[...] = jnp.zeros(zero_ref.shape, zero_ref.dtype)

        def zero_copy(row):
            return pltpu.make_async_copy(zero_ref, xs_ref.at[pl.ds(pl.multiple_of(row, bm), bm)], sem)

        def tails(fn):
            def body(e, c):
                @pl.when(meta_ref[e] >= 0)
                def _():
                    fn(zero_copy(meta_ref[e]))
                return c
            lax.fori_loop(0, N_EXPERTS, body, 0)

        def unused(fn):
            def body(b, c):
                fn(zero_copy(b * bm))
                return c
            lax.fori_loop(meta_ref[N_EXPERTS], n_blocks, body, 0)

        tails(lambda cp: cp.start())
        unused(lambda cp: cp.start())
        tails(lambda cp: cp.wait())
        unused(lambda cp: cp.wait())

    def build(h2_ref):
        h2 = h2_ref[...]
        rb = rbuf_ref[...].astype(jnp.int16)
        for c in range(bufs_ref.shape[1] // PERM_ROWS):
            r = (lax.broadcasted_iota(I32, (PERM_ROWS, tm), 0) + c * PERM_ROWS).astype(jnp.int16)
            hit = (r == rb[0:1, :]) | (r == rb[1:2, :]) | (r == rb[2:3, :]) | (r == rb[3:4, :])
            onehot = jnp.where(hit, jnp.ones((), BF16), jnp.zeros((), BF16))
            buf_ref[c * PERM_ROWS:(c + 1) * PERM_ROWS, :] = _dot(onehot, h2).astype(BF16)

    @pl.when(i < prompt_tiles)
    def _():
        build(h2p_ref)

    @pl.when(i >= prompt_tiles)
    def _():
        build(h2s_ref)

    chunks = functools.partial(_tile_chunks, xs_ref, bufs_ref, starts_ref, nq_ref, sems, to_hbm=True)
    chunks(i, wait=False)

    @pl.when(i > 0)
    def _():
        chunks(i - 1, wait=True)

    @pl.when(i == pl.num_programs(0) - 1)
    def _():
        chunks(i, wait=True)


def _dispatch_call(starts, nq, meta, rbuf, h2_p, h2_s, cap):
    d = h2_p.shape[1]
    tm = MOE_TILE
    pt, st = h2_p.shape[0] // tm, h2_s.shape[0] // tm
    return pl.pallas_call(
        functools.partial(_dispatch_kernel, prompt_tiles=pt),
        grid_spec=pltpu.PrefetchScalarGridSpec(
            num_scalar_prefetch=3,
            grid=(pt + st,),
            in_specs=[pl.BlockSpec((TOP_K, tm), lambda i, *_: (0, i)),
                      pl.BlockSpec((tm, d), lambda i, *_: (jnp.minimum(i, pt - 1), 0)),
                      pl.BlockSpec((tm, d), lambda i, *_: (jnp.maximum(i - pt, 0), 0))],
            out_specs=pl.BlockSpec(memory_space=pl.ANY),
            scratch_shapes=[pltpu.VMEM((2, TILE_BUF_ROWS, d), BF16), pltpu.VMEM((EXPERT_BLOCK, d), BF16),
                            pltpu.SemaphoreType.DMA((2,))]),
        out_shape=jax.ShapeDtypeStruct((cap, d), BF16),
        compiler_params=_params(1), name="dispatch",
    )(starts, nq, meta, rbuf, h2_p, h2_s)


def _expert_kernel(be_ref, nu_ref, xs_ref, wup_ref, bup_ref, wdn_ref, bdn_ref, ys_ref, wup_bf, wdn_bf):
    i = pl.program_id(0)
    prev = be_ref[jnp.maximum(i - 1, 0)]

    @pl.when((i == 0) | (be_ref[i] != prev))
    def _():
        def cast(r, c):
            rows = pl.ds(pl.multiple_of(r * 64, 64), 64)
            wup_bf[rows, :] = wup_ref[0, rows, :].astype(BF16)
            wdn_bf[rows, :] = wdn_ref[0, rows, :].astype(BF16)
            return c

        lax.fori_loop(0, D_MODEL // 64, cast, 0)

    @pl.when(i < nu_ref[0])
    def _():
        hu = _dot(xs_ref[...], wup_bf[...]) + bup_ref[0]
        gate = jnp.minimum(hu[:, :D_FF], SWIGLU_LIMIT)
        lin = jnp.clip(hu[:, D_FF:], -SWIGLU_LIMIT, SWIGLU_LIMIT)
        glu = gate * _sigmoid(SWIGLU_ALPHA * gate)
        ys_ref[...] = (_dot(((lin + 1.0) * glu).astype(BF16), wdn_bf[...]) + bdn_ref[0]).astype(ys_ref.dtype)

    @pl.when(i >= nu_ref[0])
    def _():
        ys_ref[...] = jnp.zeros(ys_ref.shape, ys_ref.dtype)


def _expert_call(block_e, n_used, xs, w_up, b_up, w_down, b_down):
    cap, d = xs.shape
    bm = EXPERT_BLOCK
    blk = lambda i, be, nu: (jnp.minimum(i, nu[0] - 1), 0)
    per_e = lambda i, be, nu: (be[i], 0, 0)
    return pl.pallas_call(
        _expert_kernel,
        grid_spec=pltpu.PrefetchScalarGridSpec(
            num_scalar_prefetch=2,
            grid=(cap // bm,),
            in_specs=[pl.BlockSpec((bm, d), blk),
                      pl.BlockSpec((1, d, 2 * D_FF), per_e),
                      pl.BlockSpec((1, 1, 2 * D_FF), per_e),
                      pl.BlockSpec((1, D_FF, d), per_e),
                      pl.BlockSpec((1, 1, d), per_e)],
            out_specs=pl.BlockSpec((bm, d), lambda i, be, nu: (i, 0)),
            scratch_shapes=[pltpu.VMEM((d, 2 * D_FF), BF16), pltpu.VMEM((D_FF, d), BF16)]),
        out_shape=jax.ShapeDtypeStruct((cap, d), BF16),
        compiler_params=_params(1), name="experts",
    )(block_e, n_used, xs, w_up, b_up.reshape(N_EXPERTS, 1, -1), w_down, b_down.reshape(N_EXPERTS, 1, -1))


def _combine_kernel(starts_ref, nq_ref, x1_ref, gt_ref, gate_ref, rcol_ref, gfin_ref, ys_ref, o_ref, bufs_ref, sems,
                    *, tile0, tiles_per_group):
    x1 = x1_ref[...]
    gb, tt, d = x1.shape
    tm = gb * tt
    step = pl.program_id(0) * tiles_per_group + pl.program_id(1)
    n_steps = pl.num_programs(0) * tiles_per_group
    tile = tile0 + step
    chunks = functools.partial(_tile_chunks, ys_ref, bufs_ref, starts_ref, nq_ref, sems, to_hbm=False)

    @pl.when(step == 0)
    def _():
        bufs_ref[...] = jnp.zeros(bufs_ref.shape, bufs_ref.dtype)
        chunks(tile, wait=False)

    @pl.when(step + 1 < n_steps)
    def _():
        chunks(tile + 1, wait=False)

    chunks(tile, wait=True)
    buf_ref = bufs_ref.at[tile % 2]

    gate_hi, gate_lo = _split(gate_ref[...])
    rows = rcol_ref[...].astype(jnp.int16)
    zero = jnp.zeros((), BF16)
    moe = jnp.zeros((tm, d), F32)
    for c in range(bufs_ref.shape[1] // PERM_COLS):
        r = (lax.broadcasted_iota(I32, (tm, PERM_COLS), 1) + c * PERM_COLS).astype(jnp.int16)
        hits = [r == rows[:, k:k + 1] for k in range(TOP_K)]
        g_hi = sum(jnp.where(hits[k], gate_hi[:, k:k + 1], zero) for k in range(1, TOP_K)) + jnp.where(
            hits[0], gate_hi[:, 0:1], zero)
        g_lo = sum(jnp.where(hits[k], gate_lo[:, k:k + 1], zero) for k in range(1, TOP_K)) + jnp.where(
            hits[0], gate_lo[:, 0:1], zero)
        part = buf_ref[c * PERM_COLS:(c + 1) * PERM_COLS, :]
        both = _dot(jnp.concatenate([g_hi, g_lo], axis=0), part)
        moe = moe + (both[:tm] + both[tm:])
    xo = x1 + gt_ref[...] * moe.reshape(gb, tt, d)
    o_ref[...] = _rms(xo, gfin_ref[...])


def _combine_call(starts, nq, x1, mod3, gates_t, rcol, g_final, ys, tok0, gb, tt):
    g, t, d = x1.shape
    tm = gb * tt
    tpg = t // tt
    t0 = tok0 // tm
    tok = lambda i, j, *_: (t0 + i * tpg + j, 0)
    return pl.pallas_call(
        functools.partial(_combine_kernel, tile0=t0, tiles_per_group=tpg),
        grid_spec=pltpu.PrefetchScalarGridSpec(
            num_scalar_prefetch=2,
            grid=(g // gb, tpg),
            in_specs=[pl.BlockSpec((gb, tt, d), lambda i, j, *_: (i, j, 0)),
                      pl.BlockSpec((gb, 1, d), lambda i, j, *_: (i, 0, 5)),
                      pl.BlockSpec((tm, TOP_K), tok),
                      pl.BlockSpec((tm, TOP_K), tok),
                      pl.BlockSpec((1, 1, d), lambda i, j, *_: (0, 0, 0)),
                      pl.BlockSpec(memory_space=pl.ANY)],
            out_specs=pl.BlockSpec((gb, tt, d), lambda i, j, *_: (i, j, 0)),
            scratch_shapes=[pltpu.VMEM((2, TILE_BUF_ROWS, d), BF16), pltpu.SemaphoreType.DMA((2,))]),
        out_shape=jax.ShapeDtypeStruct((g, t, d), F32),
        compiler_params=_params(2), name="combine",
    )(starts, nq, x1, mod3, gates_t, rcol, g_final.reshape(1, 1, d), ys)


def _group_blocking(groups, seq, tile):
    if seq >= tile:
        return 1, tile
    return tile // seq, seq


def _gates_by_chunk(gr, groups, seq, L):
    return gr.reshape(N_GATES, groups, seq // L, L).transpose(1, 2, 0, 3)


def kernel(x_prompt, x_sample, c_prompt, c_sample, state_ret, state_mlstm_c, state_mlstm_n, state_mlstm_m, state_conv, w_mod, b_mod, g_mix, g_ffn, w_in, b_igate, b_fgate, w_conv, b_conv, g_ret, g_mlstm, w_out, w_router, b_router, w_up, b_up, w_down, b_down, g_final):
    depth = w_mod.shape[0]
    assert depth == 1, "single-layer trunk"
    bp, tp, d = x_prompt.shape
    bs, ts, _ = x_sample.shape
    n_p, n_s = bp * tp, bs * ts
    hd = HEAD_DIM
    l = 0

    mod = _mod_call(jnp.concatenate([c_prompt, c_sample], axis=0), w_mod[l], b_mod[l])
    mod_p = mod[:bp].reshape(bp, 1, N_MOD * d)
    mod_s = mod[bp:].reshape(bs, 1, N_MOD * d)

    w_main_bf = w_in[l][:, :MAIN_COLS].astype(BF16)
    w_gate = jnp.pad(w_in[l][:, MAIN_COLS:], ((0, 0), (0, 128 - N_GATES)))
    w_gate_t = w_in[l][:, MAIN_COLS:].T
    w_out_bf = w_out[l].astype(BF16)
    bg_col = jnp.concatenate([b_igate[l], b_fgate[l]]).reshape(1, N_GATES)
    bg_row = bg_col.reshape(N_GATES, 1)

    groups = (
        (x_prompt, mod_p, min(CHUNK, tp), np.arange(tp),
         jnp.zeros((bp, RET_HEADS, hd, hd), F32), jnp.zeros((bp, MLSTM_HEADS, hd, hd), F32),
         jnp.zeros((bp, MLSTM_HEADS, hd), F32), jnp.zeros((bp, MLSTM_HEADS), F32),
         jnp.zeros((bp, CONV_WIDTH - 1, 2 * MLSTM_WIDTH), F32)),
        (x_sample, mod_s, min(CHUNK, ts), PAST_LEN + np.arange(ts),
         state_ret[l], state_mlstm_c[l], state_mlstm_n[l], state_mlstm_m[l], state_conv[l]),
    )

    ustrict = jnp.asarray(np.arange(MOE_TILE)[:, None] < np.arange(MOE_TILE)[None, :], BF16)
    staged = []
    for x3, mod3, L, pos, s0, c0, n0, m0, conv0 in groups:
        g, t, _ = x3.shape
        gb, tt = _group_blocking(g, t, TOKEN_TILE)
        p, gc, gr = _inproj_call(x3, mod3, g_mix[l], w_main_bf, w_gate, w_gate_t, gb, tt)
        conv0p = jnp.pad(conv0.astype(F32), ((0, 0), (CONV_TAIL_ROWS - (CONV_WIDTH - 1), 0), (0, 0)))
        y, s_new, c_new, n_new, m_new, tail = _mixer_call(
            p.reshape(g, t, MAIN_COLS), gc.reshape(g, t, N_GATES), _gates_by_chunk(gr, g, t, L), _mixer_consts(L, pos),
            w_conv[l], b_conv[l], g_ret[l], g_mlstm[l],
            bg_col, bg_row, s0.astype(F32), c0.astype(F32), n0.astype(F32).reshape(g, MLSTM_HEADS, 1, hd),
            jnp.broadcast_to(m0.astype(F32)[:, :, None, None], (g, MLSTM_HEADS, 1, hd)), conv0p,
            g, t, L, MIXER_SEQS_CHUNKED if t > L else MIXER_SEQS_SHORT, F32)
        states = (s_new, c_new, n_new.reshape(g, MLSTM_HEADS, hd), m_new[:, :, 0, 0],
                  tail[:, CONV_TAIL_ROWS - (CONV_WIDTH - 1):, :])
        gb, tt = _group_blocking(g, t, MOE_TILE)
        x1, h2, idx, gates, rank, cnt = _outproj_call(
            y.reshape(g * t, d), x3, mod3, g_ffn[l], w_out_bf, w_router[l].T, b_router[l], ustrict, gb, tt)
        staged.append((x1, mod3, h2, idx, gates, rank, states, cnt))

    n_tok = n_p + n_s
    tm, bm, ra = MOE_TILE, EXPERT_BLOCK, RUN_ALIGN
    n_tiles = n_tok // tm
    q_max = TILE_BUF_ROWS // ra
    n_blocks = -(-(n_tok * TOP_K + n_tiles * N_EXPERTS * (ra - 1)) // bm) + N_EXPERTS
    cap = n_blocks * bm
    counts = jnp.concatenate([s[7][:, :, 0] for s in staged], axis=0).astype(I32)
    run = (counts + ra - 1) // ra * ra
    region = jnp.sum(run, axis=0)
    padded = (region + bm - 1) // bm * bm
    pad_end = jnp.cumsum(padded)
    pad_start = pad_end - padded
    run_start = pad_start[None, :] + jnp.cumsum(run, axis=0) - run
    buf_end = jnp.cumsum(run, axis=1)
    buf_start = buf_end - run
    nq = (buf_end[:, -1] // ra).astype(I32)
    chunk_row = jnp.arange(q_max, dtype=I32) * ra
    chunk_e = jnp.minimum(jnp.sum((buf_end[:, None, :] <= chunk_row[None, :, None]).astype(I32), axis=2),
                          N_EXPERTS - 1)
    e_ids = jnp.arange(N_EXPERTS, dtype=I32)
    shift = run_start - buf_start
    starts = jnp.sum(jnp.where(chunk_e[:, :, None] == e_ids, shift[:, None, :], 0), axis=2) + chunk_row[None, :]
    starts = jnp.where(chunk_row[None, :] < buf_end[:, -1:], starts, 0).reshape(-1).astype(I32)

    idx_all = jnp.concatenate([s[3] for s in staged], axis=1)
    rank_all = jnp.concatenate([s[5] for s in staged], axis=1)
    gates_t = jnp.concatenate([s[4] for s in staged], axis=1).T
    buf_start_tok = jnp.repeat(buf_start, tm, axis=0).T
    rbuf = jnp.sum(jnp.where(idx_all[None] == e_ids[:, None, None], buf_start_tok[:, None, :], 0), axis=0) + rank_all
    rcol = rbuf.T
    block_row = jnp.arange(n_blocks, dtype=I32) * bm
    block_e = jnp.minimum(jnp.sum((pad_end[None, :] <= block_row[:, None]).astype(I32), axis=1), N_EXPERTS - 1)
    n_used = (pad_end[-1:] // bm).astype(I32)
    meta = jnp.concatenate([jnp.where(region > 0, pad_end - bm, -1), n_used]).astype(I32)

    xs = _dispatch_call(starts, nq, meta, rbuf, staged[0][2], staged[1][2], cap)
    ys = _expert_call(block_e, n_used, xs, w_up[l], b_up[l], w_down[l], b_down[l])

    outs = []
    tok0 = 0
    for x1, mod3, *_ in staged:
        g, t, _ = x1.shape
        gb, tt = _group_blocking(g, t, tm)
        outs.append(_combine_call(starts, nq, x1, mod3, gates_t, rcol, g_final, ys, tok0, gb, tt))
        tok0 += g * t

    st_p, st_s = staged[0][6], staged[1][6]
    return (outs[0], outs[1]) + tuple(a[None] for a in st_p) + tuple(a[None] for a in st_s)
```

```python
import functools

import numpy as np
import jax
import jax.numpy as jnp
from jax import lax
from jax.experimental import pallas as pl
from jax.experimental.pallas import tpu as pltpu

F32 = jnp.float32
BF16 = jnp.bfloat16
I32 = jnp.int32

D_MODEL = 1024
PAST_LEN = 16384
RET_HEADS = 4
MLSTM_HEADS = 4
HEAD_DIM = 128
RET_WIDTH = RET_HEADS * HEAD_DIM
MLSTM_WIDTH = MLSTM_HEADS * HEAD_DIM
CONV_WIDTH = 4
CHUNK = 128
ROPE_BASE = 10000.0
N_EXPERTS = 32
TOP_K = 4
D_FF = D_MODEL
SWIGLU_LIMIT = 7.0
SWIGLU_ALPHA = 1.702
N_MOD = 6
EPS = 1e-6
MAIN_COLS = 4 * RET_WIDTH + 2 * MLSTM_WIDTH + 2 * MLSTM_WIDTH
N_GATES = 2 * MLSTM_HEADS
OFF_RQ, OFF_RK, OFF_RV, OFF_RG = 0, RET_WIDTH, 2 * RET_WIDTH, 3 * RET_WIDTH
OFF_MQK = 4 * RET_WIDTH
OFF_MV = OFF_MQK + 2 * MLSTM_WIDTH
OFF_MO = OFF_MV + MLSTM_WIDTH

VMEM_LIMIT_BYTES = 56 * 1024 * 1024
MOE_TILE = 512
EXPERT_BLOCK = 512
MIXER_SEQS_CHUNKED = 2
MIXER_SEQS_SHORT = 8
RUN_ALIGN = 16
TILE_BUF_ROWS = MOE_TILE * TOP_K + N_EXPERTS * RUN_ALIGN
PERM_ROWS = 1280
PERM_COLS = 512
CONV_TAIL_ROWS = 8


def _params(n_axes=1):
    return pltpu.CompilerParams(dimension_semantics=("arbitrary",) * n_axes, vmem_limit_bytes=VMEM_LIMIT_BYTES)


def _dot(a, b):
    return jnp.dot(a, b, preferred_element_type=F32)


def _dot_nt(a, b):
    return lax.dot_general(a, b, (((1,), (1,)), ((), ())), preferred_element_type=F32)


def _dot_tn(a, b):
    return lax.dot_general(a, b, (((0,), (0,)), ((), ())), preferred_element_type=F32)


def _split(a):
    hi = a.astype(BF16)
    lo = (a - hi.astype(F32)).astype(BF16)
    return hi, lo


def _dot3(a, b, dot=_dot):
    ah, al = _split(a)
    bh, bl = _split(b)
    return dot(ah, bh) + (dot(al, bh) + dot(ah, bl))


def _sigmoid(x):
    return 0.5 * (jnp.tanh(0.5 * x) + 1.0)


def _log_sigmoid(x):
    return jnp.minimum(x, 0.0) - jnp.log1p(jnp.exp(-jnp.abs(x)))


def _rms(x, g):
    ms = jnp.mean(x * x, axis=-1, keepdims=True)
    return (x * lax.rsqrt(ms + EPS)) * g


def _layer_norm(x, g):
    mu = jnp.mean(x, axis=-1, keepdims=True)
    xc = x - mu
    var = jnp.mean(xc * xc, axis=-1, keepdims=True)
    return xc * lax.rsqrt(var + EPS) * g


def _mod_kernel(c_ref, w_ref, b_ref, o_ref):
    c = c_ref[...]
    o_ref[...] = _dot3(c * _sigmoid(c), w_ref[...]) + b_ref[...]


def _mod_call(c_all, w_mod, b_mod):
    rows, d = c_all.shape
    cols = w_mod.shape[1]
    tn = 1024
    return pl.pallas_call(
        _mod_kernel,
        grid=(cols // tn,),
        in_specs=[pl.BlockSpec((rows, d), lambda j: (0, 0)),
                  pl.BlockSpec((d, tn), lambda j: (0, j)),
                  pl.BlockSpec((1, tn), lambda j: (0, j))],
        out_specs=pl.BlockSpec((rows, tn), lambda j: (0, j)),
        out_shape=jax.ShapeDtypeStruct((rows, cols), F32),
        compiler_params=_params(1), name="mod",
    )(c_all, w_mod, b_mod.reshape(1, cols))


def _mixer_kernel(x_ref, sh_ref, sc_ref, gmix_ref, w_ref, wg_ref, wgt_ref,
                  cos_ref, sin_ref, dec_ref, qd_ref, kd_ref, cd_ref,
                  tril_ref, triu_ref, wconv_ref, bconv_ref, gret_ref, gml_ref, bgc_ref, bgr_ref,
                  s0_ref, c0_ref, n0_ref, m0_ref, conv0_ref,
                  y_ref, s_ref, c_ref, n_ref, m_ref, tail_ref, p_ref, xp_ref):
    @pl.when(pl.program_id(1) == 0)
    def _():
        s_ref[...] = s0_ref[...]
        c_ref[...] = c0_ref[...]
        n_ref[...] = n0_ref[...]
        m_ref[...] = m0_ref[...]
        tail_ref[...] = conv0_ref[...]

    x = x_ref[...]
    gs, L, d = x.shape
    h = (_rms(x, gmix_ref[...]) * (1.0 + sc_ref[...]) + sh_ref[...]).reshape(gs * L, d)
    hb = h.astype(BF16)
    for j in range(MAIN_COLS // 1024):
        p_ref[:, j * 1024:(j + 1) * 1024] = _dot(hb, w_ref[:, j * 1024:(j + 1) * 1024])

    chains = []
    for sq in range(gs):
        h_sq = h[sq * L:(sq + 1) * L, :]
        gc = _dot3(h_sq, wg_ref[...])[:, :N_GATES]
        gr = _dot3(wgt_ref[...], h_sq, dot=_dot_nt)
        chains += _mixer_sequence(p_ref.at[pl.ds(sq * L, L)], gc, gr, cos_ref, sin_ref, dec_ref, qd_ref, kd_ref,
                                  cd_ref, tril_ref, triu_ref, wconv_ref, bconv_ref, gret_ref, gml_ref, bgc_ref,
                                  bgr_ref, y_ref.at[sq], s_ref.at[sq], c_ref.at[sq], n_ref.at[sq], m_ref.at[sq],
                                  tail_ref.at[sq], xp_ref.at[sq])
    while chains:
        alive = []
        for chain in chains:
            if next(chain, None) is not None:
                alive.append(chain)
        chains = alive


def _mixer_sequence(p_ref, gc, gr, cos_ref, sin_ref, dec_ref, qd_ref, kd_ref, cd_ref,
                    tril_ref, triu_ref, wconv_ref, bconv_ref, gret_ref, gml_ref, bgc_ref, bgr_ref,
                    y_ref, s_ref, c_ref, n_ref, m_ref, tail_ref, xp_ref):
    L = p_ref.shape[0]
    cos = cos_ref[...]
    sin = sin_ref[...]
    scale = HEAD_DIM ** -0.5

    def rot(x):
        return x * cos + pltpu.roll(x, HEAD_DIM // 2, axis=1) * sin

    def retention_head(h):
        lo = h * HEAD_DIM
        q = rot(p_ref[:, OFF_RQ + lo:OFF_RQ + lo + HEAD_DIM])
        k = rot(p_ref[:, OFF_RK + lo:OFF_RK + lo + HEAD_DIM]) * scale
        v = p_ref[:, OFF_RV + lo:OFF_RV + lo + HEAD_DIM].astype(BF16)
        s_old = s_ref[h]
        yield True
        scores = _dot_nt(q.astype(BF16), k.astype(BF16)) * dec_ref[h]
        yield True
        out = _dot(scores.astype(BF16), v) + _dot((q * qd_ref[h]).astype(BF16), s_old.astype(BF16))
        s_ref[h] = cd_ref[h] * s_old + _dot_tn((k * kd_ref[h]).astype(BF16), v)
        yield True
        g = p_ref[:, OFF_RG + lo:OFF_RG + lo + HEAD_DIM]
        y_ref[:, lo:lo + HEAD_DIM] = ((g * _sigmoid(g)) * _layer_norm(out, gret_ref[:, lo:lo + HEAD_DIM])).astype(y_ref.dtype)

    xp_ref[0:CONV_TAIL_ROWS, :] = tail_ref[...]
    xp_ref[CONV_TAIL_ROWS:CONV_TAIL_ROWS + L, :] = p_ref[:, OFF_MQK:OFF_MQK + 2 * MLSTM_WIDTH]
    acc = bconv_ref[...] + wconv_ref[0:1, :] * xp_ref[CONV_TAIL_ROWS - 3:CONV_TAIL_ROWS - 3 + L, :]
    for j in range(1, CONV_WIDTH):
        acc = acc + wconv_ref[j:j + 1, :] * xp_ref[CONV_TAIL_ROWS - 3 + j:CONV_TAIL_ROWS - 3 + j + L, :]
    tail_ref[...] = xp_ref[L:L + CONV_TAIL_ROWS, :]
    xp_ref[CONV_TAIL_ROWS:CONV_TAIL_ROWS + L, :] = acc * _sigmoid(acc)

    gcol = gc + bgc_ref[...]
    is_f_col = lax.broadcasted_iota(I32, gcol.shape, 1) >= MLSTM_HEADS
    gcol = jnp.where(is_f_col, _log_sigmoid(gcol), gcol)
    grow = gr + bgr_ref[...]
    is_f_row = lax.broadcasted_iota(I32, grow.shape, 0) >= MLSTM_HEADS
    grow = jnp.where(is_f_row, _log_sigmoid(grow), grow)
    bcol_all = _dot3(tril_ref[...], gcol)
    brow_all = _dot3(grow, triu_ref[...])
    causal = lax.broadcasted_iota(I32, (L, L), 0) >= lax.broadcasted_iota(I32, (L, L), 1)

    def mlstm_head(h):
        lo = h * HEAD_DIM
        q = xp_ref[CONV_TAIL_ROWS:CONV_TAIL_ROWS + L, lo:lo + HEAD_DIM]
        k = xp_ref[CONV_TAIL_ROWS:CONV_TAIL_ROWS + L, MLSTM_WIDTH + lo:MLSTM_WIDTH + lo + HEAD_DIM] * scale
        v = p_ref[:, OFF_MV + lo:OFF_MV + lo + HEAD_DIM].astype(BF16)
        ic_col = gcol[:, h:h + 1]
        ic_row = grow[h:h + 1, :]
        b_col = bcol_all[:, MLSTM_HEADS + h:MLSTM_HEADS + h + 1]
        b_row = brow_all[MLSTM_HEADS + h:MLSTM_HEADS + h + 1, :]
        c_old = c_ref[h]
        n_old = n_ref[h]
        m_old = m_ref[h][:, 0:1]

        d_log = jnp.where(causal, b_col - b_row + ic_row, -jnp.inf)
        inter = b_col + m_old
        m_t = jnp.maximum(inter, jnp.max(d_log, axis=1, keepdims=True))
        yield True
        w_intra = jnp.exp(d_log - m_t)
        w_inter = jnp.exp(inter - m_t)
        qb = q.astype(BF16)
        s = _dot_nt(qb, k.astype(BF16)) * w_intra
        yield True
        num = _dot(s.astype(BF16), v) + w_inter * _dot(qb, c_old.astype(BF16))
        den = jnp.sum(s, axis=1, keepdims=True) + w_inter * jnp.sum(q * n_old, axis=1, keepdims=True)
        yield True
        hh = num / jnp.maximum(jnp.abs(den), jnp.exp(-m_t))

        b_last = b_col[L - 1:L, :]
        w_log_col = b_last - b_col + ic_col
        m_new = jnp.maximum(b_last + m_old, jnp.max(w_log_col, axis=0, keepdims=True))
        wk = jnp.exp(w_log_col - m_new) * k
        cdec = jnp.exp(b_last + m_old - m_new)
        yield True
        c_ref[h] = cdec * c_old + _dot_tn(wk.astype(BF16), v)
        n_ref[h] = cdec * n_old + jnp.sum(wk, axis=0, keepdims=True)
        m_ref[h] = jnp.broadcast_to(m_new, (1, HEAD_DIM))
        yield True
        o = p_ref[:, OFF_MO + lo:OFF_MO + lo + HEAD_DIM]
        y_ref[:, RET_WIDTH + lo:RET_WIDTH + lo + HEAD_DIM] = (
            _sigmoid(o) * _layer_norm(hh, gml_ref[:, lo:lo + HEAD_DIM])).astype(y_ref.dtype)

    return [retention_head(h) for h in range(RET_HEADS)] + [mlstm_head(h) for h in range(MLSTM_HEADS)]


def _mixer_consts(L, pos):
    f32 = np.float32
    half = HEAD_DIM // 2
    inv_freq = np.power(f32(ROPE_BASE), -np.arange(half, dtype=f32) / f32(half)).astype(f32)
    ang = (pos.astype(f32)[:, None] * inv_freq[None, :]).astype(f32)
    cos = np.concatenate([np.cos(ang), np.cos(ang)], axis=-1).astype(f32)
    sin = np.concatenate([-np.sin(ang), np.sin(ang)], axis=-1).astype(f32)
    log_gamma = np.log1p(-np.exp2(-5.0 - np.arange(RET_HEADS, dtype=np.float64)))
    idx = np.arange(L, dtype=np.float64)
    rel = idx[:, None] - idx[None, :]
    dec = np.where(rel >= 0, np.exp(log_gamma[:, None, None] * np.maximum(rel, 0.0)), 0.0)
    qd = np.broadcast_to(np.exp(log_gamma[:, None] * (idx + 1.0))[..., None], (RET_HEADS, L, HEAD_DIM))
    kd = np.broadcast_to(np.exp(log_gamma[:, None] * (L - 1.0 - idx))[..., None], (RET_HEADS, L, HEAD_DIM))
    cd = np.broadcast_to(np.exp(log_gamma * L)[:, None, None], (RET_HEADS, 1, HEAD_DIM))
    tril = rel >= 0
    triu = rel <= 0
    return tuple(jnp.asarray(a, F32) for a in (cos, sin, dec, qd, kd, cd, tril, triu))


def _mixer_call(x3, mod3, g_mix, w_main_bf, w_gate, w_gate_t, consts, w_conv, b_conv, g_ret, g_mlstm, bg_col, bg_row,
                s0, c0, n0, m0, conv0, L, gs, y_dtype):
    cos, sin, dec, qd, kd, cd, tril, triu = consts
    groups, seq, d = x3.shape
    nc = seq // L
    hd = HEAD_DIM
    full = lambda *shape: pl.BlockSpec(shape, lambda g, c: (0,) * len(shape))
    state4 = pl.BlockSpec((gs, RET_HEADS, hd, hd), lambda g, c: (g, 0, 0, 0))
    vec4 = pl.BlockSpec((gs, MLSTM_HEADS, 1, hd), lambda g, c: (g, 0, 0, 0))
    tail3 = pl.BlockSpec((gs, CONV_TAIL_ROWS, 2 * MLSTM_WIDTH), lambda g, c: (g, 0, 0))
    row = lambda g, c: (g, c, 0)
    return pl.pallas_call(
        _mixer_kernel,
        grid=(groups // gs, nc),
        in_specs=[pl.BlockSpec((gs, L, d), row),
                  pl.BlockSpec((gs, 1, d), lambda g, c: (g, 0, 0)),
                  pl.BlockSpec((gs, 1, d), lambda g, c: (g, 0, 1)),
                  full(1, 1, d), full(d, MAIN_COLS), full(d, 128), full(N_GATES, d),
                  pl.BlockSpec((L, hd), lambda g, c: (c, 0)),
                  pl.BlockSpec((L, hd), lambda g, c: (c, 0)),
                  full(RET_HEADS, L, L), full(RET_HEADS, L, hd), full(RET_HEADS, L, hd), full(RET_HEADS, 1, hd),
                  full(L, L), full(L, L),
                  full(CONV_WIDTH, 2 * MLSTM_WIDTH), full(1, 2 * MLSTM_WIDTH),
                  full(1, RET_WIDTH), full(1, MLSTM_WIDTH), full(1, N_GATES), full(N_GATES, 1),
                  state4, state4, vec4, vec4, tail3],
        out_specs=[pl.BlockSpec((gs, L, RET_WIDTH + MLSTM_WIDTH), row),
                   state4, state4, vec4, vec4, tail3],
        out_shape=[jax.ShapeDtypeStruct((groups, seq, RET_WIDTH + MLSTM_WIDTH), y_dtype),
                   jax.ShapeDtypeStruct((groups, RET_HEADS, hd, hd), F32),
                   jax.ShapeDtypeStruct((groups, MLSTM_HEADS, hd, hd), F32),
                   jax.ShapeDtypeStruct((groups, MLSTM_HEADS, 1, hd), F32),
                   jax.ShapeDtypeStruct((groups, MLSTM_HEADS, 1, hd), F32),
                   jax.ShapeDtypeStruct((groups, CONV_TAIL_ROWS, 2 * MLSTM_WIDTH), F32)],
        scratch_shapes=[pltpu.VMEM((gs * L, MAIN_COLS), F32),
                        pltpu.VMEM((gs, CONV_TAIL_ROWS + L, 2 * MLSTM_WIDTH), F32)],
        compiler_params=_params(2), name="mixer",
    )(x3, mod3, mod3, g_mix.reshape(1, 1, d), w_main_bf, w_gate, w_gate_t,
      cos, sin, dec, qd, kd, cd, tril, triu, w_conv, b_conv.reshape(1, -1),
      g_ret.reshape(1, -1), g_mlstm.reshape(1, -1), bg_col, bg_row, s0, c0, n0, m0, conv0)


def _outproj_kernel(y_ref, x_ref, gt_ref, sh_ref, sc_ref, g_ref, w_ref, wrt_ref, br_ref, ustrict_ref,
                    x1_ref, h2_ref, idx_ref, gate_ref, rank_ref, cnt_ref):
    x = x_ref[...]
    gb, tt, d = x.shape
    tm = gb * tt
    mixed = _dot(y_ref[...].astype(BF16), w_ref[...])
    x1 = x + gt_ref[...] * mixed.reshape(gb, tt, d)
    x1_ref[...] = x1
    h2 = (_rms(x1, g_ref[...]) * (1.0 + sc_ref[...]) + sh_ref[...]).reshape(tm, d)
    h2_ref[...] = h2.astype(BF16)

    work = _dot3(wrt_ref[...], h2, dot=_dot_nt) + br_ref[...]
    e_iota = lax.broadcasted_iota(I32, work.shape, 0).astype(F32)
    vals, idxs, sels = [], [], []
    for _ in range(TOP_K):
        mx = jnp.max(work, axis=0, keepdims=True)
        ik = jnp.min(jnp.where(work == mx, e_iota, float(N_EXPERTS)), axis=0, keepdims=True)
        sel = e_iota == ik
        vals.append(mx)
        idxs.append(ik)
        sels.append(sel)
        work = jnp.where(sel, -jnp.inf, work)
    exps = [jnp.exp(v - vals[0]) for v in vals]
    denom = exps[0] + exps[1] + exps[2] + exps[3]
    gate_ref[...] = jnp.concatenate([e / denom for e in exps], axis=0)
    idx_ref[...] = jnp.concatenate(idxs, axis=0).astype(I32)

    mask = (sels[0] | sels[1] | sels[2] | sels[3]).astype(F32)
    before = _dot(mask.astype(BF16), ustrict_ref[...])
    ranks = [jnp.sum(jnp.where(sel, before, 0.0), axis=0, keepdims=True) for sel in sels]
    rank_ref[...] = jnp.concatenate(ranks, axis=0).astype(I32)
    cnt_ref[0] = jnp.broadcast_to(jnp.sum(mask, axis=1, keepdims=True), cnt_ref.shape[1:])


def _outproj_call(y, x3, mod3, g_ffn, w_out_bf, w_router_t, b_router, ustrict, gb, tt):
    g, t, d = x3.shape
    n = g * t
    tm = gb * tt
    tpg = t // tt
    row = lambda i, j: (i * tpg + j, 0)
    col = lambda i, j: (0, i * tpg + j)
    mod_spec = lambda k: pl.BlockSpec((gb, 1, d), lambda i, j: (i, 0, k))
    const = lambda *shape: pl.BlockSpec(shape, lambda i, j: (0,) * len(shape))
    return pl.pallas_call(
        _outproj_kernel,
        grid=(g // gb, tpg),
        in_specs=[pl.BlockSpec((tm, d), row),
                  pl.BlockSpec((gb, tt, d), lambda i, j: (i, j, 0)),
                  mod_spec(2), mod_spec(3), mod_spec(4),
                  const(1, 1, d), const(d, d), const(N_EXPERTS, d), const(N_EXPERTS, 1),
                  const(tm, tm)],
        out_specs=[pl.BlockSpec((gb, tt, d), lambda i, j: (i, j, 0)),
                   pl.BlockSpec((tm, d), row),
                   pl.BlockSpec((TOP_K, tm), col),
                   pl.BlockSpec((TOP_K, tm), col),
                   pl.BlockSpec((TOP_K, tm), col),
                   pl.BlockSpec((1, N_EXPERTS, 128), lambda i, j: (i * tpg + j, 0, 0))],
        out_shape=[jax.ShapeDtypeStruct((g, t, d), F32),
                   jax.ShapeDtypeStruct((n, d), BF16),
                   jax.ShapeDtypeStruct((TOP_K, n), I32),
                   jax.ShapeDtypeStruct((TOP_K, n), F32),
                   jax.ShapeDtypeStruct((TOP_K, n), I32),
                   jax.ShapeDtypeStruct((n // tm, N_EXPERTS, 128), F32)],
        compiler_params=_params(2), name="outproj_router",
    )(y, x3, mod3, mod3, mod3, g_ffn.reshape(1, 1, d), w_out_bf, w_router_t, b_router.reshape(N_EXPERTS, 1),
      ustrict)


def _chunk_copy(hbm_ref, hbm_row, buf_ref, chunk, sem, to_hbm):
    hbm = hbm_ref.at[pl.ds(pl.multiple_of(hbm_row, RUN_ALIGN), RUN_ALIGN)]
    buf = buf_ref.at[pl.ds(pl.multiple_of(chunk * RUN_ALIGN, RUN_ALIGN), RUN_ALIGN)]
    return pltpu.make_async_copy(buf, hbm, sem) if to_hbm else pltpu.make_async_copy(hbm, buf, sem)


def _tile_chunks(hbm_ref, bufs_ref, starts_ref, nq_ref, sems, tile, to_hbm, wait):
    slot = tile % 2
    first = tile * (bufs_ref.shape[1] // RUN_ALIGN)

    def body(q, c):
        cp = _chunk_copy(hbm_ref, starts_ref[first + q], bufs_ref.at[slot], q, sems.at[slot], to_hbm)
        cp.wait() if wait else cp.start()
        return c

    lax.fori_loop(0, nq_ref[tile], body, 0)


def _dispatch_kernel(starts_ref, nq_ref, meta_ref, rbuf_ref, h2p_ref, h2s_ref, xs_ref, bufs_ref, zero_ref, sems,
                     *, prompt_tiles):
    i = pl.program_id(0)
    tm = h2p_ref.shape[0]
    bm = zero_ref.shape[0]
    n_blocks = xs_ref.shape[0] // bm
    sem = sems.at[0]
    buf_ref = bufs_ref.at[i % 2]

    @pl.when(i == 0)
    def _():
        zero_ref[...] = jnp.zeros(zero_ref.shape, zero_ref.dtype)

        def zero_copy(row):
            return pltpu.make_async_copy(zero_ref, xs_ref.at[pl.ds(pl.multiple_of(row, bm), bm)], sem)

        def tails(fn):
            def body(e, c):
                @pl.when(meta_ref[e] >= 0)
                def _():
                    fn(zero_copy(meta_ref[e]))
                return c
            lax.fori_loop(0, N_EXPERTS, body, 0)

        def unused(fn):
            def body(b, c):
                fn(zero_copy(b * bm))
                return c
            lax.fori_loop(meta_ref[N_EXPERTS], n_blocks, body, 0)

        tails(lambda cp: cp.start())
        unused(lambda cp: cp.start())
        tails(lambda cp: cp.wait())
        unused(lambda cp: cp.wait())

    def build(h2_ref):
        h2 = h2_ref[...]
        rb = rbuf_ref[...].astype(jnp.int16)
        for c in range(bufs_ref.shape[1] // PERM_ROWS):
            r = (lax.broadcasted_iota(I32, (PERM_ROWS, tm), 0) + c * PERM_ROWS).astype(jnp.int16)
            hit = (r == rb[0:1, :]) | (r == rb[1:2, :]) | (r == rb[2:3, :]) | (r == rb[3:4, :])
            onehot = jnp.where(hit, jnp.ones((), BF16), jnp.zeros((), BF16))
            buf_ref[c * PERM_ROWS:(c + 1) * PERM_ROWS, :] = _dot(onehot, h2).astype(BF16)

    @pl.when(i < prompt_tiles)
    def _():
        build(h2p_ref)

    @pl.when(i >= prompt_tiles)
    def _():
        build(h2s_ref)

    chunks = functools.partial(_tile_chunks, xs_ref, bufs_ref, starts_ref, nq_ref, sems, to_hbm=True)
    chunks(i, wait=False)

    @pl.when(i > 0)
    def _():
        chunks(i - 1, wait=True)

    @pl.when(i == pl.num_programs(0) - 1)
    def _():
        chunks(i, wait=True)


def _dispatch_call(starts, nq, meta, rbuf, h2_p, h2_s, cap):
    d = h2_p.shape[1]
    tm = MOE_TILE
    pt, st = h2_p.shape[0] // tm, h2_s.shape[0] // tm
    return pl.pallas_call(
        functools.partial(_dispatch_kernel, prompt_tiles=pt),
        grid_spec=pltpu.PrefetchScalarGridSpec(
            num_scalar_prefetch=3,
            grid=(pt + st,),
            in_specs=[pl.BlockSpec((TOP_K, tm), lambda i, *_: (0, i)),
                      pl.BlockSpec((tm, d), lambda i, *_: (jnp.minimum(i, pt - 1), 0)),
                      pl.BlockSpec((tm, d), lambda i, *_: (jnp.maximum(i - pt, 0), 0))],
            out_specs=pl.BlockSpec(memory_space=pl.ANY),
            scratch_shapes=[pltpu.VMEM((2, TILE_BUF_ROWS, d), BF16), pltpu.VMEM((EXPERT_BLOCK, d), BF16),
                            pltpu.SemaphoreType.DMA((2,))]),
        out_shape=jax.ShapeDtypeStruct((cap, d), BF16),
        compiler_params=_params(1), name="dispatch",
    )(starts, nq, meta, rbuf, h2_p, h2_s)


def _expert_kernel(be_ref, nu_ref, xs_ref, wup_ref, bup_ref, wdn_ref, bdn_ref, ys_ref, wup_bf, wdn_bf):
    i = pl.program_id(0)
    prev = be_ref[jnp.maximum(i - 1, 0)]

    @pl.when((i == 0) | (be_ref[i] != prev))
    def _():
        def cast(r, c):
            rows = pl.ds(pl.multiple_of(r * 64, 64), 64)
            wup_bf[rows, :] = wup_ref[0, rows, :].astype(BF16)
            wdn_bf[rows, :] = wdn_ref[0, rows, :].astype(BF16)
            return c

        lax.fori_loop(0, D_MODEL // 64, cast, 0)

    @pl.when(i < nu_ref[0])
    def _():
        hu = _dot(xs_ref[...], wup_bf[...]) + bup_ref[0]
        gate = jnp.minimum(hu[:, :D_FF], SWIGLU_LIMIT)
        lin = jnp.clip(hu[:, D_FF:], -SWIGLU_LIMIT, SWIGLU_LIMIT)
        glu = gate * _sigmoid(SWIGLU_ALPHA * gate)
        ys_ref[...] = (_dot(((lin + 1.0) * glu).astype(BF16), wdn_bf[...]) + bdn_ref[0]).astype(ys_ref.dtype)

    @pl.when(i >= nu_ref[0])
    def _():
        ys_ref[...] = jnp.zeros(ys_ref.shape, ys_ref.dtype)


def _expert_call(block_e, n_used, xs, w_up, b_up, w_down, b_down):
    cap, d = xs.shape
    bm = EXPERT_BLOCK
    blk = lambda i, be, nu: (jnp.minimum(i, nu[0] - 1), 0)
    per_e = lambda i, be, nu: (be[i], 0, 0)
    return pl.pallas_call(
        _expert_kernel,
        grid_spec=pltpu.PrefetchScalarGridSpec(
            num_scalar_prefetch=2,
            grid=(cap // bm,),
            in_specs=[pl.BlockSpec((bm, d), blk),
                      pl.BlockSpec((1, d, 2 * D_FF), per_e),
                      pl.BlockSpec((1, 1, 2 * D_FF), per_e),
                      pl.BlockSpec((1, D_FF, d), per_e),
                      pl.BlockSpec((1, 1, d), per_e)],
            out_specs=pl.BlockSpec((bm, d), lambda i, be, nu: (i, 0)),
            scratch_shapes=[pltpu.VMEM((d, 2 * D_FF), BF16), pltpu.VMEM((D_FF, d), BF16)]),
        out_shape=jax.ShapeDtypeStruct((cap, d), BF16),
        compiler_params=_params(1), name="experts",
    )(block_e, n_used, xs, w_up, b_up.reshape(N_EXPERTS, 1, -1), w_down, b_down.reshape(N_EXPERTS, 1, -1))


def _combine_kernel(starts_ref, nq_ref, x1_ref, gt_ref, gate_ref, rcol_ref, gfin_ref, ys_ref, o_ref, bufs_ref, sems,
                    *, tile0, tiles_per_group):
    x1 = x1_ref[...]
    gb, tt, d = x1.shape
    tm = gb * tt
    step = pl.program_id(0) * tiles_per_group + pl.program_id(1)
    n_steps = pl.num_programs(0) * tiles_per_group
    tile = tile0 + step
    chunks = functools.partial(_tile_chunks, ys_ref, bufs_ref, starts_ref, nq_ref, sems, to_hbm=False)

    @pl.when(step == 0)
    def _():
        bufs_ref[...] = jnp.zeros(bufs_ref.shape, bufs_ref.dtype)
        chunks(tile, wait=False)

    @pl.when(step + 1 < n_steps)
    def _():
        chunks(tile + 1, wait=False)

    chunks(tile, wait=True)
    buf_ref = bufs_ref.at[tile % 2]

    gate_hi, gate_lo = _split(gate_ref[...])
    rows = rcol_ref[...].astype(jnp.int16)
    zero = jnp.zeros((), BF16)
    moe = jnp.zeros((tm, d), F32)
    for c in range(bufs_ref.shape[1] // PERM_COLS):
        r = (lax.broadcasted_iota(I32, (tm, PERM_COLS), 1) + c * PERM_COLS).astype(jnp.int16)
        hits = [r == rows[:, k:k + 1] for k in range(TOP_K)]
        g_hi = sum(jnp.where(hits[k], gate_hi[:, k:k + 1], zero) for k in range(1, TOP_K)) + jnp.where(
            hits[0], gate_hi[:, 0:1], zero)
        g_lo = sum(jnp.where(hits[k], gate_lo[:, k:k + 1], zero) for k in range(1, TOP_K)) + jnp.where(
            hits[0], gate_lo[:, 0:1], zero)
        part = buf_ref[c * PERM_COLS:(c + 1) * PERM_COLS, :]
        both = _dot(jnp.concatenate([g_hi, g_lo], axis=0), part)
        moe = moe + (both[:tm] + both[tm:])
    xo = x1 + gt_ref[...] * moe.reshape(gb, tt, d)
    o_ref[...] = _rms(xo, gfin_ref[...])


def _combine_call(starts, nq, x1, mod3, gates_t, rcol, g_final, ys, tok0, gb, tt):
    g, t, d = x1.shape
    tm = gb * tt
    tpg = t // tt
    t0 = tok0 // tm
    tok = lambda i, j, *_: (t0 + i * tpg + j, 0)
    return pl.pallas_call(
        functools.partial(_combine_kernel, tile0=t0, tiles_per_group=tpg),
        grid_spec=pltpu.PrefetchScalarGridSpec(
            num_scalar_prefetch=2,
            grid=(g // gb, tpg),
            in_specs=[pl.BlockSpec((gb, tt, d), lambda i, j, *_: (i, j, 0)),
                      pl.BlockSpec((gb, 1, d), lambda i, j, *_: (i, 0, 5)),
                      pl.BlockSpec((tm, TOP_K), tok),
                      pl.BlockSpec((tm, TOP_K), tok),
                      pl.BlockSpec((1, 1, d), lambda i, j, *_: (0, 0, 0)),
                      pl.BlockSpec(memory_space=pl.ANY)],
            out_specs=pl.BlockSpec((gb, tt, d), lambda i, j, *_: (i, j, 0)),
            scratch_shapes=[pltpu.VMEM((2, TILE_BUF_ROWS, d), BF16), pltpu.SemaphoreType.DMA((2,))]),
        out_shape=jax.ShapeDtypeStruct((g, t, d), F32),
        compiler_params=_params(2), name="combine",
    )(starts, nq, x1, mod3, gates_t, rcol, g_final.reshape(1, 1, d), ys)


def _group_blocking(groups, seq, tile):
    if seq >= tile:
        return 1, tile
    return tile // seq, seq


def kernel(x_prompt, x_sample, c_prompt, c_sample, state_ret, state_mlstm_c, state_mlstm_n, state_mlstm_m, state_conv, w_mod, b_mod, g_mix, g_ffn, w_in, b_igate, b_fgate, w_conv, b_conv, g_ret, g_mlstm, w_out, w_router, b_router, w_up, b_up, w_down, b_down, g_final):
    depth = w_mod.shape[0]
    assert depth == 1, "single-layer trunk"
    bp, tp, d = x_prompt.shape
    bs, ts, _ = x_sample.shape
    n_p, n_s = bp * tp, bs * ts
    hd = HEAD_DIM
    l = 0

    mod = _mod_call(jnp.concatenate([c_prompt, c_sample], axis=0), w_mod[l], b_mod[l])
    mod_p = mod[:bp].reshape(bp, 1, N_MOD * d)
    mod_s = mod[bp:].reshape(bs, 1, N_MOD * d)

    w_main_bf = w_in[l][:, :MAIN_COLS].astype(BF16)
    w_gate = jnp.pad(w_in[l][:, MAIN_COLS:], ((0, 0), (0, 128 - N_GATES)))
    w_gate_t = w_in[l][:, MAIN_COLS:].T
    w_out_bf = w_out[l].astype(BF16)
    bg_col = jnp.concatenate([b_igate[l], b_fgate[l]]).reshape(1, N_GATES)
    bg_row = bg_col.reshape(N_GATES, 1)

    groups = (
        (x_prompt, mod_p, min(CHUNK, tp), np.arange(tp),
         jnp.zeros((bp, RET_HEADS, hd, hd), F32), jnp.zeros((bp, MLSTM_HEADS, hd, hd), F32),
         jnp.zeros((bp, MLSTM_HEADS, hd), F32), jnp.zeros((bp, MLSTM_HEADS), F32),
         jnp.zeros((bp, CONV_WIDTH - 1, 2 * MLSTM_WIDTH), F32)),
        (x_sample, mod_s, min(CHUNK, ts), PAST_LEN + np.arange(ts),
         state_ret[l], state_mlstm_c[l], state_mlstm_n[l], state_mlstm_m[l], state_conv[l]),
    )

    ustrict = jnp.asarray(np.arange(MOE_TILE)[:, None] < np.arange(MOE_TILE)[None, :], BF16)
    staged = []
    for x3, mod3, L, pos, s0, c0, n0, m0, conv0 in groups:
        g, t, _ = x3.shape
        conv0p = jnp.pad(conv0.astype(F32), ((0, 0), (CONV_TAIL_ROWS - (CONV_WIDTH - 1), 0), (0, 0)))
        y, s_new, c_new, n_new, m_new, tail = _mixer_call(
            x3, mod3, g_mix[l], w_main_bf, w_gate, w_gate_t, _mixer_consts(L, pos),
            w_conv[l], b_conv[l], g_ret[l], g_mlstm[l],
            bg_col, bg_row, s0.astype(F32), c0.astype(F32), n0.astype(F32).reshape(g, MLSTM_HEADS, 1, hd),
            jnp.broadcast_to(m0.astype(F32)[:, :, None, None], (g, MLSTM_HEADS, 1, hd)), conv0p,
            L, MIXER_SEQS_CHUNKED if t > L else MIXER_SEQS_SHORT, F32)
        states = (s_new, c_new, n_new.reshape(g, MLSTM_HEADS, hd), m_new[:, :, 0, 0],
                  tail[:, CONV_TAIL_ROWS - (CONV_WIDTH - 1):, :])
        gb, tt = _group_blocking(g, t, MOE_TILE)
        x1, h2, idx, gates, rank, cnt = _outproj_call(
            y.reshape(g * t, d), x3, mod3, g_ffn[l], w_out_bf, w_router[l].T, b_router[l], ustrict, gb, tt)
        staged.append((x1, mod3, h2, idx, gates, rank, states, cnt))

    n_tok = n_p + n_s
    tm, bm, ra = MOE_TILE, EXPERT_BLOCK, RUN_ALIGN
    n_tiles = n_tok // tm
    q_max = TILE_BUF_ROWS // ra
    n_blocks = -(-(n_tok * TOP_K + n_tiles * N_EXPERTS * (ra - 1)) // bm) + N_EXPERTS
    cap = n_blocks * bm
    counts = jnp.concatenate([s[7][:, :, 0] for s in staged], axis=0).astype(I32)
    run = (counts + ra - 1) // ra * ra
    region = jnp.sum(run, axis=0)
    padded = (region + bm - 1) // bm * bm
    pad_end = jnp.cumsum(padded)
    pad_start = pad_end - padded
    run_start = pad_start[None, :] + jnp.cumsum(run, axis=0) - run
    buf_end = jnp.cumsum(run, axis=1)
    buf_start = buf_end - run
    nq = (buf_end[:, -1] // ra).astype(I32)
    chunk_row = jnp.arange(q_max, dtype=I32) * ra
    chunk_e = jnp.minimum(jnp.sum((buf_end[:, None, :] <= chunk_row[None, :, None]).astype(I32), axis=2),
                          N_EXPERTS - 1)
    e_ids = jnp.arange(N_EXPERTS, dtype=I32)
    shift = run_start - buf_start
    starts = jnp.sum(jnp.where(chunk_e[:, :, None] == e_ids, shift[:, None, :], 0), axis=2) + chunk_row[None, :]
    starts = jnp.where(chunk_row[None, :] < buf_end[:, -1:], starts, 0).reshape(-1).astype(I32)

    idx_all = jnp.concatenate([s[3] for s in staged], axis=1)
    rank_all = jnp.concatenate([s[5] for s in staged], axis=1)
    gates_t = jnp.concatenate([s[4] for s in staged], axis=1).T
    buf_start_tok = jnp.repeat(buf_start, tm, axis=0).T
    rbuf = jnp.sum(jnp.where(idx_all[None] == e_ids[:, None, None], buf_start_tok[:, None, :], 0), axis=0) + rank_all
    rcol = rbuf.T
    block_row = jnp.arange(n_blocks, dtype=I32) * bm
    block_e = jnp.minimum(jnp.sum((pad_end[None, :] <= block_row[:, None]).astype(I32), axis=1), N_EXPERTS - 1)
    n_used = (pad_end[-1:] // bm).astype(I32)
    meta = jnp.concatenate([jnp.where(region > 0, pad_end - bm, -1), n_used]).astype(I32)

    xs = _dispatch_call(starts, nq, meta, rbuf, staged[0][2], staged[1][2], cap)
    ys = _expert_call(block_e, n_used, xs, w_up[l], b_up[l], w_down[l], b_down[l])

    outs = []
    tok0 = 0
    for x1, mod3, *_ in staged:
        g, t, _ = x1.shape
        gb, tt = _group_blocking(g, t, tm)
        outs.append(_combine_call(starts, nq, x1, mod3, gates_t, rcol, g_final, ys, tok0, gb, tt))
        tok0 += g * t

    st_p, st_s = staged[0][6], staged[1][6]
    return (outs[0], outs[1]) + tuple(a[None] for a in st_p) + tuple(a[None] for a in st_s)
```

```python
import functools

import numpy as np
import jax
import jax.numpy as jnp
from jax import lax
from jax.experimental import pallas as pl
from jax.experimental.pallas import tpu as pltpu

F32 = jnp.float32
BF16 = jnp.bfloat16
I32 = jnp.int32

D_MODEL = 1024
PAST_LEN = 16384
RET_HEADS = 4
MLSTM_HEADS = 4
HEAD_DIM = 128
RET_WIDTH = RET_HEADS * HEAD_DIM
MLSTM_WIDTH = MLSTM_HEADS * HEAD_DIM
CONV_WIDTH = 4
CHUNK = 128
ROPE_BASE = 10000.0
N_EXPERTS = 32
TOP_K = 4
D_FF = D_MODEL
SWIGLU_LIMIT = 7.0
SWIGLU_ALPHA = 1.702
N_MOD = 6
EPS = 1e-6
MAIN_COLS = 4 * RET_WIDTH + 2 * MLSTM_WIDTH + 2 * MLSTM_WIDTH
N_GATES = 2 * MLSTM_HEADS
OFF_RQ, OFF_RK, OFF_RV, OFF_RG = 0, RET_WIDTH, 2 * RET_WIDTH, 3 * RET_WIDTH
OFF_MQK = 4 * RET_WIDTH
OFF_MV = OFF_MQK + 2 * MLSTM_WIDTH
OFF_MO = OFF_MV + MLSTM_WIDTH

VMEM_LIMIT_BYTES = 56 * 1024 * 1024
MOE_TILE = 512
EXPERT_BLOCK = 512
MIXER_SEQS_CHUNKED = 2
MIXER_SEQS_SHORT = 8
RUN_ALIGN = 16
SIDE_ROWS = 4 * TOP_K
SIDE_LANES = 128
TILE_BUF_ROWS = MOE_TILE * TOP_K + N_EXPERTS * RUN_ALIGN
PERM_ROWS = 1280
PERM_COLS = 512
CONV_TAIL_ROWS = 8


def _params(n_axes=1):
    return pltpu.CompilerParams(dimension_semantics=("arbitrary",) * n_axes, vmem_limit_bytes=VMEM_LIMIT_BYTES)


def _dot(a, b):
    return jnp.dot(a, b, preferred_element_type=F32)


def _dot_nt(a, b):
    return lax.dot_general(a, b, (((1,), (1,)), ((), ())), preferred_element_type=F32)


def _dot_tn(a, b):
    return lax.dot_general(a, b, (((0,), (0,)), ((), ())), preferred_element_type=F32)


def _split(a):
    hi = a.astype(BF16)
    lo = (a - hi.astype(F32)).astype(BF16)
    return hi, lo


def _dot3(a, b, dot=_dot):
    ah, al = _split(a)
    bh, bl = _split(b)
    return dot(ah, bh) + (dot(al, bh) + dot(ah, bl))


def _sigmoid(x):
    return 0.5 * (jnp.tanh(0.5 * x) + 1.0)


def _log_sigmoid(x):
    return jnp.minimum(x, 0.0) - jnp.log1p(jnp.exp(-jnp.abs(x)))


def _rms(x, g):
    ms = jnp.mean(x * x, axis=-1, keepdims=True)
    return (x * lax.rsqrt(ms + EPS)) * g


def _layer_norm(x, g):
    mu = jnp.mean(x, axis=-1, keepdims=True)
    xc = x - mu
    var = jnp.mean(xc * xc, axis=-1, keepdims=True)
    return xc * lax.rsqrt(var + EPS) * g


def _mod_kernel(c_ref, w_ref, b_ref, o_ref):
    c = c_ref[...]
    o_ref[...] = _dot3(c * _sigmoid(c), w_ref[...]) + b_ref[...]


def _mod_call(c_all, w_mod, b_mod):
    rows, d = c_all.shape
    cols = w_mod.shape[1]
    tn = 1024
    return pl.pallas_call(
        _mod_kernel,
        grid=(cols // tn,),
        in_specs=[pl.BlockSpec((rows, d), lambda j: (0, 0)),
                  pl.BlockSpec((d, tn), lambda j: (0, j)),
                  pl.BlockSpec((1, tn), lambda j: (0, j))],
        out_specs=pl.BlockSpec((rows, tn), lambda j: (0, j)),
        out_shape=jax.ShapeDtypeStruct((rows, cols), F32),
        compiler_params=_params(1), name="mod",
    )(c_all, w_mod, b_mod.reshape(1, cols))


def _mixer_kernel(x_ref, sh_ref, sc_ref, gmix_ref, w_ref, wg_ref, wgt_ref,
                  cos_ref, sin_ref, dec_ref, qd_ref, kd_ref, cd_ref,
                  tril_ref, triu_ref, wconv_ref, bconv_ref, gret_ref, gml_ref, bgc_ref, bgr_ref,
                  s0_ref, c0_ref, n0_ref, m0_ref, conv0_ref,
                  y_ref, s_ref, c_ref, n_ref, m_ref, tail_ref, p_ref, xp_ref):
    @pl.when(pl.program_id(1) == 0)
    def _():
        s_ref[...] = s0_ref[...]
        c_ref[...] = c0_ref[...]
        n_ref[...] = n0_ref[...]
        m_ref[...] = m0_ref[...]
        tail_ref[...] = conv0_ref[...]

    x = x_ref[...]
    gs, L, d = x.shape
    h = (_rms(x, gmix_ref[...]) * (1.0 + sc_ref[...]) + sh_ref[...]).reshape(gs * L, d)
    hb = h.astype(BF16)
    for j in range(MAIN_COLS // 1024):
        p_ref[:, j * 1024:(j + 1) * 1024] = _dot(hb, w_ref[:, j * 1024:(j + 1) * 1024])

    chains = []
    for sq in range(gs):
        h_sq = h[sq * L:(sq + 1) * L, :]
        gc = _dot3(h_sq, wg_ref[...])[:, :N_GATES]
        gr = _dot3(wgt_ref[...], h_sq, dot=_dot_nt)
        chains += _mixer_sequence(p_ref.at[pl.ds(sq * L, L)], gc, gr, cos_ref, sin_ref, dec_ref, qd_ref, kd_ref,
                                  cd_ref, tril_ref, triu_ref, wconv_ref, bconv_ref, gret_ref, gml_ref, bgc_ref,
                                  bgr_ref, y_ref.at[sq], s_ref.at[sq], c_ref.at[sq], n_ref.at[sq], m_ref.at[sq],
                                  tail_ref.at[sq], xp_ref.at[sq])
    while chains:
        alive = []
        for chain in chains:
            if next(chain, None) is not None:
                alive.append(chain)
        chains = alive


def _mixer_sequence(p_ref, gc, gr, cos_ref, sin_ref, dec_ref, qd_ref, kd_ref, cd_ref,
                    tril_ref, triu_ref, wconv_ref, bconv_ref, gret_ref, gml_ref, bgc_ref, bgr_ref,
                    y_ref, s_ref, c_ref, n_ref, m_ref, tail_ref, xp_ref):
    L = p_ref.shape[0]
    cos = cos_ref[...]
    sin = sin_ref[...]
    scale = HEAD_DIM ** -0.5

    def rot(x):
        return x * cos + pltpu.roll(x, HEAD_DIM // 2, axis=1) * sin

    def retention_head(h):
        lo = h * HEAD_DIM
        q = rot(p_ref[:, OFF_RQ + lo:OFF_RQ + lo + HEAD_DIM])
        k = rot(p_ref[:, OFF_RK + lo:OFF_RK + lo + HEAD_DIM]) * scale
        v = p_ref[:, OFF_RV + lo:OFF_RV + lo + HEAD_DIM].astype(BF16)
        s_old = s_ref[h]
        yield True
        scores = _dot_nt(q.astype(BF16), k.astype(BF16)) * dec_ref[h]
        yield True
        out = _dot(scores.astype(BF16), v) + _dot((q * qd_ref[h]).astype(BF16), s_old.astype(BF16))
        s_ref[h] = cd_ref[h] * s_old + _dot_tn((k * kd_ref[h]).astype(BF16), v)
        yield True
        g = p_ref[:, OFF_RG + lo:OFF_RG + lo + HEAD_DIM]
        y_ref[:, lo:lo + HEAD_DIM] = ((g * _sigmoid(g)) * _layer_norm(out, gret_ref[:, lo:lo + HEAD_DIM])).astype(y_ref.dtype)

    xp_ref[0:CONV_TAIL_ROWS, :] = tail_ref[...]
    xp_ref[CONV_TAIL_ROWS:CONV_TAIL_ROWS + L, :] = p_ref[:, OFF_MQK:OFF_MQK + 2 * MLSTM_WIDTH]
    acc = bconv_ref[...] + wconv_ref[0:1, :] * xp_ref[CONV_TAIL_ROWS - 3:CONV_TAIL_ROWS - 3 + L, :]
    for j in range(1, CONV_WIDTH):
        acc = acc + wconv_ref[j:j + 1, :] * xp_ref[CONV_TAIL_ROWS - 3 + j:CONV_TAIL_ROWS - 3 + j + L, :]
    tail_ref[...] = xp_ref[L:L + CONV_TAIL_ROWS, :]
    xp_ref[CONV_TAIL_ROWS:CONV_TAIL_ROWS + L, :] = acc * _sigmoid(acc)

    gcol = gc + bgc_ref[...]
    is_f_col = lax.broadcasted_iota(I32, gcol.shape, 1) >= MLSTM_HEADS
    gcol = jnp.where(is_f_col, _log_sigmoid(gcol), gcol)
    grow = gr + bgr_ref[...]
    is_f_row = lax.broadcasted_iota(I32, grow.shape, 0) >= MLSTM_HEADS
    grow = jnp.where(is_f_row, _log_sigmoid(grow), grow)
    bcol_all = _dot3(tril_ref[...], gcol)
    brow_all = _dot3(grow, triu_ref[...])
    causal = lax.broadcasted_iota(I32, (L, L), 0) >= lax.broadcasted_iota(I32, (L, L), 1)

    def mlstm_head(h):
        lo = h * HEAD_DIM
        q = xp_ref[CONV_TAIL_ROWS:CONV_TAIL_ROWS + L, lo:lo + HEAD_DIM]
        k = xp_ref[CONV_TAIL_ROWS:CONV_TAIL_ROWS + L, MLSTM_WIDTH + lo:MLSTM_WIDTH + lo + HEAD_DIM] * scale
        v = p_ref[:, OFF_MV + lo:OFF_MV + lo + HEAD_DIM].astype(BF16)
        ic_col = gcol[:, h:h + 1]
        ic_row = grow[h:h + 1, :]
        b_col = bcol_all[:, MLSTM_HEADS + h:MLSTM_HEADS + h + 1]
        b_row = brow_all[MLSTM_HEADS + h:MLSTM_HEADS + h + 1, :]
        c_old = c_ref[h]
        n_old = n_ref[h]
        m_old = m_ref[h][:, 0:1]

        d_log = jnp.where(causal, b_col - b_row + ic_row, -jnp.inf)
        inter = b_col + m_old
        m_t = jnp.maximum(inter, jnp.max(d_log, axis=1, keepdims=True))
        yield True
        w_intra = jnp.exp(d_log - m_t)
        w_inter = jnp.exp(inter - m_t)
        qb = q.astype(BF16)
        s = _dot_nt(qb, k.astype(BF16)) * w_intra
        yield True
        num = _dot(s.astype(BF16), v) + w_inter * _dot(qb, c_old.astype(BF16))
        den = jnp.sum(s, axis=1, keepdims=True) + w_inter * jnp.sum(q * n_old, axis=1, keepdims=True)
        yield True
        hh = num / jnp.maximum(jnp.abs(den), jnp.exp(-m_t))

        b_last = b_col[L - 1:L, :]
        w_log_col = b_last - b_col + ic_col
        m_new = jnp.maximum(b_last + m_old, jnp.max(w_log_col, axis=0, keepdims=True))
        wk = jnp.exp(w_log_col - m_new) * k
        cdec = jnp.exp(b_last + m_old - m_new)
        yield True
        c_ref[h] = cdec * c_old + _dot_tn(wk.astype(BF16), v)
        n_ref[h] = cdec * n_old + jnp.sum(wk, axis=0, keepdims=True)
        m_ref[h] = jnp.broadcast_to(m_new, (1, HEAD_DIM))
        yield True
        o = p_ref[:, OFF_MO + lo:OFF_MO + lo + HEAD_DIM]
        y_ref[:, RET_WIDTH + lo:RET_WIDTH + lo + HEAD_DIM] = (
            _sigmoid(o) * _layer_norm(hh, gml_ref[:, lo:lo + HEAD_DIM])).astype(y_ref.dtype)

    return [retention_head(h) for h in range(RET_HEADS)] + [mlstm_head(h) for h in range(MLSTM_HEADS)]


def _mixer_consts(L, pos):
    f32 = np.float32
    half = HEAD_DIM // 2
    inv_freq = np.power(f32(ROPE_BASE), -np.arange(half, dtype=f32) / f32(half)).astype(f32)
    ang = (pos.astype(f32)[:, None] * inv_freq[None, :]).astype(f32)
    cos = np.concatenate([np.cos(ang), np.cos(ang)], axis=-1).astype(f32)
    sin = np.concatenate([-np.sin(ang), np.sin(ang)], axis=-1).astype(f32)
    log_gamma = np.log1p(-np.exp2(-5.0 - np.arange(RET_HEADS, dtype=np.float64)))
    idx = np.arange(L, dtype=np.float64)
    rel = idx[:, None] - idx[None, :]
    dec = np.where(rel >= 0, np.exp(log_gamma[:, None, None] * np.maximum(rel, 0.0)), 0.0)
    qd = np.broadcast_to(np.exp(log_gamma[:, None] * (idx + 1.0))[..., None], (RET_HEADS, L, HEAD_DIM))
    kd = np.broadcast_to(np.exp(log_gamma[:, None] * (L - 1.0 - idx))[..., None], (RET_HEADS, L, HEAD_DIM))
    cd = np.broadcast_to(np.exp(log_gamma * L)[:, None, None], (RET_HEADS, 1, HEAD_DIM))
    tril = rel >= 0
    triu = rel <= 0
    return tuple(jnp.asarray(a, F32) for a in (cos, sin, dec, qd, kd, cd, tril, triu))


def _mixer_call(x3, mod3, g_mix, w_main_bf, w_gate, w_gate_t, consts, w_conv, b_conv, g_ret, g_mlstm, bg_col, bg_row,
                s0, c0, n0, m0, conv0, L, gs, y_dtype):
    cos, sin, dec, qd, kd, cd, tril, triu = consts
    groups, seq, d = x3.shape
    nc = seq // L
    hd = HEAD_DIM
    full = lambda *shape: pl.BlockSpec(shape, lambda g, c: (0,) * len(shape))
    state4 = pl.BlockSpec((gs, RET_HEADS, hd, hd), lambda g, c: (g, 0, 0, 0))
    vec4 = pl.BlockSpec((gs, MLSTM_HEADS, 1, hd), lambda g, c: (g, 0, 0, 0))
    tail3 = pl.BlockSpec((gs, CONV_TAIL_ROWS, 2 * MLSTM_WIDTH), lambda g, c: (g, 0, 0))
    row = lambda g, c: (g, c, 0)
    return pl.pallas_call(
        _mixer_kernel,
        grid=(groups // gs, nc),
        in_specs=[pl.BlockSpec((gs, L, d), row),
                  pl.BlockSpec((gs, 1, d), lambda g, c: (g, 0, 0)),
                  pl.BlockSpec((gs, 1, d), lambda g, c: (g, 0, 1)),
                  full(1, 1, d), full(d, MAIN_COLS), full(d, 128), full(N_GATES, d),
                  pl.BlockSpec((L, hd), lambda g, c: (c, 0)),
                  pl.BlockSpec((L, hd), lambda g, c: (c, 0)),
                  full(RET_HEADS, L, L), full(RET_HEADS, L, hd), full(RET_HEADS, L, hd), full(RET_HEADS, 1, hd),
                  full(L, L), full(L, L),
                  full(CONV_WIDTH, 2 * MLSTM_WIDTH), full(1, 2 * MLSTM_WIDTH),
                  full(1, RET_WIDTH), full(1, MLSTM_WIDTH), full(1, N_GATES), full(N_GATES, 1),
                  state4, state4, vec4, vec4, tail3],
        out_specs=[pl.BlockSpec((gs, L, RET_WIDTH + MLSTM_WIDTH), row),
                   state4, state4, vec4, vec4, tail3],
        out_shape=[jax.ShapeDtypeStruct((groups, seq, RET_WIDTH + MLSTM_WIDTH), y_dtype),
                   jax.ShapeDtypeStruct((groups, RET_HEADS, hd, hd), F32),
                   jax.ShapeDtypeStruct((groups, MLSTM_HEADS, hd, hd), F32),
                   jax.ShapeDtypeStruct((groups, MLSTM_HEADS, 1, hd), F32),
                   jax.ShapeDtypeStruct((groups, MLSTM_HEADS, 1, hd), F32),
                   jax.ShapeDtypeStruct((groups, CONV_TAIL_ROWS, 2 * MLSTM_WIDTH), F32)],
        scratch_shapes=[pltpu.VMEM((gs * L, MAIN_COLS), F32),
                        pltpu.VMEM((gs, CONV_TAIL_ROWS + L, 2 * MLSTM_WIDTH), F32)],
        compiler_params=_params(2), name="mixer",
    )(x3, mod3, mod3, g_mix.reshape(1, 1, d), w_main_bf, w_gate, w_gate_t,
      cos, sin, dec, qd, kd, cd, tril, triu, w_conv, b_conv.reshape(1, -1),
      g_ret.reshape(1, -1), g_mlstm.reshape(1, -1), bg_col, bg_row, s0, c0, n0, m0, conv0)


def _outproj_kernel(y_ref, x_ref, gt_ref, sh_ref, sc_ref, g_ref, w_ref, wrt_ref, br_ref, ustrict_ref,
                    x1_ref, h2_ref, idx_ref, rank_ref, cnt_ref):
    x = x_ref[...]
    gb, tt, d = x.shape
    tm = gb * tt
    mixed = _dot(y_ref[...].astype(BF16), w_ref[...])
    x1 = x + gt_ref[...] * mixed.reshape(gb, tt, d)
    x1_ref[...] = x1
    h2 = (_rms(x1, g_ref[...]) * (1.0 + sc_ref[...]) + sh_ref[...]).reshape(tm, d)
    h2_ref[:, :d] = h2.astype(BF16)

    work = _dot3(wrt_ref[...], h2, dot=_dot_nt) + br_ref[...]
    e_iota = lax.broadcasted_iota(I32, work.shape, 0).astype(F32)
    vals, idxs, sels = [], [], []
    for _ in range(TOP_K):
        mx = jnp.max(work, axis=0, keepdims=True)
        ik = jnp.min(jnp.where(work == mx, e_iota, float(N_EXPERTS)), axis=0, keepdims=True)
        sel = e_iota == ik
        vals.append(mx)
        idxs.append(ik)
        sels.append(sel)
        work = jnp.where(sel, -jnp.inf, work)
    exps = [jnp.exp(v - vals[0]) for v in vals]
    denom = exps[0] + exps[1] + exps[2] + exps[3]
    gates = [e / denom for e in exps]
    idx_ref[...] = jnp.concatenate(idxs, axis=0).astype(I32)

    pieces = []
    for gk in gates:
        p1 = gk.astype(BF16)
        r1 = gk - p1.astype(F32)
        p2 = r1.astype(BF16)
        pieces += [p1, p2, (r1 - p2.astype(F32)).astype(BF16)]
    side = jnp.concatenate(pieces + [ik.astype(BF16) for ik in idxs], axis=0)
    eye = (lax.broadcasted_iota(I32, (SIDE_ROWS, SIDE_LANES), 0)
           == lax.broadcasted_iota(I32, (SIDE_ROWS, SIDE_LANES), 1)).astype(BF16)
    h2_ref[:, d:] = _dot_tn(side, eye).astype(BF16)

    mask = (sels[0] | sels[1] | sels[2] | sels[3]).astype(F32)
    before = _dot(mask.astype(BF16), ustrict_ref[...])
    ranks = [jnp.sum(jnp.where(sel, before, 0.0), axis=0, keepdims=True) for sel in sels]
    rank_ref[...] = jnp.concatenate(ranks, axis=0).astype(I32)
    cnt_ref[0] = jnp.broadcast_to(jnp.sum(mask, axis=1, keepdims=True), cnt_ref.shape[1:])


def _outproj_call(y, x3, mod3, g_ffn, w_out_bf, w_router_t, b_router, ustrict, gb, tt):
    g, t, d = x3.shape
    n = g * t
    tm = gb * tt
    tpg = t // tt
    row = lambda i, j: (i * tpg + j, 0)
    col = lambda i, j: (0, i * tpg + j)
    mod_spec = lambda k: pl.BlockSpec((gb, 1, d), lambda i, j: (i, 0, k))
    const = lambda *shape: pl.BlockSpec(shape, lambda i, j: (0,) * len(shape))
    return pl.pallas_call(
        _outproj_kernel,
        grid=(g // gb, tpg),
        in_specs=[pl.BlockSpec((tm, d), row),
                  pl.BlockSpec((gb, tt, d), lambda i, j: (i, j, 0)),
                  mod_spec(2), mod_spec(3), mod_spec(4),
                  const(1, 1, d), const(d, d), const(N_EXPERTS, d), const(N_EXPERTS, 1),
                  const(tm, tm)],
        out_specs=[pl.BlockSpec((gb, tt, d), lambda i, j: (i, j, 0)),
                   pl.BlockSpec((tm, d + SIDE_LANES), row),
                   pl.BlockSpec((TOP_K, tm), col),
                   pl.BlockSpec((TOP_K, tm), col),
                   pl.BlockSpec((1, N_EXPERTS, 128), lambda i, j: (i * tpg + j, 0, 0))],
        out_shape=[jax.ShapeDtypeStruct((g, t, d), F32),
                   jax.ShapeDtypeStruct((n, d + SIDE_LANES), BF16),
                   jax.ShapeDtypeStruct((TOP_K, n), I32),
                   jax.ShapeDtypeStruct((TOP_K, n), I32),
                   jax.ShapeDtypeStruct((n // tm, N_EXPERTS, 128), F32)],
        compiler_params=_params(2), name="outproj_router",
    )(y, x3, mod3, mod3, mod3, g_ffn.reshape(1, 1, d), w_out_bf, w_router_t, b_router.reshape(N_EXPERTS, 1),
      ustrict)


def _chunk_copy(hbm_ref, hbm_row, buf_ref, chunk, sem, to_hbm):
    hbm = hbm_ref.at[pl.ds(pl.multiple_of(hbm_row, RUN_ALIGN), RUN_ALIGN)]
    buf = buf_ref.at[pl.ds(pl.multiple_of(chunk * RUN_ALIGN, RUN_ALIGN), RUN_ALIGN)]
    return pltpu.make_async_copy(buf, hbm, sem) if to_hbm else pltpu.make_async_copy(hbm, buf, sem)


def _tile_chunks(hbm_ref, bufs_ref, starts_ref, nq_ref, sems, tile, to_hbm, wait):
    slot = tile % 2
    first = tile * (bufs_ref.shape[1] // RUN_ALIGN)

    def body(q, c):
        cp = _chunk_copy(hbm_ref, starts_ref[first + q], bufs_ref.at[slot], q, sems.at[slot], to_hbm)
        cp.wait() if wait else cp.start()
        return c

    lax.fori_loop(0, nq_ref[tile], body, 0)


def _dispatch_kernel(starts_ref, nq_ref, meta_ref, rbuf_ref, h2p_ref, h2s_ref, xs_ref, bufs_ref, zero_ref, sems,
                     *, prompt_tiles):
    i = pl.program_id(0)
    tm = h2p_ref.shape[0]
    bm = zero_ref.shape[0]
    n_blocks = xs_ref.shape[0] // bm
    sem = sems.at[0]
    buf_ref = bufs_ref.at[i % 2]

    @pl.when(i == 0)
    def _():
        zero_ref[...] = jnp.zeros(zero_ref.shape, zero_ref.dtype)

        def zero_copy(row):
            return pltpu.make_async_copy(zero_ref, xs_ref.at[pl.ds(pl.multiple_of(row, bm), bm)], sem)

        def tails(fn):
            def body(e, c):
                @pl.when(meta_ref[e] >= 0)
                def _():
                    fn(zero_copy(meta_ref[e]))
                return c
            lax.fori_loop(0, N_EXPERTS, body, 0)

        def unused(fn):
            def body(b, c):
                fn(zero_copy(b * bm))
                return c
            lax.fori_loop(meta_ref[N_EXPERTS], n_blocks, body, 0)

        tails(lambda cp: cp.start())
        unused(lambda cp: cp.start())
        tails(lambda cp: cp.wait())
        unused(lambda cp: cp.wait())

    def build(h2_ref):
        h2 = h2_ref[...]
        rb = rbuf_ref[...].astype(jnp.int16)
        for c in range(bufs_ref.shape[1] // PERM_ROWS):
            r = (lax.broadcasted_iota(I32, (PERM_ROWS, tm), 0) + c * PERM_ROWS).astype(jnp.int16)
            hit = (r == rb[0:1, :]) | (r == rb[1:2, :]) | (r == rb[2:3, :]) | (r == rb[3:4, :])
            onehot = jnp.where(hit, jnp.ones((), BF16), jnp.zeros((), BF16))
            buf_ref[c * PERM_ROWS:(c + 1) * PERM_ROWS, :] = _dot(onehot, h2).astype(BF16)

    @pl.when(i < prompt_tiles)
    def _():
        build(h2p_ref)

    @pl.when(i >= prompt_tiles)
    def _():
        build(h2s_ref)

    chunks = functools.partial(_tile_chunks, xs_ref, bufs_ref, starts_ref, nq_ref, sems, to_hbm=True)
    chunks(i, wait=False)

    @pl.when(i > 0)
    def _():
        chunks(i - 1, wait=True)

    @pl.when(i == pl.num_programs(0) - 1)
    def _():
        chunks(i, wait=True)


def _dispatch_call(starts, nq, meta, rbuf, h2_p, h2_s, cap):
    d = h2_p.shape[1]
    tm = MOE_TILE
    pt, st = h2_p.shape[0] // tm, h2_s.shape[0] // tm
    return pl.pallas_call(
        functools.partial(_dispatch_kernel, prompt_tiles=pt),
        grid_spec=pltpu.PrefetchScalarGridSpec(
            num_scalar_prefetch=3,
            grid=(pt + st,),
            in_specs=[pl.BlockSpec((TOP_K, tm), lambda i, *_: (0, i)),
                      pl.BlockSpec((tm, d), lambda i, *_: (jnp.minimum(i, pt - 1), 0)),
                      pl.BlockSpec((tm, d), lambda i, *_: (jnp.maximum(i - pt, 0), 0))],
            out_specs=pl.BlockSpec(memory_space=pl.ANY),
            scratch_shapes=[pltpu.VMEM((2, TILE_BUF_ROWS, d), BF16), pltpu.VMEM((EXPERT_BLOCK, d), BF16),
                            pltpu.SemaphoreType.DMA((2,))]),
        out_shape=jax.ShapeDtypeStruct((cap, d), BF16),
        compiler_params=_params(1), name="dispatch",
    )(starts, nq, meta, rbuf, h2_p, h2_s)


def _expert_kernel(be_ref, nu_ref, xs_ref, wup_ref, bup_ref, wdn_ref, bdn_ref, ys_ref, wup_bf, wdn_bf):
    i = pl.program_id(0)
    prev = be_ref[jnp.maximum(i - 1, 0)]

    @pl.when((i == 0) | (be_ref[i] != prev))
    def _():
        def cast(r, c):
            rows = pl.ds(pl.multiple_of(r * 64, 64), 64)
            wup_bf[rows, :] = wup_ref[0, rows, :].astype(BF16)
            wdn_bf[rows, :] = wdn_ref[0, rows, :].astype(BF16)
            return c

        lax.fori_loop(0, D_MODEL // 64, cast, 0)

    @pl.when(i < nu_ref[0])
    def _():
        hu = _dot(xs_ref[:, :D_MODEL], wup_bf[...]) + bup_ref[0]
        gate = jnp.minimum(hu[:, :D_FF], SWIGLU_LIMIT)
        lin = jnp.clip(hu[:, D_FF:], -SWIGLU_LIMIT, SWIGLU_LIMIT)
        glu = gate * _sigmoid(SWIGLU_ALPHA * gate)
        y = _dot(((lin + 1.0) * glu).astype(BF16), wdn_bf[...]) + bdn_ref[0]
        side = xs_ref[:, D_MODEL:].astype(F32)
        e = be_ref[i].astype(F32)
        weight = jnp.zeros((side.shape[0], 1), F32)
        for k in range(TOP_K):
            g_k = side[:, 3 * k:3 * k + 1] + side[:, 3 * k + 1:3 * k + 2] + side[:, 3 * k + 2:3 * k + 3]
            weight = weight + jnp.where(side[:, SIDE_ROWS - TOP_K + k:SIDE_ROWS - TOP_K + k + 1] == e, g_k, 0.0)
        ys_ref[...] = (weight * y).astype(ys_ref.dtype)

    @pl.when(i >= nu_ref[0])
    def _():
        ys_ref[...] = jnp.zeros(ys_ref.shape, ys_ref.dtype)


def _expert_call(block_e, n_used, xs, w_up, b_up, w_down, b_down):
    cap, dw = xs.shape
    d = w_down.shape[2]
    bm = EXPERT_BLOCK
    blk = lambda i, be, nu: (jnp.minimum(i, nu[0] - 1), 0)
    per_e = lambda i, be, nu: (be[i], 0, 0)
    return pl.pallas_call(
        _expert_kernel,
        grid_spec=pltpu.PrefetchScalarGridSpec(
            num_scalar_prefetch=2,
            grid=(cap // bm,),
            in_specs=[pl.BlockSpec((bm, dw), blk),
                      pl.BlockSpec((1, d, 2 * D_FF), per_e),
                      pl.BlockSpec((1, 1, 2 * D_FF), per_e),
                      pl.BlockSpec((1, D_FF, d), per_e),
                      pl.BlockSpec((1, 1, d), per_e)],
            out_specs=pl.BlockSpec((bm, d), lambda i, be, nu: (i, 0)),
            scratch_shapes=[pltpu.VMEM((d, 2 * D_FF), BF16), pltpu.VMEM((D_FF, d), BF16)]),
        out_shape=jax.ShapeDtypeStruct((cap, d), BF16),
        compiler_params=_params(1), name="experts",
    )(block_e, n_used, xs, w_up, b_up.reshape(N_EXPERTS, 1, -1), w_down, b_down.reshape(N_EXPERTS, 1, -1))


def _combine_kernel(starts_ref, nq_ref, x1_ref, gt_ref, rcol_ref, gfin_ref, ys_ref, o_ref, bufs_ref, sems,
                    *, tile0, tiles_per_group):
    x1 = x1_ref[...]
    gb, tt, d = x1.shape
    tm = gb * tt
    step = pl.program_id(0) * tiles_per_group + pl.program_id(1)
    n_steps = pl.num_programs(0) * tiles_per_group
    tile = tile0 + step
    chunks = functools.partial(_tile_chunks, ys_ref, bufs_ref, starts_ref, nq_ref, sems, to_hbm=False)

    @pl.when(step == 0)
    def _():
        bufs_ref[...] = jnp.zeros(bufs_ref.shape, bufs_ref.dtype)
        chunks(tile, wait=False)

    @pl.when(step + 1 < n_steps)
    def _():
        chunks(tile + 1, wait=False)

    chunks(tile, wait=True)
    buf_ref = bufs_ref.at[tile % 2]

    rows = rcol_ref[...].astype(jnp.int16)
    moe = jnp.zeros((tm, d), F32)
    for c in range(bufs_ref.shape[1] // PERM_COLS):
        r = (lax.broadcasted_iota(I32, (tm, PERM_COLS), 1) + c * PERM_COLS).astype(jnp.int16)
        hit = (r == rows[:, 0:1]) | (r == rows[:, 1:2]) | (r == rows[:, 2:3]) | (r == rows[:, 3:4])
        onehot = jnp.where(hit, jnp.ones((), BF16), jnp.zeros((), BF16))
        moe = moe + _dot(onehot, buf_ref[c * PERM_COLS:(c + 1) * PERM_COLS, :])
    xo = x1 + gt_ref[...] * moe.reshape(gb, tt, d)
    o_ref[...] = _rms(xo, gfin_ref[...])


def _combine_call(starts, nq, x1, mod3, rcol, g_final, ys, tok0, gb, tt):
    g, t, d = x1.shape
    tm = gb * tt
    tpg = t // tt
    t0 = tok0 // tm
    tok = lambda i, j, *_: (t0 + i * tpg + j, 0)
    return pl.pallas_call(
        functools.partial(_combine_kernel, tile0=t0, tiles_per_group=tpg),
        grid_spec=pltpu.PrefetchScalarGridSpec(
            num_scalar_prefetch=2,
            grid=(g // gb, tpg),
            in_specs=[pl.BlockSpec((gb, tt, d), lambda i, j, *_: (i, j, 0)),
                      pl.BlockSpec((gb, 1, d), lambda i, j, *_: (i, 0, 5)),
                      pl.BlockSpec((tm, TOP_K), tok),
                      pl.BlockSpec((1, 1, d), lambda i, j, *_: (0, 0, 0)),
                      pl.BlockSpec(memory_space=pl.ANY)],
            out_specs=pl.BlockSpec((gb, tt, d), lambda i, j, *_: (i, j, 0)),
            scratch_shapes=[pltpu.VMEM((2, TILE_BUF_ROWS, d), BF16), pltpu.SemaphoreType.DMA((2,))]),
        out_shape=jax.ShapeDtypeStruct((g, t, d), F32),
        compiler_params=_params(2), name="combine",
    )(starts, nq, x1, mod3, rcol, g_final.reshape(1, 1, d), ys)


def _group_blocking(groups, seq, tile):
    if seq >= tile:
        return 1, tile
    return tile // seq, seq


def kernel(x_prompt, x_sample, c_prompt, c_sample, state_ret, state_mlstm_c, state_mlstm_n, state_mlstm_m, state_conv, w_mod, b_mod, g_mix, g_ffn, w_in, b_igate, b_fgate, w_conv, b_conv, g_ret, g_mlstm, w_out, w_router, b_router, w_up, b_up, w_down, b_down, g_final):
    depth = w_mod.shape[0]
    assert depth == 1, "single-layer trunk"
    bp, tp, d = x_prompt.shape
    bs, ts, _ = x_sample.shape
    n_p, n_s = bp * tp, bs * ts
    hd = HEAD_DIM
    l = 0

    mod = _mod_call(jnp.concatenate([c_prompt, c_sample], axis=0), w_mod[l], b_mod[l])
    mod_p = mod[:bp].reshape(bp, 1, N_MOD * d)
    mod_s = mod[bp:].reshape(bs, 1, N_MOD * d)

    w_main_bf = w_in[l][:, :MAIN_COLS].astype(BF16)
    w_gate = jnp.pad(w_in[l][:, MAIN_COLS:], ((0, 0), (0, 128 - N_GATES)))
    w_gate_t = w_in[l][:, MAIN_COLS:].T
    w_out_bf = w_out[l].astype(BF16)
    bg_col = jnp.concatenate([b_igate[l], b_fgate[l]]).reshape(1, N_GATES)
    bg_row = bg_col.reshape(N_GATES, 1)

    groups = (
        (x_prompt, mod_p, min(CHUNK, tp), np.arange(tp),
         jnp.zeros((bp, RET_HEADS, hd, hd), F32), jnp.zeros((bp, MLSTM_HEADS, hd, hd), F32),
         jnp.zeros((bp, MLSTM_HEADS, hd), F32), jnp.zeros((bp, MLSTM_HEADS), F32),
         jnp.zeros((bp, CONV_WIDTH - 1, 2 * MLSTM_WIDTH), F32)),
        (x_sample, mod_s, min(CHUNK, ts), PAST_LEN + np.arange(ts),
         state_ret[l], state_mlstm_c[l], state_mlstm_n[l], state_mlstm_m[l], state_conv[l]),
    )

    ustrict = jnp.asarray(np.arange(MOE_TILE)[:, None] < np.arange(MOE_TILE)[None, :], BF16)
    staged = []
    for x3, mod3, L, pos, s0, c0, n0, m0, conv0 in groups:
        g, t, _ = x3.shape
        conv0p = jnp.pad(conv0.astype(F32), ((0, 0), (CONV_TAIL_ROWS - (CONV_WIDTH - 1), 0), (0, 0)))
        y, s_new, c_new, n_new, m_new, tail = _mixer_call(
            x3, mod3, g_mix[l], w_main_bf, w_gate, w_gate_t, _mixer_consts(L, pos),
            w_conv[l], b_conv[l], g_ret[l], g_mlstm[l],
            bg_col, bg_row, s0.astype(F32), c0.astype(F32), n0.astype(F32).reshape(g, MLSTM_HEADS, 1, hd),
            jnp.broadcast_to(m0.astype(F32)[:, :, None, None], (g, MLSTM_HEADS, 1, hd)), conv0p,
            L, MIXER_SEQS_CHUNKED if t > L else MIXER_SEQS_SHORT, F32)
        states = (s_new, c_new, n_new.reshape(g, MLSTM_HEADS, hd), m_new[:, :, 0, 0],
                  tail[:, CONV_TAIL_ROWS - (CONV_WIDTH - 1):, :])
        gb, tt = _group_blocking(g, t, MOE_TILE)
        x1, h2, idx, rank, cnt = _outproj_call(
            y.reshape(g * t, d), x3, mod3, g_ffn[l], w_out_bf, w_router[l].T, b_router[l], ustrict, gb, tt)
        staged.append((x1, mod3, h2, idx, rank, states, cnt))

    n_tok = n_p + n_s
    tm, bm, ra = MOE_TILE, EXPERT_BLOCK, RUN_ALIGN
    n_tiles = n_tok // tm
    q_max = TILE_BUF_ROWS // ra
    n_blocks = -(-(n_tok * TOP_K + n_tiles * N_EXPERTS * (ra - 1)) // bm) + N_EXPERTS
    cap = n_blocks * bm
    counts = jnp.concatenate([s[6][:, :, 0] for s in staged], axis=0).astype(I32)
    run = (counts + ra - 1) // ra * ra
    region = jnp.sum(run, axis=0)
    padded = (region + bm - 1) // bm * bm
    pad_end = jnp.cumsum(padded)
    pad_start = pad_end - padded
    run_start = pad_start[None, :] + jnp.cumsum(run, axis=0) - run
    buf_end = jnp.cumsum(run, axis=1)
    buf_start = buf_end - run
    nq = (buf_end[:, -1] // ra).astype(I32)
    chunk_row = jnp.arange(q_max, dtype=I32) * ra
    chunk_e = jnp.minimum(jnp.sum((buf_end[:, None, :] <= chunk_row[None, :, None]).astype(I32), axis=2),
                          N_EXPERTS - 1)
    e_ids = jnp.arange(N_EXPERTS, dtype=I32)
    shift = run_start - buf_start
    starts = jnp.sum(jnp.where(chunk_e[:, :, None] == e_ids, shift[:, None, :], 0), axis=2) + chunk_row[None, :]
    starts = jnp.where(chunk_row[None, :] < buf_end[:, -1:], starts, 0).reshape(-1).astype(I32)

    idx_all = jnp.concatenate([s[3] for s in staged], axis=1)
    rank_all = jnp.concatenate([s[4] for s in staged], axis=1)
    buf_start_tok = jnp.repeat(buf_start, tm, axis=0).T
    rbuf = jnp.sum(jnp.where(idx_all[None] == e_ids[:, None, None], buf_start_tok[:, None, :], 0), axis=0) + rank_all
    rcol = rbuf.T
    block_row = jnp.arange(n_blocks, dtype=I32) * bm
    block_e = jnp.minimum(jnp.sum((pad_end[None, :] <= block_row[:, None]).astype(I32), axis=1), N_EXPERTS - 1)
    n_used = (pad_end[-1:] // bm).astype(I32)
    meta = jnp.concatenate([jnp.where(region > 0, pad_end - bm, -1), n_used]).astype(I32)

    xs = _dispatch_call(starts, nq, meta, rbuf, staged[0][2], staged[1][2], cap)
    ys = _expert_call(block_e, n_used, xs, w_up[l], b_up[l], w_down[l], b_down[l])

    outs = []
    tok0 = 0
    for x1, mod3, *_ in staged:
        g, t, _ = x1.shape
        gb, tt = _group_blocking(g, t, tm)
        outs.append(_combine_call(starts, nq, x1, mod3, rcol, g_final, ys, tok0, gb, tt))
        tok0 += g * t

    st_p, st_s = staged[0][5], staged[1][5]
    return (outs[0], outs[1]) + tuple(a[None] for a in st_p) + tuple(a[None] for a in st_s)
```

```python
import functools

import numpy as np
import jax
import jax.numpy as jnp
from jax import lax
from jax.experimental import pallas as pl
from jax.experimental.pallas import tpu as pltpu

F32 = jnp.float32
BF16 = jnp.bfloat16
I32 = jnp.int32

D_MODEL = 1024
PAST_LEN = 16384
RET_HEADS = 4
MLSTM_HEADS = 4
HEAD_DIM = 128
RET_WIDTH = RET_HEADS * HEAD_DIM
MLSTM_WIDTH = MLSTM_HEADS * HEAD_DIM
CONV_WIDTH = 4
CHUNK = 128
ROPE_BASE = 10000.0
N_EXPERTS = 32
TOP_K = 4
D_FF = D_MODEL
SWIGLU_LIMIT = 7.0
SWIGLU_ALPHA = 1.702
N_MOD = 6
EPS = 1e-6
MAIN_COLS = 4 * RET_WIDTH + 2 * MLSTM_WIDTH + 2 * MLSTM_WIDTH
N_GATES = 2 * MLSTM_HEADS
OFF_RQ, OFF_RK, OFF_RV, OFF_RG = 0, RET_WIDTH, 2 * RET_WIDTH, 3 * RET_WIDTH
OFF_MQK = 4 * RET_WIDTH
OFF_MV = OFF_MQK + 2 * MLSTM_WIDTH
OFF_MO = OFF_MV + MLSTM_WIDTH

VMEM_LIMIT_BYTES = 56 * 1024 * 1024
MOE_TILE = 512
EXPERT_BLOCK = 512
MIXER_SEQS_CHUNKED = 4
MIXER_SEQS_SHORT = 8
RUN_ALIGN = 16
SIDE_ROWS = 4 * TOP_K
SIDE_LANES = 128
TILE_BUF_ROWS = MOE_TILE * TOP_K + N_EXPERTS * RUN_ALIGN
PERM_ROWS = 1280
PERM_COLS = 512
CONV_TAIL_ROWS = 8


def _params(n_axes=1):
    return pltpu.CompilerParams(dimension_semantics=("arbitrary",) * n_axes, vmem_limit_bytes=VMEM_LIMIT_BYTES)


def _dot(a, b):
    return jnp.dot(a, b, preferred_element_type=F32)


def _dot_nt(a, b):
    return lax.dot_general(a, b, (((1,), (1,)), ((), ())), preferred_element_type=F32)


def _dot_tn(a, b):
    return lax.dot_general(a, b, (((0,), (0,)), ((), ())), preferred_element_type=F32)


def _split(a):
    hi = a.astype(BF16)
    lo = (a - hi.astype(F32)).astype(BF16)
    return hi, lo


def _dot3(a, b, dot=_dot):
    ah, al = _split(a)
    bh, bl = _split(b)
    return dot(ah, bh) + (dot(al, bh) + dot(ah, bl))


def _sigmoid(x):
    return 0.5 * (jnp.tanh(0.5 * x) + 1.0)


def _log_sigmoid(x):
    return jnp.minimum(x, 0.0) - jnp.log1p(jnp.exp(-jnp.abs(x)))


def _rms(x, g):
    ms = jnp.mean(x * x, axis=-1, keepdims=True)
    return (x * lax.rsqrt(ms + EPS)) * g


def _layer_norm(x, g):
    mu = jnp.mean(x, axis=-1, keepdims=True)
    xc = x - mu
    var = jnp.mean(xc * xc, axis=-1, keepdims=True)
    return xc * lax.rsqrt(var + EPS) * g


def _mod_kernel(c_ref, w_ref, b_ref, o_ref):
    c = c_ref[...]
    o_ref[...] = _dot3(c * _sigmoid(c), w_ref[...]) + b_ref[...]


def _mod_call(c_all, w_mod, b_mod):
    rows, d = c_all.shape
    cols = w_mod.shape[1]
    tn = 1024
    return pl.pallas_call(
        _mod_kernel,
        grid=(cols // tn,),
        in_specs=[pl.BlockSpec((rows, d), lambda j: (0, 0)),
                  pl.BlockSpec((d, tn), lambda j: (0, j)),
                  pl.BlockSpec((1, tn), lambda j: (0, j))],
        out_specs=pl.BlockSpec((rows, tn), lambda j: (0, j)),
        out_shape=jax.ShapeDtypeStruct((rows, cols), F32),
        compiler_params=_params(1), name="mod",
    )(c_all, w_mod, b_mod.reshape(1, cols))


def _mixer_kernel(x_ref, sh_ref, sc_ref, gmix_ref, w_ref, wg_ref, wgt_ref,
                  cos_ref, sin_ref, dec_ref, qd_ref, kd_ref, cd_ref,
                  tril_ref, triu_ref, wconv_ref, bconv_ref, gret_ref, gml_ref, bgc_ref, bgr_ref,
                  s0_ref, c0_ref, n0_ref, m0_ref, conv0_ref,
                  y_ref, s_ref, c_ref, n_ref, m_ref, tail_ref, p_ref, xp_ref):
    @pl.when(pl.program_id(1) == 0)
    def _():
        s_ref[...] = s0_ref[...]
        c_ref[...] = c0_ref[...]
        n_ref[...] = n0_ref[...]
        m_ref[...] = m0_ref[...]
        tail_ref[...] = conv0_ref[...]

    x = x_ref[...]
    gs, L, d = x.shape
    h = (_rms(x, gmix_ref[...]) * (1.0 + sc_ref[...]) + sh_ref[...]).reshape(gs * L, d)
    hb = h.astype(BF16)
    for j in range(MAIN_COLS // 1024):
        p_ref[:, j * 1024:(j + 1) * 1024] = _dot(hb, w_ref[:, j * 1024:(j + 1) * 1024])

    chains = []
    for sq in range(gs):
        h_sq = h[sq * L:(sq + 1) * L, :]
        gc = _dot3(h_sq, wg_ref[...])[:, :N_GATES]
        gr = _dot3(wgt_ref[...], h_sq, dot=_dot_nt)
        chains += _mixer_sequence(p_ref.at[pl.ds(sq * L, L)], gc, gr, cos_ref, sin_ref, dec_ref, qd_ref, kd_ref,
                                  cd_ref, tril_ref, triu_ref, wconv_ref, bconv_ref, gret_ref, gml_ref, bgc_ref,
                                  bgr_ref, y_ref.at[sq], s_ref.at[sq], c_ref.at[sq], n_ref.at[sq], m_ref.at[sq],
                                  tail_ref.at[sq], xp_ref.at[sq])
    while chains:
        alive = []
        for chain in chains:
            if next(chain, None) is not None:
                alive.append(chain)
        chains = alive


N_MIXER_INPUTS = 26
N_OUTPROJ_INPUTS = 8


def _mixer_outproj_kernel(*refs):
    mixer_in = refs[:N_MIXER_INPUTS]
    outproj_in = refs[N_MIXER_INPUTS:N_MIXER_INPUTS + N_OUTPROJ_INPUTS]
    x1_ref, h2_ref, idx_ref, rank_ref, cnt_ref, s_ref, c_ref, n_ref, m_ref, tail_ref, p_ref, xp_ref, y_ref = refs[
        N_MIXER_INPUTS + N_OUTPROJ_INPUTS:]
    _mixer_kernel(*mixer_in, y_ref, s_ref, c_ref, n_ref, m_ref, tail_ref, p_ref, xp_ref)
    gs, L, d = y_ref.shape
    _outproj_router(y_ref[...].reshape(gs * L, d), mixer_in[0], *outproj_in, x1_ref, h2_ref, idx_ref, rank_ref, cnt_ref)


def _mixer_sequence(p_ref, gc, gr, cos_ref, sin_ref, dec_ref, qd_ref, kd_ref, cd_ref,
                    tril_ref, triu_ref, wconv_ref, bconv_ref, gret_ref, gml_ref, bgc_ref, bgr_ref,
                    y_ref, s_ref, c_ref, n_ref, m_ref, tail_ref, xp_ref):
    L = p_ref.shape[0]
    cos = cos_ref[...]
    sin = sin_ref[...]
    scale = HEAD_DIM ** -0.5

    def rot(x):
        return x * cos + pltpu.roll(x, HEAD_DIM // 2, axis=1) * sin

    def retention_head(h):
        lo = h * HEAD_DIM
        q = rot(p_ref[:, OFF_RQ + lo:OFF_RQ + lo + HEAD_DIM])
        k = rot(p_ref[:, OFF_RK + lo:OFF_RK + lo + HEAD_DIM]) * scale
        v = p_ref[:, OFF_RV + lo:OFF_RV + lo + HEAD_DIM].astype(BF16)
        s_old = s_ref[h]
        yield True
        scores = _dot_nt(q.astype(BF16), k.astype(BF16)) * dec_ref[h]
        yield True
        out = _dot(scores.astype(BF16), v) + _dot((q * qd_ref[h]).astype(BF16), s_old.astype(BF16))
        s_ref[h] = cd_ref[h] * s_old + _dot_tn((k * kd_ref[h]).astype(BF16), v)
        yield True
        g = p_ref[:, OFF_RG + lo:OFF_RG + lo + HEAD_DIM]
        y_ref[:, lo:lo + HEAD_DIM] = ((g * _sigmoid(g)) * _layer_norm(out, gret_ref[:, lo:lo + HEAD_DIM])).astype(y_ref.dtype)

    xp_ref[0:CONV_TAIL_ROWS, :] = tail_ref[...]
    xp_ref[CONV_TAIL_ROWS:CONV_TAIL_ROWS + L, :] = p_ref[:, OFF_MQK:OFF_MQK + 2 * MLSTM_WIDTH]
    acc = bconv_ref[...] + wconv_ref[0:1, :] * xp_ref[CONV_TAIL_ROWS - 3:CONV_TAIL_ROWS - 3 + L, :]
    for j in range(1, CONV_WIDTH):
        acc = acc + wconv_ref[j:j + 1, :] * xp_ref[CONV_TAIL_ROWS - 3 + j:CONV_TAIL_ROWS - 3 + j + L, :]
    tail_ref[...] = xp_ref[L:L + CONV_TAIL_ROWS, :]
    xp_ref[CONV_TAIL_ROWS:CONV_TAIL_ROWS + L, :] = acc * _sigmoid(acc)

    gcol = gc + bgc_ref[...]
    is_f_col = lax.broadcasted_iota(I32, gcol.shape, 1) >= MLSTM_HEADS
    gcol = jnp.where(is_f_col, _log_sigmoid(gcol), gcol)
    grow = gr + bgr_ref[...]
    is_f_row = lax.broadcasted_iota(I32, grow.shape, 0) >= MLSTM_HEADS
    grow = jnp.where(is_f_row, _log_sigmoid(grow), grow)
    bcol_all = _dot3(tril_ref[...], gcol)
    brow_all = _dot3(grow, triu_ref[...])
    causal = lax.broadcasted_iota(I32, (L, L), 0) >= lax.broadcasted_iota(I32, (L, L), 1)

    def mlstm_head(h):
        lo = h * HEAD_DIM
        q = xp_ref[CONV_TAIL_ROWS:CONV_TAIL_ROWS + L, lo:lo + HEAD_DIM]
        k = xp_ref[CONV_TAIL_ROWS:CONV_TAIL_ROWS + L, MLSTM_WIDTH + lo:MLSTM_WIDTH + lo + HEAD_DIM] * scale
        v = p_ref[:, OFF_MV + lo:OFF_MV + lo + HEAD_DIM].astype(BF16)
        ic_col = gcol[:, h:h + 1]
        ic_row = grow[h:h + 1, :]
        b_col = bcol_all[:, MLSTM_HEADS + h:MLSTM_HEADS + h + 1]
        b_row = brow_all[MLSTM_HEADS + h:MLSTM_HEADS + h + 1, :]
        c_old = c_ref[h]
        n_old = n_ref[h]
        m_old = m_ref[h][:, 0:1]

        d_log = jnp.where(causal, b_col - b_row + ic_row, -jnp.inf)
        inter = b_col + m_old
        m_t = jnp.maximum(inter, jnp.max(d_log, axis=1, keepdims=True))
        yield True
        w_intra = jnp.exp(d_log - m_t)
        w_inter = jnp.exp(inter - m_t)
        qb = q.astype(BF16)
        s = _dot_nt(qb, k.astype(BF16)) * w_intra
        yield True
        num = _dot(s.astype(BF16), v) + w_inter * _dot(qb, c_old.astype(BF16))
        den = jnp.sum(s, axis=1, keepdims=True) + w_inter * jnp.sum(q * n_old, axis=1, keepdims=True)
        yield True
        hh = num / jnp.maximum(jnp.abs(den), jnp.exp(-m_t))

        b_last = b_col[L - 1:L, :]
        w_log_col = b_last - b_col + ic_col
        m_new = jnp.maximum(b_last + m_old, jnp.max(w_log_col, axis=0, keepdims=True))
        wk = jnp.exp(w_log_col - m_new) * k
        cdec = jnp.exp(b_last + m_old - m_new)
        yield True
        c_ref[h] = cdec * c_old + _dot_tn(wk.astype(BF16), v)
        n_ref[h] = cdec * n_old + jnp.sum(wk, axis=0, keepdims=True)
        m_ref[h] = jnp.broadcast_to(m_new, (1, HEAD_DIM))
        yield True
        o = p_ref[:, OFF_MO + lo:OFF_MO + lo + HEAD_DIM]
        y_ref[:, RET_WIDTH + lo:RET_WIDTH + lo + HEAD_DIM] = (
            _sigmoid(o) * _layer_norm(hh, gml_ref[:, lo:lo + HEAD_DIM])).astype(y_ref.dtype)

    return [retention_head(h) for h in range(RET_HEADS)] + [mlstm_head(h) for h in range(MLSTM_HEADS)]


def _mixer_consts(L, pos):
    f32 = np.float32
    half = HEAD_DIM // 2
    inv_freq = np.power(f32(ROPE_BASE), -np.arange(half, dtype=f32) / f32(half)).astype(f32)
    ang = (pos.astype(f32)[:, None] * inv_freq[None, :]).astype(f32)
    cos = np.concatenate([np.cos(ang), np.cos(ang)], axis=-1).astype(f32)
    sin = np.concatenate([-np.sin(ang), np.sin(ang)], axis=-1).astype(f32)
    log_gamma = np.log1p(-np.exp2(-5.0 - np.arange(RET_HEADS, dtype=np.float64)))
    idx = np.arange(L, dtype=np.float64)
    rel = idx[:, None] - idx[None, :]
    dec = np.where(rel >= 0, np.exp(log_gamma[:, None, None] * np.maximum(rel, 0.0)), 0.0)
    qd = np.broadcast_to(np.exp(log_gamma[:, None] * (idx + 1.0))[..., None], (RET_HEADS, L, HEAD_DIM))
    kd = np.broadcast_to(np.exp(log_gamma[:, None] * (L - 1.0 - idx))[..., None], (RET_HEADS, L, HEAD_DIM))
    cd = np.broadcast_to(np.exp(log_gamma * L)[:, None, None], (RET_HEADS, 1, HEAD_DIM))
    tril = rel >= 0
    triu = rel <= 0
    return tuple(jnp.asarray(a, F32) for a in (cos, sin, dec, qd, kd, cd, tril, triu))


def _mixer_call(x3, mod3, g_mix, w_main_bf, w_gate, w_gate_t, consts, w_conv, b_conv, g_ret, g_mlstm, bg_col, bg_row,
                s0, c0, n0, m0, conv0, L, gs, outproj=None):
    cos, sin, dec, qd, kd, cd, tril, triu = consts
    groups, seq, d = x3.shape
    nc = seq // L
    hd = HEAD_DIM
    full = lambda *shape: pl.BlockSpec(shape, lambda g, c: (0,) * len(shape))
    mod_spec = lambda k: pl.BlockSpec((gs, 1, d), lambda g, c: (g, 0, k))
    state4 = pl.BlockSpec((gs, RET_HEADS, hd, hd), lambda g, c: (g, 0, 0, 0))
    vec4 = pl.BlockSpec((gs, MLSTM_HEADS, 1, hd), lambda g, c: (g, 0, 0, 0))
    tail3 = pl.BlockSpec((gs, CONV_TAIL_ROWS, 2 * MLSTM_WIDTH), lambda g, c: (g, 0, 0))
    row = lambda g, c: (g, c, 0)
    in_specs = [pl.BlockSpec((gs, L, d), row), mod_spec(0), mod_spec(1),
                full(1, 1, d), full(d, MAIN_COLS), full(d, 128), full(N_GATES, d),
                pl.BlockSpec((L, hd), lambda g, c: (c, 0)),
                pl.BlockSpec((L, hd), lambda g, c: (c, 0)),
                full(RET_HEADS, L, L), full(RET_HEADS, L, hd), full(RET_HEADS, L, hd), full(RET_HEADS, 1, hd),
                full(L, L), full(L, L),
                full(CONV_WIDTH, 2 * MLSTM_WIDTH), full(1, 2 * MLSTM_WIDTH),
                full(1, RET_WIDTH), full(1, MLSTM_WIDTH), full(1, N_GATES), full(N_GATES, 1),
                state4, state4, vec4, vec4, tail3]
    args = [x3, mod3, mod3, g_mix.reshape(1, 1, d), w_main_bf, w_gate, w_gate_t,
            cos, sin, dec, qd, kd, cd, tril, triu, w_conv, b_conv.reshape(1, -1),
            g_ret.reshape(1, -1), g_mlstm.reshape(1, -1), bg_col, bg_row, s0, c0, n0, m0, conv0]
    assert len(in_specs) == len(args) == N_MIXER_INPUTS
    state_specs = [state4, state4, vec4, vec4, tail3]
    state_shapes = [jax.ShapeDtypeStruct((groups, RET_HEADS, hd, hd), F32),
                    jax.ShapeDtypeStruct((groups, MLSTM_HEADS, hd, hd), F32),
                    jax.ShapeDtypeStruct((groups, MLSTM_HEADS, 1, hd), F32),
                    jax.ShapeDtypeStruct((groups, MLSTM_HEADS, 1, hd), F32),
                    jax.ShapeDtypeStruct((groups, CONV_TAIL_ROWS, 2 * MLSTM_WIDTH), F32)]
    scratch = [pltpu.VMEM((gs * L, MAIN_COLS), F32), pltpu.VMEM((gs, CONV_TAIL_ROWS + L, 2 * MLSTM_WIDTH), F32)]
    y_block, y_shape = (gs, L, RET_WIDTH + MLSTM_WIDTH), (groups, seq, RET_WIDTH + MLSTM_WIDTH)
    if outproj is None:
        kernel_fn = _mixer_kernel
        out_specs = [pl.BlockSpec(y_block, row)] + state_specs
        out_shape = [jax.ShapeDtypeStruct(y_shape, F32)] + state_shapes
    else:
        g_ffn, w_out_bf, w_router_t, b_router, ustrict = outproj
        tm = gs * L
        n = groups * seq
        tile = lambda g, c: g * nc + c
        kernel_fn = _mixer_outproj_kernel
        in_specs += [mod_spec(2), mod_spec(3), mod_spec(4),
                     full(1, 1, d), full(d, d), full(N_EXPERTS, d), full(N_EXPERTS, 1), full(tm, tm)]
        args += [mod3, mod3, mod3, g_ffn.reshape(1, 1, d), w_out_bf, w_router_t, b_router.reshape(N_EXPERTS, 1),
                 ustrict]
        assert len(args) == N_MIXER_INPUTS + N_OUTPROJ_INPUTS
        out_specs = [pl.BlockSpec((gs, L, d), row),
                     pl.BlockSpec((tm, d + SIDE_LANES), lambda g, c: (tile(g, c), 0)),
                     pl.BlockSpec((TOP_K, tm), lambda g, c: (0, tile(g, c))),
                     pl.BlockSpec((TOP_K, tm), lambda g, c: (0, tile(g, c))),
                     pl.BlockSpec((1, N_EXPERTS, 128), lambda g, c: (tile(g, c), 0, 0))] + state_specs
        out_shape = [jax.ShapeDtypeStruct((groups, seq, d), F32),
                     jax.ShapeDtypeStruct((n, d + SIDE_LANES), BF16),
                     jax.ShapeDtypeStruct((TOP_K, n), I32),
                     jax.ShapeDtypeStruct((TOP_K, n), I32),
                     jax.ShapeDtypeStruct((n // tm, N_EXPERTS, 128), F32)] + state_shapes
        scratch = scratch + [pltpu.VMEM(y_block, F32)]
    return pl.pallas_call(
        kernel_fn,
        grid=(groups // gs, nc),
        in_specs=in_specs, out_specs=out_specs, out_shape=out_shape, scratch_shapes=scratch,
        compiler_params=_params(2), name="mixer",
    )(*args)


def _outproj_kernel(y_ref, x_ref, gt_ref, sh_ref, sc_ref, g_ref, w_ref, wrt_ref, br_ref, ustrict_ref,
                    x1_ref, h2_ref, idx_ref, rank_ref, cnt_ref):
    _outproj_router(y_ref[...], x_ref, gt_ref, sh_ref, sc_ref, g_ref, w_ref, wrt_ref, br_ref, ustrict_ref,
                    x1_ref, h2_ref, idx_ref, rank_ref, cnt_ref)


def _outproj_router(y, x_ref, gt_ref, sh_ref, sc_ref, g_ref, w_ref, wrt_ref, br_ref, ustrict_ref,
                    x1_ref, h2_ref, idx_ref, rank_ref, cnt_ref):
    x = x_ref[...]
    gb, tt, d = x.shape
    tm = gb * tt
    mixed = _dot(y.astype(BF16), w_ref[...])
    x1 = x + gt_ref[...] * mixed.reshape(gb, tt, d)
    x1_ref[...] = x1
    h2 = (_rms(x1, g_ref[...]) * (1.0 + sc_ref[...]) + sh_ref[...]).reshape(tm, d)
    h2_ref[:, :d] = h2.astype(BF16)

    work = _dot3(wrt_ref[...], h2, dot=_dot_nt) + br_ref[...]
    e_iota = lax.broadcasted_iota(I32, work.shape, 0).astype(F32)
    vals, idxs, sels = [], [], []
    for _ in range(TOP_K):
        mx = jnp.max(work, axis=0, keepdims=True)
        ik = jnp.min(jnp.where(work == mx, e_iota, float(N_EXPERTS)), axis=0, keepdims=True)
        sel = e_iota == ik
        vals.append(mx)
        idxs.append(ik)
        sels.append(sel)
        work = jnp.where(sel, -jnp.inf, work)
    exps = [jnp.exp(v - vals[0]) for v in vals]
    denom = exps[0] + exps[1] + exps[2] + exps[3]
    gates = [e / denom for e in exps]
    idx_ref[...] = jnp.concatenate(idxs, axis=0).astype(I32)

    pieces = []
    for gk in gates:
        p1 = gk.astype(BF16)
        r1 = gk - p1.astype(F32)
        p2 = r1.astype(BF16)
        pieces += [p1, p2, (r1 - p2.astype(F32)).astype(BF16)]
    side = jnp.concatenate(pieces + [ik.astype(BF16) for ik in idxs], axis=0)
    eye = (lax.broadcasted_iota(I32, (SIDE_ROWS, SIDE_LANES), 0)
           == lax.broadcasted_iota(I32, (SIDE_ROWS, SIDE_LANES), 1)).astype(BF16)
    h2_ref[:, d:] = _dot_tn(side, eye).astype(BF16)

    mask = (sels[0] | sels[1] | sels[2] | sels[3]).astype(F32)
    before = _dot(mask.astype(BF16), ustrict_ref[...])
    ranks = [jnp.sum(jnp.where(sel, before, 0.0), axis=0, keepdims=True) for sel in sels]
    rank_ref[...] = jnp.concatenate(ranks, axis=0).astype(I32)
    cnt_ref[0] = jnp.broadcast_to(jnp.sum(mask, axis=1, keepdims=True), cnt_ref.shape[1:])


def _outproj_call(y, x3, mod3, g_ffn, w_out_bf, w_router_t, b_router, ustrict, gb, tt):
    g, t, d = x3.shape
    n = g * t
    tm = gb * tt
    tpg = t // tt
    row = lambda i, j: (i * tpg + j, 0)
    col = lambda i, j: (0, i * tpg + j)
    mod_spec = lambda k: pl.BlockSpec((gb, 1, d), lambda i, j: (i, 0, k))
    const = lambda *shape: pl.BlockSpec(shape, lambda i, j: (0,) * len(shape))
    return pl.pallas_call(
        _outproj_kernel,
        grid=(g // gb, tpg),
        in_specs=[pl.BlockSpec((tm, d), row),
                  pl.BlockSpec((gb, tt, d), lambda i, j: (i, j, 0)),
                  mod_spec(2), mod_spec(3), mod_spec(4),
                  const(1, 1, d), const(d, d), const(N_EXPERTS, d), const(N_EXPERTS, 1),
                  const(tm, tm)],
        out_specs=[pl.BlockSpec((gb, tt, d), lambda i, j: (i, j, 0)),
                   pl.BlockSpec((tm, d + SIDE_LANES), row),
                   pl.BlockSpec((TOP_K, tm), col),
                   pl.BlockSpec((TOP_K, tm), col),
                   pl.BlockSpec((1, N_EXPERTS, 128), lambda i, j: (i * tpg + j, 0, 0))],
        out_shape=[jax.ShapeDtypeStruct((g, t, d), F32),
                   jax.ShapeDtypeStruct((n, d + SIDE_LANES), BF16),
                   jax.ShapeDtypeStruct((TOP_K, n), I32),
                   jax.ShapeDtypeStruct((TOP_K, n), I32),
                   jax.ShapeDtypeStruct((n // tm, N_EXPERTS, 128), F32)],
        compiler_params=_params(2), name="outproj_router",
    )(y, x3, mod3, mod3, mod3, g_ffn.reshape(1, 1, d), w_out_bf, w_router_t, b_router.reshape(N_EXPERTS, 1),
      ustrict)


def _chunk_copy(hbm_ref, hbm_row, buf_ref, chunk, sem, to_hbm):
    hbm = hbm_ref.at[pl.ds(pl.multiple_of(hbm_row, RUN_ALIGN), RUN_ALIGN)]
    buf = buf_ref.at[pl.ds(pl.multiple_of(chunk * RUN_ALIGN, RUN_ALIGN), RUN_ALIGN)]
    return pltpu.make_async_copy(buf, hbm, sem) if to_hbm else pltpu.make_async_copy(hbm, buf, sem)


def _tile_chunks(hbm_ref, bufs_ref, starts_ref, nq_ref, sems, tile, to_hbm, wait):
    slot = tile % 2
    first = tile * (bufs_ref.shape[1] // RUN_ALIGN)

    def body(q, c):
        cp = _chunk_copy(hbm_ref, starts_ref[first + q], bufs_ref.at[slot], q, sems.at[slot], to_hbm)
        cp.wait() if wait else cp.start()
        return c

    lax.fori_loop(0, nq_ref[tile], body, 0)


def _dispatch_kernel(starts_ref, nq_ref, meta_ref, rbuf_ref, h2p_ref, h2s_ref, xs_ref, bufs_ref, zero_ref, sems,
                     *, prompt_tiles):
    i = pl.program_id(0)
    tm = h2p_ref.shape[0]
    bm = zero_ref.shape[0]
    n_blocks = xs_ref.shape[0] // bm
    sem = sems.at[0]
    buf_ref = bufs_ref.at[i % 2]

    @pl.when(i == 0)
    def _():
        zero_ref[...] = jnp.zeros(zero_ref.shape, zero_ref.dtype)

        def zero_copy(row):
            return pltpu.make_async_copy(zero_ref, xs_ref.at[pl.ds(pl.multiple_of(row, bm), bm)], sem)

        def tails(fn):
            def body(e, c):
                @pl.when(meta_ref[e] >= 0)
                def _():
                    fn(zero_copy(meta_ref[e]))
                return c
            lax.fori_loop(0, N_EXPERTS, body, 0)

        def unused(fn):
            def body(b, c):
                fn(zero_copy(b * bm))
                return c
            lax.fori_loop(meta_ref[N_EXPERTS], n_blocks, body, 0)

        tails(lambda cp: cp.start())
        unused(lambda cp: cp.start())
        tails(lambda cp: cp.wait())
        unused(lambda cp: cp.wait())

    def build(h2_ref):
        h2 = h2_ref[...]
        rb = rbuf_ref[...].astype(jnp.int16)
        for c in range(bufs_ref.shape[1] // PERM_ROWS):
            r = (lax.broadcasted_iota(I32, (PERM_ROWS, tm), 0) + c * PERM_ROWS).astype(jnp.int16)
            hit = (r == rb[0:1, :]) | (r == rb[1:2, :]) | (r == rb[2:3, :]) | (r == rb[3:4, :])
            onehot = jnp.where(hit, jnp.ones((), BF16), jnp.zeros((), BF16))
            buf_ref[c * PERM_ROWS:(c + 1) * PERM_ROWS, :] = _dot(onehot, h2).astype(BF16)

    @pl.when(i < prompt_tiles)
    def _():
        build(h2p_ref)

    @pl.when(i >= prompt_tiles)
    def _():
        build(h2s_ref)

    chunks = functools.partial(_tile_chunks, xs_ref, bufs_ref, starts_ref, nq_ref, sems, to_hbm=True)
    chunks(i, wait=False)

    @pl.when(i > 0)
    def _():
        chunks(i - 1, wait=True)

    @pl.when(i == pl.num_programs(0) - 1)
    def _():
        chunks(i, wait=True)


def _dispatch_call(starts, nq, meta, rbuf, h2_p, h2_s, cap):
    d = h2_p.shape[1]
    tm = MOE_TILE
    pt, st = h2_p.shape[0] // tm, h2_s.shape[0] // tm
    return pl.pallas_call(
        functools.partial(_dispatch_kernel, prompt_tiles=pt),
        grid_spec=pltpu.PrefetchScalarGridSpec(
            num_scalar_prefetch=3,
            grid=(pt + st,),
            in_specs=[pl.BlockSpec((TOP_K, tm), lambda i, *_: (0, i)),
                      pl.BlockSpec((tm, d), lambda i, *_: (jnp.minimum(i, pt - 1), 0)),
                      pl.BlockSpec((tm, d), lambda i, *_: (jnp.maximum(i - pt, 0), 0))],
            out_specs=pl.BlockSpec(memory_space=pl.ANY),
            scratch_shapes=[pltpu.VMEM((2, TILE_BUF_ROWS, d), BF16), pltpu.VMEM((EXPERT_BLOCK, d), BF16),
                            pltpu.SemaphoreType.DMA((2,))]),
        out_shape=jax.ShapeDtypeStruct((cap, d), BF16),
        compiler_params=_params(1), name="dispatch",
    )(starts, nq, meta, rbuf, h2_p, h2_s)


def _expert_kernel(be_ref, nu_ref, xs_ref, wup_ref, bup_ref, wdn_ref, bdn_ref, ys_ref, wup_bf, wdn_bf):
    i = pl.program_id(0)
    prev = be_ref[jnp.maximum(i - 1, 0)]

    @pl.when((i == 0) | (be_ref[i] != prev))
    def _():
        def cast(r, c):
            rows = pl.ds(pl.multiple_of(r * 64, 64), 64)
            wup_bf[rows, :] = wup_ref[0, rows, :].astype(BF16)
            wdn_bf[rows, :] = wdn_ref[0, rows, :].astype(BF16)
            return c

        lax.fori_loop(0, D_MODEL // 64, cast, 0)

    @pl.when(i < nu_ref[0])
    def _():
        hu = _dot(xs_ref[:, :D_MODEL], wup_bf[...]) + bup_ref[0]
        gate = jnp.minimum(hu[:, :D_FF], SWIGLU_LIMIT)
        lin = jnp.clip(hu[:, D_FF:], -SWIGLU_LIMIT, SWIGLU_LIMIT)
        glu = gate * _sigmoid(SWIGLU_ALPHA * gate)
        y = _dot(((lin + 1.0) * glu).astype(BF16), wdn_bf[...]) + bdn_ref[0]
        side = xs_ref[:, D_MODEL:].astype(F32)
        e = be_ref[i].astype(F32)
        weight = jnp.zeros((side.shape[0], 1), F32)
        for k in range(TOP_K):
            g_k = side[:, 3 * k:3 * k + 1] + side[:, 3 * k + 1:3 * k + 2] + side[:, 3 * k + 2:3 * k + 3]
            weight = weight + jnp.where(side[:, SIDE_ROWS - TOP_K + k:SIDE_ROWS - TOP_K + k + 1] == e, g_k, 0.0)
        ys_ref[...] = (weight * y).astype(ys_ref.dtype)

    @pl.when(i >= nu_ref[0])
    def _():
        ys_ref[...] = jnp.zeros(ys_ref.shape, ys_ref.dtype)


def _expert_call(block_e, n_used, xs, w_up, b_up, w_down, b_down):
    cap, dw = xs.shape
    d = w_down.shape[2]
    bm = EXPERT_BLOCK
    blk = lambda i, be, nu: (jnp.minimum(i, nu[0] - 1), 0)
    per_e = lambda i, be, nu: (be[i], 0, 0)
    return pl.pallas_call(
        _expert_kernel,
        grid_spec=pltpu.PrefetchScalarGridSpec(
            num_scalar_prefetch=2,
            grid=(cap // bm,),
            in_specs=[pl.BlockSpec((bm, dw), blk),
                      pl.BlockSpec((1, d, 2 * D_FF), per_e),
                      pl.BlockSpec((1, 1, 2 * D_FF), per_e),
                      pl.BlockSpec((1, D_FF, d), per_e),
                      pl.BlockSpec((1, 1, d), per_e)],
            out_specs=pl.BlockSpec((bm, d), lambda i, be, nu: (i, 0)),
            scratch_shapes=[pltpu.VMEM((d, 2 * D_FF), BF16), pltpu.VMEM((D_FF, d), BF16)]),
        out_shape=jax.ShapeDtypeStruct((cap, d), BF16),
        compiler_params=_params(1), name="experts",
    )(block_e, n_used, xs, w_up, b_up.reshape(N_EXPERTS, 1, -1), w_down, b_down.reshape(N_EXPERTS, 1, -1))


def _combine_kernel(starts_ref, nq_ref, x1_ref, gt_ref, rcol_ref, gfin_ref, ys_ref, o_ref, bufs_ref, sems,
                    *, tile0, tiles_per_group):
    x1 = x1_ref[...]
    gb, tt, d = x1.shape
    tm = gb * tt
    step = pl.program_id(0) * tiles_per_group + pl.program_id(1)
    n_steps = pl.num_programs(0) * tiles_per_group
    tile = tile0 + step
    chunks = functools.partial(_tile_chunks, ys_ref, bufs_ref, starts_ref, nq_ref, sems, to_hbm=False)

    @pl.when(step == 0)
    def _():
        bufs_ref[...] = jnp.zeros(bufs_ref.shape, bufs_ref.dtype)
        chunks(tile, wait=False)

    @pl.when(step + 1 < n_steps)
    def _():
        chunks(tile + 1, wait=False)

    chunks(tile, wait=True)
    buf_ref = bufs_ref.at[tile % 2]

    rows = rcol_ref[...].astype(jnp.int16)
    moe = jnp.zeros((tm, d), F32)
    for c in range(bufs_ref.shape[1] // PERM_COLS):
        r = (lax.broadcasted_iota(I32, (tm, PERM_COLS), 1) + c * PERM_COLS).astype(jnp.int16)
        hit = (r == rows[:, 0:1]) | (r == rows[:, 1:2]) | (r == rows[:, 2:3]) | (r == rows[:, 3:4])
        onehot = jnp.where(hit, jnp.ones((), BF16), jnp.zeros((), BF16))
        moe = moe + _dot(onehot, buf_ref[c * PERM_COLS:(c + 1) * PERM_COLS, :])
    xo = x1 + gt_ref[...] * moe.reshape(gb, tt, d)
    o_ref[...] = _rms(xo, gfin_ref[...])


def _combine_call(starts, nq, x1, mod3, rcol, g_final, ys, tok0, gb, tt):
    g, t, d = x1.shape
    tm = gb * tt
    tpg = t // tt
    t0 = tok0 // tm
    tok = lambda i, j, *_: (t0 + i * tpg + j, 0)
    return pl.pallas_call(
        functools.partial(_combine_kernel, tile0=t0, tiles_per_group=tpg),
        grid_spec=pltpu.PrefetchScalarGridSpec(
            num_scalar_prefetch=2,
            grid=(g // gb, tpg),
            in_specs=[pl.BlockSpec((gb, tt, d), lambda i, j, *_: (i, j, 0)),
                      pl.BlockSpec((gb, 1, d), lambda i, j, *_: (i, 0, 5)),
                      pl.BlockSpec((tm, TOP_K), tok),
                      pl.BlockSpec((1, 1, d), lambda i, j, *_: (0, 0, 0)),
                      pl.BlockSpec(memory_space=pl.ANY)],
            out_specs=pl.BlockSpec((gb, tt, d), lambda i, j, *_: (i, j, 0)),
            scratch_shapes=[pltpu.VMEM((2, TILE_BUF_ROWS, d), BF16), pltpu.SemaphoreType.DMA((2,))]),
        out_shape=jax.ShapeDtypeStruct((g, t, d), F32),
        compiler_params=_params(2), name="combine",
    )(starts, nq, x1, mod3, rcol, g_final.reshape(1, 1, d), ys)


def _group_blocking(groups, seq, tile):
    if seq >= tile:
        return 1, tile
    return tile // seq, seq


def kernel(x_prompt, x_sample, c_prompt, c_sample, state_ret, state_mlstm_c, state_mlstm_n, state_mlstm_m, state_conv, w_mod, b_mod, g_mix, g_ffn, w_in, b_igate, b_fgate, w_conv, b_conv, g_ret, g_mlstm, w_out, w_router, b_router, w_up, b_up, w_down, b_down, g_final):
    depth = w_mod.shape[0]
    assert depth == 1, "single-layer trunk"
    bp, tp, d = x_prompt.shape
    bs, ts, _ = x_sample.shape
    n_p, n_s = bp * tp, bs * ts
    hd = HEAD_DIM
    l = 0

    mod = _mod_call(jnp.concatenate([c_prompt, c_sample], axis=0), w_mod[l], b_mod[l])
    mod_p = mod[:bp].reshape(bp, 1, N_MOD * d)
    mod_s = mod[bp:].reshape(bs, 1, N_MOD * d)

    w_main_bf = w_in[l][:, :MAIN_COLS].astype(BF16)
    w_gate = jnp.pad(w_in[l][:, MAIN_COLS:], ((0, 0), (0, 128 - N_GATES)))
    w_gate_t = w_in[l][:, MAIN_COLS:].T
    w_out_bf = w_out[l].astype(BF16)
    bg_col = jnp.concatenate([b_igate[l], b_fgate[l]]).reshape(1, N_GATES)
    bg_row = bg_col.reshape(N_GATES, 1)

    groups = (
        (x_prompt, mod_p, min(CHUNK, tp), np.arange(tp),
         jnp.zeros((bp, RET_HEADS, hd, hd), F32), jnp.zeros((bp, MLSTM_HEADS, hd, hd), F32),
         jnp.zeros((bp, MLSTM_HEADS, hd), F32), jnp.zeros((bp, MLSTM_HEADS), F32),
         jnp.zeros((bp, CONV_WIDTH - 1, 2 * MLSTM_WIDTH), F32)),
        (x_sample, mod_s, min(CHUNK, ts), PAST_LEN + np.arange(ts),
         state_ret[l], state_mlstm_c[l], state_mlstm_n[l], state_mlstm_m[l], state_conv[l]),
    )

    ustrict = jnp.asarray(np.arange(MOE_TILE)[:, None] < np.arange(MOE_TILE)[None, :], BF16)
    staged = []
    for x3, mod3, L, pos, s0, c0, n0, m0, conv0 in groups:
        g, t, _ = x3.shape
        conv0p = jnp.pad(conv0.astype(F32), ((0, 0), (CONV_TAIL_ROWS - (CONV_WIDTH - 1), 0), (0, 0)))
        outproj = (g_ffn[l], w_out_bf, w_router[l].T, b_router[l], ustrict)
        chunked = t > L
        gs = MIXER_SEQS_CHUNKED if chunked else MIXER_SEQS_SHORT
        fused = chunked and gs * L == MOE_TILE
        res = _mixer_call(
            x3, mod3, g_mix[l], w_main_bf, w_gate, w_gate_t, _mixer_consts(L, pos),
            w_conv[l], b_conv[l], g_ret[l], g_mlstm[l],
            bg_col, bg_row, s0.astype(F32), c0.astype(F32), n0.astype(F32).reshape(g, MLSTM_HEADS, 1, hd),
            jnp.broadcast_to(m0.astype(F32)[:, :, None, None], (g, MLSTM_HEADS, 1, hd)), conv0p,
            L, gs, outproj if fused else None)
        s_new, c_new, n_new, m_new, tail = res[-5:]
        states = (s_new, c_new, n_new.reshape(g, MLSTM_HEADS, hd), m_new[:, :, 0, 0],
                  tail[:, CONV_TAIL_ROWS - (CONV_WIDTH - 1):, :])
        if fused:
            x1, h2, idx, rank, cnt = res[:5]
            blocking = (gs, L)
        else:
            blocking = _group_blocking(g, t, MOE_TILE)
            x1, h2, idx, rank, cnt = _outproj_call(res[0].reshape(g * t, d), x3, mod3, *outproj, *blocking)
        staged.append((x1, mod3, h2, idx, rank, states, cnt, blocking))

    n_tok = n_p + n_s
    tm, bm, ra = MOE_TILE, EXPERT_BLOCK, RUN_ALIGN
    n_tiles = n_tok // tm
    q_max = TILE_BUF_ROWS // ra
    n_blocks = -(-(n_tok * TOP_K + n_tiles * N_EXPERTS * (ra - 1)) // bm) + N_EXPERTS
    cap = n_blocks * bm
    counts = jnp.concatenate([s[6][:, :, 0] for s in staged], axis=0).astype(I32)
    run = (counts + ra - 1) // ra * ra
    region = jnp.sum(run, axis=0)
    padded = (region + bm - 1) // bm * bm
    pad_end = jnp.cumsum(padded)
    pad_start = pad_end - padded
    run_start = pad_start[None, :] + jnp.cumsum(run, axis=0) - run
    buf_end = jnp.cumsum(run, axis=1)
    buf_start = buf_end - run
    nq = (buf_end[:, -1] // ra).astype(I32)
    chunk_row = jnp.arange(q_max, dtype=I32) * ra
    chunk_e = jnp.minimum(jnp.sum((buf_end[:, None, :] <= chunk_row[None, :, None]).astype(I32), axis=2),
                          N_EXPERTS - 1)
    e_ids = jnp.arange(N_EXPERTS, dtype=I32)
    shift = run_start - buf_start
    starts = jnp.sum(jnp.where(chunk_e[:, :, None] == e_ids, shift[:, None, :], 0), axis=2) + chunk_row[None, :]
    starts = jnp.where(chunk_row[None, :] < buf_end[:, -1:], starts, 0).reshape(-1).astype(I32)

    idx_all = jnp.concatenate([s[3] for s in staged], axis=1)
    rank_all = jnp.concatenate([s[4] for s in staged], axis=1)
    buf_start_tok = jnp.repeat(buf_start, tm, axis=0).T
    rbuf = jnp.sum(jnp.where(idx_all[None] == e_ids[:, None, None], buf_start_tok[:, None, :], 0), axis=0) + rank_all
    rcol = rbuf.T
    block_row = jnp.arange(n_blocks, dtype=I32) * bm
    block_e = jnp.minimum(jnp.sum((pad_end[None, :] <= block_row[:, None]).astype(I32), axis=1), N_EXPERTS - 1)
    n_used = (pad_end[-1:] // bm).astype(I32)
    meta = jnp.concatenate([jnp.where(region > 0, pad_end - bm, -1), n_used]).astype(I32)

    xs = _dispatch_call(starts, nq, meta, rbuf, staged[0][2], staged[1][2], cap)
    ys = _expert_call(block_e, n_used, xs, w_up[l], b_up[l], w_down[l], b_down[l])

    outs = []
    tok0 = 0
    for x1, mod3, *_, blocking in staged:
        outs.append(_combine_call(starts, nq, x1, mod3, rcol, g_final, ys, tok0, *blocking))
        tok0 += x1.shape[0] * x1.shape[1]

    st_p, st_s = staged[0][5], staged[1][5]
    return (outs[0], outs[1]) + tuple(a[None] for a in st_p) + tuple(a[None] for a in st_s)
```

```python
import functools

import numpy as np
import jax
import jax.numpy as jnp
from jax import lax
from jax.experimental import pallas as pl
from jax.experimental.pallas import tpu as pltpu

F32 = jnp.float32
BF16 = jnp.bfloat16
I32 = jnp.int32

D_MODEL = 1024
PAST_LEN = 16384
RET_HEADS = 4
MLSTM_HEADS = 4
HEAD_DIM = 128
RET_WIDTH = RET_HEADS * HEAD_DIM
MLSTM_WIDTH = MLSTM_HEADS * HEAD_DIM
CONV_WIDTH = 4
CHUNK = 128
ROPE_BASE = 10000.0
N_EXPERTS = 32
TOP_K = 4
D_FF = D_MODEL
SWIGLU_LIMIT = 7.0
SWIGLU_ALPHA = 1.702
N_MOD = 6
EPS = 1e-6
MAIN_COLS = 4 * RET_WIDTH + 2 * MLSTM_WIDTH + 2 * MLSTM_WIDTH
N_GATES = 2 * MLSTM_HEADS
OFF_RQ, OFF_RK, OFF_RV, OFF_RG = 0, RET_WIDTH, 2 * RET_WIDTH, 3 * RET_WIDTH
OFF_MQK = 4 * RET_WIDTH
OFF_MV = OFF_MQK + 2 * MLSTM_WIDTH
OFF_MO = OFF_MV + MLSTM_WIDTH

VMEM_LIMIT_BYTES = 56 * 1024 * 1024
MOE_TILE = 512
EXPERT_BLOCK = 512
EXPERT_ROW_STEP = 128
MIXER_SEQS_CHUNKED = 4
MIXER_SEQS_SHORT = 8
RUN_ALIGN = 16
SIDE_ROWS = 4 * TOP_K
SIDE_LANES = 128
TILE_BUF_ROWS = MOE_TILE * TOP_K + N_EXPERTS * RUN_ALIGN
PERM_ROWS = 1280
PERM_COLS = 512
CONV_TAIL_ROWS = 8


def _params(n_axes=1):
    return pltpu.CompilerParams(dimension_semantics=("arbitrary",) * n_axes, vmem_limit_bytes=VMEM_LIMIT_BYTES)


def _dot(a, b):
    return jnp.dot(a, b, preferred_element_type=F32)


def _dot_nt(a, b):
    return lax.dot_general(a, b, (((1,), (1,)), ((), ())), preferred_element_type=F32)


def _dot_tn(a, b):
    return lax.dot_general(a, b, (((0,), (0,)), ((), ())), preferred_element_type=F32)


def _split(a):
    hi = a.astype(BF16)
    lo = (a - hi.astype(F32)).astype(BF16)
    return hi, lo


def _dot3(a, b, dot=_dot):
    ah, al = _split(a)
    bh, bl = _split(b)
    return dot(ah, bh) + (dot(al, bh) + dot(ah, bl))


def _sigmoid(x):
    return 0.5 * (jnp.tanh(0.5 * x) + 1.0)


def _log_sigmoid(x):
    return jnp.minimum(x, 0.0) - jnp.log1p(jnp.exp(-jnp.abs(x)))


def _rms(x, g):
    ms = jnp.mean(x * x, axis=-1, keepdims=True)
    return (x * lax.rsqrt(ms + EPS)) * g


def _layer_norm(x, g):
    mu = jnp.mean(x, axis=-1, keepdims=True)
    xc = x - mu
    var = jnp.mean(xc * xc, axis=-1, keepdims=True)
    return xc * lax.rsqrt(var + EPS) * g


def _mod_kernel(c_ref, w_ref, b_ref, o_ref):
    c = c_ref[...]
    o_ref[...] = _dot3(c * _sigmoid(c), w_ref[...]) + b_ref[...]


def _mod_call(c_all, w_mod, b_mod):
    rows, d = c_all.shape
    cols = w_mod.shape[1]
    tn = 1024
    return pl.pallas_call(
        _mod_kernel,
        grid=(cols // tn,),
        in_specs=[pl.BlockSpec((rows, d), lambda j: (0, 0)),
                  pl.BlockSpec((d, tn), lambda j: (0, j)),
                  pl.BlockSpec((1, tn), lambda j: (0, j))],
        out_specs=pl.BlockSpec((rows, tn), lambda j: (0, j)),
        out_shape=jax.ShapeDtypeStruct((rows, cols), F32),
        compiler_params=_params(1), name="mod",
    )(c_all, w_mod, b_mod.reshape(1, cols))


def _mixer_kernel(x_ref, sh_ref, sc_ref, gmix_ref, w_ref, wg_ref, wgt_ref,
                  cos_ref, sin_ref, dec_ref, qd_ref, kd_ref, cd_ref,
                  tril_ref, triu_ref, wconv_ref, bconv_ref, gret_ref, gml_ref, bgc_ref, bgr_ref,
                  s0_ref, c0_ref, n0_ref, m0_ref, conv0_ref,
                  y_ref, s_ref, c_ref, n_ref, m_ref, tail_ref, p_ref, xp_ref):
    @pl.when(pl.program_id(1) == 0)
    def _():
        s_ref[...] = s0_ref[...]
        c_ref[...] = c0_ref[...]
        n_ref[...] = n0_ref[...]
        m_ref[...] = m0_ref[...]
        tail_ref[...] = conv0_ref[...]

    x = x_ref[...]
    gs, L, d = x.shape
    h = (_rms(x, gmix_ref[...]) * (1.0 + sc_ref[...]) + sh_ref[...]).reshape(gs * L, d)
    hb = h.astype(BF16)
    for j in range(MAIN_COLS // 1024):
        p_ref[:, j * 1024:(j + 1) * 1024] = _dot(hb, w_ref[:, j * 1024:(j + 1) * 1024])

    chains = []
    for sq in range(gs):
        h_sq = h[sq * L:(sq + 1) * L, :]
        gc = _dot3(h_sq, wg_ref[...])[:, :N_GATES]
        gr = _dot3(wgt_ref[...], h_sq, dot=_dot_nt)
        chains += _mixer_sequence(p_ref.at[pl.ds(sq * L, L)], gc, gr, cos_ref, sin_ref, dec_ref, qd_ref, kd_ref,
                                  cd_ref, tril_ref, triu_ref, wconv_ref, bconv_ref, gret_ref, gml_ref, bgc_ref,
                                  bgr_ref, y_ref.at[sq], s_ref.at[sq], c_ref.at[sq], n_ref.at[sq], m_ref.at[sq],
                                  tail_ref.at[sq], xp_ref.at[sq])
    while chains:
        alive = []
        for chain in chains:
            if next(chain, None) is not None:
                alive.append(chain)
        chains = alive


N_MIXER_INPUTS = 26
N_OUTPROJ_INPUTS = 8


def _mixer_outproj_kernel(*refs):
    mixer_in = refs[:N_MIXER_INPUTS]
    outproj_in = refs[N_MIXER_INPUTS:N_MIXER_INPUTS + N_OUTPROJ_INPUTS]
    x1_ref, h2_ref, idx_ref, rank_ref, cnt_ref, s_ref, c_ref, n_ref, m_ref, tail_ref, p_ref, xp_ref, y_ref = refs[
        N_MIXER_INPUTS + N_OUTPROJ_INPUTS:]
    _mixer_kernel(*mixer_in, y_ref, s_ref, c_ref, n_ref, m_ref, tail_ref, p_ref, xp_ref)
    gs, L, d = y_ref.shape
    _outproj_router(y_ref[...].reshape(gs * L, d), mixer_in[0], *outproj_in, x1_ref, h2_ref, idx_ref, rank_ref, cnt_ref)


def _mixer_sequence(p_ref, gc, gr, cos_ref, sin_ref, dec_ref, qd_ref, kd_ref, cd_ref,
                    tril_ref, triu_ref, wconv_ref, bconv_ref, gret_ref, gml_ref, bgc_ref, bgr_ref,
                    y_ref, s_ref, c_ref, n_ref, m_ref, tail_ref, xp_ref):
    L = p_ref.shape[0]
    cos = cos_ref[...]
    sin = sin_ref[...]
    scale = HEAD_DIM ** -0.5

    def rot(x):
        return x * cos + pltpu.roll(x, HEAD_DIM // 2, axis=1) * sin

    def retention_head(h):
        lo = h * HEAD_DIM
        q = rot(p_ref[:, OFF_RQ + lo:OFF_RQ + lo + HEAD_DIM])
        k = rot(p_ref[:, OFF_RK + lo:OFF_RK + lo + HEAD_DIM]) * scale
        v = p_ref[:, OFF_RV + lo:OFF_RV + lo + HEAD_DIM].astype(BF16)
        s_old = s_ref[h]
        yield True
        scores = _dot_nt(q.astype(BF16), k.astype(BF16)) * dec_ref[h]
        yield True
        out = _dot(scores.astype(BF16), v) + _dot((q * qd_ref[h]).astype(BF16), s_old.astype(BF16))
        s_ref[h] = cd_ref[h] * s_old + _dot_tn((k * kd_ref[h]).astype(BF16), v)
        yield True
        g = p_ref[:, OFF_RG + lo:OFF_RG + lo + HEAD_DIM]
        y_ref[:, lo:lo + HEAD_DIM] = ((g * _sigmoid(g)) * _layer_norm(out, gret_ref[:, lo:lo + HEAD_DIM])).astype(y_ref.dtype)

    xp_ref[0:CONV_TAIL_ROWS, :] = tail_ref[...]
    xp_ref[CONV_TAIL_ROWS:CONV_TAIL_ROWS + L, :] = p_ref[:, OFF_MQK:OFF_MQK + 2 * MLSTM_WIDTH]
    acc = bconv_ref[...] + wconv_ref[0:1, :] * xp_ref[CONV_TAIL_ROWS - 3:CONV_TAIL_ROWS - 3 + L, :]
    for j in range(1, CONV_WIDTH):
        acc = acc + wconv_ref[j:j + 1, :] * xp_ref[CONV_TAIL_ROWS - 3 + j:CONV_TAIL_ROWS - 3 + j + L, :]
    tail_ref[...] = xp_ref[L:L + CONV_TAIL_ROWS, :]
    xp_ref[CONV_TAIL_ROWS:CONV_TAIL_ROWS + L, :] = acc * _sigmoid(acc)

    gcol = gc + bgc_ref[...]
    is_f_col = lax.broadcasted_iota(I32, gcol.shape, 1) >= MLSTM_HEADS
    gcol = jnp.where(is_f_col, _log_sigmoid(gcol), gcol)
    grow = gr + bgr_ref[...]
    is_f_row = lax.broadcasted_iota(I32, grow.shape, 0) >= MLSTM_HEADS
    grow = jnp.where(is_f_row, _log_sigmoid(grow), grow)
    bcol_all = _dot3(tril_ref[...], gcol)
    brow_all = _dot3(grow, triu_ref[...])
    causal = lax.broadcasted_iota(I32, (L, L), 0) >= lax.broadcasted_iota(I32, (L, L), 1)

    def mlstm_head(h):
        lo = h * HEAD_DIM
        q = xp_ref[CONV_TAIL_ROWS:CONV_TAIL_ROWS + L, lo:lo + HEAD_DIM]
        k = xp_ref[CONV_TAIL_ROWS:CONV_TAIL_ROWS + L, MLSTM_WIDTH + lo:MLSTM_WIDTH + lo + HEAD_DIM] * scale
        v = p_ref[:, OFF_MV + lo:OFF_MV + lo + HEAD_DIM].astype(BF16)
        ic_col = gcol[:, h:h + 1]
        ic_row = grow[h:h + 1, :]
        b_col = bcol_all[:, MLSTM_HEADS + h:MLSTM_HEADS + h + 1]
        b_row = brow_all[MLSTM_HEADS + h:MLSTM_HEADS + h + 1, :]
        c_old = c_ref[h]
        n_old = n_ref[h]
        m_old = m_ref[h][:, 0:1]

        d_log = jnp.where(causal, b_col - b_row + ic_row, -jnp.inf)
        inter = b_col + m_old
        m_t = jnp.maximum(inter, jnp.max(d_log, axis=1, keepdims=True))
        yield True
        w_intra = jnp.exp(d_log - m_t)
        w_inter = jnp.exp(inter - m_t)
        qb = q.astype(BF16)
        s = _dot_nt(qb, k.astype(BF16)) * w_intra
        yield True
        num = _dot(s.astype(BF16), v) + w_inter * _dot(qb, c_old.astype(BF16))
        den = jnp.sum(s, axis=1, keepdims=True) + w_inter * jnp.sum(q * n_old, axis=1, keepdims=True)
        yield True
        hh = num / jnp.maximum(jnp.abs(den), jnp.exp(-m_t))

        b_last = b_col[L - 1:L, :]
        w_log_col = b_last - b_col + ic_col
        m_new = jnp.maximum(b_last + m_old, jnp.max(w_log_col, axis=0, keepdims=True))
        wk = jnp.exp(w_log_col - m_new) * k
        cdec = jnp.exp(b_last + m_old - m_new)
        yield True
        c_ref[h] = cdec * c_old + _dot_tn(wk.astype(BF16), v)
        n_ref[h] = cdec * n_old + jnp.sum(wk, axis=0, keepdims=True)
        m_ref[h] = jnp.broadcast_to(m_new, (1, HEAD_DIM))
        yield True
        o = p_ref[:, OFF_MO + lo:OFF_MO + lo + HEAD_DIM]
        y_ref[:, RET_WIDTH + lo:RET_WIDTH + lo + HEAD_DIM] = (
            _sigmoid(o) * _layer_norm(hh, gml_ref[:, lo:lo + HEAD_DIM])).astype(y_ref.dtype)

    return [retention_head(h) for h in range(RET_HEADS)] + [mlstm_head(h) for h in range(MLSTM_HEADS)]


def _mixer_consts(L, pos):
    f32 = np.float32
    half = HEAD_DIM // 2
    inv_freq = np.power(f32(ROPE_BASE), -np.arange(half, dtype=f32) / f32(half)).astype(f32)
    ang = (pos.astype(f32)[:, None] * inv_freq[None, :]).astype(f32)
    cos = np.concatenate([np.cos(ang), np.cos(ang)], axis=-1).astype(f32)
    sin = np.concatenate([-np.sin(ang), np.sin(ang)], axis=-1).astype(f32)
    log_gamma = np.log1p(-np.exp2(-5.0 - np.arange(RET_HEADS, dtype=np.float64)))
    idx = np.arange(L, dtype=np.float64)
    rel = idx[:, None] - idx[None, :]
    dec = np.where(rel >= 0, np.exp(log_gamma[:, None, None] * np.maximum(rel, 0.0)), 0.0)
    qd = np.broadcast_to(np.exp(log_gamma[:, None] * (idx + 1.0))[..., None], (RET_HEADS, L, HEAD_DIM))
    kd = np.broadcast_to(np.exp(log_gamma[:, None] * (L - 1.0 - idx))[..., None], (RET_HEADS, L, HEAD_DIM))
    cd = np.broadcast_to(np.exp(log_gamma * L)[:, None, None], (RET_HEADS, 1, HEAD_DIM))
    tril = rel >= 0
    triu = rel <= 0
    return tuple(jnp.asarray(a, F32) for a in (cos, sin, dec, qd, kd, cd, tril, triu))


def _mixer_call(x3, mod3, g_mix, w_main_bf, w_gate, w_gate_t, consts, w_conv, b_conv, g_ret, g_mlstm, bg_col, bg_row,
                s0, c0, n0, m0, conv0, L, gs, outproj=None):
    cos, sin, dec, qd, kd, cd, tril, triu = consts
    groups, seq, d = x3.shape
    nc = seq // L
    hd = HEAD_DIM
    full = lambda *shape: pl.BlockSpec(shape, lambda g, c: (0,) * len(shape))
    mod_spec = lambda k: pl.BlockSpec((gs, 1, d), lambda g, c: (g, 0, k))
    state4 = pl.BlockSpec((gs, RET_HEADS, hd, hd), lambda g, c: (g, 0, 0, 0))
    vec4 = pl.BlockSpec((gs, MLSTM_HEADS, 1, hd), lambda g, c: (g, 0, 0, 0))
    tail3 = pl.BlockSpec((gs, CONV_TAIL_ROWS, 2 * MLSTM_WIDTH), lambda g, c: (g, 0, 0))
    row = lambda g, c: (g, c, 0)
    in_specs = [pl.BlockSpec((gs, L, d), row), mod_spec(0), mod_spec(1),
                full(1, 1, d), full(d, MAIN_COLS), full(d, 128), full(N_GATES, d),
                pl.BlockSpec((L, hd), lambda g, c: (c, 0)),
                pl.BlockSpec((L, hd), lambda g, c: (c, 0)),
                full(RET_HEADS, L, L), full(RET_HEADS, L, hd), full(RET_HEADS, L, hd), full(RET_HEADS, 1, hd),
                full(L, L), full(L, L),
                full(CONV_WIDTH, 2 * MLSTM_WIDTH), full(1, 2 * MLSTM_WIDTH),
                full(1, RET_WIDTH), full(1, MLSTM_WIDTH), full(1, N_GATES), full(N_GATES, 1),
                state4, state4, vec4, vec4, tail3]
    args = [x3, mod3, mod3, g_mix.reshape(1, 1, d), w_main_bf, w_gate, w_gate_t,
            cos, sin, dec, qd, kd, cd, tril, triu, w_conv, b_conv.reshape(1, -1),
            g_ret.reshape(1, -1), g_mlstm.reshape(1, -1), bg_col, bg_row, s0, c0, n0, m0, conv0]
    assert len(in_specs) == len(args) == N_MIXER_INPUTS
    state_specs = [state4, state4, vec4, vec4, tail3]
    state_shapes = [jax.ShapeDtypeStruct((groups, RET_HEADS, hd, hd), F32),
                    jax.ShapeDtypeStruct((groups, MLSTM_HEADS, hd, hd), F32),
                    jax.ShapeDtypeStruct((groups, MLSTM_HEADS, 1, hd), F32),
                    jax.ShapeDtypeStruct((groups, MLSTM_HEADS, 1, hd), F32),
                    jax.ShapeDtypeStruct((groups, CONV_TAIL_ROWS, 2 * MLSTM_WIDTH), F32)]
    scratch = [pltpu.VMEM((gs * L, MAIN_COLS), F32), pltpu.VMEM((gs, CONV_TAIL_ROWS + L, 2 * MLSTM_WIDTH), F32)]
    y_block, y_shape = (gs, L, RET_WIDTH + MLSTM_WIDTH), (groups, seq, RET_WIDTH + MLSTM_WIDTH)
    if outproj is None:
        kernel_fn = _mixer_kernel
        out_specs = [pl.BlockSpec(y_block, row)] + state_specs
        out_shape = [jax.ShapeDtypeStruct(y_shape, F32)] + state_shapes
    else:
        g_ffn, w_out_bf, w_router_t, b_router, ustrict = outproj
        tm = gs * L
        n = groups * seq
        tile = lambda g, c: g * nc + c
        kernel_fn = _mixer_outproj_kernel
        in_specs += [mod_spec(2), mod_spec(3), mod_spec(4),
                     full(1, 1, d), full(d, d), full(N_EXPERTS, d), full(N_EXPERTS, 1), full(tm, tm)]
        args += [mod3, mod3, mod3, g_ffn.reshape(1, 1, d), w_out_bf, w_router_t, b_router.reshape(N_EXPERTS, 1),
                 ustrict]
        assert len(args) == N_MIXER_INPUTS + N_OUTPROJ_INPUTS
        out_specs = [pl.BlockSpec((gs, L, d), row),
                     pl.BlockSpec((tm, d + SIDE_LANES), lambda g, c: (tile(g, c), 0)),
                     pl.BlockSpec((TOP_K, tm), lambda g, c: (0, tile(g, c))),
                     pl.BlockSpec((TOP_K, tm), lambda g, c: (0, tile(g, c))),
                     pl.BlockSpec((1, N_EXPERTS, 128), lambda g, c: (tile(g, c), 0, 0))] + state_specs
        out_shape = [jax.ShapeDtypeStruct((groups, seq, d), F32),
                     jax.ShapeDtypeStruct((n, d + SIDE_LANES), BF16),
                     jax.ShapeDtypeStruct((TOP_K, n), I32),
                     jax.ShapeDtypeStruct((TOP_K, n), I32),
                     jax.ShapeDtypeStruct((n // tm, N_EXPERTS, 128), F32)] + state_shapes
        scratch = scratch + [pltpu.VMEM(y_block, F32)]
    return pl.pallas_call(
        kernel_fn,
        grid=(groups // gs, nc),
        in_specs=in_specs, out_specs=out_specs, out_shape=out_shape, scratch_shapes=scratch,
        compiler_params=_params(2), name="mixer",
    )(*args)


def _outproj_kernel(y_ref, x_ref, gt_ref, sh_ref, sc_ref, g_ref, w_ref, wrt_ref, br_ref, ustrict_ref,
                    x1_ref, h2_ref, idx_ref, rank_ref, cnt_ref):
    _outproj_router(y_ref[...], x_ref, gt_ref, sh_ref, sc_ref, g_ref, w_ref, wrt_ref, br_ref, ustrict_ref,
                    x1_ref, h2_ref, idx_ref, rank_ref, cnt_ref)


def _outproj_router(y, x_ref, gt_ref, sh_ref, sc_ref, g_ref, w_ref, wrt_ref, br_ref, ustrict_ref,
                    x1_ref, h2_ref, idx_ref, rank_ref, cnt_ref):
    x = x_ref[...]
    gb, tt, d = x.shape
    tm = gb * tt
    mixed = _dot(y.astype(BF16), w_ref[...])
    x1 = x + gt_ref[...] * mixed.reshape(gb, tt, d)
    x1_ref[...] = x1
    h2 = (_rms(x1, g_ref[...]) * (1.0 + sc_ref[...]) + sh_ref[...]).reshape(tm, d)
    h2_ref[:, :d] = h2.astype(BF16)

    work = _dot3(wrt_ref[...], h2, dot=_dot_nt) + br_ref[...]
    e_iota = lax.broadcasted_iota(I32, work.shape, 0).astype(F32)
    vals, idxs, sels = [], [], []
    for _ in range(TOP_K):
        mx = jnp.max(work, axis=0, keepdims=True)
        ik = jnp.min(jnp.where(work == mx, e_iota, float(N_EXPERTS)), axis=0, keepdims=True)
        sel = e_iota == ik
        vals.append(mx)
        idxs.append(ik)
        sels.append(sel)
        work = jnp.where(sel, -jnp.inf, work)
    exps = [jnp.exp(v - vals[0]) for v in vals]
    denom = exps[0] + exps[1] + exps[2] + exps[3]
    gates = [e / denom for e in exps]
    idx_ref[...] = jnp.concatenate(idxs, axis=0).astype(I32)

    pieces = []
    for gk in gates:
        p1 = gk.astype(BF16)
        r1 = gk - p1.astype(F32)
        p2 = r1.astype(BF16)
        pieces += [p1, p2, (r1 - p2.astype(F32)).astype(BF16)]
    side = jnp.concatenate(pieces + [ik.astype(BF16) for ik in idxs], axis=0)
    eye = (lax.broadcasted_iota(I32, (SIDE_ROWS, SIDE_LANES), 0)
           == lax.broadcasted_iota(I32, (SIDE_ROWS, SIDE_LANES), 1)).astype(BF16)
    h2_ref[:, d:] = _dot_tn(side, eye).astype(BF16)

    mask = (sels[0] | sels[1] | sels[2] | sels[3]).astype(F32)
    before = _dot(mask.astype(BF16), ustrict_ref[...])
    ranks = [jnp.sum(jnp.where(sel, before, 0.0), axis=0, keepdims=True) for sel in sels]
    rank_ref[...] = jnp.concatenate(ranks, axis=0).astype(I32)
    cnt_ref[0] = jnp.broadcast_to(jnp.sum(mask, axis=1, keepdims=True), cnt_ref.shape[1:])


def _outproj_call(y, x3, mod3, g_ffn, w_out_bf, w_router_t, b_router, ustrict, gb, tt):
    g, t, d = x3.shape
    n = g * t
    tm = gb * tt
    tpg = t // tt
    row = lambda i, j: (i * tpg + j, 0)
    col = lambda i, j: (0, i * tpg + j)
    mod_spec = lambda k: pl.BlockSpec((gb, 1, d), lambda i, j: (i, 0, k))
    const = lambda *shape: pl.BlockSpec(shape, lambda i, j: (0,) * len(shape))
    return pl.pallas_call(
        _outproj_kernel,
        grid=(g // gb, tpg),
        in_specs=[pl.BlockSpec((tm, d), row),
                  pl.BlockSpec((gb, tt, d), lambda i, j: (i, j, 0)),
                  mod_spec(2), mod_spec(3), mod_spec(4),
                  const(1, 1, d), const(d, d), const(N_EXPERTS, d), const(N_EXPERTS, 1),
                  const(tm, tm)],
        out_specs=[pl.BlockSpec((gb, tt, d), lambda i, j: (i, j, 0)),
                   pl.BlockSpec((tm, d + SIDE_LANES), row),
                   pl.BlockSpec((TOP_K, tm), col),
                   pl.BlockSpec((TOP_K, tm), col),
                   pl.BlockSpec((1, N_EXPERTS, 128), lambda i, j: (i * tpg + j, 0, 0))],
        out_shape=[jax.ShapeDtypeStruct((g, t, d), F32),
                   jax.ShapeDtypeStruct((n, d + SIDE_LANES), BF16),
                   jax.ShapeDtypeStruct((TOP_K, n), I32),
                   jax.ShapeDtypeStruct((TOP_K, n), I32),
                   jax.ShapeDtypeStruct((n // tm, N_EXPERTS, 128), F32)],
        compiler_params=_params(2), name="outproj_router",
    )(y, x3, mod3, mod3, mod3, g_ffn.reshape(1, 1, d), w_out_bf, w_router_t, b_router.reshape(N_EXPERTS, 1),
      ustrict)


def _chunk_copy(hbm_ref, hbm_row, buf_ref, chunk, sem, to_hbm):
    hbm = hbm_ref.at[pl.ds(pl.multiple_of(hbm_row, RUN_ALIGN), RUN_ALIGN)]
    buf = buf_ref.at[pl.ds(pl.multiple_of(chunk * RUN_ALIGN, RUN_ALIGN), RUN_ALIGN)]
    return pltpu.make_async_copy(buf, hbm, sem) if to_hbm else pltpu.make_async_copy(hbm, buf, sem)


def _tile_chunks(hbm_ref, bufs_ref, starts_ref, nq_ref, sems, tile, to_hbm, wait):
    slot = tile % 2
    first = tile * (bufs_ref.shape[1] // RUN_ALIGN)

    def body(q, c):
        cp = _chunk_copy(hbm_ref, starts_ref[first + q], bufs_ref.at[slot], q, sems.at[slot], to_hbm)
        cp.wait() if wait else cp.start()
        return c

    lax.fori_loop(0, nq_ref[tile], body, 0)


def _dispatch_kernel(starts_ref, nq_ref, meta_ref, rbuf_ref, h2p_ref, h2s_ref, xs_ref, bufs_ref, zero_ref, sems,
                     *, prompt_tiles):
    i = pl.program_id(0)
    tm = h2p_ref.shape[0]
    bm = zero_ref.shape[0]
    n_blocks = xs_ref.shape[0] // bm
    sem = sems.at[0]
    buf_ref = bufs_ref.at[i % 2]

    @pl.when(i == 0)
    def _():
        zero_ref[...] = jnp.zeros(zero_ref.shape, zero_ref.dtype)

        def zero_copy(row):
            return pltpu.make_async_copy(zero_ref, xs_ref.at[pl.ds(pl.multiple_of(row, bm), bm)], sem)

        def tails(fn):
            def body(e, c):
                @pl.when(meta_ref[e] >= 0)
                def _():
                    fn(zero_copy(meta_ref[e]))
                return c
            lax.fori_loop(0, N_EXPERTS, body, 0)

        def unused(fn):
            def body(b, c):
                fn(zero_copy(b * bm))
                return c
            lax.fori_loop(meta_ref[N_EXPERTS], n_blocks, body, 0)

        tails(lambda cp: cp.start())
        unused(lambda cp: cp.start())
        tails(lambda cp: cp.wait())
        unused(lambda cp: cp.wait())

    def build(h2_ref):
        h2 = h2_ref[...]
        rb = rbuf_ref[...].astype(jnp.int16)
        for c in range(bufs_ref.shape[1] // PERM_ROWS):
            r = (lax.broadcasted_iota(I32, (PERM_ROWS, tm), 0) + c * PERM_ROWS).astype(jnp.int16)
            hit = (r == rb[0:1, :]) | (r == rb[1:2, :]) | (r == rb[2:3, :]) | (r == rb[3:4, :])
            onehot = jnp.where(hit, jnp.ones((), BF16), jnp.zeros((), BF16))
            buf_ref[c * PERM_ROWS:(c + 1) * PERM_ROWS, :] = _dot(onehot, h2).astype(BF16)

    @pl.when(i < prompt_tiles)
    def _():
        build(h2p_ref)

    @pl.when(i >= prompt_tiles)
    def _():
        build(h2s_ref)

    chunks = functools.partial(_tile_chunks, xs_ref, bufs_ref, starts_ref, nq_ref, sems, to_hbm=True)
    chunks(i, wait=False)

    @pl.when(i > 0)
    def _():
        chunks(i - 1, wait=True)

    @pl.when(i == pl.num_programs(0) - 1)
    def _():
        chunks(i, wait=True)


def _dispatch_call(starts, nq, meta, rbuf, h2_p, h2_s, cap):
    d = h2_p.shape[1]
    tm = MOE_TILE
    pt, st = h2_p.shape[0] // tm, h2_s.shape[0] // tm
    return pl.pallas_call(
        functools.partial(_dispatch_kernel, prompt_tiles=pt),
        grid_spec=pltpu.PrefetchScalarGridSpec(
            num_scalar_prefetch=3,
            grid=(pt + st,),
            in_specs=[pl.BlockSpec((TOP_K, tm), lambda i, *_: (0, i)),
                      pl.BlockSpec((tm, d), lambda i, *_: (jnp.minimum(i, pt - 1), 0)),
                      pl.BlockSpec((tm, d), lambda i, *_: (jnp.maximum(i - pt, 0), 0))],
            out_specs=pl.BlockSpec(memory_space=pl.ANY),
            scratch_shapes=[pltpu.VMEM((2, TILE_BUF_ROWS, d), BF16), pltpu.VMEM((EXPERT_BLOCK, d), BF16),
                            pltpu.SemaphoreType.DMA((2,))]),
        out_shape=jax.ShapeDtypeStruct((cap, d), BF16),
        compiler_params=_params(1), name="dispatch",
    )(starts, nq, meta, rbuf, h2_p, h2_s)


def _expert_kernel(be_ref, nu_ref, valid_ref, xs_ref, wup_ref, bup_ref, wdn_ref, bdn_ref, ys_ref, wup_bf, wdn_bf):
    i = pl.program_id(0)
    bm = ys_ref.shape[0]
    prev = be_ref[jnp.maximum(i - 1, 0)]

    @pl.when((i == 0) | (be_ref[i] != prev))
    def _():
        def cast(r, c):
            rows = pl.ds(pl.multiple_of(r * 64, 64), 64)
            wup_bf[rows, :] = wup_ref[0, rows, :].astype(BF16)
            wdn_bf[rows, :] = wdn_ref[0, rows, :].astype(BF16)
            return c

        lax.fori_loop(0, D_MODEL // 64, cast, 0)

    def compute(m):
        hu = _dot(xs_ref[0:m, :D_MODEL], wup_bf[...]) + bup_ref[0]
        gate = jnp.minimum(hu[:, :D_FF], SWIGLU_LIMIT)
        lin = jnp.clip(hu[:, D_FF:], -SWIGLU_LIMIT, SWIGLU_LIMIT)
        glu = gate * _sigmoid(SWIGLU_ALPHA * gate)
        y = _dot(((lin + 1.0) * glu).astype(BF16), wdn_bf[...]) + bdn_ref[0]
        side = xs_ref[0:m, D_MODEL:].astype(F32)
        e = be_ref[i].astype(F32)
        weight = jnp.zeros((m, 1), F32)
        for k in range(TOP_K):
            g_k = side[:, 3 * k:3 * k + 1] + side[:, 3 * k + 1:3 * k + 2] + side[:, 3 * k + 2:3 * k + 3]
            weight = weight + jnp.where(side[:, SIDE_ROWS - TOP_K + k:SIDE_ROWS - TOP_K + k + 1] == e, g_k, 0.0)
        ys_ref[0:m, :] = (weight * y).astype(ys_ref.dtype)
        if m < bm:
            ys_ref[m:, :] = jnp.zeros((bm - m, ys_ref.shape[1]), ys_ref.dtype)

    nv = valid_ref[i]
    for m in range(EXPERT_ROW_STEP, bm + 1, EXPERT_ROW_STEP):
        @pl.when((nv > m - EXPERT_ROW_STEP) & (nv <= m))
        def _(m=m):
            compute(m)

    @pl.when(nv == 0)
    def _():
        ys_ref[...] = jnp.zeros(ys_ref.shape, ys_ref.dtype)


def _expert_call(block_e, n_used, valid, xs, w_up, b_up, w_down, b_down):
    cap, dw = xs.shape
    d = w_down.shape[2]
    bm = EXPERT_BLOCK
    blk = lambda i, be, nu, nv: (jnp.minimum(i, nu[0] - 1), 0)
    per_e = lambda i, be, nu, nv: (be[i], 0, 0)
    return pl.pallas_call(
        _expert_kernel,
        grid_spec=pltpu.PrefetchScalarGridSpec(
            num_scalar_prefetch=3,
            grid=(cap // bm,),
            in_specs=[pl.BlockSpec((bm, dw), blk),
                      pl.BlockSpec((1, d, 2 * D_FF), per_e),
                      pl.BlockSpec((1, 1, 2 * D_FF), per_e),
                      pl.BlockSpec((1, D_FF, d), per_e),
                      pl.BlockSpec((1, 1, d), per_e)],
            out_specs=pl.BlockSpec((bm, d), lambda i, be, nu, nv: (i, 0)),
            scratch_shapes=[pltpu.VMEM((d, 2 * D_FF), BF16), pltpu.VMEM((D_FF, d), BF16)]),
        out_shape=jax.ShapeDtypeStruct((cap, d), BF16),
        compiler_params=_params(1), name="experts",
    )(block_e, n_used, valid, xs, w_up, b_up.reshape(N_EXPERTS, 1, -1), w_down, b_down.reshape(N_EXPERTS, 1, -1))


def _combine_kernel(starts_ref, nq_ref, x1_ref, gt_ref, rcol_ref, gfin_ref, ys_ref, o_ref, bufs_ref, sems,
                    *, tile0, tiles_per_group):
    x1 = x1_ref[...]
    gb, tt, d = x1.shape
    tm = gb * tt
    step = pl.program_id(0) * tiles_per_group + pl.program_id(1)
    n_steps = pl.num_programs(0) * tiles_per_group
    tile = tile0 + step
    chunks = functools.partial(_tile_chunks, ys_ref, bufs_ref, starts_ref, nq_ref, sems, to_hbm=False)

    @pl.when(step == 0)
    def _():
        bufs_ref[...] = jnp.zeros(bufs_ref.shape, bufs_ref.dtype)
        chunks(tile, wait=False)

    @pl.when(step + 1 < n_steps)
    def _():
        chunks(tile + 1, wait=False)

    chunks(tile, wait=True)
    buf_ref = bufs_ref.at[tile % 2]

    rows = rcol_ref[...].astype(jnp.int16)
    moe = jnp.zeros((tm, d), F32)
    for c in range(bufs_ref.shape[1] // PERM_COLS):
        r = (lax.broadcasted_iota(I32, (tm, PERM_COLS), 1) + c * PERM_COLS).astype(jnp.int16)
        hit = (r == rows[:, 0:1]) | (r == rows[:, 1:2]) | (r == rows[:, 2:3]) | (r == rows[:, 3:4])
        onehot = jnp.where(hit, jnp.ones((), BF16), jnp.zeros((), BF16))
        moe = moe + _dot(onehot, buf_ref[c * PERM_COLS:(c + 1) * PERM_COLS, :])
    xo = x1 + gt_ref[...] * moe.reshape(gb, tt, d)
    o_ref[...] = _rms(xo, gfin_ref[...])


def _combine_call(starts, nq, x1, mod3, rcol, g_final, ys, tok0, gb, tt):
    g, t, d = x1.shape
    tm = gb * tt
    tpg = t // tt
    t0 = tok0 // tm
    tok = lambda i, j, *_: (t0 + i * tpg + j, 0)
    return pl.pallas_call(
        functools.partial(_combine_kernel, tile0=t0, tiles_per_group=tpg),
        grid_spec=pltpu.PrefetchScalarGridSpec(
            num_scalar_prefetch=2,
            grid=(g // gb, tpg),
            in_specs=[pl.BlockSpec((gb, tt, d), lambda i, j, *_: (i, j, 0)),
                      pl.BlockSpec((gb, 1, d), lambda i, j, *_: (i, 0, 5)),
                      pl.BlockSpec((tm, TOP_K), tok),
                      pl.BlockSpec((1, 1, d), lambda i, j, *_: (0, 0, 0)),
                      pl.BlockSpec(memory_space=pl.ANY)],
            out_specs=pl.BlockSpec((gb, tt, d), lambda i, j, *_: (i, j, 0)),
            scratch_shapes=[pltpu.VMEM((2, TILE_BUF_ROWS, d), BF16), pltpu.SemaphoreType.DMA((2,))]),
        out_shape=jax.ShapeDtypeStruct((g, t, d), F32),
        compiler_params=_params(2), name="combine",
    )(starts, nq, x1, mod3, rcol, g_final.reshape(1, 1, d), ys)


def _group_blocking(groups, seq, tile):
    if seq >= tile:
        return 1, tile
    return tile // seq, seq


def kernel(x_prompt, x_sample, c_prompt, c_sample, state_ret, state_mlstm_c, state_mlstm_n, state_mlstm_m, state_conv, w_mod, b_mod, g_mix, g_ffn, w_in, b_igate, b_fgate, w_conv, b_conv, g_ret, g_mlstm, w_out, w_router, b_router, w_up, b_up, w_down, b_down, g_final):
    depth = w_mod.shape[0]
    assert depth == 1, "single-layer trunk"
    bp, tp, d = x_prompt.shape
    bs, ts, _ = x_sample.shape
    n_p, n_s = bp * tp, bs * ts
    hd = HEAD_DIM
    l = 0

    mod = _mod_call(jnp.concatenate([c_prompt, c_sample], axis=0), w_mod[l], b_mod[l])
    mod_p = mod[:bp].reshape(bp, 1, N_MOD * d)
    mod_s = mod[bp:].reshape(bs, 1, N_MOD * d)

    w_main_bf = w_in[l][:, :MAIN_COLS].astype(BF16)
    w_gate = jnp.pad(w_in[l][:, MAIN_COLS:], ((0, 0), (0, 128 - N_GATES)))
    w_gate_t = w_in[l][:, MAIN_COLS:].T
    w_out_bf = w_out[l].astype(BF16)
    bg_col = jnp.concatenate([b_igate[l], b_fgate[l]]).reshape(1, N_GATES)
    bg_row = bg_col.reshape(N_GATES, 1)

    groups = (
        (x_prompt, mod_p, min(CHUNK, tp), np.arange(tp),
         jnp.zeros((bp, RET_HEADS, hd, hd), F32), jnp.zeros((bp, MLSTM_HEADS, hd, hd), F32),
         jnp.zeros((bp, MLSTM_HEADS, hd), F32), jnp.zeros((bp, MLSTM_HEADS), F32),
         jnp.zeros((bp, CONV_WIDTH - 1, 2 * MLSTM_WIDTH), F32)),
        (x_sample, mod_s, min(CHUNK, ts), PAST_LEN + np.arange(ts),
         state_ret[l], state_mlstm_c[l], state_mlstm_n[l], state_mlstm_m[l], state_conv[l]),
    )

    ustrict = jnp.asarray(np.arange(MOE_TILE)[:, None] < np.arange(MOE_TILE)[None, :], BF16)
    staged = []
    for x3, mod3, L, pos, s0, c0, n0, m0, conv0 in groups:
        g, t, _ = x3.shape
        conv0p = jnp.pad(conv0.astype(F32), ((0, 0), (CONV_TAIL_ROWS - (CONV_WIDTH - 1), 0), (0, 0)))
        outproj = (g_ffn[l], w_out_bf, w_router[l].T, b_router[l], ustrict)
        chunked = t > L
        gs = MIXER_SEQS_CHUNKED if chunked else MIXER_SEQS_SHORT
        fused = chunked and gs * L == MOE_TILE
        res = _mixer_call(
            x3, mod3, g_mix[l], w_main_bf, w_gate, w_gate_t, _mixer_consts(L, pos),
            w_conv[l], b_conv[l], g_ret[l], g_mlstm[l],
            bg_col, bg_row, s0.astype(F32), c0.astype(F32), n0.astype(F32).reshape(g, MLSTM_HEADS, 1, hd),
            jnp.broadcast_to(m0.astype(F32)[:, :, None, None], (g, MLSTM_HEADS, 1, hd)), conv0p,
            L, gs, outproj if fused else None)
        s_new, c_new, n_new, m_new, tail = res[-5:]
        states = (s_new, c_new, n_new.reshape(g, MLSTM_HEADS, hd), m_new[:, :, 0, 0],
                  tail[:, CONV_TAIL_ROWS - (CONV_WIDTH - 1):, :])
        if fused:
            x1, h2, idx, rank, cnt = res[:5]
            blocking = (gs, L)
        else:
            blocking = _group_blocking(g, t, MOE_TILE)
            x1, h2, idx, rank, cnt = _outproj_call(res[0].reshape(g * t, d), x3, mod3, *outproj, *blocking)
        staged.append((x1, mod3, h2, idx, rank, states, cnt, blocking))

    n_tok = n_p + n_s
    tm, bm, ra = MOE_TILE, EXPERT_BLOCK, RUN_ALIGN
    n_tiles = n_tok // tm
    q_max = TILE_BUF_ROWS // ra
    n_blocks = -(-(n_tok * TOP_K + n_tiles * N_EXPERTS * (ra - 1)) // bm) + N_EXPERTS
    cap = n_blocks * bm
    counts = jnp.concatenate([s[6][:, :, 0] for s in staged], axis=0).astype(I32)
    run = (counts + ra - 1) // ra * ra
    region = jnp.sum(run, axis=0)
    padded = (region + bm - 1) // bm * bm
    pad_end = jnp.cumsum(padded)
    pad_start = pad_end - padded
    run_start = pad_start[None, :] + jnp.cumsum(run, axis=0) - run
    buf_end = jnp.cumsum(run, axis=1)
    buf_start = buf_end - run
    nq = (buf_end[:, -1] // ra).astype(I32)
    chunk_row = jnp.arange(q_max, dtype=I32) * ra
    chunk_e = jnp.minimum(jnp.sum((buf_end[:, None, :] <= chunk_row[None, :, None]).astype(I32), axis=2),
                          N_EXPERTS - 1)
    e_ids = jnp.arange(N_EXPERTS, dtype=I32)
    shift = run_start - buf_start
    starts = jnp.sum(jnp.where(chunk_e[:, :, None] == e_ids, shift[:, None, :], 0), axis=2) + chunk_row[None, :]
    starts = jnp.where(chunk_row[None, :] < buf_end[:, -1:], starts, 0).reshape(-1).astype(I32)

    idx_all = jnp.concatenate([s[3] for s in staged], axis=1)
    rank_all = jnp.concatenate([s[4] for s in staged], axis=1)
    buf_start_tok = jnp.repeat(buf_start, tm, axis=0).T
    rbuf = jnp.sum(jnp.where(idx_all[None] == e_ids[:, None, None], buf_start_tok[:, None, :], 0), axis=0) + rank_all
    rcol = rbuf.T
    block_row = jnp.arange(n_blocks, dtype=I32) * bm
    block_e = jnp.minimum(jnp.sum((pad_end[None, :] <= block_row[:, None]).astype(I32), axis=1), N_EXPERTS - 1)
    n_used = (pad_end[-1:] // bm).astype(I32)
    meta = jnp.concatenate([jnp.where(region > 0, pad_end - bm, -1), n_used]).astype(I32)

    xs = _dispatch_call(starts, nq, meta, rbuf, staged[0][2], staged[1][2], cap)
    region_end = pad_start + region
    valid = jnp.clip(jnp.sum(jnp.where(block_e[:, None] == e_ids[None, :], region_end[None, :], 0), axis=1) - block_row,
                     0, bm)
    valid = jnp.where(block_row < pad_end[-1], valid, 0).astype(I32)
    ys = _expert_call(block_e, n_used, valid, xs, w_up[l], b_up[l], w_down[l], b_down[l])

    outs = []
    tok0 = 0
    for x1, mod3, *_, blocking in staged:
        outs.append(_combine_call(starts, nq, x1, mod3, rcol, g_final, ys, tok0, *blocking))
        tok0 += x1.shape[0] * x1.shape[1]

    st_p, st_s = staged[0][5], staged[1][5]
    return (outs[0], outs[1]) + tuple(a[None] for a in st_p) + tuple(a[None] for a in st_s)
```

```python
import functools

import numpy as np
import jax
import jax.numpy as jnp
from jax import lax
from jax.experimental import pallas as pl
from jax.experimental.pallas import tpu as pltpu

F32 = jnp.float32
BF16 = jnp.bfloat16
I32 = jnp.int32

D_MODEL = 1024
PAST_LEN = 16384
RET_HEADS = 4
MLSTM_HEADS = 4
HEAD_DIM = 128
RET_WIDTH = RET_HEADS * HEAD_DIM
MLSTM_WIDTH = MLSTM_HEADS * HEAD_DIM
CONV_WIDTH = 4
CHUNK = 128
ROPE_BASE = 10000.0
N_EXPERTS = 32
TOP_K = 4
D_FF = D_MODEL
SWIGLU_LIMIT = 7.0
SWIGLU_ALPHA = 1.702
N_MOD = 6
EPS = 1e-6
MAIN_COLS = 4 * RET_WIDTH + 2 * MLSTM_WIDTH + 2 * MLSTM_WIDTH
N_GATES = 2 * MLSTM_HEADS
OFF_RQ, OFF_RK, OFF_RV, OFF_RG = 0, RET_WIDTH, 2 * RET_WIDTH, 3 * RET_WIDTH
OFF_MQK = 4 * RET_WIDTH
OFF_MV = OFF_MQK + 2 * MLSTM_WIDTH
OFF_MO = OFF_MV + MLSTM_WIDTH

VMEM_LIMIT_BYTES = 56 * 1024 * 1024
MOE_TILE = 512
EXPERT_BLOCK = 512
MIXER_SEQS_CHUNKED = 4
MIXER_SEQS_SHORT = 8
MIXER_INTERLEAVE_SEQS = 2
RUN_ALIGN = 16
SIDE_ROWS = 4 * TOP_K
SIDE_LANES = 128
TILE_BUF_ROWS = MOE_TILE * TOP_K + N_EXPERTS * RUN_ALIGN
PERM_ROWS = 1280
PERM_COLS = 512
CONV_TAIL_ROWS = 8


def _params(n_axes=1):
    return pltpu.CompilerParams(dimension_semantics=("arbitrary",) * n_axes, vmem_limit_bytes=VMEM_LIMIT_BYTES)


def _dot(a, b):
    return jnp.dot(a, b, preferred_element_type=F32)


def _dot_nt(a, b):
    return lax.dot_general(a, b, (((1,), (1,)), ((), ())), preferred_element_type=F32)


def _dot_tn(a, b):
    return lax.dot_general(a, b, (((0,), (0,)), ((), ())), preferred_element_type=F32)


def _split(a):
    hi = a.astype(BF16)
    lo = (a - hi.astype(F32)).astype(BF16)
    return hi, lo


def _dot3(a, b, dot=_dot):
    ah, al = _split(a)
    bh, bl = _split(b)
    return dot(ah, bh) + (dot(al, bh) + dot(ah, bl))


def _sigmoid(x):
    return 0.5 * (jnp.tanh(0.5 * x) + 1.0)


def _log_sigmoid(x):
    return jnp.minimum(x, 0.0) - jnp.log1p(jnp.exp(-jnp.abs(x)))


def _rms(x, g):
    ms = jnp.mean(x * x, axis=-1, keepdims=True)
    return (x * lax.rsqrt(ms + EPS)) * g


def _layer_norm(x, g):
    mu = jnp.mean(x, axis=-1, keepdims=True)
    xc = x - mu
    var = jnp.mean(xc * xc, axis=-1, keepdims=True)
    return xc * lax.rsqrt(var + EPS) * g


def _mod_kernel(c_ref, w_ref, b_ref, o_ref):
    c = c_ref[...]
    o_ref[...] = _dot3(c * _sigmoid(c), w_ref[...]) + b_ref[...]


def _mod_call(c_all, w_mod, b_mod):
    rows, d = c_all.shape
    cols = w_mod.shape[1]
    tn = 1024
    return pl.pallas_call(
        _mod_kernel,
        grid=(cols // tn,),
        in_specs=[pl.BlockSpec((rows, d), lambda j: (0, 0)),
                  pl.BlockSpec((d, tn), lambda j: (0, j)),
                  pl.BlockSpec((1, tn), lambda j: (0, j))],
        out_specs=pl.BlockSpec((rows, tn), lambda j: (0, j)),
        out_shape=jax.ShapeDtypeStruct((rows, cols), F32),
        compiler_params=_params(1), name="mod",
    )(c_all, w_mod, b_mod.reshape(1, cols))


def _mixer_kernel(x_ref, sh_ref, sc_ref, gmix_ref, w_ref, wg_ref, wgt_ref,
                  cos_ref, sin_ref, dec_ref, qd_ref, kd_ref, cd_ref,
                  tril_ref, triu_ref, wconv_ref, bconv_ref, gret_ref, gml_ref, bgc_ref, bgr_ref,
                  s0_ref, c0_ref, n0_ref, m0_ref, conv0_ref,
                  y_ref, s_ref, c_ref, n_ref, m_ref, tail_ref, p_ref, xp_ref):
    @pl.when(pl.program_id(1) == 0)
    def _():
        s_ref[...] = s0_ref[...]
        c_ref[...] = c0_ref[...]
        n_ref[...] = n0_ref[...]
        m_ref[...] = m0_ref[...]
        tail_ref[...] = conv0_ref[...]

    x = x_ref[...]
    gs, L, d = x.shape
    h = (_rms(x, gmix_ref[...]) * (1.0 + sc_ref[...]) + sh_ref[...]).reshape(gs * L, d)
    hb = h.astype(BF16)
    for j in range(MAIN_COLS // 1024):
        p_ref[:, j * 1024:(j + 1) * 1024] = _dot(hb, w_ref[:, j * 1024:(j + 1) * 1024])

    chains = []
    for sq in range(gs):
        h_sq = h[sq * L:(sq + 1) * L, :]
        gc = _dot3(h_sq, wg_ref[...])[:, :N_GATES]
        gr = _dot3(wgt_ref[...], h_sq, dot=_dot_nt)
        chains += _mixer_sequence(p_ref.at[pl.ds(sq * L, L)], gc, gr, cos_ref, sin_ref, dec_ref, qd_ref, kd_ref,
                                  cd_ref, tril_ref, triu_ref, wconv_ref, bconv_ref, gret_ref, gml_ref, bgc_ref,
                                  bgr_ref, y_ref.at[sq], s_ref.at[sq], c_ref.at[sq], n_ref.at[sq], m_ref.at[sq],
                                  tail_ref.at[sq], xp_ref.at[sq])
    width = MIXER_INTERLEAVE_SEQS * (RET_HEADS + MLSTM_HEADS)
    for first in range(0, len(chains), width):
        group = chains[first:first + width]
        while group:
            alive = []
            for chain in group:
                if next(chain, None) is not None:
                    alive.append(chain)
            group = alive


N_MIXER_INPUTS = 26
N_OUTPROJ_INPUTS = 8


def _mixer_outproj_kernel(*refs):
    mixer_in = refs[:N_MIXER_INPUTS]
    outproj_in = refs[N_MIXER_INPUTS:N_MIXER_INPUTS + N_OUTPROJ_INPUTS]
    x1_ref, h2_ref, idx_ref, rank_ref, cnt_ref, s_ref, c_ref, n_ref, m_ref, tail_ref, p_ref, xp_ref, y_ref = refs[
        N_MIXER_INPUTS + N_OUTPROJ_INPUTS:]
    _mixer_kernel(*mixer_in, y_ref, s_ref, c_ref, n_ref, m_ref, tail_ref, p_ref, xp_ref)
    gs, L, d = y_ref.shape
    _outproj_router(y_ref[...].reshape(gs * L, d), mixer_in[0], *outproj_in, x1_ref, h2_ref, idx_ref, rank_ref, cnt_ref)


def _mixer_sequence(p_ref, gc, gr, cos_ref, sin_ref, dec_ref, qd_ref, kd_ref, cd_ref,
                    tril_ref, triu_ref, wconv_ref, bconv_ref, gret_ref, gml_ref, bgc_ref, bgr_ref,
                    y_ref, s_ref, c_ref, n_ref, m_ref, tail_ref, xp_ref):
    L = p_ref.shape[0]
    cos = cos_ref[...]
    sin = sin_ref[...]
    scale = HEAD_DIM ** -0.5

    def rot(x):
        return x * cos + pltpu.roll(x, HEAD_DIM // 2, axis=1) * sin

    def retention_head(h):
        lo = h * HEAD_DIM
        q = rot(p_ref[:, OFF_RQ + lo:OFF_RQ + lo + HEAD_DIM])
        k = rot(p_ref[:, OFF_RK + lo:OFF_RK + lo + HEAD_DIM]) * scale
        v = p_ref[:, OFF_RV + lo:OFF_RV + lo + HEAD_DIM].astype(BF16)
        s_old = s_ref[h]
        yield True
        scores = _dot_nt(q.astype(BF16), k.astype(BF16)) * dec_ref[h]
        yield True
        out = _dot(scores.astype(BF16), v) + _dot((q * qd_ref[h]).astype(BF16), s_old.astype(BF16))
        s_ref[h] = cd_ref[h] * s_old + _dot_tn((k * kd_ref[h]).astype(BF16), v)
        yield True
        g = p_ref[:, OFF_RG + lo:OFF_RG + lo + HEAD_DIM]
        y_ref[:, lo:lo + HEAD_DIM] = ((g * _sigmoid(g)) * _layer_norm(out, gret_ref[:, lo:lo + HEAD_DIM])).astype(y_ref.dtype)

    xp_ref[0:CONV_TAIL_ROWS, :] = tail_ref[...]
    xp_ref[CONV_TAIL_ROWS:CONV_TAIL_ROWS + L, :] = p_ref[:, OFF_MQK:OFF_MQK + 2 * MLSTM_WIDTH]
    acc = bconv_ref[...] + wconv_ref[0:1, :] * xp_ref[CONV_TAIL_ROWS - 3:CONV_TAIL_ROWS - 3 + L, :]
    for j in range(1, CONV_WIDTH):
        acc = acc + wconv_ref[j:j + 1, :] * xp_ref[CONV_TAIL_ROWS - 3 + j:CONV_TAIL_ROWS - 3 + j + L, :]
    tail_ref[...] = xp_ref[L:L + CONV_TAIL_ROWS, :]
    xp_ref[CONV_TAIL_ROWS:CONV_TAIL_ROWS + L, :] = acc * _sigmoid(acc)

    gcol = gc + bgc_ref[...]
    is_f_col = lax.broadcasted_iota(I32, gcol.shape, 1) >= MLSTM_HEADS
    gcol = jnp.where(is_f_col, _log_sigmoid(gcol), gcol)
    grow = gr + bgr_ref[...]
    is_f_row = lax.broadcasted_iota(I32, grow.shape, 0) >= MLSTM_HEADS
    grow = jnp.where(is_f_row, _log_sigmoid(grow), grow)
    bcol_all = _dot3(tril_ref[...], gcol)
    brow_all = _dot3(grow, triu_ref[...])
    causal = lax.broadcasted_iota(I32, (L, L), 0) >= lax.broadcasted_iota(I32, (L, L), 1)

    def mlstm_head(h):
        lo = h * HEAD_DIM
        q = xp_ref[CONV_TAIL_ROWS:CONV_TAIL_ROWS + L, lo:lo + HEAD_DIM]
        k = xp_ref[CONV_TAIL_ROWS:CONV_TAIL_ROWS + L, MLSTM_WIDTH + lo:MLSTM_WIDTH + lo + HEAD_DIM] * scale
        v = p_ref[:, OFF_MV + lo:OFF_MV + lo + HEAD_DIM].astype(BF16)
        ic_col = gcol[:, h:h + 1]
        ic_row = grow[h:h + 1, :]
        b_col = bcol_all[:, MLSTM_HEADS + h:MLSTM_HEADS + h + 1]
        b_row = brow_all[MLSTM_HEADS + h:MLSTM_HEADS + h + 1, :]
        c_old = c_ref[h]
        n_old = n_ref[h]
        m_old = m_ref[h][:, 0:1]

        d_log = jnp.where(causal, b_col - b_row + ic_row, -jnp.inf)
        inter = b_col + m_old
        m_t = jnp.maximum(inter, jnp.max(d_log, axis=1, keepdims=True))
        yield True
        w_intra = jnp.exp(d_log - m_t)
        w_inter = jnp.exp(inter - m_t)
        qb = q.astype(BF16)
        s = _dot_nt(qb, k.astype(BF16)) * w_intra
        yield True
        num = _dot(s.astype(BF16), v) + w_inter * _dot(qb, c_old.astype(BF16))
        den = jnp.sum(s, axis=1, keepdims=True) + w_inter * jnp.sum(q * n_old, axis=1, keepdims=True)
        yield True
        hh = num / jnp.maximum(jnp.abs(den), jnp.exp(-m_t))

        b_last = b_col[L - 1:L, :]
        w_log_col = b_last - b_col + ic_col
        m_new = jnp.maximum(b_last + m_old, jnp.max(w_log_col, axis=0, keepdims=True))
        wk = jnp.exp(w_log_col - m_new) * k
        cdec = jnp.exp(b_last + m_old - m_new)
        yield True
        c_ref[h] = cdec * c_old + _dot_tn(wk.astype(BF16), v)
        n_ref[h] = cdec * n_old + jnp.sum(wk, axis=0, keepdims=True)
        m_ref[h] = jnp.broadcast_to(m_new, (1, HEAD_DIM))
        yield True
        o = p_ref[:, OFF_MO + lo:OFF_MO + lo + HEAD_DIM]
        y_ref[:, RET_WIDTH + lo:RET_WIDTH + lo + HEAD_DIM] = (
            _sigmoid(o) * _layer_norm(hh, gml_ref[:, lo:lo + HEAD_DIM])).astype(y_ref.dtype)

    return [retention_head(h) for h in range(RET_HEADS)] + [mlstm_head(h) for h in range(MLSTM_HEADS)]


def _mixer_consts(L, pos):
    f32 = np.float32
    half = HEAD_DIM // 2
    inv_freq = np.power(f32(ROPE_BASE), -np.arange(half, dtype=f32) / f32(half)).astype(f32)
    ang = (pos.astype(f32)[:, None] * inv_freq[None, :]).astype(f32)
    cos = np.concatenate([np.cos(ang), np.cos(ang)], axis=-1).astype(f32)
    sin = np.concatenate([-np.sin(ang), np.sin(ang)], axis=-1).astype(f32)
    log_gamma = np.log1p(-np.exp2(-5.0 - np.arange(RET_HEADS, dtype=np.float64)))
    idx = np.arange(L, dtype=np.float64)
    rel = idx[:, None] - idx[None, :]
    dec = np.where(rel >= 0, np.exp(log_gamma[:, None, None] * np.maximum(rel, 0.0)), 0.0)
    qd = np.broadcast_to(np.exp(log_gamma[:, None] * (idx + 1.0))[..., None], (RET_HEADS, L, HEAD_DIM))
    kd = np.broadcast_to(np.exp(log_gamma[:, None] * (L - 1.0 - idx))[..., None], (RET_HEADS, L, HEAD_DIM))
    cd = np.broadcast_to(np.exp(log_gamma * L)[:, None, None], (RET_HEADS, 1, HEAD_DIM))
    tril = rel >= 0
    triu = rel <= 0
    return tuple(jnp.asarray(a, F32) for a in (cos, sin, dec, qd, kd, cd, tril, triu))


def _mixer_call(x3, mod3, g_mix, w_main_bf, w_gate, w_gate_t, consts, w_conv, b_conv, g_ret, g_mlstm, bg_col, bg_row,
                s0, c0, n0, m0, conv0, L, gs, outproj=None):
    cos, sin, dec, qd, kd, cd, tril, triu = consts
    groups, seq, d = x3.shape
    nc = seq // L
    hd = HEAD_DIM
    full = lambda *shape: pl.BlockSpec(shape, lambda g, c: (0,) * len(shape))
    mod_spec = lambda k: pl.BlockSpec((gs, 1, d), lambda g, c: (g, 0, k))
    state4 = pl.BlockSpec((gs, RET_HEADS, hd, hd), lambda g, c: (g, 0, 0, 0))
    vec4 = pl.BlockSpec((gs, MLSTM_HEADS, 1, hd), lambda g, c: (g, 0, 0, 0))
    tail3 = pl.BlockSpec((gs, CONV_TAIL_ROWS, 2 * MLSTM_WIDTH), lambda g, c: (g, 0, 0))
    row = lambda g, c: (g, c, 0)
    in_specs = [pl.BlockSpec((gs, L, d), row), mod_spec(0), mod_spec(1),
                full(1, 1, d), full(d, MAIN_COLS), full(d, 128), full(N_GATES, d),
                pl.BlockSpec((L, hd), lambda g, c: (c, 0)),
                pl.BlockSpec((L, hd), lambda g, c: (c, 0)),
                full(RET_HEADS, L, L), full(RET_HEADS, L, hd), full(RET_HEADS, L, hd), full(RET_HEADS, 1, hd),
                full(L, L), full(L, L),
                full(CONV_WIDTH, 2 * MLSTM_WIDTH), full(1, 2 * MLSTM_WIDTH),
                full(1, RET_WIDTH), full(1, MLSTM_WIDTH), full(1, N_GATES), full(N_GATES, 1),
                state4, state4, vec4, vec4, tail3]
    args = [x3, mod3, mod3, g_mix.reshape(1, 1, d), w_main_bf, w_gate, w_gate_t,
            cos, sin, dec, qd, kd, cd, tril, triu, w_conv, b_conv.reshape(1, -1),
            g_ret.reshape(1, -1), g_mlstm.reshape(1, -1), bg_col, bg_row, s0, c0, n0, m0, conv0]
    assert len(in_specs) == len(args) == N_MIXER_INPUTS
    state_specs = [state4, state4, vec4, vec4, tail3]
    state_shapes = [jax.ShapeDtypeStruct((groups, RET_HEADS, hd, hd), F32),
                    jax.ShapeDtypeStruct((groups, MLSTM_HEADS, hd, hd), F32),
                    jax.ShapeDtypeStruct((groups, MLSTM_HEADS, 1, hd), F32),
                    jax.ShapeDtypeStruct((groups, MLSTM_HEADS, 1, hd), F32),
                    jax.ShapeDtypeStruct((groups, CONV_TAIL_ROWS, 2 * MLSTM_WIDTH), F32)]
    scratch = [pltpu.VMEM((gs * L, MAIN_COLS), F32), pltpu.VMEM((gs, CONV_TAIL_ROWS + L, 2 * MLSTM_WIDTH), F32)]
    y_block, y_shape = (gs, L, RET_WIDTH + MLSTM_WIDTH), (groups, seq, RET_WIDTH + MLSTM_WIDTH)
    if outproj is None:
        kernel_fn = _mixer_kernel
        out_specs = [pl.BlockSpec(y_block, row)] + state_specs
        out_shape = [jax.ShapeDtypeStruct(y_shape, F32)] + state_shapes
    else:
        g_ffn, w_out_bf, w_router_t, b_router, ustrict = outproj
        tm = gs * L
        n = groups * seq
        tile = lambda g, c: g * nc + c
        kernel_fn = _mixer_outproj_kernel
        in_specs += [mod_spec(2), mod_spec(3), mod_spec(4),
                     full(1, 1, d), full(d, d), full(N_EXPERTS, d), full(N_EXPERTS, 1), full(tm, tm)]
        args += [mod3, mod3, mod3, g_ffn.reshape(1, 1, d), w_out_bf, w_router_t, b_router.reshape(N_EXPERTS, 1),
                 ustrict]
        assert len(args) == N_MIXER_INPUTS + N_OUTPROJ_INPUTS
        out_specs = [pl.BlockSpec((gs, L, d), row),
                     pl.BlockSpec((tm, d + SIDE_LANES), lambda g, c: (tile(g, c), 0)),
                     pl.BlockSpec((TOP_K, tm), lambda g, c: (0, tile(g, c))),
                     pl.BlockSpec((TOP_K, tm), lambda g, c: (0, tile(g, c))),
                     pl.BlockSpec((1, N_EXPERTS, 128), lambda g, c: (tile(g, c), 0, 0))] + state_specs
        out_shape = [jax.ShapeDtypeStruct((groups, seq, d), F32),
                     jax.ShapeDtypeStruct((n, d + SIDE_LANES), BF16),
                     jax.ShapeDtypeStruct((TOP_K, n), I32),
                     jax.ShapeDtypeStruct((TOP_K, n), I32),
                     jax.ShapeDtypeStruct((n // tm, N_EXPERTS, 128), F32)] + state_shapes
        scratch = scratch + [pltpu.VMEM(y_block, F32)]
    return pl.pallas_call(
        kernel_fn,
        grid=(groups // gs, nc),
        in_specs=in_specs, out_specs=out_specs, out_shape=out_shape, scratch_shapes=scratch,
        compiler_params=_params(2), name="mixer",
    )(*args)


def _outproj_kernel(y_ref, x_ref, gt_ref, sh_ref, sc_ref, g_ref, w_ref, wrt_ref, br_ref, ustrict_ref,
                    x1_ref, h2_ref, idx_ref, rank_ref, cnt_ref):
    _outproj_router(y_ref[...], x_ref, gt_ref, sh_ref, sc_ref, g_ref, w_ref, wrt_ref, br_ref, ustrict_ref,
                    x1_ref, h2_ref, idx_ref, rank_ref, cnt_ref)


def _outproj_router(y, x_ref, gt_ref, sh_ref, sc_ref, g_ref, w_ref, wrt_ref, br_ref, ustrict_ref,
                    x1_ref, h2_ref, idx_ref, rank_ref, cnt_ref):
    x = x_ref[...]
    gb, tt, d = x.shape
    tm = gb * tt
    mixed = _dot(y.astype(BF16), w_ref[...])
    x1 = x + gt_ref[...] * mixed.reshape(gb, tt, d)
    x1_ref[...] = x1
    h2 = (_rms(x1, g_ref[...]) * (1.0 + sc_ref[...]) + sh_ref[...]).reshape(tm, d)
    h2_ref[:, :d] = h2.astype(BF16)

    work = _dot3(wrt_ref[...], h2, dot=_dot_nt) + br_ref[...]
    e_iota = lax.broadcasted_iota(I32, work.shape, 0).astype(F32)
    vals, idxs, sels = [], [], []
    for _ in range(TOP_K):
        mx = jnp.max(work, axis=0, keepdims=True)
        ik = jnp.min(jnp.where(work == mx, e_iota, float(N_EXPERTS)), axis=0, keepdims=True)
        sel = e_iota == ik
        vals.append(mx)
        idxs.append(ik)
        sels.append(sel)
        work = jnp.where(sel, -jnp.inf, work)
    exps = [jnp.exp(v - vals[0]) for v in vals]
    denom = exps[0] + exps[1] + exps[2] + exps[3]
    gates = [e / denom for e in exps]
    idx_ref[...] = jnp.concatenate(idxs, axis=0).astype(I32)

    pieces = []
    for gk in gates:
        p1 = gk.astype(BF16)
        r1 = gk - p1.astype(F32)
        p2 = r1.astype(BF16)
        pieces += [p1, p2, (r1 - p2.astype(F32)).astype(BF16)]
    side = jnp.concatenate(pieces + [ik.astype(BF16) for ik in idxs], axis=0)
    eye = (lax.broadcasted_iota(I32, (SIDE_ROWS, SIDE_LANES), 0)
           == lax.broadcasted_iota(I32, (SIDE_ROWS, SIDE_LANES), 1)).astype(BF16)
    h2_ref[:, d:] = _dot_tn(side, eye).astype(BF16)

    mask = (sels[0] | sels[1] | sels[2] | sels[3]).astype(F32)
    before = _dot(mask.astype(BF16), ustrict_ref[...])
    ranks = [jnp.sum(jnp.where(sel, before, 0.0), axis=0, keepdims=True) for sel in sels]
    rank_ref[...] = jnp.concatenate(ranks, axis=0).astype(I32)
    cnt_ref[0] = jnp.broadcast_to(jnp.sum(mask, axis=1, keepdims=True), cnt_ref.shape[1:])


def _outproj_call(y, x3, mod3, g_ffn, w_out_bf, w_router_t, b_router, ustrict, gb, tt):
    g, t, d = x3.shape
    n = g * t
    tm = gb * tt
    tpg = t // tt
    row = lambda i, j: (i * tpg + j, 0)
    col = lambda i, j: (0, i * tpg + j)
    mod_spec = lambda k: pl.BlockSpec((gb, 1, d), lambda i, j: (i, 0, k))
    const = lambda *shape: pl.BlockSpec(shape, lambda i, j: (0,) * len(shape))
    return pl.pallas_call(
        _outproj_kernel,
        grid=(g // gb, tpg),
        in_specs=[pl.BlockSpec((tm, d), row),
                  pl.BlockSpec((gb, tt, d), lambda i, j: (i, j, 0)),
                  mod_spec(2), mod_spec(3), mod_spec(4),
                  const(1, 1, d), const(d, d), const(N_EXPERTS, d), const(N_EXPERTS, 1),
                  const(tm, tm)],
        out_specs=[pl.BlockSpec((gb, tt, d), lambda i, j: (i, j, 0)),
                   pl.BlockSpec((tm, d + SIDE_LANES), row),
                   pl.BlockSpec((TOP_K, tm), col),
                   pl.BlockSpec((TOP_K, tm), col),
                   pl.BlockSpec((1, N_EXPERTS, 128), lambda i, j: (i * tpg + j, 0, 0))],
        out_shape=[jax.ShapeDtypeStruct((g, t, d), F32),
                   jax.ShapeDtypeStruct((n, d + SIDE_LANES), BF16),
                   jax.ShapeDtypeStruct((TOP_K, n), I32),
                   jax.ShapeDtypeStruct((TOP_K, n), I32),
                   jax.ShapeDtypeStruct((n // tm, N_EXPERTS, 128), F32)],
        compiler_params=_params(2), name="outproj_router",
    )(y, x3, mod3, mod3, mod3, g_ffn.reshape(1, 1, d), w_out_bf, w_router_t, b_router.reshape(N_EXPERTS, 1),
      ustrict)


def _chunk_copy(hbm_ref, hbm_row, buf_ref, chunk, sem, to_hbm):
    hbm = hbm_ref.at[pl.ds(pl.multiple_of(hbm_row, RUN_ALIGN), RUN_ALIGN)]
    buf = buf_ref.at[pl.ds(pl.multiple_of(chunk * RUN_ALIGN, RUN_ALIGN), RUN_ALIGN)]
    return pltpu.make_async_copy(buf, hbm, sem) if to_hbm else pltpu.make_async_copy(hbm, buf, sem)


def _tile_chunks(hbm_ref, bufs_ref, starts_ref, nq_ref, sems, tile, to_hbm, wait):
    slot = tile % 2
    first = tile * (bufs_ref.shape[1] // RUN_ALIGN)

    def body(q, c):
        cp = _chunk_copy(hbm_ref, starts_ref[first + q], bufs_ref.at[slot], q, sems.at[slot], to_hbm)
        cp.wait() if wait else cp.start()
        return c

    lax.fori_loop(0, nq_ref[tile], body, 0)


def _dispatch_kernel(starts_ref, nq_ref, meta_ref, rbuf_ref, h2p_ref, h2s_ref, xs_ref, bufs_ref, zero_ref, sems,
                     *, prompt_tiles):
    i = pl.program_id(0)
    tm = h2p_ref.shape[0]
    bm = zero_ref.shape[0]
    n_blocks = xs_ref.shape[0] // bm
    sem = sems.at[0]
    buf_ref = bufs_ref.at[i % 2]

    @pl.when(i == 0)
    def _():
        zero_ref[...] = jnp.zeros(zero_ref.shape, zero_ref.dtype)

        def zero_copy(row):
            return pltpu.make_async_copy(zero_ref, xs_ref.at[pl.ds(pl.multiple_of(row, bm), bm)], sem)

        def tails(fn):
            def body(e, c):
                @pl.when(meta_ref[e] >= 0)
                def _():
                    fn(zero_copy(meta_ref[e]))
                return c
            lax.fori_loop(0, N_EXPERTS, body, 0)

        def unused(fn):
            def body(b, c):
                fn(zero_copy(b * bm))
                return c
            lax.fori_loop(meta_ref[N_EXPERTS], n_blocks, body, 0)

        tails(lambda cp: cp.start())
        unused(lambda cp: cp.start())
        tails(lambda cp: cp.wait())
        unused(lambda cp: cp.wait())

    def build(h2_ref):
        h2 = h2_ref[...]
        rb = rbuf_ref[...].astype(jnp.int16)
        for c in range(bufs_ref.shape[1] // PERM_ROWS):
            r = (lax.broadcasted_iota(I32, (PERM_ROWS, tm), 0) + c * PERM_ROWS).astype(jnp.int16)
            hit = (r == rb[0:1, :]) | (r == rb[1:2, :]) | (r == rb[2:3, :]) | (r == rb[3:4, :])
            onehot = jnp.where(hit, jnp.ones((), BF16), jnp.zeros((), BF16))
            buf_ref[c * PERM_ROWS:(c + 1) * PERM_ROWS, :] = _dot(onehot, h2).astype(BF16)

    @pl.when(i < prompt_tiles)
    def _():
        build(h2p_ref)

    @pl.when(i >= prompt_tiles)
    def _():
        build(h2s_ref)

    chunks = functools.partial(_tile_chunks, xs_ref, bufs_ref, starts_ref, nq_ref, sems, to_hbm=True)
    chunks(i, wait=False)

    @pl.when(i > 0)
    def _():
        chunks(i - 1, wait=True)

    @pl.when(i == pl.num_programs(0) - 1)
    def _():
        chunks(i, wait=True)


def _dispatch_call(starts, nq, meta, rbuf, h2_p, h2_s, cap):
    d = h2_p.shape[1]
    tm = MOE_TILE
    pt, st = h2_p.shape[0] // tm, h2_s.shape[0] // tm
    return pl.pallas_call(
        functools.partial(_dispatch_kernel, prompt_tiles=pt),
        grid_spec=pltpu.PrefetchScalarGridSpec(
            num_scalar_prefetch=3,
            grid=(pt + st,),
            in_specs=[pl.BlockSpec((TOP_K, tm), lambda i, *_: (0, i)),
                      pl.BlockSpec((tm, d), lambda i, *_: (jnp.minimum(i, pt - 1), 0)),
                      pl.BlockSpec((tm, d), lambda i, *_: (jnp.maximum(i - pt, 0), 0))],
            out_specs=pl.BlockSpec(memory_space=pl.ANY),
            scratch_shapes=[pltpu.VMEM((2, TILE_BUF_ROWS, d), BF16), pltpu.VMEM((EXPERT_BLOCK, d), BF16),
                            pltpu.SemaphoreType.DMA((2,))]),
        out_shape=jax.ShapeDtypeStruct((cap, d), BF16),
        compiler_params=_params(1), name="dispatch",
    )(starts, nq, meta, rbuf, h2_p, h2_s)


def _expert_kernel(be_ref, nu_ref, xs_ref, wup_ref, bup_ref, wdn_ref, bdn_ref, ys_ref, wup_bf, wdn_bf):
    i = pl.program_id(0)
    prev = be_ref[jnp.maximum(i - 1, 0)]

    @pl.when((i == 0) | (be_ref[i] != prev))
    def _():
        def cast(r, c):
            rows = pl.ds(pl.multiple_of(r * 64, 64), 64)
            wup_bf[rows, :] = wup_ref[0, rows, :].astype(BF16)
            wdn_bf[rows, :] = wdn_ref[0, rows, :].astype(BF16)
            return c

        lax.fori_loop(0, D_MODEL // 64, cast, 0)

    @pl.when(i < nu_ref[0])
    def _():
        hu = _dot(xs_ref[:, :D_MODEL], wup_bf[...]) + bup_ref[0]
        gate = jnp.minimum(hu[:, :D_FF], SWIGLU_LIMIT)
        lin = jnp.clip(hu[:, D_FF:], -SWIGLU_LIMIT, SWIGLU_LIMIT)
        glu = gate * _sigmoid(SWIGLU_ALPHA * gate)
        y = _dot(((lin + 1.0) * glu).astype(BF16), wdn_bf[...]) + bdn_ref[0]
        side = xs_ref[:, D_MODEL:].astype(F32)
        e = be_ref[i].astype(F32)
        weight = jnp.zeros((side.shape[0], 1), F32)
        for k in range(TOP_K):
            g_k = side[:, 3 * k:3 * k + 1] + side[:, 3 * k + 1:3 * k + 2] + side[:, 3 * k + 2:3 * k + 3]
            weight = weight + jnp.where(side[:, SIDE_ROWS - TOP_K + k:SIDE_ROWS - TOP_K + k + 1] == e, g_k, 0.0)
        ys_ref[...] = (weight * y).astype(ys_ref.dtype)

    @pl.when(i >= nu_ref[0])
    def _():
        ys_ref[...] = jnp.zeros(ys_ref.shape, ys_ref.dtype)


def _expert_call(block_e, n_used, xs, w_up, b_up, w_down, b_down):
    cap, dw = xs.shape
    d = w_down.shape[2]
    bm = EXPERT_BLOCK
    blk = lambda i, be, nu: (jnp.minimum(i, nu[0] - 1), 0)
    per_e = lambda i, be, nu: (be[i], 0, 0)
    return pl.pallas_call(
        _expert_kernel,
        grid_spec=pltpu.PrefetchScalarGridSpec(
            num_scalar_prefetch=2,
            grid=(cap // bm,),
            in_specs=[pl.BlockSpec((bm, dw), blk),
                      pl.BlockSpec((1, d, 2 * D_FF), per_e),
                      pl.BlockSpec((1, 1, 2 * D_FF), per_e),
                      pl.BlockSpec((1, D_FF, d), per_e),
                      pl.BlockSpec((1, 1, d), per_e)],
            out_specs=pl.BlockSpec((bm, d), lambda i, be, nu: (i, 0)),
            scratch_shapes=[pltpu.VMEM((d, 2 * D_FF), BF16), pltpu.VMEM((D_FF, d), BF16)]),
        out_shape=jax.ShapeDtypeStruct((cap, d), BF16),
        compiler_params=_params(1), name="experts",
    )(block_e, n_used, xs, w_up, b_up.reshape(N_EXPERTS, 1, -1), w_down, b_down.reshape(N_EXPERTS, 1, -1))


def _combine_kernel(starts_ref, nq_ref, x1_ref, gt_ref, rcol_ref, gfin_ref, ys_ref, o_ref, bufs_ref, sems,
                    *, tile0, tiles_per_group):
    x1 = x1_ref[...]
    gb, tt, d = x1.shape
    tm = gb * tt
    step = pl.program_id(0) * tiles_per_group + pl.program_id(1)
    n_steps = pl.num_programs(0) * tiles_per_group
    tile = tile0 + step
    chunks = functools.partial(_tile_chunks, ys_ref, bufs_ref, starts_ref, nq_ref, sems, to_hbm=False)

    @pl.when(step == 0)
    def _():
        bufs_ref[...] = jnp.zeros(bufs_ref.shape, bufs_ref.dtype)
        chunks(tile, wait=False)

    @pl.when(step + 1 < n_steps)
    def _():
        chunks(tile + 1, wait=False)

    chunks(tile, wait=True)
    buf_ref = bufs_ref.at[tile % 2]

    rows = rcol_ref[...].astype(jnp.int16)
    moe = jnp.zeros((tm, d), F32)
    for c in range(bufs_ref.shape[1] // PERM_COLS):
        r = (lax.broadcasted_iota(I32, (tm, PERM_COLS), 1) + c * PERM_COLS).astype(jnp.int16)
        hit = (r == rows[:, 0:1]) | (r == rows[:, 1:2]) | (r == rows[:, 2:3]) | (r == rows[:, 3:4])
        onehot = jnp.where(hit, jnp.ones((), BF16), jnp.zeros((), BF16))
        moe = moe + _dot(onehot, buf_ref[c * PERM_COLS:(c + 1) * PERM_COLS, :])
    xo = x1 + gt_ref[...] * moe.reshape(gb, tt, d)
    o_ref[...] = _rms(xo, gfin_ref[...])


def _combine_call(starts, nq, x1, mod3, rcol, g_final, ys, tok0, gb, tt):
    g, t, d = x1.shape
    tm = gb * tt
    tpg = t // tt
    t0 = tok0 // tm
    tok = lambda i, j, *_: (t0 + i * tpg + j, 0)
    return pl.pallas_call(
        functools.partial(_combine_kernel, tile0=t0, tiles_per_group=tpg),
        grid_spec=pltpu.PrefetchScalarGridSpec(
            num_scalar_prefetch=2,
            grid=(g // gb, tpg),
            in_specs=[pl.BlockSpec((gb, tt, d), lambda i, j, *_: (i, j, 0)),
                      pl.BlockSpec((gb, 1, d), lambda i, j, *_: (i, 0, 5)),
                      pl.BlockSpec((tm, TOP_K), tok),
                      pl.BlockSpec((1, 1, d), lambda i, j, *_: (0, 0, 0)),
                      pl.BlockSpec(memory_space=pl.ANY)],
            out_specs=pl.BlockSpec((gb, tt, d), lambda i, j, *_: (i, j, 0)),
            scratch_shapes=[pltpu.VMEM((2, TILE_BUF_ROWS, d), BF16), pltpu.SemaphoreType.DMA((2,))]),
        out_shape=jax.ShapeDtypeStruct((g, t, d), F32),
        compiler_params=_params(2), name="combine",
    )(starts, nq, x1, mod3, rcol, g_final.reshape(1, 1, d), ys)


def _group_blocking(groups, seq, tile):
    if seq >= tile:
        return 1, tile
    return tile // seq, seq


def kernel(x_prompt, x_sample, c_prompt, c_sample, state_ret, state_mlstm_c, state_mlstm_n, state_mlstm_m, state_conv, w_mod, b_mod, g_mix, g_ffn, w_in, b_igate, b_fgate, w_conv, b_conv, g_ret, g_mlstm, w_out, w_router, b_router, w_up, b_up, w_down, b_down, g_final):
    depth = w_mod.shape[0]
    assert depth == 1, "single-layer trunk"
    bp, tp, d = x_prompt.shape
    bs, ts, _ = x_sample.shape
    n_p, n_s = bp * tp, bs * ts
    hd = HEAD_DIM
    l = 0

    mod = _mod_call(jnp.concatenate([c_prompt, c_sample], axis=0), w_mod[l], b_mod[l])
    mod_p = mod[:bp].reshape(bp, 1, N_MOD * d)
    mod_s = mod[bp:].reshape(bs, 1, N_MOD * d)

    w_main_bf = w_in[l][:, :MAIN_COLS].astype(BF16)
    w_gate = jnp.pad(w_in[l][:, MAIN_COLS:], ((0, 0), (0, 128 - N_GATES)))
    w_gate_t = w_in[l][:, MAIN_COLS:].T
    w_out_bf = w_out[l].astype(BF16)
    bg_col = jnp.concatenate([b_igate[l], b_fgate[l]]).reshape(1, N_GATES)
    bg_row = bg_col.reshape(N_GATES, 1)

    groups = (
        (x_prompt, mod_p, min(CHUNK, tp), np.arange(tp),
         jnp.zeros((bp, RET_HEADS, hd, hd), F32), jnp.zeros((bp, MLSTM_HEADS, hd, hd), F32),
         jnp.zeros((bp, MLSTM_HEADS, hd), F32), jnp.zeros((bp, MLSTM_HEADS), F32),
         jnp.zeros((bp, CONV_WIDTH - 1, 2 * MLSTM_WIDTH), F32)),
        (x_sample, mod_s, min(CHUNK, ts), PAST_LEN + np.arange(ts),
         state_ret[l], state_mlstm_c[l], state_mlstm_n[l], state_mlstm_m[l], state_conv[l]),
    )

    ustrict = jnp.asarray(np.arange(MOE_TILE)[:, None] < np.arange(MOE_TILE)[None, :], BF16)
    staged = []
    for x3, mod3, L, pos, s0, c0, n0, m0, conv0 in groups:
        g, t, _ = x3.shape
        conv0p = jnp.pad(conv0.astype(F32), ((0, 0), (CONV_TAIL_ROWS - (CONV_WIDTH - 1), 0), (0, 0)))
        outproj = (g_ffn[l], w_out_bf, w_router[l].T, b_router[l], ustrict)
        chunked = t > L
        gs = MIXER_SEQS_CHUNKED if chunked else MIXER_SEQS_SHORT
        fused = chunked and gs * L == MOE_TILE
        res = _mixer_call(
            x3, mod3, g_mix[l], w_main_bf, w_gate, w_gate_t, _mixer_consts(L, pos),
            w_conv[l], b_conv[l], g_ret[l], g_mlstm[l],
            bg_col, bg_row, s0.astype(F32), c0.astype(F32), n0.astype(F32).reshape(g, MLSTM_HEADS, 1, hd),
            jnp.broadcast_to(m0.astype(F32)[:, :, None, None], (g, MLSTM_HEADS, 1, hd)), conv0p,
            L, gs, outproj if fused else None)
        s_new, c_new, n_new, m_new, tail = res[-5:]
        states = (s_new, c_new, n_new.reshape(g, MLSTM_HEADS, hd), m_new[:, :, 0, 0],
                  tail[:, CONV_TAIL_ROWS - (CONV_WIDTH - 1):, :])
        if fused:
            x1, h2, idx, rank, cnt = res[:5]
            blocking = (gs, L)
        else:
            blocking = _group_blocking(g, t, MOE_TILE)
            x1, h2, idx, rank, cnt = _outproj_call(res[0].reshape(g * t, d), x3, mod3, *outproj, *blocking)
        staged.append((x1, mod3, h2, idx, rank, states, cnt, blocking))

    n_tok = n_p + n_s
    tm, bm, ra = MOE_TILE, EXPERT_BLOCK, RUN_ALIGN
    n_tiles = n_tok // tm
    q_max = TILE_BUF_ROWS // ra
    n_blocks = -(-(n_tok * TOP_K + n_tiles * N_EXPERTS * (ra - 1)) // bm) + N_EXPERTS
    cap = n_blocks * bm
    counts = jnp.concatenate([s[6][:, :, 0] for s in staged], axis=0).astype(I32)
    run = (counts + ra - 1) // ra * ra
    region = jnp.sum(run, axis=0)
    padded = (region + bm - 1) // bm * bm
    pad_end = jnp.cumsum(padded)
    pad_start = pad_end - padded
    run_start = pad_start[None, :] + jnp.cumsum(run, axis=0) - run
    buf_end = jnp.cumsum(run, axis=1)
    buf_start = buf_end - run
    nq = (buf_end[:, -1] // ra).astype(I32)
    chunk_row = jnp.arange(q_max, dtype=I32) * ra
    chunk_e = jnp.minimum(jnp.sum((buf_end[:, None, :] <= chunk_row[None, :, None]).astype(I32), axis=2),
                          N_EXPERTS - 1)
    e_ids = jnp.arange(N_EXPERTS, dtype=I32)
    shift = run_start - buf_start
    starts = jnp.sum(jnp.where(chunk_e[:, :, None] == e_ids, shift[:, None, :], 0), axis=2) + chunk_row[None, :]
    starts = jnp.where(chunk_row[None, :] < buf_end[:, -1:], starts, 0).reshape(-1).astype(I32)

    idx_all = jnp.concatenate([s[3] for s in staged], axis=1)
    rank_all = jnp.concatenate([s[4] for s in staged], axis=1)
    buf_start_tok = jnp.repeat(buf_start, tm, axis=0).T
    rbuf = jnp.sum(jnp.where(idx_all[None] == e_ids[:, None, None], buf_start_tok[:, None, :], 0), axis=0) + rank_all
    rcol = rbuf.T
    block_row = jnp.arange(n_blocks, dtype=I32) * bm
    block_e = jnp.minimum(jnp.sum((pad_end[None, :] <= block_row[:, None]).astype(I32), axis=1), N_EXPERTS - 1)
    n_used = (pad_end[-1:] // bm).astype(I32)
    meta = jnp.concatenate([jnp.where(region > 0, pad_end - bm, -1), n_used]).astype(I32)

    xs = _dispatch_call(starts, nq, meta, rbuf, staged[0][2], staged[1][2], cap)
    ys = _expert_call(block_e, n_used, xs, w_up[l], b_up[l], w_down[l], b_down[l])

    outs = []
    tok0 = 0
    for x1, mod3, *_, blocking in staged:
        outs.append(_combine_call(starts, nq, x1, mod3, rcol, g_final, ys, tok0, *blocking))
        tok0 += x1.shape[0] * x1.shape[1]

    st_p, st_s = staged[0][5], staged[1][5]
    return (outs[0], outs[1]) + tuple(a[None] for a in st_p) + tuple(a[None] for a in st_s)
```

```python
import functools

import numpy as np
import jax
import jax.numpy as jnp
from jax import lax
from jax.experimental import pallas as pl
from jax.experimental.pallas import tpu as pltpu

F32 = jnp.float32
BF16 = jnp.bfloat16
I32 = jnp.int32

D_MODEL = 1024
PAST_LEN = 16384
RET_HEADS = 4
MLSTM_HEADS = 4
HEAD_DIM = 128
RET_WIDTH = RET_HEADS * HEAD_DIM
MLSTM_WIDTH = MLSTM_HEADS * HEAD_DIM
CONV_WIDTH = 4
CHUNK = 128
ROPE_BASE = 10000.0
N_EXPERTS = 32
TOP_K = 4
D_FF = D_MODEL
SWIGLU_LIMIT = 7.0
SWIGLU_ALPHA = 1.702
N_MOD = 6
EPS = 1e-6
MAIN_COLS = 4 * RET_WIDTH + 2 * MLSTM_WIDTH + 2 * MLSTM_WIDTH
N_GATES = 2 * MLSTM_HEADS
OFF_RQ, OFF_RK, OFF_RV, OFF_RG = 0, RET_WIDTH, 2 * RET_WIDTH, 3 * RET_WIDTH
OFF_MQK = 4 * RET_WIDTH
OFF_MV = OFF_MQK + 2 * MLSTM_WIDTH
OFF_MO = OFF_MV + MLSTM_WIDTH

VMEM_LIMIT_BYTES = 56 * 1024 * 1024
MOE_TILE = 512
EXPERT_BLOCK = 512
MIXER_SEQS_CHUNKED = 4
MIXER_SEQS_SHORT = 16
MIXER_INTERLEAVE_SEQS = 2
RUN_ALIGN = 16
SIDE_ROWS = 4 * TOP_K
SIDE_LANES = 128
TILE_BUF_ROWS = MOE_TILE * TOP_K + N_EXPERTS * RUN_ALIGN
PERM_ROWS = 1280
PERM_COLS = 512
CONV_TAIL_ROWS = 8


def _params(n_axes=1):
    return pltpu.CompilerParams(dimension_semantics=("arbitrary",) * n_axes, vmem_limit_bytes=VMEM_LIMIT_BYTES)


def _dot(a, b):
    return jnp.dot(a, b, preferred_element_type=F32)


def _dot_nt(a, b):
    return lax.dot_general(a, b, (((1,), (1,)), ((), ())), preferred_element_type=F32)


def _dot_tn(a, b):
    return lax.dot_general(a, b, (((0,), (0,)), ((), ())), preferred_element_type=F32)


def _split(a):
    hi = a.astype(BF16)
    lo = (a - hi.astype(F32)).astype(BF16)
    return hi, lo


def _dot3(a, b, dot=_dot):
    ah, al = _split(a)
    bh, bl = _split(b)
    return dot(ah, bh) + (dot(al, bh) + dot(ah, bl))


def _sigmoid(x):
    return 0.5 * (jnp.tanh(0.5 * x) + 1.0)


def _log_sigmoid(x):
    return jnp.minimum(x, 0.0) - jnp.log1p(jnp.exp(-jnp.abs(x)))


def _rms(x, g):
    ms = jnp.mean(x * x, axis=-1, keepdims=True)
    return (x * lax.rsqrt(ms + EPS)) * g


def _layer_norm(x, g):
    mu = jnp.mean(x, axis=-1, keepdims=True)
    xc = x - mu
    var = jnp.mean(xc * xc, axis=-1, keepdims=True)
    return xc * lax.rsqrt(var + EPS) * g


def _mod_kernel(c_ref, w_ref, b_ref, o_ref):
    c = c_ref[...]
    o_ref[...] = _dot3(c * _sigmoid(c), w_ref[...]) + b_ref[...]


def _mod_call(c_all, w_mod, b_mod):
    rows, d = c_all.shape
    cols = w_mod.shape[1]
    tn = 1024
    return pl.pallas_call(
        _mod_kernel,
        grid=(cols // tn,),
        in_specs=[pl.BlockSpec((rows, d), lambda j: (0, 0)),
                  pl.BlockSpec((d, tn), lambda j: (0, j)),
                  pl.BlockSpec((1, tn), lambda j: (0, j))],
        out_specs=pl.BlockSpec((rows, tn), lambda j: (0, j)),
        out_shape=jax.ShapeDtypeStruct((rows, cols), F32),
        compiler_params=_params(1), name="mod",
    )(c_all, w_mod, b_mod.reshape(1, cols))


def _mixer_kernel(x_ref, sh_ref, sc_ref, gmix_ref, w_ref, wg_ref, wgt_ref,
                  cos_ref, sin_ref, dec_ref, qd_ref, kd_ref, cd_ref,
                  tril_ref, triu_ref, wconv_ref, bconv_ref, gret_ref, gml_ref, bgc_ref, bgr_ref,
                  s0_ref, c0_ref, n0_ref, m0_ref, conv0_ref,
                  y_ref, s_ref, c_ref, n_ref, m_ref, tail_ref, p_ref, xp_ref):
    @pl.when(pl.program_id(1) == 0)
    def _():
        s_ref[...] = s0_ref[...]
        c_ref[...] = c0_ref[...]
        n_ref[...] = n0_ref[...]
        m_ref[...] = m0_ref[...]
        tail_ref[...] = conv0_ref[...]

    x = x_ref[...]
    gs, L, d = x.shape
    h = (_rms(x, gmix_ref[...]) * (1.0 + sc_ref[...]) + sh_ref[...]).reshape(gs * L, d)
    hb = h.astype(BF16)
    for j in range(MAIN_COLS // 1024):
        p_ref[:, j * 1024:(j + 1) * 1024] = _dot(hb, w_ref[:, j * 1024:(j + 1) * 1024])

    chains = []
    for sq in range(gs):
        h_sq = h[sq * L:(sq + 1) * L, :]
        gc = _dot3(h_sq, wg_ref[...])[:, :N_GATES]
        gr = _dot3(wgt_ref[...], h_sq, dot=_dot_nt)
        chains += _mixer_sequence(p_ref.at[pl.ds(sq * L, L)], gc, gr, cos_ref, sin_ref, dec_ref, qd_ref, kd_ref,
                                  cd_ref, tril_ref, triu_ref, wconv_ref, bconv_ref, gret_ref, gml_ref, bgc_ref,
                                  bgr_ref, y_ref.at[sq], s_ref.at[sq], c_ref.at[sq], n_ref.at[sq], m_ref.at[sq],
                                  tail_ref.at[sq], xp_ref.at[sq])
    width = (gs if L == CHUNK else MIXER_INTERLEAVE_SEQS) * (RET_HEADS + MLSTM_HEADS)
    for first in range(0, len(chains), width):
        group = chains[first:first + width]
        while group:
            alive = []
            for chain in group:
                if next(chain, None) is not None:
                    alive.append(chain)
            group = alive


N_MIXER_INPUTS = 26
N_OUTPROJ_INPUTS = 8


def _mixer_outproj_kernel(*refs):
    mixer_in = refs[:N_MIXER_INPUTS]
    outproj_in = refs[N_MIXER_INPUTS:N_MIXER_INPUTS + N_OUTPROJ_INPUTS]
    x1_ref, h2_ref, idx_ref, rank_ref, cnt_ref, s_ref, c_ref, n_ref, m_ref, tail_ref, p_ref, xp_ref, y_ref = refs[
        N_MIXER_INPUTS + N_OUTPROJ_INPUTS:]
    _mixer_kernel(*mixer_in, y_ref, s_ref, c_ref, n_ref, m_ref, tail_ref, p_ref, xp_ref)
    gs, L, d = y_ref.shape
    _outproj_router(y_ref[...].reshape(gs * L, d), mixer_in[0], *outproj_in, x1_ref, h2_ref, idx_ref, rank_ref, cnt_ref)


def _mixer_sequence(p_ref, gc, gr, cos_ref, sin_ref, dec_ref, qd_ref, kd_ref, cd_ref,
                    tril_ref, triu_ref, wconv_ref, bconv_ref, gret_ref, gml_ref, bgc_ref, bgr_ref,
                    y_ref, s_ref, c_ref, n_ref, m_ref, tail_ref, xp_ref):
    L = p_ref.shape[0]
    cos = cos_ref[...]
    sin = sin_ref[...]
    scale = HEAD_DIM ** -0.5

    def rot(x):
        return x * cos + pltpu.roll(x, HEAD_DIM // 2, axis=1) * sin

    def retention_head(h):
        lo = h * HEAD_DIM
        q = rot(p_ref[:, OFF_RQ + lo:OFF_RQ + lo + HEAD_DIM])
        k = rot(p_ref[:, OFF_RK + lo:OFF_RK + lo + HEAD_DIM]) * scale
        v = p_ref[:, OFF_RV + lo:OFF_RV + lo + HEAD_DIM].astype(BF16)
        s_old = s_ref[h]
        yield True
        scores = _dot_nt(q.astype(BF16), k.astype(BF16)) * dec_ref[h]
        yield True
        out = _dot(scores.astype(BF16), v) + _dot((q * qd_ref[h]).astype(BF16), s_old.astype(BF16))
        s_ref[h] = cd_ref[h] * s_old + _dot_tn((k * kd_ref[h]).astype(BF16), v)
        yield True
        g = p_ref[:, OFF_RG + lo:OFF_RG + lo + HEAD_DIM]
        y_ref[:, lo:lo + HEAD_DIM] = ((g * _sigmoid(g)) * _layer_norm(out, gret_ref[:, lo:lo + HEAD_DIM])).astype(y_ref.dtype)

    xp_ref[0:CONV_TAIL_ROWS, :] = tail_ref[...]
    xp_ref[CONV_TAIL_ROWS:CONV_TAIL_ROWS + L, :] = p_ref[:, OFF_MQK:OFF_MQK + 2 * MLSTM_WIDTH]
    acc = bconv_ref[...] + wconv_ref[0:1, :] * xp_ref[CONV_TAIL_ROWS - 3:CONV_TAIL_ROWS - 3 + L, :]
    for j in range(1, CONV_WIDTH):
        acc = acc + wconv_ref[j:j + 1, :] * xp_ref[CONV_TAIL_ROWS - 3 + j:CONV_TAIL_ROWS - 3 + j + L, :]
    tail_ref[...] = xp_ref[L:L + CONV_TAIL_ROWS, :]
    xp_ref[CONV_TAIL_ROWS:CONV_TAIL_ROWS + L, :] = acc * _sigmoid(acc)

    gcol = gc + bgc_ref[...]
    is_f_col = lax.broadcasted_iota(I32, gcol.shape, 1) >= MLSTM_HEADS
    gcol = jnp.where(is_f_col, _log_sigmoid(gcol), gcol)
    grow = gr + bgr_ref[...]
    is_f_row = lax.broadcasted_iota(I32, grow.shape, 0) >= MLSTM_HEADS
    grow = jnp.where(is_f_row, _log_sigmoid(grow), grow)
    bcol_all = _dot3(tril_ref[...], gcol)
    brow_all = _dot3(grow, triu_ref[...])
    causal = lax.broadcasted_iota(I32, (L, L), 0) >= lax.broadcasted_iota(I32, (L, L), 1)

    def mlstm_head(h):
        lo = h * HEAD_DIM
        q = xp_ref[CONV_TAIL_ROWS:CONV_TAIL_ROWS + L, lo:lo + HEAD_DIM]
        k = xp_ref[CONV_TAIL_ROWS:CONV_TAIL_ROWS + L, MLSTM_WIDTH + lo:MLSTM_WIDTH + lo + HEAD_DIM] * scale
        v = p_ref[:, OFF_MV + lo:OFF_MV + lo + HEAD_DIM].astype(BF16)
        ic_col = gcol[:, h:h + 1]
        ic_row = grow[h:h + 1, :]
        b_col = bcol_all[:, MLSTM_HEADS + h:MLSTM_HEADS + h + 1]
        b_row = brow_all[MLSTM_HEADS + h:MLSTM_HEADS + h + 1, :]
        c_old = c_ref[h]
        n_old = n_ref[h]
        m_old = m_ref[h][:, 0:1]

        d_log = jnp.where(causal, b_col - b_row + ic_row, -jnp.inf)
        inter = b_col + m_old
        m_t = jnp.maximum(inter, jnp.max(d_log, axis=1, keepdims=True))
        yield True
        w_intra = jnp.exp(d_log - m_t)
        w_inter = jnp.exp(inter - m_t)
        qb = q.astype(BF16)
        s = _dot_nt(qb, k.astype(BF16)) * w_intra
        yield True
        num = _dot(s.astype(BF16), v) + w_inter * _dot(qb, c_old.astype(BF16))
        den = jnp.sum(s, axis=1, keepdims=True) + w_inter * jnp.sum(q * n_old, axis=1, keepdims=True)
        yield True
        hh = num / jnp.maximum(jnp.abs(den), jnp.exp(-m_t))

        b_last = b_col[L - 1:L, :]
        w_log_col = b_last - b_col + ic_col
        m_new = jnp.maximum(b_last + m_old, jnp.max(w_log_col, axis=0, keepdims=True))
        wk = jnp.exp(w_log_col - m_new) * k
        cdec = jnp.exp(b_last + m_old - m_new)
        yield True
        c_ref[h] = cdec * c_old + _dot_tn(wk.astype(BF16), v)
        n_ref[h] = cdec * n_old + jnp.sum(wk, axis=0, keepdims=True)
        m_ref[h] = jnp.broadcast_to(m_new, (1, HEAD_DIM))
        yield True
        o = p_ref[:, OFF_MO + lo:OFF_MO + lo + HEAD_DIM]
        y_ref[:, RET_WIDTH + lo:RET_WIDTH + lo + HEAD_DIM] = (
            _sigmoid(o) * _layer_norm(hh, gml_ref[:, lo:lo + HEAD_DIM])).astype(y_ref.dtype)

    return [retention_head(h) for h in range(RET_HEADS)] + [mlstm_head(h) for h in range(MLSTM_HEADS)]


def _mixer_consts(L, pos):
    f32 = np.float32
    half = HEAD_DIM // 2
    inv_freq = np.power(f32(ROPE_BASE), -np.arange(half, dtype=f32) / f32(half)).astype(f32)
    ang = (pos.astype(f32)[:, None] * inv_freq[None, :]).astype(f32)
    cos = np.concatenate([np.cos(ang), np.cos(ang)], axis=-1).astype(f32)
    sin = np.concatenate([-np.sin(ang), np.sin(ang)], axis=-1).astype(f32)
    log_gamma = np.log1p(-np.exp2(-5.0 - np.arange(RET_HEADS, dtype=np.float64)))
    idx = np.arange(L, dtype=np.float64)
    rel = idx[:, None] - idx[None, :]
    dec = np.where(rel >= 0, np.exp(log_gamma[:, None, None] * np.maximum(rel, 0.0)), 0.0)
    qd = np.broadcast_to(np.exp(log_gamma[:, None] * (idx + 1.0))[..., None], (RET_HEADS, L, HEAD_DIM))
    kd = np.broadcast_to(np.exp(log_gamma[:, None] * (L - 1.0 - idx))[..., None], (RET_HEADS, L, HEAD_DIM))
    cd = np.broadcast_to(np.exp(log_gamma * L)[:, None, None], (RET_HEADS, 1, HEAD_DIM))
    tril = rel >= 0
    triu = rel <= 0
    return tuple(jnp.asarray(a, F32) for a in (cos, sin, dec, qd, kd, cd, tril, triu))


def _mixer_call(x3, mod3, g_mix, w_main_bf, w_gate, w_gate_t, consts, w_conv, b_conv, g_ret, g_mlstm, bg_col, bg_row,
                s0, c0, n0, m0, conv0, L, gs, outproj=None):
    cos, sin, dec, qd, kd, cd, tril, triu = consts
    groups, seq, d = x3.shape
    nc = seq // L
    hd = HEAD_DIM
    full = lambda *shape: pl.BlockSpec(shape, lambda g, c: (0,) * len(shape))
    mod_spec = lambda k: pl.BlockSpec((gs, 1, d), lambda g, c: (g, 0, k))
    state4 = pl.BlockSpec((gs, RET_HEADS, hd, hd), lambda g, c: (g, 0, 0, 0))
    vec4 = pl.BlockSpec((gs, MLSTM_HEADS, 1, hd), lambda g, c: (g, 0, 0, 0))
    tail3 = pl.BlockSpec((gs, CONV_TAIL_ROWS, 2 * MLSTM_WIDTH), lambda g, c: (g, 0, 0))
    row = lambda g, c: (g, c, 0)
    in_specs = [pl.BlockSpec((gs, L, d), row), mod_spec(0), mod_spec(1),
                full(1, 1, d), full(d, MAIN_COLS), full(d, 128), full(N_GATES, d),
                pl.BlockSpec((L, hd), lambda g, c: (c, 0)),
                pl.BlockSpec((L, hd), lambda g, c: (c, 0)),
                full(RET_HEADS, L, L), full(RET_HEADS, L, hd), full(RET_HEADS, L, hd), full(RET_HEADS, 1, hd),
                full(L, L), full(L, L),
                full(CONV_WIDTH, 2 * MLSTM_WIDTH), full(1, 2 * MLSTM_WIDTH),
                full(1, RET_WIDTH), full(1, MLSTM_WIDTH), full(1, N_GATES), full(N_GATES, 1),
                state4, state4, vec4, vec4, tail3]
    args = [x3, mod3, mod3, g_mix.reshape(1, 1, d), w_main_bf, w_gate, w_gate_t,
            cos, sin, dec, qd, kd, cd, tril, triu, w_conv, b_conv.reshape(1, -1),
            g_ret.reshape(1, -1), g_mlstm.reshape(1, -1), bg_col, bg_row, s0, c0, n0, m0, conv0]
    assert len(in_specs) == len(args) == N_MIXER_INPUTS
    state_specs = [state4, state4, vec4, vec4, tail3]
    state_shapes = [jax.ShapeDtypeStruct((groups, RET_HEADS, hd, hd), F32),
                    jax.ShapeDtypeStruct((groups, MLSTM_HEADS, hd, hd), F32),
                    jax.ShapeDtypeStruct((groups, MLSTM_HEADS, 1, hd), F32),
                    jax.ShapeDtypeStruct((groups, MLSTM_HEADS, 1, hd), F32),
                    jax.ShapeDtypeStruct((groups, CONV_TAIL_ROWS, 2 * MLSTM_WIDTH), F32)]
    scratch = [pltpu.VMEM((gs * L, MAIN_COLS), F32), pltpu.VMEM((gs, CONV_TAIL_ROWS + L, 2 * MLSTM_WIDTH), F32)]
    y_block, y_shape = (gs, L, RET_WIDTH + MLSTM_WIDTH), (groups, seq, RET_WIDTH + MLSTM_WIDTH)
    if outproj is None:
        kernel_fn = _mixer_kernel
        out_specs = [pl.BlockSpec(y_block, row)] + state_specs
        out_shape = [jax.ShapeDtypeStruct(y_shape, F32)] + state_shapes
    else:
        g_ffn, w_out_bf, w_router_t, b_router, ustrict = outproj
        tm = gs * L
        n = groups * seq
        tile = lambda g, c: g * nc + c
        kernel_fn = _mixer_outproj_kernel
        in_specs += [mod_spec(2), mod_spec(3), mod_spec(4),
                     full(1, 1, d), full(d, d), full(N_EXPERTS, d), full(N_EXPERTS, 1), full(tm, tm)]
        args += [mod3, mod3, mod3, g_ffn.reshape(1, 1, d), w_out_bf, w_router_t, b_router.reshape(N_EXPERTS, 1),
                 ustrict]
        assert len(args) == N_MIXER_INPUTS + N_OUTPROJ_INPUTS
        out_specs = [pl.BlockSpec((gs, L, d), row),
                     pl.BlockSpec((tm, d + SIDE_LANES), lambda g, c: (tile(g, c), 0)),
                     pl.BlockSpec((TOP_K, tm), lambda g, c: (0, tile(g, c))),
                     pl.BlockSpec((TOP_K, tm), lambda g, c: (0, tile(g, c))),
                     pl.BlockSpec((1, N_EXPERTS, 128), lambda g, c: (tile(g, c), 0, 0))] + state_specs
        out_shape = [jax.ShapeDtypeStruct((groups, seq, d), F32),
                     jax.ShapeDtypeStruct((n, d + SIDE_LANES), BF16),
                     jax.ShapeDtypeStruct((TOP_K, n), I32),
                     jax.ShapeDtypeStruct((TOP_K, n), I32),
                     jax.ShapeDtypeStruct((n // tm, N_EXPERTS, 128), F32)] + state_shapes
        scratch = scratch + [pltpu.VMEM(y_block, F32)]
    return pl.pallas_call(
        kernel_fn,
        grid=(groups // gs, nc),
        in_specs=in_specs, out_specs=out_specs, out_shape=out_shape, scratch_shapes=scratch,
        compiler_params=_params(2), name="mixer",
    )(*args)


def _outproj_kernel(y_ref, x_ref, gt_ref, sh_ref, sc_ref, g_ref, w_ref, wrt_ref, br_ref, ustrict_ref,
                    x1_ref, h2_ref, idx_ref, rank_ref, cnt_ref):
    _outproj_router(y_ref[...], x_ref, gt_ref, sh_ref, sc_ref, g_ref, w_ref, wrt_ref, br_ref, ustrict_ref,
                    x1_ref, h2_ref, idx_ref, rank_ref, cnt_ref)


def _outproj_router(y, x_ref, gt_ref, sh_ref, sc_ref, g_ref, w_ref, wrt_ref, br_ref, ustrict_ref,
                    x1_ref, h2_ref, idx_ref, rank_ref, cnt_ref):
    x = x_ref[...]
    gb, tt, d = x.shape
    tm = gb * tt
    mixed = _dot(y.astype(BF16), w_ref[...])
    x1 = x + gt_ref[...] * mixed.reshape(gb, tt, d)
    x1_ref[...] = x1
    h2 = (_rms(x1, g_ref[...]) * (1.0 + sc_ref[...]) + sh_ref[...]).reshape(tm, d)
    h2_ref[:, :d] = h2.astype(BF16)

    work = _dot3(wrt_ref[...], h2, dot=_dot_nt) + br_ref[...]
    e_iota = lax.broadcasted_iota(I32, work.shape, 0).astype(F32)
    vals, idxs, sels = [], [], []
    for _ in range(TOP_K):
        mx = jnp.max(work, axis=0, keepdims=True)
        ik = jnp.min(jnp.where(work == mx, e_iota, float(N_EXPERTS)), axis=0, keepdims=True)
        sel = e_iota == ik
        vals.append(mx)
        idxs.append(ik)
        sels.append(sel)
        work = jnp.where(sel, -jnp.inf, work)
    exps = [jnp.exp(v - vals[0]) for v in vals]
    denom = exps[0] + exps[1] + exps[2] + exps[3]
    gates = [e / denom for e in exps]
    idx_ref[...] = jnp.concatenate(idxs, axis=0).astype(I32)

    pieces = []
    for gk in gates:
        p1 = gk.astype(BF16)
        r1 = gk - p1.astype(F32)
        p2 = r1.astype(BF16)
        pieces += [p1, p2, (r1 - p2.astype(F32)).astype(BF16)]
    side = jnp.concatenate(pieces + [ik.astype(BF16) for ik in idxs], axis=0)
    eye = (lax.broadcasted_iota(I32, (SIDE_ROWS, SIDE_LANES), 0)
           == lax.broadcasted_iota(I32, (SIDE_ROWS, SIDE_LANES), 1)).astype(BF16)
    h2_ref[:, d:] = _dot_tn(side, eye).astype(BF16)

    mask = (sels[0] | sels[1] | sels[2] | sels[3]).astype(F32)
    before = _dot(mask.astype(BF16), ustrict_ref[...])
    ranks = [jnp.sum(jnp.where(sel, before, 0.0), axis=0, keepdims=True) for sel in sels]
    rank_ref[...] = jnp.concatenate(ranks, axis=0).astype(I32)
    cnt_ref[0] = jnp.broadcast_to(jnp.sum(mask, axis=1, keepdims=True), cnt_ref.shape[1:])


def _outproj_call(y, x3, mod3, g_ffn, w_out_bf, w_router_t, b_router, ustrict, gb, tt):
    g, t, d = x3.shape
    n = g * t
    tm = gb * tt
    tpg = t // tt
    row = lambda i, j: (i * tpg + j, 0)
    col = lambda i, j: (0, i * tpg + j)
    mod_spec = lambda k: pl.BlockSpec((gb, 1, d), lambda i, j: (i, 0, k))
    const = lambda *shape: pl.BlockSpec(shape, lambda i, j: (0,) * len(shape))
    return pl.pallas_call(
        _outproj_kernel,
        grid=(g // gb, tpg),
        in_specs=[pl.BlockSpec((tm, d), row),
                  pl.BlockSpec((gb, tt, d), lambda i, j: (i, j, 0)),
                  mod_spec(2), mod_spec(3), mod_spec(4),
                  const(1, 1, d), const(d, d), const(N_EXPERTS, d), const(N_EXPERTS, 1),
                  const(tm, tm)],
        out_specs=[pl.BlockSpec((gb, tt, d), lambda i, j: (i, j, 0)),
                   pl.BlockSpec((tm, d + SIDE_LANES), row),
                   pl.BlockSpec((TOP_K, tm), col),
                   pl.BlockSpec((TOP_K, tm), col),
                   pl.BlockSpec((1, N_EXPERTS, 128), lambda i, j: (i * tpg + j, 0, 0))],
        out_shape=[jax.ShapeDtypeStruct((g, t, d), F32),
                   jax.ShapeDtypeStruct((n, d + SIDE_LANES), BF16),
                   jax.ShapeDtypeStruct((TOP_K, n), I32),
                   jax.ShapeDtypeStruct((TOP_K, n), I32),
                   jax.ShapeDtypeStruct((n // tm, N_EXPERTS, 128), F32)],
        compiler_params=_params(2), name="outproj_router",
    )(y, x3, mod3, mod3, mod3, g_ffn.reshape(1, 1, d), w_out_bf, w_router_t, b_router.reshape(N_EXPERTS, 1),
      ustrict)


def _chunk_copy(hbm_ref, hbm_row, buf_ref, chunk, sem, to_hbm):
    hbm = hbm_ref.at[pl.ds(pl.multiple_of(hbm_row, RUN_ALIGN), RUN_ALIGN)]
    buf = buf_ref.at[pl.ds(pl.multiple_of(chunk * RUN_ALIGN, RUN_ALIGN), RUN_ALIGN)]
    return pltpu.make_async_copy(buf, hbm, sem) if to_hbm else pltpu.make_async_copy(hbm, buf, sem)


def _tile_chunks(hbm_ref, bufs_ref, starts_ref, nq_ref, sems, tile, to_hbm, wait):
    slot = tile % 2
    first = tile * (bufs_ref.shape[1] // RUN_ALIGN)

    def body(q, c):
        cp = _chunk_copy(hbm_ref, starts_ref[first + q], bufs_ref.at[slot], q, sems.at[slot], to_hbm)
        cp.wait() if wait else cp.start()
        return c

    lax.fori_loop(0, nq_ref[tile], body, 0)


def _dispatch_kernel(starts_ref, nq_ref, meta_ref, rbuf_ref, h2p_ref, h2s_ref, xs_ref, bufs_ref, zero_ref, sems,
                     *, prompt_tiles):
    i = pl.program_id(0)
    tm = h2p_ref.shape[0]
    bm = zero_ref.shape[0]
    n_blocks = xs_ref.shape[0] // bm
    sem = sems.at[0]
    buf_ref = bufs_ref.at[i % 2]

    @pl.when(i == 0)
    def _():
        zero_ref[...] = jnp.zeros(zero_ref.shape, zero_ref.dtype)

        def zero_copy(row):
            return pltpu.make_async_copy(zero_ref, xs_ref.at[pl.ds(pl.multiple_of(row, bm), bm)], sem)

        def tails(fn):
            def body(e, c):
                @pl.when(meta_ref[e] >= 0)
                def _():
                    fn(zero_copy(meta_ref[e]))
                return c
            lax.fori_loop(0, N_EXPERTS, body, 0)

        def unused(fn):
            def body(b, c):
                fn(zero_copy(b * bm))
                return c
            lax.fori_loop(meta_ref[N_EXPERTS], n_blocks, body, 0)

        tails(lambda cp: cp.start())
        unused(lambda cp: cp.start())
        tails(lambda cp: cp.wait())
        unused(lambda cp: cp.wait())

    def build(h2_ref):
        h2 = h2_ref[...]
        rb = rbuf_ref[...].astype(jnp.int16)
        for c in range(bufs_ref.shape[1] // PERM_ROWS):
            r = (lax.broadcasted_iota(I32, (PERM_ROWS, tm), 0) + c * PERM_ROWS).astype(jnp.int16)
            hit = (r == rb[0:1, :]) | (r == rb[1:2, :]) | (r == rb[2:3, :]) | (r == rb[3:4, :])
            onehot = jnp.where(hit, jnp.ones((), BF16), jnp.zeros((), BF16))
            buf_ref[c * PERM_ROWS:(c + 1) * PERM_ROWS, :] = _dot(onehot, h2).astype(BF16)

    @pl.when(i < prompt_tiles)
    def _():
        build(h2p_ref)

    @pl.when(i >= prompt_tiles)
    def _():
        build(h2s_ref)

    chunks = functools.partial(_tile_chunks, xs_ref, bufs_ref, starts_ref, nq_ref, sems, to_hbm=True)
    chunks(i, wait=False)

    @pl.when(i > 0)
    def _():
        chunks(i - 1, wait=True)

    @pl.when(i == pl.num_programs(0) - 1)
    def _():
        chunks(i, wait=True)


def _dispatch_call(starts, nq, meta, rbuf, h2_p, h2_s, cap):
    d = h2_p.shape[1]
    tm = MOE_TILE
    pt, st = h2_p.shape[0] // tm, h2_s.shape[0] // tm
    return pl.pallas_call(
        functools.partial(_dispatch_kernel, prompt_tiles=pt),
        grid_spec=pltpu.PrefetchScalarGridSpec(
            num_scalar_prefetch=3,
            grid=(pt + st,),
            in_specs=[pl.BlockSpec((TOP_K, tm), lambda i, *_: (0, i)),
                      pl.BlockSpec((tm, d), lambda i, *_: (jnp.minimum(i, pt - 1), 0)),
                      pl.BlockSpec((tm, d), lambda i, *_: (jnp.maximum(i - pt, 0), 0))],
            out_specs=pl.BlockSpec(memory_space=pl.ANY),
            scratch_shapes=[pltpu.VMEM((2, TILE_BUF_ROWS, d), BF16), pltpu.VMEM((EXPERT_BLOCK, d), BF16),
                            pltpu.SemaphoreType.DMA((2,))]),
        out_shape=jax.ShapeDtypeStruct((cap, d), BF16),
        compiler_params=_params(1), name="dispatch",
    )(starts, nq, meta, rbuf, h2_p, h2_s)


def _expert_kernel(be_ref, nu_ref, xs_ref, wup_ref, bup_ref, wdn_ref, bdn_ref, ys_ref, wup_bf, wdn_bf):
    i = pl.program_id(0)
    prev = be_ref[jnp.maximum(i - 1, 0)]

    @pl.when((i == 0) | (be_ref[i] != prev))
    def _():
        def cast(r, c):
            rows = pl.ds(pl.multiple_of(r * 64, 64), 64)
            wup_bf[rows, :] = wup_ref[0, rows, :].astype(BF16)
            wdn_bf[rows, :] = wdn_ref[0, rows, :].astype(BF16)
            return c

        lax.fori_loop(0, D_MODEL // 64, cast, 0)

    @pl.when(i < nu_ref[0])
    def _():
        hu = _dot(xs_ref[:, :D_MODEL], wup_bf[...]) + bup_ref[0]
        gate = jnp.minimum(hu[:, :D_FF], SWIGLU_LIMIT)
        lin = jnp.clip(hu[:, D_FF:], -SWIGLU_LIMIT, SWIGLU_LIMIT)
        glu = gate * _sigmoid(SWIGLU_ALPHA * gate)
        y = _dot(((lin + 1.0) * glu).astype(BF16), wdn_bf[...]) + bdn_ref[0]
        side = xs_ref[:, D_MODEL:].astype(F32)
        e = be_ref[i].astype(F32)
        weight = jnp.zeros((side.shape[0], 1), F32)
        for k in range(TOP_K):
            g_k = side[:, 3 * k:3 * k + 1] + side[:, 3 * k + 1:3 * k + 2] + side[:, 3 * k + 2:3 * k + 3]
            weight = weight + jnp.where(side[:, SIDE_ROWS - TOP_K + k:SIDE_ROWS - TOP_K + k + 1] == e, g_k, 0.0)
        ys_ref[...] = (weight * y).astype(ys_ref.dtype)

    @pl.when(i >= nu_ref[0])
    def _():
        ys_ref[...] = jnp.zeros(ys_ref.shape, ys_ref.dtype)


def _expert_call(block_e, n_used, xs, w_up, b_up, w_down, b_down):
    cap, dw = xs.shape
    d = w_down.shape[2]
    bm = EXPERT_BLOCK
    blk = lambda i, be, nu: (jnp.minimum(i, nu[0] - 1), 0)
    per_e = lambda i, be, nu: (be[i], 0, 0)
    return pl.pallas_call(
        _expert_kernel,
        grid_spec=pltpu.PrefetchScalarGridSpec(
            num_scalar_prefetch=2,
            grid=(cap // bm,),
            in_specs=[pl.BlockSpec((bm, dw), blk),
                      pl.BlockSpec((1, d, 2 * D_FF), per_e),
                      pl.BlockSpec((1, 1, 2 * D_FF), per_e),
                      pl.BlockSpec((1, D_FF, d), per_e),
                      pl.BlockSpec((1, 1, d), per_e)],
            out_specs=pl.BlockSpec((bm, d), lambda i, be, nu: (i, 0)),
            scratch_shapes=[pltpu.VMEM((d, 2 * D_FF), BF16), pltpu.VMEM((D_FF, d), BF16)]),
        out_shape=jax.ShapeDtypeStruct((cap, d), BF16),
        compiler_params=_params(1), name="experts",
    )(block_e, n_used, xs, w_up, b_up.reshape(N_EXPERTS, 1, -1), w_down, b_down.reshape(N_EXPERTS, 1, -1))


def _combine_kernel(starts_ref, nq_ref, x1_ref, gt_ref, rcol_ref, gfin_ref, ys_ref, o_ref, bufs_ref, sems,
                    *, tile0, tiles_per_group):
    x1 = x1_ref[...]
    gb, tt, d = x1.shape
    tm = gb * tt
    step = pl.program_id(0) * tiles_per_group + pl.program_id(1)
    n_steps = pl.num_programs(0) * tiles_per_group
    tile = tile0 + step
    chunks = functools.partial(_tile_chunks, ys_ref, bufs_ref, starts_ref, nq_ref, sems, to_hbm=False)

    @pl.when(step == 0)
    def _():
        bufs_ref[...] = jnp.zeros(bufs_ref.shape, bufs_ref.dtype)
        chunks(tile, wait=False)

    @pl.when(step + 1 < n_steps)
    def _():
        chunks(tile + 1, wait=False)

    chunks(tile, wait=True)
    buf_ref = bufs_ref.at[tile % 2]

    rows = rcol_ref[...].astype(jnp.int16)
    moe = jnp.zeros((tm, d), F32)
    for c in range(bufs_ref.shape[1] // PERM_COLS):
        r = (lax.broadcasted_iota(I32, (tm, PERM_COLS), 1) + c * PERM_COLS).astype(jnp.int16)
        hit = (r == rows[:, 0:1]) | (r == rows[:, 1:2]) | (r == rows[:, 2:3]) | (r == rows[:, 3:4])
        onehot = jnp.where(hit, jnp.ones((), BF16), jnp.zeros((), BF16))
        moe = moe + _dot(onehot, buf_ref[c * PERM_COLS:(c + 1) * PERM_COLS, :])
    xo = x1 + gt_ref[...] * moe.reshape(gb, tt, d)
    o_ref[...] = _rms(xo, gfin_ref[...])


def _combine_call(starts, nq, x1, mod3, rcol, g_final, ys, tok0, gb, tt):
    g, t, d = x1.shape
    tm = gb * tt
    tpg = t // tt
    t0 = tok0 // tm
    tok = lambda i, j, *_: (t0 + i * tpg + j, 0)
    return pl.pallas_call(
        functools.partial(_combine_kernel, tile0=t0, tiles_per_group=tpg),
        grid_spec=pltpu.PrefetchScalarGridSpec(
            num_scalar_prefetch=2,
            grid=(g // gb, tpg),
            in_specs=[pl.BlockSpec((gb, tt, d), lambda i, j, *_: (i, j, 0)),
                      pl.BlockSpec((gb, 1, d), lambda i, j, *_: (i, 0, 5)),
                      pl.BlockSpec((tm, TOP_K), tok),
                      pl.BlockSpec((1, 1, d), lambda i, j, *_: (0, 0, 0)),
                      pl.BlockSpec(memory_space=pl.ANY)],
            out_specs=pl.BlockSpec((gb, tt, d), lambda i, j, *_: (i, j, 0)),
            scratch_shapes=[pltpu.VMEM((2, TILE_BUF_ROWS, d), BF16), pltpu.SemaphoreType.DMA((2,))]),
        out_shape=jax.ShapeDtypeStruct((g, t, d), F32),
        compiler_params=_params(2), name="combine",
    )(starts, nq, x1, mod3, rcol, g_final.reshape(1, 1, d), ys)


def _group_blocking(groups, seq, tile):
    if seq >= tile:
        return 1, tile
    return tile // seq, seq


def kernel(x_prompt, x_sample, c_prompt, c_sample, state_ret, state_mlstm_c, state_mlstm_n, state_mlstm_m, state_conv, w_mod, b_mod, g_mix, g_ffn, w_in, b_igate, b_fgate, w_conv, b_conv, g_ret, g_mlstm, w_out, w_router, b_router, w_up, b_up, w_down, b_down, g_final):
    depth = w_mod.shape[0]
    assert depth == 1, "single-layer trunk"
    bp, tp, d = x_prompt.shape
    bs, ts, _ = x_sample.shape
    n_p, n_s = bp * tp, bs * ts
    hd = HEAD_DIM
    l = 0

    mod = _mod_call(jnp.concatenate([c_prompt, c_sample], axis=0), w_mod[l], b_mod[l])
    mod_p = mod[:bp].reshape(bp, 1, N_MOD * d)
    mod_s = mod[bp:].reshape(bs, 1, N_MOD * d)

    w_main_bf = w_in[l][:, :MAIN_COLS].astype(BF16)
    w_gate = jnp.pad(w_in[l][:, MAIN_COLS:], ((0, 0), (0, 128 - N_GATES)))
    w_gate_t = w_in[l][:, MAIN_COLS:].T
    w_out_bf = w_out[l].astype(BF16)
    bg_col = jnp.concatenate([b_igate[l], b_fgate[l]]).reshape(1, N_GATES)
    bg_row = bg_col.reshape(N_GATES, 1)

    groups = (
        (x_prompt, mod_p, min(CHUNK, tp), np.arange(tp),
         jnp.zeros((bp, RET_HEADS, hd, hd), F32), jnp.zeros((bp, MLSTM_HEADS, hd, hd), F32),
         jnp.zeros((bp, MLSTM_HEADS, hd), F32), jnp.zeros((bp, MLSTM_HEADS), F32),
         jnp.zeros((bp, CONV_WIDTH - 1, 2 * MLSTM_WIDTH), F32)),
        (x_sample, mod_s, min(CHUNK, ts), PAST_LEN + np.arange(ts),
         state_ret[l], state_mlstm_c[l], state_mlstm_n[l], state_mlstm_m[l], state_conv[l]),
    )

    ustrict = jnp.asarray(np.arange(MOE_TILE)[:, None] < np.arange(MOE_TILE)[None, :], BF16)
    staged = []
    for x3, mod3, L, pos, s0, c0, n0, m0, conv0 in groups:
        g, t, _ = x3.shape
        conv0p = jnp.pad(conv0.astype(F32), ((0, 0), (CONV_TAIL_ROWS - (CONV_WIDTH - 1), 0), (0, 0)))
        outproj = (g_ffn[l], w_out_bf, w_router[l].T, b_router[l], ustrict)
        chunked = t > L
        gs = MIXER_SEQS_CHUNKED if chunked else MIXER_SEQS_SHORT
        fused = chunked and gs * L == MOE_TILE
        res = _mixer_call(
            x3, mod3, g_mix[l], w_main_bf, w_gate, w_gate_t, _mixer_consts(L, pos),
            w_conv[l], b_conv[l], g_ret[l], g_mlstm[l],
            bg_col, bg_row, s0.astype(F32), c0.astype(F32), n0.astype(F32).reshape(g, MLSTM_HEADS, 1, hd),
            jnp.broadcast_to(m0.astype(F32)[:, :, None, None], (g, MLSTM_HEADS, 1, hd)), conv0p,
            L, gs, outproj if fused else None)
        s_new, c_new, n_new, m_new, tail = res[-5:]
        states = (s_new, c_new, n_new.reshape(g, MLSTM_HEADS, hd), m_new[:, :, 0, 0],
                  tail[:, CONV_TAIL_ROWS - (CONV_WIDTH - 1):, :])
        if fused:
            x1, h2, idx, rank, cnt = res[:5]
            blocking = (gs, L)
        else:
            blocking = _group_blocking(g, t, MOE_TILE)
            x1, h2, idx, rank, cnt = _outproj_call(res[0].reshape(g * t, d), x3, mod3, *outproj, *blocking)
        staged.append((x1, mod3, h2, idx, rank, states, cnt, blocking))

    n_tok = n_p + n_s
    tm, bm, ra = MOE_TILE, EXPERT_BLOCK, RUN_ALIGN
    n_tiles = n_tok // tm
    q_max = TILE_BUF_ROWS // ra
    n_blocks = -(-(n_tok * TOP_K + n_tiles * N_EXPERTS * (ra - 1)) // bm) + N_EXPERTS
    cap = n_blocks * bm
    counts = jnp.concatenate([s[6][:, :, 0] for s in staged], axis=0).astype(I32)
    run = (counts + ra - 1) // ra * ra
    region = jnp.sum(run, axis=0)
    padded = (region + bm - 1) // bm * bm
    pad_end = jnp.cumsum(padded)
    pad_start = pad_end - padded
    run_start = pad_start[None, :] + jnp.cumsum(run, axis=0) - run
    buf_end = jnp.cumsum(run, axis=1)
    buf_start = buf_end - run
    nq = (buf_end[:, -1] // ra).astype(I32)
    chunk_row = jnp.arange(q_max, dtype=I32) * ra
    chunk_e = jnp.minimum(jnp.sum((buf_end[:, None, :] <= chunk_row[None, :, None]).astype(I32), axis=2),
                          N_EXPERTS - 1)
    e_ids = jnp.arange(N_EXPERTS, dtype=I32)
    shift = run_start - buf_start
    starts = jnp.sum(jnp.where(chunk_e[:, :, None] == e_ids, shift[:, None, :], 0), axis=2) + chunk_row[None, :]
    starts = jnp.where(chunk_row[None, :] < buf_end[:, -1:], starts, 0).reshape(-1).astype(I32)

    idx_all = jnp.concatenate([s[3] for s in staged], axis=1)
    rank_all = jnp.concatenate([s[4] for s in staged], axis=1)
    buf_start_tok = jnp.repeat(buf_start, tm, axis=0).T
    rbuf = jnp.sum(jnp.where(idx_all[None] == e_ids[:, None, None], buf_start_tok[:, None, :], 0), axis=0) + rank_all
    rcol = rbuf.T
    block_row = jnp.arange(n_blocks, dtype=I32) * bm
    block_e = jnp.minimum(jnp.sum((pad_end[None, :] <= block_row[:, None]).astype(I32), axis=1), N_EXPERTS - 1)
    n_used = (pad_end[-1:] // bm).astype(I32)
    meta = jnp.concatenate([jnp.where(region > 0, pad_end - bm, -1), n_used]).astype(I32)

    xs = _dispatch_call(starts, nq, meta, rbuf, staged[0][2], staged[1][2], cap)
    ys = _expert_call(block_e, n_used, xs, w_up[l], b_up[l], w_down[l], b_down[l])

    outs = []
    tok0 = 0
    for x1, mod3, *_, blocking in staged:
        outs.append(_combine_call(starts, nq, x1, mod3, rcol, g_final, ys, tok0, *blocking))
        tok0 += x1.shape[0] * x1.shape[1]

    st_p, st_s = staged[0][5], staged[1][5]
    return (outs[0], outs[1]) + tuple(a[None] for a in st_p) + tuple(a[None] for a in st_s)
```

```python
import functools

import numpy as np
import jax
import jax.numpy as jnp
from jax import lax
from jax.experimental import pallas as pl
from jax.experimental.pallas import tpu as pltpu

F32 = jnp.float32
BF16 = jnp.bfloat16
I32 = jnp.int32

D_MODEL = 1024
PAST_LEN = 16384
RET_HEADS = 4
MLSTM_HEADS = 4
HEAD_DIM = 128
RET_WIDTH = RET_HEADS * HEAD_DIM
MLSTM_WIDTH = MLSTM_HEADS * HEAD_DIM
CONV_WIDTH = 4
CHUNK = 128
ROPE_BASE = 10000.0
N_EXPERTS = 32
TOP_K = 4
D_FF = D_MODEL
SWIGLU_LIMIT = 7.0
SWIGLU_ALPHA = 1.702
N_MOD = 6
EPS = 1e-6
MAIN_COLS = 4 * RET_WIDTH + 2 * MLSTM_WIDTH + 2 * MLSTM_WIDTH
N_GATES = 2 * MLSTM_HEADS
OFF_RQ, OFF_RK, OFF_RV, OFF_RG = 0, RET_WIDTH, 2 * RET_WIDTH, 3 * RET_WIDTH
OFF_MQK = 4 * RET_WIDTH
OFF_MV = OFF_MQK + 2 * MLSTM_WIDTH
OFF_MO = OFF_MV + MLSTM_WIDTH

VMEM_LIMIT_BYTES = 56 * 1024 * 1024
MOE_TILE = 512
EXPERT_BLOCK = 512
MIXER_SEQS_CHUNKED = 4
MIXER_SEQS_SHORT = 16
MIXER_INTERLEAVE_SEQS = 2
RUN_ALIGN = 16
SIDE_ROWS = 4 * TOP_K
SIDE_LANES = 128
TILE_BUF_ROWS = MOE_TILE * TOP_K + N_EXPERTS * RUN_ALIGN
PERM_ROWS = 1280
PERM_COLS = 512
CONV_TAIL_ROWS = 8


def _params(n_axes=1):
    return pltpu.CompilerParams(dimension_semantics=("arbitrary",) * n_axes, vmem_limit_bytes=VMEM_LIMIT_BYTES)


def _dot(a, b):
    return jnp.dot(a, b, preferred_element_type=F32)


def _dot_nt(a, b):
    return lax.dot_general(a, b, (((1,), (1,)), ((), ())), preferred_element_type=F32)


def _dot_tn(a, b):
    return lax.dot_general(a, b, (((0,), (0,)), ((), ())), preferred_element_type=F32)


def _split(a):
    hi = a.astype(BF16)
    lo = (a - hi.astype(F32)).astype(BF16)
    return hi, lo


def _dot3(a, b, dot=_dot):
    ah, al = _split(a)
    bh, bl = _split(b)
    return dot(ah, bh) + (dot(al, bh) + dot(ah, bl))


def _sigmoid(x):
    return 0.5 * (jnp.tanh(0.5 * x) + 1.0)


def _log_sigmoid(x):
    return jnp.minimum(x, 0.0) - jnp.log1p(jnp.exp(-jnp.abs(x)))


def _rms(x, g):
    ms = jnp.mean(x * x, axis=-1, keepdims=True)
    return (x * lax.rsqrt(ms + EPS)) * g


def _layer_norm(x, g):
    mu = jnp.mean(x, axis=-1, keepdims=True)
    xc = x - mu
    var = jnp.mean(xc * xc, axis=-1, keepdims=True)
    return xc * lax.rsqrt(var + EPS) * g


def _mod_kernel(c_ref, w_ref, b_ref, o_ref):
    c = c_ref[...]
    o_ref[...] = _dot3(c * _sigmoid(c), w_ref[...]) + b_ref[...]


def _mod_call(c_all, w_mod, b_mod):
    rows, d = c_all.shape
    cols = w_mod.shape[1]
    tn = 1024
    return pl.pallas_call(
        _mod_kernel,
        grid=(cols // tn,),
        in_specs=[pl.BlockSpec((rows, d), lambda j: (0, 0)),
                  pl.BlockSpec((d, tn), lambda j: (0, j)),
                  pl.BlockSpec((1, tn), lambda j: (0, j))],
        out_specs=pl.BlockSpec((rows, tn), lambda j: (0, j)),
        out_shape=jax.ShapeDtypeStruct((rows, cols), F32),
        compiler_params=_params(1), name="mod",
    )(c_all, w_mod, b_mod.reshape(1, cols))


def _mixer_kernel(x_ref, sh_ref, sc_ref, gmix_ref, w_ref, wg_ref, wgt_ref,
                  cos_ref, sin_ref, dec_ref, qd_ref, kd_ref, cd_ref,
                  tril_ref, triu_ref, wconv_ref, bconv_ref, gret_ref, gml_ref, bgc_ref, bgr_ref,
                  s0_ref, c0_ref, n0_ref, m0_ref, conv0_ref,
                  y_ref, s_ref, c_ref, n_ref, m_ref, tail_ref, p_ref, xp_ref):
    @pl.when(pl.program_id(1) == 0)
    def _():
        s_ref[...] = s0_ref[...]
        c_ref[...] = c0_ref[...]
        n_ref[...] = n0_ref[...]
        m_ref[...] = m0_ref[...]
        tail_ref[...] = conv0_ref[...]

    x = x_ref[...]
    gs, L, d = x.shape
    h = (_rms(x, gmix_ref[...]) * (1.0 + sc_ref[...]) + sh_ref[...]).reshape(gs * L, d)
    hb = h.astype(BF16)
    for j in range(MAIN_COLS // 1024):
        p_ref[:, j * 1024:(j + 1) * 1024] = _dot(hb, w_ref[:, j * 1024:(j + 1) * 1024])

    chains = []
    for sq in range(gs):
        h_sq = h[sq * L:(sq + 1) * L, :]
        gc = _dot3(h_sq, wg_ref[...])[:, :N_GATES]
        gr = _dot3(wgt_ref[...], h_sq, dot=_dot_nt)
        chains += _mixer_sequence(p_ref.at[pl.ds(sq * L, L)], gc, gr, cos_ref, sin_ref, dec_ref, qd_ref, kd_ref,
                                  cd_ref, tril_ref, triu_ref, wconv_ref, bconv_ref, gret_ref, gml_ref, bgc_ref,
                                  bgr_ref, y_ref.at[sq], s_ref.at[sq], c_ref.at[sq], n_ref.at[sq], m_ref.at[sq],
                                  tail_ref.at[sq], xp_ref.at[sq])
    width = (gs if L == CHUNK else MIXER_INTERLEAVE_SEQS) * (RET_HEADS + MLSTM_HEADS)
    for first in range(0, len(chains), width):
        group = chains[first:first + width]
        while group:
            alive = []
            for chain in group:
                if next(chain, None) is not None:
                    alive.append(chain)
            group = alive


N_MIXER_INPUTS = 26
N_OUTPROJ_INPUTS = 8


def _mixer_outproj_kernel(*refs):
    mixer_in = refs[:N_MIXER_INPUTS]
    outproj_in = refs[N_MIXER_INPUTS:N_MIXER_INPUTS + N_OUTPROJ_INPUTS]
    x1_ref, h2_ref, idx_ref, rank_ref, cnt_ref, s_ref, c_ref, n_ref, m_ref, tail_ref, p_ref, xp_ref, y_ref = refs[
        N_MIXER_INPUTS + N_OUTPROJ_INPUTS:]
    _mixer_kernel(*mixer_in, y_ref, s_ref, c_ref, n_ref, m_ref, tail_ref, p_ref, xp_ref)
    gs, L, d = y_ref.shape
    _outproj_router(y_ref[...].reshape(gs * L, d), mixer_in[0], *outproj_in, x1_ref, h2_ref, idx_ref, rank_ref, cnt_ref)


def _mixer_sequence(p_ref, gc, gr, cos_ref, sin_ref, dec_ref, qd_ref, kd_ref, cd_ref,
                    tril_ref, triu_ref, wconv_ref, bconv_ref, gret_ref, gml_ref, bgc_ref, bgr_ref,
                    y_ref, s_ref, c_ref, n_ref, m_ref, tail_ref, xp_ref):
    L = p_ref.shape[0]
    cos = cos_ref[...]
    sin = sin_ref[...]
    scale = HEAD_DIM ** -0.5

    def rot(x):
        return x * cos + pltpu.roll(x, HEAD_DIM // 2, axis=1) * sin

    def retention_head(h):
        lo = h * HEAD_DIM
        q = rot(p_ref[:, OFF_RQ + lo:OFF_RQ + lo + HEAD_DIM])
        k = rot(p_ref[:, OFF_RK + lo:OFF_RK + lo + HEAD_DIM]) * scale
        v = p_ref[:, OFF_RV + lo:OFF_RV + lo + HEAD_DIM].astype(BF16)
        s_old = s_ref[h]
        yield True
        scores = _dot_nt(q.astype(BF16), k.astype(BF16)) * dec_ref[h]
        yield True
        out = _dot(scores.astype(BF16), v) + _dot((q * qd_ref[h]).astype(BF16), s_old.astype(BF16))
        s_ref[h] = cd_ref[h] * s_old + _dot_tn((k * kd_ref[h]).astype(BF16), v)
        yield True
        g = p_ref[:, OFF_RG + lo:OFF_RG + lo + HEAD_DIM]
        y_ref[:, lo:lo + HEAD_DIM] = ((g * _sigmoid(g)) * _layer_norm(out, gret_ref[:, lo:lo + HEAD_DIM])).astype(y_ref.dtype)

    xp_ref[0:CONV_TAIL_ROWS, :] = tail_ref[...]
    xp_ref[CONV_TAIL_ROWS:CONV_TAIL_ROWS + L, :] = p_ref[:, OFF_MQK:OFF_MQK + 2 * MLSTM_WIDTH]
    acc = bconv_ref[...] + wconv_ref[0:1, :] * xp_ref[CONV_TAIL_ROWS - 3:CONV_TAIL_ROWS - 3 + L, :]
    for j in range(1, CONV_WIDTH):
        acc = acc + wconv_ref[j:j + 1, :] * xp_ref[CONV_TAIL_ROWS - 3 + j:CONV_TAIL_ROWS - 3 + j + L, :]
    tail_ref[...] = xp_ref[L:L + CONV_TAIL_ROWS, :]
    xp_ref[CONV_TAIL_ROWS:CONV_TAIL_ROWS + L, :] = acc * _sigmoid(acc)

    gcol = gc + bgc_ref[...]
    is_f_col = lax.broadcasted_iota(I32, gcol.shape, 1) >= MLSTM_HEADS
    gcol = jnp.where(is_f_col, _log_sigmoid(gcol), gcol)
    grow = gr + bgr_ref[...]
    is_f_row = lax.broadcasted_iota(I32, grow.shape, 0) >= MLSTM_HEADS
    grow = jnp.where(is_f_row, _log_sigmoid(grow), grow)
    bcol_all = _dot3(tril_ref[...], gcol)
    brow_all = _dot3(grow, triu_ref[...])
    causal = lax.broadcasted_iota(I32, (L, L), 0) >= lax.broadcasted_iota(I32, (L, L), 1)

    def mlstm_head(h):
        lo = h * HEAD_DIM
        q = xp_ref[CONV_TAIL_ROWS:CONV_TAIL_ROWS + L, lo:lo + HEAD_DIM]
        k = xp_ref[CONV_TAIL_ROWS:CONV_TAIL_ROWS + L, MLSTM_WIDTH + lo:MLSTM_WIDTH + lo + HEAD_DIM] * scale
        v = p_ref[:, OFF_MV + lo:OFF_MV + lo + HEAD_DIM].astype(BF16)
        ic_col = gcol[:, h:h + 1]
        ic_row = grow[h:h + 1, :]
        b_col = bcol_all[:, MLSTM_HEADS + h:MLSTM_HEADS + h + 1]
        b_row = brow_all[MLSTM_HEADS + h:MLSTM_HEADS + h + 1, :]
        c_old = c_ref[h]
        n_old = n_ref[h]
        m_old = m_ref[h][:, 0:1]

        d_log = jnp.where(causal, b_col - b_row + ic_row, -jnp.inf)
        inter = b_col + m_old
        m_t = jnp.maximum(inter, jnp.max(d_log, axis=1, keepdims=True))
        yield True
        w_intra = jnp.exp(d_log - m_t)
        w_inter = jnp.exp(inter - m_t)
        qb = q.astype(BF16)
        s = _dot_nt(qb, k.astype(BF16)) * w_intra
        yield True
        num = _dot(s.astype(BF16), v) + w_inter * _dot(qb, c_old.astype(BF16))
        den = jnp.sum(s, axis=1, keepdims=True) + w_inter * jnp.sum(q * n_old, axis=1, keepdims=True)
        yield True
        hh = num / jnp.maximum(jnp.abs(den), jnp.exp(-m_t))

        b_last = b_col[L - 1:L, :]
        w_log_col = b_last - b_col + ic_col
        m_new = jnp.maximum(b_last + m_old, jnp.max(w_log_col, axis=0, keepdims=True))
        wk = jnp.exp(w_log_col - m_new) * k
        cdec = jnp.exp(b_last + m_old - m_new)
        yield True
        c_ref[h] = cdec * c_old + _dot_tn(wk.astype(BF16), v)
        n_ref[h] = cdec * n_old + jnp.sum(wk, axis=0, keepdims=True)
        m_ref[h] = jnp.broadcast_to(m_new, (1, HEAD_DIM))
        yield True
        o = p_ref[:, OFF_MO + lo:OFF_MO + lo + HEAD_DIM]
        y_ref[:, RET_WIDTH + lo:RET_WIDTH + lo + HEAD_DIM] = (
            _sigmoid(o) * _layer_norm(hh, gml_ref[:, lo:lo + HEAD_DIM])).astype(y_ref.dtype)

    return [retention_head(h) for h in range(RET_HEADS)] + [mlstm_head(h) for h in range(MLSTM_HEADS)]


def _mixer_consts(L, pos):
    f32 = np.float32
    half = HEAD_DIM // 2
    inv_freq = np.power(f32(ROPE_BASE), -np.arange(half, dtype=f32) / f32(half)).astype(f32)
    ang = (pos.astype(f32)[:, None] * inv_freq[None, :]).astype(f32)
    cos = np.concatenate([np.cos(ang), np.cos(ang)], axis=-1).astype(f32)
    sin = np.concatenate([-np.sin(ang), np.sin(ang)], axis=-1).astype(f32)
    log_gamma = np.log1p(-np.exp2(-5.0 - np.arange(RET_HEADS, dtype=np.float64)))
    idx = np.arange(L, dtype=np.float64)
    rel = idx[:, None] - idx[None, :]
    dec = np.where(rel >= 0, np.exp(log_gamma[:, None, None] * np.maximum(rel, 0.0)), 0.0)
    qd = np.broadcast_to(np.exp(log_gamma[:, None] * (idx + 1.0))[..., None], (RET_HEADS, L, HEAD_DIM))
    kd = np.broadcast_to(np.exp(log_gamma[:, None] * (L - 1.0 - idx))[..., None], (RET_HEADS, L, HEAD_DIM))
    cd = np.broadcast_to(np.exp(log_gamma * L)[:, None, None], (RET_HEADS, 1, HEAD_DIM))
    tril = rel >= 0
    triu = rel <= 0
    return tuple(jnp.asarray(a, F32) for a in (cos, sin, dec, qd, kd, cd, tril, triu))


def _mixer_call(x3, mod3, g_mix, w_main_bf, w_gate, w_gate_t, consts, w_conv, b_conv, g_ret, g_mlstm, bg_col, bg_row,
                s0, c0, n0, m0, conv0, L, gs, outproj=None):
    cos, sin, dec, qd, kd, cd, tril, triu = consts
    groups, seq, d = x3.shape
    nc = seq // L
    hd = HEAD_DIM
    full = lambda *shape: pl.BlockSpec(shape, lambda g, c: (0,) * len(shape))
    mod_spec = lambda k: pl.BlockSpec((gs, 1, d), lambda g, c: (g, 0, k))
    state4 = pl.BlockSpec((gs, RET_HEADS, hd, hd), lambda g, c: (g, 0, 0, 0))
    vec4 = pl.BlockSpec((gs, MLSTM_HEADS, 1, hd), lambda g, c: (g, 0, 0, 0))
    tail3 = pl.BlockSpec((gs, CONV_TAIL_ROWS, 2 * MLSTM_WIDTH), lambda g, c: (g, 0, 0))
    row = lambda g, c: (g, c, 0)
    in_specs = [pl.BlockSpec((gs, L, d), row), mod_spec(0), mod_spec(1),
                full(1, 1, d), full(*w_main_bf.shape), full(d, 128), full(N_GATES, d),
                pl.BlockSpec((L, hd), lambda g, c: (c, 0)),
                pl.BlockSpec((L, hd), lambda g, c: (c, 0)),
                full(RET_HEADS, L, L), full(RET_HEADS, L, hd), full(RET_HEADS, L, hd), full(RET_HEADS, 1, hd),
                full(L, L), full(L, L),
                full(CONV_WIDTH, 2 * MLSTM_WIDTH), full(1, 2 * MLSTM_WIDTH),
                full(1, RET_WIDTH), full(1, MLSTM_WIDTH), full(1, N_GATES), full(N_GATES, 1),
                state4, state4, vec4, vec4, tail3]
    args = [x3, mod3, mod3, g_mix.reshape(1, 1, d), w_main_bf, w_gate, w_gate_t,
            cos, sin, dec, qd, kd, cd, tril, triu, w_conv, b_conv.reshape(1, -1),
            g_ret.reshape(1, -1), g_mlstm.reshape(1, -1), bg_col, bg_row, s0, c0, n0, m0, conv0]
    assert len(in_specs) == len(args) == N_MIXER_INPUTS
    state_specs = [state4, state4, vec4, vec4, tail3]
    state_shapes = [jax.ShapeDtypeStruct((groups, RET_HEADS, hd, hd), F32),
                    jax.ShapeDtypeStruct((groups, MLSTM_HEADS, hd, hd), F32),
                    jax.ShapeDtypeStruct((groups, MLSTM_HEADS, 1, hd), F32),
                    jax.ShapeDtypeStruct((groups, MLSTM_HEADS, 1, hd), F32),
                    jax.ShapeDtypeStruct((groups, CONV_TAIL_ROWS, 2 * MLSTM_WIDTH), F32)]
    scratch = [pltpu.VMEM((gs * L, MAIN_COLS), F32), pltpu.VMEM((gs, CONV_TAIL_ROWS + L, 2 * MLSTM_WIDTH), F32)]
    y_block, y_shape = (gs, L, RET_WIDTH + MLSTM_WIDTH), (groups, seq, RET_WIDTH + MLSTM_WIDTH)
    if outproj is None:
        kernel_fn = _mixer_kernel
        out_specs = [pl.BlockSpec(y_block, row)] + state_specs
        out_shape = [jax.ShapeDtypeStruct(y_shape, F32)] + state_shapes
    else:
        g_ffn, w_out_bf, w_router_t, b_router, ustrict = outproj
        tm = gs * L
        n = groups * seq
        tile = lambda g, c: g * nc + c
        kernel_fn = _mixer_outproj_kernel
        in_specs += [mod_spec(2), mod_spec(3), mod_spec(4),
                     full(1, 1, d), full(d, d), full(N_EXPERTS, d), full(N_EXPERTS, 1), full(tm, tm)]
        args += [mod3, mod3, mod3, g_ffn.reshape(1, 1, d), w_out_bf, w_router_t, b_router.reshape(N_EXPERTS, 1),
                 ustrict]
        assert len(args) == N_MIXER_INPUTS + N_OUTPROJ_INPUTS
        out_specs = [pl.BlockSpec((gs, L, d), row),
                     pl.BlockSpec((tm, d + SIDE_LANES), lambda g, c: (tile(g, c), 0)),
                     pl.BlockSpec((TOP_K, tm), lambda g, c: (0, tile(g, c))),
                     pl.BlockSpec((TOP_K, tm), lambda g, c: (0, tile(g, c))),
                     pl.BlockSpec((1, N_EXPERTS, 128), lambda g, c: (tile(g, c), 0, 0))] + state_specs
        out_shape = [jax.ShapeDtypeStruct((groups, seq, d), F32),
                     jax.ShapeDtypeStruct((n, d + SIDE_LANES), BF16),
                     jax.ShapeDtypeStruct((TOP_K, n), I32),
                     jax.ShapeDtypeStruct((TOP_K, n), I32),
                     jax.ShapeDtypeStruct((n // tm, N_EXPERTS, 128), F32)] + state_shapes
        scratch = scratch + [pltpu.VMEM(y_block, F32)]
    return pl.pallas_call(
        kernel_fn,
        grid=(groups // gs, nc),
        in_specs=in_specs, out_specs=out_specs, out_shape=out_shape, scratch_shapes=scratch,
        compiler_params=_params(2), name="mixer",
    )(*args)


def _outproj_kernel(y_ref, x_ref, gt_ref, sh_ref, sc_ref, g_ref, w_ref, wrt_ref, br_ref, ustrict_ref,
                    x1_ref, h2_ref, idx_ref, rank_ref, cnt_ref):
    _outproj_router(y_ref[...], x_ref, gt_ref, sh_ref, sc_ref, g_ref, w_ref, wrt_ref, br_ref, ustrict_ref,
                    x1_ref, h2_ref, idx_ref, rank_ref, cnt_ref)


def _outproj_router(y, x_ref, gt_ref, sh_ref, sc_ref, g_ref, w_ref, wrt_ref, br_ref, ustrict_ref,
                    x1_ref, h2_ref, idx_ref, rank_ref, cnt_ref):
    x = x_ref[...]
    gb, tt, d = x.shape
    tm = gb * tt
    mixed = _dot(y.astype(BF16), w_ref[...])
    x1 = x + gt_ref[...] * mixed.reshape(gb, tt, d)
    x1_ref[...] = x1
    h2 = (_rms(x1, g_ref[...]) * (1.0 + sc_ref[...]) + sh_ref[...]).reshape(tm, d)
    h2_ref[:, :d] = h2.astype(BF16)

    work = _dot3(wrt_ref[...], h2, dot=_dot_nt) + br_ref[...]
    e_iota = lax.broadcasted_iota(I32, work.shape, 0).astype(F32)
    vals, idxs, sels = [], [], []
    for _ in range(TOP_K):
        mx = jnp.max(work, axis=0, keepdims=True)
        ik = jnp.min(jnp.where(work == mx, e_iota, float(N_EXPERTS)), axis=0, keepdims=True)
        sel = e_iota == ik
        vals.append(mx)
        idxs.append(ik)
        sels.append(sel)
        work = jnp.where(sel, -jnp.inf, work)
    exps = [jnp.exp(v - vals[0]) for v in vals]
    denom = exps[0] + exps[1] + exps[2] + exps[3]
    gates = [e / denom for e in exps]
    idx_ref[...] = jnp.concatenate(idxs, axis=0).astype(I32)

    pieces = []
    for gk in gates:
        p1 = gk.astype(BF16)
        r1 = gk - p1.astype(F32)
        p2 = r1.astype(BF16)
        pieces += [p1, p2, (r1 - p2.astype(F32)).astype(BF16)]
    side = jnp.concatenate(pieces + [ik.astype(BF16) for ik in idxs], axis=0)
    eye = (lax.broadcasted_iota(I32, (SIDE_ROWS, SIDE_LANES), 0)
           == lax.broadcasted_iota(I32, (SIDE_ROWS, SIDE_LANES), 1)).astype(BF16)
    h2_ref[:, d:] = _dot_tn(side, eye).astype(BF16)

    mask = (sels[0] | sels[1] | sels[2] | sels[3]).astype(F32)
    before = _dot(mask.astype(BF16), ustrict_ref[...])
    ranks = [jnp.sum(jnp.where(sel, before, 0.0), axis=0, keepdims=True) for sel in sels]
    rank_ref[...] = jnp.concatenate(ranks, axis=0).astype(I32)
    cnt_ref[0] = jnp.broadcast_to(jnp.sum(mask, axis=1, keepdims=True), cnt_ref.shape[1:])


def _outproj_call(y, x3, mod3, g_ffn, w_out_bf, w_router_t, b_router, ustrict, gb, tt):
    g, t, d = x3.shape
    n = g * t
    tm = gb * tt
    tpg = t // tt
    row = lambda i, j: (i * tpg + j, 0)
    col = lambda i, j: (0, i * tpg + j)
    mod_spec = lambda k: pl.BlockSpec((gb, 1, d), lambda i, j: (i, 0, k))
    const = lambda *shape: pl.BlockSpec(shape, lambda i, j: (0,) * len(shape))
    return pl.pallas_call(
        _outproj_kernel,
        grid=(g // gb, tpg),
        in_specs=[pl.BlockSpec((tm, d), row),
                  pl.BlockSpec((gb, tt, d), lambda i, j: (i, j, 0)),
                  mod_spec(2), mod_spec(3), mod_spec(4),
                  const(1, 1, d), const(d, d), const(N_EXPERTS, d), const(N_EXPERTS, 1),
                  const(tm, tm)],
        out_specs=[pl.BlockSpec((gb, tt, d), lambda i, j: (i, j, 0)),
                   pl.BlockSpec((tm, d + SIDE_LANES), row),
                   pl.BlockSpec((TOP_K, tm), col),
                   pl.BlockSpec((TOP_K, tm), col),
                   pl.BlockSpec((1, N_EXPERTS, 128), lambda i, j: (i * tpg + j, 0, 0))],
        out_shape=[jax.ShapeDtypeStruct((g, t, d), F32),
                   jax.ShapeDtypeStruct((n, d + SIDE_LANES), BF16),
                   jax.ShapeDtypeStruct((TOP_K, n), I32),
                   jax.ShapeDtypeStruct((TOP_K, n), I32),
                   jax.ShapeDtypeStruct((n // tm, N_EXPERTS, 128), F32)],
        compiler_params=_params(2), name="outproj_router",
    )(y, x3, mod3, mod3, mod3, g_ffn.reshape(1, 1, d), w_out_bf, w_router_t, b_router.reshape(N_EXPERTS, 1),
      ustrict)


def _chunk_copy(hbm_ref, hbm_row, buf_ref, chunk, sem, to_hbm):
    hbm = hbm_ref.at[pl.ds(pl.multiple_of(hbm_row, RUN_ALIGN), RUN_ALIGN)]
    buf = buf_ref.at[pl.ds(pl.multiple_of(chunk * RUN_ALIGN, RUN_ALIGN), RUN_ALIGN)]
    return pltpu.make_async_copy(buf, hbm, sem) if to_hbm else pltpu.make_async_copy(hbm, buf, sem)


def _tile_chunks(hbm_ref, bufs_ref, starts_ref, nq_ref, sems, tile, to_hbm, wait):
    slot = tile % 2
    first = tile * (bufs_ref.shape[1] // RUN_ALIGN)

    def body(q, c):
        cp = _chunk_copy(hbm_ref, starts_ref[first + q], bufs_ref.at[slot], q, sems.at[slot], to_hbm)
        cp.wait() if wait else cp.start()
        return c

    lax.fori_loop(0, nq_ref[tile], body, 0)


def _dispatch_kernel(starts_ref, nq_ref, meta_ref, rbuf_ref, h2p_ref, h2s_ref, xs_ref, bufs_ref, zero_ref, sems,
                     *, prompt_tiles):
    i = pl.program_id(0)
    tm = h2p_ref.shape[0]
    bm = zero_ref.shape[0]
    n_blocks = xs_ref.shape[0] // bm
    sem = sems.at[0]
    buf_ref = bufs_ref.at[i % 2]

    @pl.when(i == 0)
    def _():
        zero_ref[...] = jnp.zeros(zero_ref.shape, zero_ref.dtype)

        def zero_copy(row):
            return pltpu.make_async_copy(zero_ref, xs_ref.at[pl.ds(pl.multiple_of(row, bm), bm)], sem)

        def tails(fn):
            def body(e, c):
                @pl.when(meta_ref[e] >= 0)
                def _():
                    fn(zero_copy(meta_ref[e]))
                return c
            lax.fori_loop(0, N_EXPERTS, body, 0)

        def unused(fn):
            def body(b, c):
                fn(zero_copy(b * bm))
                return c
            lax.fori_loop(meta_ref[N_EXPERTS], n_blocks, body, 0)

        tails(lambda cp: cp.start())
        unused(lambda cp: cp.start())
        tails(lambda cp: cp.wait())
        unused(lambda cp: cp.wait())

    def build(h2_ref):
        h2 = h2_ref[...]
        rb = rbuf_ref[...].astype(jnp.int16)
        for c in range(bufs_ref.shape[1] // PERM_ROWS):
            r = (lax.broadcasted_iota(I32, (PERM_ROWS, tm), 0) + c * PERM_ROWS).astype(jnp.int16)
            hit = (r == rb[0:1, :]) | (r == rb[1:2, :]) | (r == rb[2:3, :]) | (r == rb[3:4, :])
            onehot = jnp.where(hit, jnp.ones((), BF16), jnp.zeros((), BF16))
            buf_ref[c * PERM_ROWS:(c + 1) * PERM_ROWS, :] = _dot(onehot, h2).astype(BF16)

    @pl.when(i < prompt_tiles)
    def _():
        build(h2p_ref)

    @pl.when(i >= prompt_tiles)
    def _():
        build(h2s_ref)

    chunks = functools.partial(_tile_chunks, xs_ref, bufs_ref, starts_ref, nq_ref, sems, to_hbm=True)
    chunks(i, wait=False)

    @pl.when(i > 0)
    def _():
        chunks(i - 1, wait=True)

    @pl.when(i == pl.num_programs(0) - 1)
    def _():
        chunks(i, wait=True)


def _dispatch_call(starts, nq, meta, rbuf, h2_p, h2_s, cap):
    d = h2_p.shape[1]
    tm = MOE_TILE
    pt, st = h2_p.shape[0] // tm, h2_s.shape[0] // tm
    return pl.pallas_call(
        functools.partial(_dispatch_kernel, prompt_tiles=pt),
        grid_spec=pltpu.PrefetchScalarGridSpec(
            num_scalar_prefetch=3,
            grid=(pt + st,),
            in_specs=[pl.BlockSpec((TOP_K, tm), lambda i, *_: (0, i)),
                      pl.BlockSpec((tm, d), lambda i, *_: (jnp.minimum(i, pt - 1), 0)),
                      pl.BlockSpec((tm, d), lambda i, *_: (jnp.maximum(i - pt, 0), 0))],
            out_specs=pl.BlockSpec(memory_space=pl.ANY),
            scratch_shapes=[pltpu.VMEM((2, TILE_BUF_ROWS, d), BF16), pltpu.VMEM((EXPERT_BLOCK, d), BF16),
                            pltpu.SemaphoreType.DMA((2,))]),
        out_shape=jax.ShapeDtypeStruct((cap, d), BF16),
        compiler_params=_params(1), name="dispatch",
    )(starts, nq, meta, rbuf, h2_p, h2_s)


def _expert_kernel(be_ref, nu_ref, xs_ref, wup_ref, bup_ref, wdn_ref, bdn_ref, ys_ref, wup_bf, wdn_bf):
    i = pl.program_id(0)
    prev = be_ref[jnp.maximum(i - 1, 0)]

    @pl.when((i == 0) | (be_ref[i] != prev))
    def _():
        def cast(r, c):
            rows = pl.ds(pl.multiple_of(r * 64, 64), 64)
            wup_bf[rows, :] = wup_ref[0, rows, :].astype(BF16)
            wdn_bf[rows, :] = wdn_ref[0, rows, :].astype(BF16)
            return c

        lax.fori_loop(0, D_MODEL // 64, cast, 0)

    @pl.when(i < nu_ref[0])
    def _():
        hu = _dot(xs_ref[:, :D_MODEL], wup_bf[...]) + bup_ref[0]
        gate = jnp.minimum(hu[:, :D_FF], SWIGLU_LIMIT)
        lin = jnp.clip(hu[:, D_FF:], -SWIGLU_LIMIT, SWIGLU_LIMIT)
        glu = gate * _sigmoid(SWIGLU_ALPHA * gate)
        y = _dot(((lin + 1.0) * glu).astype(BF16), wdn_bf[...]) + bdn_ref[0]
        side = xs_ref[:, D_MODEL:].astype(F32)
        e = be_ref[i].astype(F32)
        weight = jnp.zeros((side.shape[0], 1), F32)
        for k in range(TOP_K):
            g_k = side[:, 3 * k:3 * k + 1] + side[:, 3 * k + 1:3 * k + 2] + side[:, 3 * k + 2:3 * k + 3]
            weight = weight + jnp.where(side[:, SIDE_ROWS - TOP_K + k:SIDE_ROWS - TOP_K + k + 1] == e, g_k, 0.0)
        ys_ref[...] = (weight * y).astype(ys_ref.dtype)

    @pl.when(i >= nu_ref[0])
    def _():
        ys_ref[...] = jnp.zeros(ys_ref.shape, ys_ref.dtype)


def _expert_call(block_e, n_used, xs, w_up, b_up, w_down, b_down):
    cap, dw = xs.shape
    d = w_down.shape[2]
    bm = EXPERT_BLOCK
    blk = lambda i, be, nu: (jnp.minimum(i, nu[0] - 1), 0)
    per_e = lambda i, be, nu: (be[i], 0, 0)
    return pl.pallas_call(
        _expert_kernel,
        grid_spec=pltpu.PrefetchScalarGridSpec(
            num_scalar_prefetch=2,
            grid=(cap // bm,),
            in_specs=[pl.BlockSpec((bm, dw), blk),
                      pl.BlockSpec((1, d, 2 * D_FF), per_e),
                      pl.BlockSpec((1, 1, 2 * D_FF), per_e),
                      pl.BlockSpec((1, D_FF, d), per_e),
                      pl.BlockSpec((1, 1, d), per_e)],
            out_specs=pl.BlockSpec((bm, d), lambda i, be, nu: (i, 0)),
            scratch_shapes=[pltpu.VMEM((d, 2 * D_FF), BF16), pltpu.VMEM((D_FF, d), BF16)]),
        out_shape=jax.ShapeDtypeStruct((cap, d), BF16),
        compiler_params=_params(1), name="experts",
    )(block_e, n_used, xs, w_up, b_up.reshape(N_EXPERTS, 1, -1), w_down, b_down.reshape(N_EXPERTS, 1, -1))


def _combine_kernel(starts_ref, nq_ref, x1_ref, gt_ref, rcol_ref, gfin_ref, ys_ref, o_ref, bufs_ref, sems,
                    *, tile0, tiles_per_group):
    x1 = x1_ref[...]
    gb, tt, d = x1.shape
    tm = gb * tt
    step = pl.program_id(0) * tiles_per_group + pl.program_id(1)
    n_steps = pl.num_programs(0) * tiles_per_group
    tile = tile0 + step
    chunks = functools.partial(_tile_chunks, ys_ref, bufs_ref, starts_ref, nq_ref, sems, to_hbm=False)

    @pl.when(step == 0)
    def _():
        bufs_ref[...] = jnp.zeros(bufs_ref.shape, bufs_ref.dtype)
        chunks(tile, wait=False)

    @pl.when(step + 1 < n_steps)
    def _():
        chunks(tile + 1, wait=False)

    chunks(tile, wait=True)
    buf_ref = bufs_ref.at[tile % 2]

    rows = rcol_ref[...].astype(jnp.int16)
    moe = jnp.zeros((tm, d), F32)
    for c in range(bufs_ref.shape[1] // PERM_COLS):
        r = (lax.broadcasted_iota(I32, (tm, PERM_COLS), 1) + c * PERM_COLS).astype(jnp.int16)
        hit = (r == rows[:, 0:1]) | (r == rows[:, 1:2]) | (r == rows[:, 2:3]) | (r == rows[:, 3:4])
        onehot = jnp.where(hit, jnp.ones((), BF16), jnp.zeros((), BF16))
        moe = moe + _dot(onehot, buf_ref[c * PERM_COLS:(c + 1) * PERM_COLS, :])
    xo = x1 + gt_ref[...] * moe.reshape(gb, tt, d)
    o_ref[...] = _rms(xo, gfin_ref[...])


def _combine_call(starts, nq, x1, mod3, rcol, g_final, ys, tok0, gb, tt):
    g, t, d = x1.shape
    tm = gb * tt
    tpg = t // tt
    t0 = tok0 // tm
    tok = lambda i, j, *_: (t0 + i * tpg + j, 0)
    return pl.pallas_call(
        functools.partial(_combine_kernel, tile0=t0, tiles_per_group=tpg),
        grid_spec=pltpu.PrefetchScalarGridSpec(
            num_scalar_prefetch=2,
            grid=(g // gb, tpg),
            in_specs=[pl.BlockSpec((gb, tt, d), lambda i, j, *_: (i, j, 0)),
                      pl.BlockSpec((gb, 1, d), lambda i, j, *_: (i, 0, 5)),
                      pl.BlockSpec((tm, TOP_K), tok),
                      pl.BlockSpec((1, 1, d), lambda i, j, *_: (0, 0, 0)),
                      pl.BlockSpec(memory_space=pl.ANY)],
            out_specs=pl.BlockSpec((gb, tt, d), lambda i, j, *_: (i, j, 0)),
            scratch_shapes=[pltpu.VMEM((2, TILE_BUF_ROWS, d), BF16), pltpu.SemaphoreType.DMA((2,))]),
        out_shape=jax.ShapeDtypeStruct((g, t, d), F32),
        compiler_params=_params(2), name="combine",
    )(starts, nq, x1, mod3, rcol, g_final.reshape(1, 1, d), ys)


def _group_blocking(groups, seq, tile):
    if seq >= tile:
        return 1, tile
    return tile // seq, seq


def kernel(x_prompt, x_sample, c_prompt, c_sample, state_ret, state_mlstm_c, state_mlstm_n, state_mlstm_m, state_conv, w_mod, b_mod, g_mix, g_ffn, w_in, b_igate, b_fgate, w_conv, b_conv, g_ret, g_mlstm, w_out, w_router, b_router, w_up, b_up, w_down, b_down, g_final):
    depth = w_mod.shape[0]
    assert depth == 1, "single-layer trunk"
    bp, tp, d = x_prompt.shape
    bs, ts, _ = x_sample.shape
    n_p, n_s = bp * tp, bs * ts
    hd = HEAD_DIM
    l = 0

    mod = _mod_call(jnp.concatenate([c_prompt, c_sample], axis=0), w_mod[l], b_mod[l])
    mod_p = mod[:bp].reshape(bp, 1, N_MOD * d)
    mod_s = mod[bp:].reshape(bs, 1, N_MOD * d)

    w_main_bf = w_in[l].astype(BF16)
    w_gate = jnp.pad(w_in[l][:, MAIN_COLS:], ((0, 0), (0, 128 - N_GATES)))
    w_gate_t = w_in[l][:, MAIN_COLS:].T
    w_out_bf = w_out[l].astype(BF16)
    bg_col = jnp.concatenate([b_igate[l], b_fgate[l]]).reshape(1, N_GATES)
    bg_row = bg_col.reshape(N_GATES, 1)

    groups = (
        (x_prompt, mod_p, min(CHUNK, tp), np.arange(tp),
         jnp.zeros((bp, RET_HEADS, hd, hd), F32), jnp.zeros((bp, MLSTM_HEADS, hd, hd), F32),
         jnp.zeros((bp, MLSTM_HEADS, hd), F32), jnp.zeros((bp, MLSTM_HEADS), F32),
         jnp.zeros((bp, CONV_WIDTH - 1, 2 * MLSTM_WIDTH), F32)),
        (x_sample, mod_s, min(CHUNK, ts), PAST_LEN + np.arange(ts),
         state_ret[l], state_mlstm_c[l], state_mlstm_n[l], state_mlstm_m[l], state_conv[l]),
    )

    ustrict = jnp.asarray(np.arange(MOE_TILE)[:, None] < np.arange(MOE_TILE)[None, :], BF16)
    staged = []
    for x3, mod3, L, pos, s0, c0, n0, m0, conv0 in groups:
        g, t, _ = x3.shape
        conv0p = jnp.pad(conv0.astype(F32), ((0, 0), (CONV_TAIL_ROWS - (CONV_WIDTH - 1), 0), (0, 0)))
        outproj = (g_ffn[l], w_out_bf, w_router[l].T, b_router[l], ustrict)
        chunked = t > L
        gs = MIXER_SEQS_CHUNKED if chunked else MIXER_SEQS_SHORT
        fused = chunked and gs * L == MOE_TILE
        res = _mixer_call(
            x3, mod3, g_mix[l], w_main_bf, w_gate, w_gate_t, _mixer_consts(L, pos),
            w_conv[l], b_conv[l], g_ret[l], g_mlstm[l],
            bg_col, bg_row, s0.astype(F32), c0.astype(F32), n0.astype(F32).reshape(g, MLSTM_HEADS, 1, hd),
            jnp.broadcast_to(m0.astype(F32)[:, :, None, None], (g, MLSTM_HEADS, 1, hd)), conv0p,
            L, gs, outproj if fused else None)
        s_new, c_new, n_new, m_new, tail = res[-5:]
        states = (s_new, c_new, n_new.reshape(g, MLSTM_HEADS, hd), m_new[:, :, 0, 0],
                  tail[:, CONV_TAIL_ROWS - (CONV_WIDTH - 1):, :])
        if fused:
            x1, h2, idx, rank, cnt = res[:5]
            blocking = (gs, L)
        else:
            blocking = _group_blocking(g, t, MOE_TILE)
            x1, h2, idx, rank, cnt = _outproj_call(res[0].reshape(g * t, d), x3, mod3, *outproj, *blocking)
        staged.append((x1, mod3, h2, idx, rank, states, cnt, blocking))

    n_tok = n_p + n_s
    tm, bm, ra = MOE_TILE, EXPERT_BLOCK, RUN_ALIGN
    n_tiles = n_tok // tm
    q_max = TILE_BUF_ROWS // ra
    n_blocks = -(-(n_tok * TOP_K + n_tiles * N_EXPERTS * (ra - 1)) // bm) + N_EXPERTS
    cap = n_blocks * bm
    counts = jnp.concatenate([s[6][:, :, 0] for s in staged], axis=0).astype(I32)
    run = (counts + ra - 1) // ra * ra
    region = jnp.sum(run, axis=0)
    padded = (region + bm - 1) // bm * bm
    pad_end = jnp.cumsum(padded)
    pad_start = pad_end - padded
    run_start = pad_start[None, :] + jnp.cumsum(run, axis=0) - run
    buf_end = jnp.cumsum(run, axis=1)
    buf_start = buf_end - run
    nq = (buf_end[:, -1] // ra).astype(I32)
    chunk_row = jnp.arange(q_max, dtype=I32) * ra
    chunk_e = jnp.minimum(jnp.sum((buf_end[:, None, :] <= chunk_row[None, :, None]).astype(I32), axis=2),
                          N_EXPERTS - 1)
    e_ids = jnp.arange(N_EXPERTS, dtype=I32)
    shift = run_start - buf_start
    starts = jnp.sum(jnp.where(chunk_e[:, :, None] == e_ids, shift[:, None, :], 0), axis=2) + chunk_row[None, :]
    starts = jnp.where(chunk_row[None, :] < buf_end[:, -1:], starts, 0).reshape(-1).astype(I32)

    idx_all = jnp.concatenate([s[3] for s in staged], axis=1)
    rank_all = jnp.concatenate([s[4] for s in staged], axis=1)
    buf_start_tok = jnp.repeat(buf_start, tm, axis=0).T
    rbuf = jnp.sum(jnp.where(idx_all[None] == e_ids[:, None, None], buf_start_tok[:, None, :], 0), axis=0) + rank_all
    rcol = rbuf.T
    block_row = jnp.arange(n_blocks, dtype=I32) * bm
    block_e = jnp.minimum(jnp.sum((pad_end[None, :] <= block_row[:, None]).astype(I32), axis=1), N_EXPERTS - 1)
    n_used = (pad_end[-1:] // bm).astype(I32)
    meta = jnp.concatenate([jnp.where(region > 0, pad_end - bm, -1), n_used]).astype(I32)

    xs = _dispatch_call(starts, nq, meta, rbuf, staged[0][2], staged[1][2], cap)
    ys = _expert_call(block_e, n_used, xs, w_up[l], b_up[l], w_down[l], b_down[l])

    outs = []
    tok0 = 0
    for x1, mod3, *_, blocking in staged:
        outs.append(_combine_call(starts, nq, x1, mod3, rcol, g_final, ys, tok0, *blocking))
        tok0 += x1.shape[0] * x1.shape[1]

    st_p, st_s = staged[0][5], staged[1][5]
    return (outs[0], outs[1]) + tuple(a[None] for a in st_p) + tuple(a[None] for a in st_s)
```

```python
import functools

import numpy as np
import jax
import jax.numpy as jnp
from jax import lax
from jax.experimental import pallas as pl
from jax.experimental.pallas import tpu as pltpu

F32 = jnp.float32
BF16 = jnp.bfloat16
I32 = jnp.int32

D_MODEL = 1024
PAST_LEN = 16384
RET_HEADS = 4
MLSTM_HEADS = 4
HEAD_DIM = 128
RET_WIDTH = RET_HEADS * HEAD_DIM
MLSTM_WIDTH = MLSTM_HEADS * HEAD_DIM
CONV_WIDTH = 4
CHUNK = 128
ROPE_BASE = 10000.0
N_EXPERTS = 32
TOP_K = 4
D_FF = D_MODEL
SWIGLU_LIMIT = 7.0
SWIGLU_ALPHA = 1.702
N_MOD = 6
EPS = 1e-6
MAIN_COLS = 4 * RET_WIDTH + 2 * MLSTM_WIDTH + 2 * MLSTM_WIDTH
N_GATES = 2 * MLSTM_HEADS
OFF_RQ, OFF_RK, OFF_RV, OFF_RG = 0, RET_WIDTH, 2 * RET_WIDTH, 3 * RET_WIDTH
OFF_MQK = 4 * RET_WIDTH
OFF_MV = OFF_MQK + 2 * MLSTM_WIDTH
OFF_MO = OFF_MV + MLSTM_WIDTH

VMEM_LIMIT_BYTES = 56 * 1024 * 1024
MOE_TILE = 512
EXPERT_BLOCK = 512
MIXER_SEQS_CHUNKED = 4
MIXER_SEQS_SHORT = 16
MIXER_INTERLEAVE_SEQS = 2
RUN_ALIGN = 16
SIDE_ROWS = 4 * TOP_K
SIDE_LANES = 128
TILE_BUF_ROWS = MOE_TILE * TOP_K + N_EXPERTS * RUN_ALIGN
PERM_ROWS = 1280
PERM_COLS = 512
CONV_TAIL_ROWS = 8


def _params(n_axes=1):
    return pltpu.CompilerParams(dimension_semantics=("arbitrary",) * n_axes, vmem_limit_bytes=VMEM_LIMIT_BYTES)


def _dot(a, b):
    return jnp.dot(a, b, preferred_element_type=F32)


def _dot_nt(a, b):
    return lax.dot_general(a, b, (((1,), (1,)), ((), ())), preferred_element_type=F32)


def _dot_tn(a, b):
    return lax.dot_general(a, b, (((0,), (0,)), ((), ())), preferred_element_type=F32)


def _split(a):
    hi = a.astype(BF16)
    lo = (a - hi.astype(F32)).astype(BF16)
    return hi, lo


def _dot3(a, b, dot=_dot):
    ah, al = _split(a)
    bh, bl = _split(b)
    return dot(ah, bh) + (dot(al, bh) + dot(ah, bl))


def _sigmoid(x):
    return 0.5 * (jnp.tanh(0.5 * x) + 1.0)


def _log_sigmoid(x):
    return jnp.minimum(x, 0.0) - jnp.log1p(jnp.exp(-jnp.abs(x)))


def _rms(x, g):
    ms = jnp.mean(x * x, axis=-1, keepdims=True)
    return (x * lax.rsqrt(ms + EPS)) * g


def _layer_norm(x, g):
    mu = jnp.mean(x, axis=-1, keepdims=True)
    xc = x - mu
    var = jnp.mean(xc * xc, axis=-1, keepdims=True)
    return xc * lax.rsqrt(var + EPS) * g


def _mod_kernel(c_ref, w_ref, b_ref, o_ref):
    c = c_ref[...]
    o_ref[...] = _dot3(c * _sigmoid(c), w_ref[...]) + b_ref[...]


def _mod_call(c_all, w_mod, b_mod):
    rows, d = c_all.shape
    cols = w_mod.shape[1]
    tn = 1024
    return pl.pallas_call(
        _mod_kernel,
        grid=(cols // tn,),
        in_specs=[pl.BlockSpec((rows, d), lambda j: (0, 0)),
                  pl.BlockSpec((d, tn), lambda j: (0, j)),
                  pl.BlockSpec((1, tn), lambda j: (0, j))],
        out_specs=pl.BlockSpec((rows, tn), lambda j: (0, j)),
        out_shape=jax.ShapeDtypeStruct((rows, cols), F32),
        compiler_params=_params(1), name="mod",
    )(c_all, w_mod, b_mod.reshape(1, cols))


def _mixer_kernel(x_ref, sh_ref, sc_ref, gmix_ref, w_ref, wg_ref, wgt_ref,
                  cos_ref, sin_ref, dec_ref, qd_ref, kd_ref, cd_ref,
                  tril_ref, triu_ref, wconv_ref, bconv_ref, gret_ref, gml_ref, bgc_ref, bgr_ref,
                  s0_ref, c0_ref, n0_ref, m0_ref, conv0_ref,
                  y_ref, s_ref, c_ref, n_ref, m_ref, tail_ref, p_ref, xp_ref):
    @pl.when(pl.program_id(1) == 0)
    def _():
        s_ref[...] = s0_ref[...]
        c_ref[...] = c0_ref[...]
        n_ref[...] = n0_ref[...]
        m_ref[...] = m0_ref[...]
        tail_ref[...] = conv0_ref[...]

    x = x_ref[...]
    gs, L, d = x.shape
    h = (_rms(x, gmix_ref[...]) * (1.0 + sc_ref[...]) + sh_ref[...]).reshape(gs * L, d)
    hb = h.astype(BF16)
    for j in range(MAIN_COLS // 1024):
        p_ref[:, j * 1024:(j + 1) * 1024] = _dot(hb, w_ref[:, j * 1024:(j + 1) * 1024])

    chains = []
    for sq in range(gs):
        h_sq = h[sq * L:(sq + 1) * L, :]
        gc = _dot3(h_sq, wg_ref[...])[:, :N_GATES]
        gr = _dot3(wgt_ref[...], h_sq, dot=_dot_nt)
        chains += _mixer_sequence(p_ref.at[pl.ds(sq * L, L)], gc, gr, cos_ref, sin_ref, dec_ref, qd_ref, kd_ref,
                                  cd_ref, tril_ref, triu_ref, wconv_ref, bconv_ref, gret_ref, gml_ref, bgc_ref,
                                  bgr_ref, y_ref.at[sq], s_ref.at[sq], c_ref.at[sq], n_ref.at[sq], m_ref.at[sq],
                                  tail_ref.at[sq], xp_ref.at[sq])
    width = (gs if L == CHUNK else MIXER_INTERLEAVE_SEQS) * (RET_HEADS + MLSTM_HEADS)
    for first in range(0, len(chains), width):
        group = chains[first:first + width]
        while group:
            alive = []
            for chain in group:
                if next(chain, None) is not None:
                    alive.append(chain)
            group = alive


N_MIXER_INPUTS = 26
N_OUTPROJ_INPUTS = 8


def _mixer_outproj_kernel(*refs):
    mixer_in = refs[:N_MIXER_INPUTS]
    outproj_in = refs[N_MIXER_INPUTS:N_MIXER_INPUTS + N_OUTPROJ_INPUTS]
    x1_ref, h2_ref, idx_ref, rank_ref, cnt_ref, s_ref, c_ref, n_ref, m_ref, tail_ref, p_ref, xp_ref, y_ref = refs[
        N_MIXER_INPUTS + N_OUTPROJ_INPUTS:]
    _mixer_kernel(*mixer_in, y_ref, s_ref, c_ref, n_ref, m_ref, tail_ref, p_ref, xp_ref)
    gs, L, d = y_ref.shape
    _outproj_router(y_ref[...].reshape(gs * L, d), mixer_in[0], *outproj_in, x1_ref, h2_ref, idx_ref, rank_ref, cnt_ref)


def _mixer_sequence(p_ref, gc, gr, cos_ref, sin_ref, dec_ref, qd_ref, kd_ref, cd_ref,
                    tril_ref, triu_ref, wconv_ref, bconv_ref, gret_ref, gml_ref, bgc_ref, bgr_ref,
                    y_ref, s_ref, c_ref, n_ref, m_ref, tail_ref, xp_ref):
    L = p_ref.shape[0]
    cos = cos_ref[...]
    sin = sin_ref[...]
    scale = HEAD_DIM ** -0.5

    def rot(x):
        return x * cos + pltpu.roll(x, HEAD_DIM // 2, axis=1) * sin

    def retention_head(h):
        lo = h * HEAD_DIM
        q = rot(p_ref[:, OFF_RQ + lo:OFF_RQ + lo + HEAD_DIM])
        k = rot(p_ref[:, OFF_RK + lo:OFF_RK + lo + HEAD_DIM]) * scale
        v = p_ref[:, OFF_RV + lo:OFF_RV + lo + HEAD_DIM].astype(BF16)
        s_old = s_ref[h]
        yield True
        scores = _dot_nt(q.astype(BF16), k.astype(BF16)) * dec_ref[h]
        yield True
        out = _dot(scores.astype(BF16), v) + _dot((q * qd_ref[h]).astype(BF16), s_old.astype(BF16))
        s_ref[h] = cd_ref[h] * s_old + _dot_tn((k * kd_ref[h]).astype(BF16), v)
        yield True
        g = p_ref[:, OFF_RG + lo:OFF_RG + lo + HEAD_DIM]
        y_ref[:, lo:lo + HEAD_DIM] = ((g * _sigmoid(g)) * _layer_norm(out, gret_ref[:, lo:lo + HEAD_DIM])).astype(y_ref.dtype)

    xp_ref[0:CONV_TAIL_ROWS, :] = tail_ref[...]
    xp_ref[CONV_TAIL_ROWS:CONV_TAIL_ROWS + L, :] = p_ref[:, OFF_MQK:OFF_MQK + 2 * MLSTM_WIDTH]
    acc = bconv_ref[...] + wconv_ref[0:1, :] * xp_ref[CONV_TAIL_ROWS - 3:CONV_TAIL_ROWS - 3 + L, :]
    for j in range(1, CONV_WIDTH):
        acc = acc + wconv_ref[j:j + 1, :] * xp_ref[CONV_TAIL_ROWS - 3 + j:CONV_TAIL_ROWS - 3 + j + L, :]
    tail_ref[...] = xp_ref[L:L + CONV_TAIL_ROWS, :]
    xp_ref[CONV_TAIL_ROWS:CONV_TAIL_ROWS + L, :] = acc * _sigmoid(acc)

    gcol = gc + bgc_ref[...]
    is_f_col = lax.broadcasted_iota(I32, gcol.shape, 1) >= MLSTM_HEADS
    gcol = jnp.where(is_f_col, _log_sigmoid(gcol), gcol)
    grow = gr + bgr_ref[...]
    is_f_row = lax.broadcasted_iota(I32, grow.shape, 0) >= MLSTM_HEADS
    grow = jnp.where(is_f_row, _log_sigmoid(grow), grow)
    bcol_all = _dot3(tril_ref[...], gcol)
    brow_all = _dot3(grow, triu_ref[...])
    causal = lax.broadcasted_iota(I32, (L, L), 0) >= lax.broadcasted_iota(I32, (L, L), 1)

    def mlstm_head(h):
        lo = h * HEAD_DIM
        q = xp_ref[CONV_TAIL_ROWS:CONV_TAIL_ROWS + L, lo:lo + HEAD_DIM]
        k = xp_ref[CONV_TAIL_ROWS:CONV_TAIL_ROWS + L, MLSTM_WIDTH + lo:MLSTM_WIDTH + lo + HEAD_DIM] * scale
        v = p_ref[:, OFF_MV + lo:OFF_MV + lo + HEAD_DIM].astype(BF16)
        ic_col = gcol[:, h:h + 1]
        ic_row = grow[h:h + 1, :]
        b_col = bcol_all[:, MLSTM_HEADS + h:MLSTM_HEADS + h + 1]
        b_row = brow_all[MLSTM_HEADS + h:MLSTM_HEADS + h + 1, :]
        c_old = c_ref[h]
        n_old = n_ref[h]
        m_old = m_ref[h][:, 0:1]

        d_log = jnp.where(causal, b_col - b_row + ic_row, -jnp.inf)
        inter = b_col + m_old
        m_t = jnp.maximum(inter, jnp.max(d_log, axis=1, keepdims=True))
        yield True
        w_intra = jnp.exp(d_log - m_t)
        w_inter = jnp.exp(inter - m_t)
        qb = q.astype(BF16)
        s = _dot_nt(qb, k.astype(BF16)) * w_intra
        yield True
        num = _dot(s.astype(BF16), v) + w_inter * _dot(qb, c_old.astype(BF16))
        den = jnp.sum(s, axis=1, keepdims=True) + w_inter * jnp.sum(q * n_old, axis=1, keepdims=True)
        yield True
        hh = num / jnp.maximum(jnp.abs(den), jnp.exp(-m_t))

        b_last = b_col[L - 1:L, :]
        w_log_col = b_last - b_col + ic_col
        m_new = jnp.maximum(b_last + m_old, jnp.max(w_log_col, axis=0, keepdims=True))
        wk = jnp.exp(w_log_col - m_new) * k
        cdec = jnp.exp(b_last + m_old - m_new)
        yield True
        c_ref[h] = cdec * c_old + _dot_tn(wk.astype(BF16), v)
        n_ref[h] = cdec * n_old + jnp.sum(wk, axis=0, keepdims=True)
        m_ref[h] = jnp.broadcast_to(m_new, (1, HEAD_DIM))
        yield True
        o = p_ref[:, OFF_MO + lo:OFF_MO + lo + HEAD_DIM]
        y_ref[:, RET_WIDTH + lo:RET_WIDTH + lo + HEAD_DIM] = (
            _sigmoid(o) * _layer_norm(hh, gml_ref[:, lo:lo + HEAD_DIM])).astype(y_ref.dtype)

    return [retention_head(h) for h in range(RET_HEADS)] + [mlstm_head(h) for h in range(MLSTM_HEADS)]


def _mixer_consts(L, pos):
    f32 = np.float32
    half = HEAD_DIM // 2
    inv_freq = np.power(f32(ROPE_BASE), -np.arange(half, dtype=f32) / f32(half)).astype(f32)
    ang = (pos.astype(f32)[:, None] * inv_freq[None, :]).astype(f32)
    cos = np.concatenate([np.cos(ang), np.cos(ang)], axis=-1).astype(f32)
    sin = np.concatenate([-np.sin(ang), np.sin(ang)], axis=-1).astype(f32)
    log_gamma = np.log1p(-np.exp2(-5.0 - np.arange(RET_HEADS, dtype=np.float64)))
    idx = np.arange(L, dtype=np.float64)
    rel = idx[:, None] - idx[None, :]
    dec = np.where(rel >= 0, np.exp(log_gamma[:, None, None] * np.maximum(rel, 0.0)), 0.0)
    qd = np.broadcast_to(np.exp(log_gamma[:, None] * (idx + 1.0))[..., None], (RET_HEADS, L, HEAD_DIM))
    kd = np.broadcast_to(np.exp(log_gamma[:, None] * (L - 1.0 - idx))[..., None], (RET_HEADS, L, HEAD_DIM))
    cd = np.broadcast_to(np.exp(log_gamma * L)[:, None, None], (RET_HEADS, 1, HEAD_DIM))
    tril = rel >= 0
    triu = rel <= 0
    return tuple(jnp.asarray(a, F32) for a in (cos, sin, dec, qd, kd, cd, tril, triu))


def _mixer_call(x3, mod3, g_mix, w_main_bf, w_gate, w_gate_t, consts, w_conv, b_conv, g_ret, g_mlstm, bg_col, bg_row,
                s0, c0, n0, m0, conv0, L, gs, outproj=None):
    cos, sin, dec, qd, kd, cd, tril, triu = consts
    groups, seq, d = x3.shape
    nc = seq // L
    hd = HEAD_DIM
    full = lambda *shape: pl.BlockSpec(shape, lambda g, c: (0,) * len(shape))
    mod_spec = lambda k: pl.BlockSpec((gs, 1, d), lambda g, c: (g, 0, k))
    state4 = pl.BlockSpec((gs, RET_HEADS, hd, hd), lambda g, c: (g, 0, 0, 0))
    vec4 = pl.BlockSpec((gs, MLSTM_HEADS, 1, hd), lambda g, c: (g, 0, 0, 0))
    tail3 = pl.BlockSpec((gs, CONV_TAIL_ROWS, 2 * MLSTM_WIDTH), lambda g, c: (g, 0, 0))
    row = lambda g, c: (g, c, 0)
    in_specs = [pl.BlockSpec((gs, L, d), row), mod_spec(0), mod_spec(1),
                full(1, 1, d), full(*w_main_bf.shape), full(d, 128), full(N_GATES, d),
                pl.BlockSpec((L, hd), lambda g, c: (c, 0)),
                pl.BlockSpec((L, hd), lambda g, c: (c, 0)),
                full(RET_HEADS, L, L), full(RET_HEADS, L, hd), full(RET_HEADS, L, hd), full(RET_HEADS, 1, hd),
                full(L, L), full(L, L),
                full(CONV_WIDTH, 2 * MLSTM_WIDTH), full(1, 2 * MLSTM_WIDTH),
                full(1, RET_WIDTH), full(1, MLSTM_WIDTH), full(1, N_GATES), full(N_GATES, 1),
                state4, state4, vec4, vec4, tail3]
    args = [x3, mod3, mod3, g_mix.reshape(1, 1, d), w_main_bf, w_gate, w_gate_t,
            cos, sin, dec, qd, kd, cd, tril, triu, w_conv, b_conv.reshape(1, -1),
            g_ret.reshape(1, -1), g_mlstm.reshape(1, -1), bg_col, bg_row, s0, c0, n0, m0, conv0]
    assert len(in_specs) == len(args) == N_MIXER_INPUTS
    state_specs = [state4, state4, vec4, vec4, tail3]
    state_shapes = [jax.ShapeDtypeStruct((groups, RET_HEADS, hd, hd), F32),
                    jax.ShapeDtypeStruct((groups, MLSTM_HEADS, hd, hd), F32),
                    jax.ShapeDtypeStruct((groups, MLSTM_HEADS, 1, hd), F32),
                    jax.ShapeDtypeStruct((groups, MLSTM_HEADS, 1, hd), F32),
                    jax.ShapeDtypeStruct((groups, CONV_TAIL_ROWS, 2 * MLSTM_WIDTH), F32)]
    scratch = [pltpu.VMEM((gs * L, MAIN_COLS), F32), pltpu.VMEM((gs, CONV_TAIL_ROWS + L, 2 * MLSTM_WIDTH), F32)]
    y_block, y_shape = (gs, L, RET_WIDTH + MLSTM_WIDTH), (groups, seq, RET_WIDTH + MLSTM_WIDTH)
    if outproj is None:
        kernel_fn = _mixer_kernel
        out_specs = [pl.BlockSpec(y_block, row)] + state_specs
        out_shape = [jax.ShapeDtypeStruct(y_shape, F32)] + state_shapes
    else:
        g_ffn, w_out_bf, w_router_t, b_router, ustrict = outproj
        tm = gs * L
        n = groups * seq
        tile = lambda g, c: g * nc + c
        kernel_fn = _mixer_outproj_kernel
        in_specs += [mod_spec(2), mod_spec(3), mod_spec(4),
                     full(1, 1, d), full(d, d), full(N_EXPERTS, d), full(N_EXPERTS, 1), full(tm, tm)]
        args += [mod3, mod3, mod3, g_ffn.reshape(1, 1, d), w_out_bf, w_router_t, b_router.reshape(N_EXPERTS, 1),
                 ustrict]
        assert len(args) == N_MIXER_INPUTS + N_OUTPROJ_INPUTS
        out_specs = [pl.BlockSpec((gs, L, d), row),
                     pl.BlockSpec((tm, d + SIDE_LANES), lambda g, c: (tile(g, c), 0)),
                     pl.BlockSpec((TOP_K, tm), lambda g, c: (0, tile(g, c))),
                     pl.BlockSpec((TOP_K, tm), lambda g, c: (0, tile(g, c))),
                     pl.BlockSpec((1, N_EXPERTS, 128), lambda g, c: (tile(g, c), 0, 0))] + state_specs
        out_shape = [jax.ShapeDtypeStruct((groups, seq, d), F32),
                     jax.ShapeDtypeStruct((n, d + SIDE_LANES), BF16),
                     jax.ShapeDtypeStruct((TOP_K, n), I32),
                     jax.ShapeDtypeStruct((TOP_K, n), I32),
                     jax.ShapeDtypeStruct((n // tm, N_EXPERTS, 128), F32)] + state_shapes
        scratch = scratch + [pltpu.VMEM(y_block, F32)]
    return pl.pallas_call(
        kernel_fn,
        grid=(groups // gs, nc),
        in_specs=in_specs, out_specs=out_specs, out_shape=out_shape, scratch_shapes=scratch,
        compiler_params=_params(2), name="mixer",
    )(*args)


def _outproj_kernel(y_ref, x_ref, gt_ref, sh_ref, sc_ref, g_ref, w_ref, wrt_ref, br_ref, ustrict_ref,
                    x1_ref, h2_ref, idx_ref, rank_ref, cnt_ref):
    _outproj_router(y_ref[...], x_ref, gt_ref, sh_ref, sc_ref, g_ref, w_ref, wrt_ref, br_ref, ustrict_ref,
                    x1_ref, h2_ref, idx_ref, rank_ref, cnt_ref)


def _outproj_router(y, x_ref, gt_ref, sh_ref, sc_ref, g_ref, w_ref, wrt_ref, br_ref, ustrict_ref,
                    x1_ref, h2_ref, idx_ref, rank_ref, cnt_ref):
    x = x_ref[...]
    gb, tt, d = x.shape
    tm = gb * tt
    mixed = _dot(y.astype(BF16), w_ref[...])
    x1 = x + gt_ref[...] * mixed.reshape(gb, tt, d)
    x1_ref[...] = x1
    h2 = (_rms(x1, g_ref[...]) * (1.0 + sc_ref[...]) + sh_ref[...]).reshape(tm, d)
    h2_ref[:, :d] = h2.astype(BF16)

    work = _dot3(wrt_ref[...], h2, dot=_dot_nt) + br_ref[...]
    e_iota = lax.broadcasted_iota(I32, work.shape, 0).astype(F32)
    vals, idxs, sels = [], [], []
    for _ in range(TOP_K):
        mx = jnp.max(work, axis=0, keepdims=True)
        ik = jnp.min(jnp.where(work == mx, e_iota, float(N_EXPERTS)), axis=0, keepdims=True)
        sel = e_iota == ik
        vals.append(mx)
        idxs.append(ik)
        sels.append(sel)
        work = jnp.where(sel, -jnp.inf, work)
    exps = [jnp.exp(v - vals[0]) for v in vals]
    denom = exps[0] + exps[1] + exps[2] + exps[3]
    gates = [e / denom for e in exps]
    idx_ref[...] = jnp.concatenate(idxs, axis=0).astype(I32)

    pieces = []
    for gk in gates:
        p1 = gk.astype(BF16)
        r1 = gk - p1.astype(F32)
        p2 = r1.astype(BF16)
        pieces += [p1, p2, (r1 - p2.astype(F32)).astype(BF16)]
    side = jnp.concatenate(pieces + [ik.astype(BF16) for ik in idxs], axis=0)
    eye = (lax.broadcasted_iota(I32, (SIDE_ROWS, SIDE_LANES), 0)
           == lax.broadcasted_iota(I32, (SIDE_ROWS, SIDE_LANES), 1)).astype(BF16)
    h2_ref[:, d:] = _dot_tn(side, eye).astype(BF16)

    mask = (sels[0] | sels[1] | sels[2] | sels[3]).astype(F32)
    before = _dot(mask.astype(BF16), ustrict_ref[...])
    ranks = [jnp.sum(jnp.where(sel, before, 0.0), axis=0, keepdims=True) for sel in sels]
    rank_ref[...] = jnp.concatenate(ranks, axis=0).astype(I32)
    cnt_ref[0] = jnp.broadcast_to(jnp.sum(mask, axis=1, keepdims=True), cnt_ref.shape[1:])


def _outproj_call(y, x3, mod3, g_ffn, w_out_bf, w_router_t, b_router, ustrict, gb, tt):
    g, t, d = x3.shape
    n = g * t
    tm = gb * tt
    tpg = t // tt
    row = lambda i, j: (i * tpg + j, 0)
    col = lambda i, j: (0, i * tpg + j)
    mod_spec = lambda k: pl.BlockSpec((gb, 1, d), lambda i, j: (i, 0, k))
    const = lambda *shape: pl.BlockSpec(shape, lambda i, j: (0,) * len(shape))
    return pl.pallas_call(
        _outproj_kernel,
        grid=(g // gb, tpg),
        in_specs=[pl.BlockSpec((tm, d), row),
                  pl.BlockSpec((gb, tt, d), lambda i, j: (i, j, 0)),
                  mod_spec(2), mod_spec(3), mod_spec(4),
                  const(1, 1, d), const(d, d), const(N_EXPERTS, d), const(N_EXPERTS, 1),
                  const(tm, tm)],
        out_specs=[pl.BlockSpec((gb, tt, d), lambda i, j: (i, j, 0)),
                   pl.BlockSpec((tm, d + SIDE_LANES), row),
                   pl.BlockSpec((TOP_K, tm), col),
                   pl.BlockSpec((TOP_K, tm), col),
                   pl.BlockSpec((1, N_EXPERTS, 128), lambda i, j: (i * tpg + j, 0, 0))],
        out_shape=[jax.ShapeDtypeStruct((g, t, d), F32),
                   jax.ShapeDtypeStruct((n, d + SIDE_LANES), BF16),
                   jax.ShapeDtypeStruct((TOP_K, n), I32),
                   jax.ShapeDtypeStruct((TOP_K, n), I32),
                   jax.ShapeDtypeStruct((n // tm, N_EXPERTS, 128), F32)],
        compiler_params=_params(2), name="outproj_router",
    )(y, x3, mod3, mod3, mod3, g_ffn.reshape(1, 1, d), w_out_bf, w_router_t, b_router.reshape(N_EXPERTS, 1),
      ustrict)


def _chunk_copy(hbm_ref, hbm_row, buf_ref, chunk, sem, to_hbm):
    hbm = hbm_ref.at[pl.ds(pl.multiple_of(hbm_row, RUN_ALIGN), RUN_ALIGN)]
    buf = buf_ref.at[pl.ds(pl.multiple_of(chunk * RUN_ALIGN, RUN_ALIGN), RUN_ALIGN)]
    return pltpu.make_async_copy(buf, hbm, sem) if to_hbm else pltpu.make_async_copy(hbm, buf, sem)


def _tile_chunks(hbm_ref, bufs_ref, starts_ref, nq_ref, sems, tile, to_hbm, wait):
    slot = tile % 2
    first = tile * (bufs_ref.shape[1] // RUN_ALIGN)

    def body(q, c):
        cp = _chunk_copy(hbm_ref, starts_ref[first + q], bufs_ref.at[slot], q, sems.at[slot], to_hbm)
        cp.wait() if wait else cp.start()
        return c

    lax.fori_loop(0, nq_ref[tile], body, 0)


def _dispatch_kernel(starts_ref, nq_ref, meta_ref, rbuf_ref, h2p_ref, h2s_ref, xs_ref, bufs_ref, zero_ref, sems,
                     *, prompt_tiles):
    i = pl.program_id(0)
    tm = h2p_ref.shape[0]
    bm = zero_ref.shape[0]
    n_blocks = xs_ref.shape[0] // bm
    buf_ref = bufs_ref.at[i % 2]

    def zero_copy(row, sem):
        return pltpu.make_async_copy(zero_ref, xs_ref.at[pl.ds(pl.multiple_of(row, bm), bm)], sem)

    def tails(fn):
        def body(e, c):
            @pl.when(meta_ref[e] >= 0)
            def _():
                fn(zero_copy(meta_ref[e], sems.at[2]))
            return c
        lax.fori_loop(0, N_EXPERTS, body, 0)

    def unused(fn):
        def body(b, c):
            fn(zero_copy(b * bm, sems.at[3]))
            return c
        lax.fori_loop(meta_ref[N_EXPERTS], n_blocks, body, 0)

    @pl.when(i == 0)
    def _():
        zero_ref[...] = jnp.zeros(zero_ref.shape, zero_ref.dtype)
        tails(lambda cp: cp.start())
        unused(lambda cp: cp.start())
        tails(lambda cp: cp.wait())

    @pl.when(i == pl.num_programs(0) - 1)
    def _():
        unused(lambda cp: cp.wait())

    def build(h2_ref):
        h2 = h2_ref[...]
        rb = rbuf_ref[...].astype(jnp.int16)
        for c in range(bufs_ref.shape[1] // PERM_ROWS):
            r = (lax.broadcasted_iota(I32, (PERM_ROWS, tm), 0) + c * PERM_ROWS).astype(jnp.int16)
            hit = (r == rb[0:1, :]) | (r == rb[1:2, :]) | (r == rb[2:3, :]) | (r == rb[3:4, :])
            onehot = jnp.where(hit, jnp.ones((), BF16), jnp.zeros((), BF16))
            buf_ref[c * PERM_ROWS:(c + 1) * PERM_ROWS, :] = _dot(onehot, h2).astype(BF16)

    @pl.when(i < prompt_tiles)
    def _():
        build(h2p_ref)

    @pl.when(i >= prompt_tiles)
    def _():
        build(h2s_ref)

    chunks = functools.partial(_tile_chunks, xs_ref, bufs_ref, starts_ref, nq_ref, sems, to_hbm=True)
    chunks(i, wait=False)

    @pl.when(i > 0)
    def _():
        chunks(i - 1, wait=True)

    @pl.when(i == pl.num_programs(0) - 1)
    def _():
        chunks(i, wait=True)


def _dispatch_call(starts, nq, meta, rbuf, h2_p, h2_s, cap):
    d = h2_p.shape[1]
    tm = MOE_TILE
    pt, st = h2_p.shape[0] // tm, h2_s.shape[0] // tm
    return pl.pallas_call(
        functools.partial(_dispatch_kernel, prompt_tiles=pt),
        grid_spec=pltpu.PrefetchScalarGridSpec(
            num_scalar_prefetch=3,
            grid=(pt + st,),
            in_specs=[pl.BlockSpec((TOP_K, tm), lambda i, *_: (0, i)),
                      pl.BlockSpec((tm, d), lambda i, *_: (jnp.minimum(i, pt - 1), 0)),
                      pl.BlockSpec((tm, d), lambda i, *_: (jnp.maximum(i - pt, 0), 0))],
            out_specs=pl.BlockSpec(memory_space=pl.ANY),
            scratch_shapes=[pltpu.VMEM((2, TILE_BUF_ROWS, d), BF16), pltpu.VMEM((EXPERT_BLOCK, d), BF16),
                            pltpu.SemaphoreType.DMA((4,))]),
        out_shape=jax.ShapeDtypeStruct((cap, d), BF16),
        compiler_params=_params(1), name="dispatch",
    )(starts, nq, meta, rbuf, h2_p, h2_s)


def _expert_kernel(be_ref, nu_ref, xs_ref, wup_ref, bup_ref, wdn_ref, bdn_ref, ys_ref, wup_bf, wdn_bf):
    i = pl.program_id(0)
    prev = be_ref[jnp.maximum(i - 1, 0)]

    @pl.when((i == 0) | (be_ref[i] != prev))
    def _():
        def cast(r, c):
            rows = pl.ds(pl.multiple_of(r * 64, 64), 64)
            wup_bf[rows, :] = wup_ref[0, rows, :].astype(BF16)
            wdn_bf[rows, :] = wdn_ref[0, rows, :].astype(BF16)
            return c

        lax.fori_loop(0, D_MODEL // 64, cast, 0)

    @pl.when(i < nu_ref[0])
    def _():
        hu = _dot(xs_ref[:, :D_MODEL], wup_bf[...]) + bup_ref[0]
        gate = jnp.minimum(hu[:, :D_FF], SWIGLU_LIMIT)
        lin = jnp.clip(hu[:, D_FF:], -SWIGLU_LIMIT, SWIGLU_LIMIT)
        glu = gate * _sigmoid(SWIGLU_ALPHA * gate)
        y = _dot(((lin + 1.0) * glu).astype(BF16), wdn_bf[...]) + bdn_ref[0]
        side = xs_ref[:, D_MODEL:].astype(F32)
        e = be_ref[i].astype(F32)
        weight = jnp.zeros((side.shape[0], 1), F32)
        for k in range(TOP_K):
            g_k = side[:, 3 * k:3 * k + 1] + side[:, 3 * k + 1:3 * k + 2] + side[:, 3 * k + 2:3 * k + 3]
            weight = weight + jnp.where(side[:, SIDE_ROWS - TOP_K + k:SIDE_ROWS - TOP_K + k + 1] == e, g_k, 0.0)
        ys_ref[...] = (weight * y).astype(ys_ref.dtype)

    @pl.when(i >= nu_ref[0])
    def _():
        ys_ref[...] = jnp.zeros(ys_ref.shape, ys_ref.dtype)


def _expert_call(block_e, n_used, xs, w_up, b_up, w_down, b_down):
    cap, dw = xs.shape
    d = w_down.shape[2]
    bm = EXPERT_BLOCK
    blk = lambda i, be, nu: (jnp.minimum(i, nu[0] - 1), 0)
    per_e = lambda i, be, nu: (be[i], 0, 0)
    return pl.pallas_call(
        _expert_kernel,
        grid_spec=pltpu.PrefetchScalarGridSpec(
            num_scalar_prefetch=2,
            grid=(cap // bm,),
            in_specs=[pl.BlockSpec((bm, dw), blk),
                      pl.BlockSpec((1, d, 2 * D_FF), per_e),
                      pl.BlockSpec((1, 1, 2 * D_FF), per_e),
                      pl.BlockSpec((1, D_FF, d), per_e),
                      pl.BlockSpec((1, 1, d), per_e)],
            out_specs=pl.BlockSpec((bm, d), lambda i, be, nu: (i, 0)),
            scratch_shapes=[pltpu.VMEM((d, 2 * D_FF), BF16), pltpu.VMEM((D_FF, d), BF16)]),
        out_shape=jax.ShapeDtypeStruct((cap, d), BF16),
        compiler_params=_params(1), name="experts",
    )(block_e, n_used, xs, w_up, b_up.reshape(N_EXPERTS, 1, -1), w_down, b_down.reshape(N_EXPERTS, 1, -1))


def _combine_kernel(starts_ref, nq_ref, x1_ref, gt_ref, rcol_ref, gfin_ref, ys_ref, o_ref, bufs_ref, sems,
                    *, tile0, tiles_per_group):
    x1 = x1_ref[...]
    gb, tt, d = x1.shape
    tm = gb * tt
    step = pl.program_id(0) * tiles_per_group + pl.program_id(1)
    n_steps = pl.num_programs(0) * tiles_per_group
    tile = tile0 + step
    chunks = functools.partial(_tile_chunks, ys_ref, bufs_ref, starts_ref, nq_ref, sems, to_hbm=False)

    @pl.when(step == 0)
    def _():
        bufs_ref[...] = jnp.zeros(bufs_ref.shape, bufs_ref.dtype)
        chunks(tile, wait=False)

    @pl.when(step + 1 < n_steps)
    def _():
        chunks(tile + 1, wait=False)

    chunks(tile, wait=True)
    buf_ref = bufs_ref.at[tile % 2]

    rows = rcol_ref[...].astype(jnp.int16)
    moe = jnp.zeros((tm, d), F32)
    for c in range(bufs_ref.shape[1] // PERM_COLS):
        r = (lax.broadcasted_iota(I32, (tm, PERM_COLS), 1) + c * PERM_COLS).astype(jnp.int16)
        hit = (r == rows[:, 0:1]) | (r == rows[:, 1:2]) | (r == rows[:, 2:3]) | (r == rows[:, 3:4])
        onehot = jnp.where(hit, jnp.ones((), BF16), jnp.zeros((), BF16))
        moe = moe + _dot(onehot, buf_ref[c * PERM_COLS:(c + 1) * PERM_COLS, :])
    xo = x1 + gt_ref[...] * moe.reshape(gb, tt, d)
    o_ref[...] = _rms(xo, gfin_ref[...])


def _combine_call(starts, nq, x1, mod3, rcol, g_final, ys, tok0, gb, tt):
    g, t, d = x1.shape
    tm = gb * tt
    tpg = t // tt
    t0 = tok0 // tm
    tok = lambda i, j, *_: (t0 + i * tpg + j, 0)
    return pl.pallas_call(
        functools.partial(_combine_kernel, tile0=t0, tiles_per_group=tpg),
        grid_spec=pltpu.PrefetchScalarGridSpec(
            num_scalar_prefetch=2,
            grid=(g // gb, tpg),
            in_specs=[pl.BlockSpec((gb, tt, d), lambda i, j, *_: (i, j, 0)),
                      pl.BlockSpec((gb, 1, d), lambda i, j, *_: (i, 0, 5)),
                      pl.BlockSpec((tm, TOP_K), tok),
                      pl.BlockSpec((1, 1, d), lambda i, j, *_: (0, 0, 0)),
                      pl.BlockSpec(memory_space=pl.ANY)],
            out_specs=pl.BlockSpec((gb, tt, d), lambda i, j, *_: (i, j, 0)),
            scratch_shapes=[pltpu.VMEM((2, TILE_BUF_ROWS, d), BF16), pltpu.SemaphoreType.DMA((2,))]),
        out_shape=jax.ShapeDtypeStruct((g, t, d), F32),
        compiler_params=_params(2), name="combine",
    )(starts, nq, x1, mod3, rcol, g_final.reshape(1, 1, d), ys)


def _group_blocking(groups, seq, tile):
    if seq >= tile:
        return 1, tile
    return tile // seq, seq


def kernel(x_prompt, x_sample, c_prompt, c_sample, state_ret, state_mlstm_c, state_mlstm_n, state_mlstm_m, state_conv, w_mod, b_mod, g_mix, g_ffn, w_in, b_igate, b_fgate, w_conv, b_conv, g_ret, g_mlstm, w_out, w_router, b_router, w_up, b_up, w_down, b_down, g_final):
    depth = w_mod.shape[0]
    assert depth == 1, "single-layer trunk"
    bp, tp, d = x_prompt.shape
    bs, ts, _ = x_sample.shape
    n_p, n_s = bp * tp, bs * ts
    hd = HEAD_DIM
    l = 0

    mod = _mod_call(jnp.concatenate([c_prompt, c_sample], axis=0), w_mod[l], b_mod[l])
    mod_p = mod[:bp].reshape(bp, 1, N_MOD * d)
    mod_s = mod[bp:].reshape(bs, 1, N_MOD * d)

    w_main_bf = w_in[l].astype(BF16)
    w_gate = jnp.pad(w_in[l][:, MAIN_COLS:], ((0, 0), (0, 128 - N_GATES)))
    w_gate_t = w_in[l][:, MAIN_COLS:].T
    w_out_bf = w_out[l].astype(BF16)
    bg_col = jnp.concatenate([b_igate[l], b_fgate[l]]).reshape(1, N_GATES)
    bg_row = bg_col.reshape(N_GATES, 1)

    groups = (
        (x_prompt, mod_p, min(CHUNK, tp), np.arange(tp),
         jnp.zeros((bp, RET_HEADS, hd, hd), F32), jnp.zeros((bp, MLSTM_HEADS, hd, hd), F32),
         jnp.zeros((bp, MLSTM_HEADS, hd), F32), jnp.zeros((bp, MLSTM_HEADS), F32),
         jnp.zeros((bp, CONV_WIDTH - 1, 2 * MLSTM_WIDTH), F32)),
        (x_sample, mod_s, min(CHUNK, ts), PAST_LEN + np.arange(ts),
         state_ret[l], state_mlstm_c[l], state_mlstm_n[l], state_mlstm_m[l], state_conv[l]),
    )

    ustrict = jnp.asarray(np.arange(MOE_TILE)[:, None] < np.arange(MOE_TILE)[None, :], BF16)
    staged = []
    for x3, mod3, L, pos, s0, c0, n0, m0, conv0 in groups:
        g, t, _ = x3.shape
        conv0p = jnp.pad(conv0.astype(F32), ((0, 0), (CONV_TAIL_ROWS - (CONV_WIDTH - 1), 0), (0, 0)))
        outproj = (g_ffn[l], w_out_bf, w_router[l].T, b_router[l], ustrict)
        chunked = t > L
        gs = MIXER_SEQS_CHUNKED if chunked else MIXER_SEQS_SHORT
        fused = chunked and gs * L == MOE_TILE
        res = _mixer_call(
            x3, mod3, g_mix[l], w_main_bf, w_gate, w_gate_t, _mixer_consts(L, pos),
            w_conv[l], b_conv[l], g_ret[l], g_mlstm[l],
            bg_col, bg_row, s0.astype(F32), c0.astype(F32), n0.astype(F32).reshape(g, MLSTM_HEADS, 1, hd),
            jnp.broadcast_to(m0.astype(F32)[:, :, None, None], (g, MLSTM_HEADS, 1, hd)), conv0p,
            L, gs, outproj if fused else None)
        s_new, c_new, n_new, m_new, tail = res[-5:]
        states = (s_new, c_new, n_new.reshape(g, MLSTM_HEADS, hd), m_new[:, :, 0, 0],
                  tail[:, CONV_TAIL_ROWS - (CONV_WIDTH - 1):, :])
        if fused:
            x1, h2, idx, rank, cnt = res[:5]
            blocking = (gs, L)
        else:
            blocking = _group_blocking(g, t, MOE_TILE)
            x1, h2, idx, rank, cnt = _outproj_call(res[0].reshape(g * t, d), x3, mod3, *outproj, *blocking)
        staged.append((x1, mod3, h2, idx, rank, states, cnt, blocking))

    n_tok = n_p + n_s
    tm, bm, ra = MOE_TILE, EXPERT_BLOCK, RUN_ALIGN
    n_tiles = n_tok // tm
    q_max = TILE_BUF_ROWS // ra
    n_blocks = -(-(n_tok * TOP_K + n_tiles * N_EXPERTS * (ra - 1)) // bm) + N_EXPERTS
    cap = n_blocks * bm
    counts = jnp.concatenate([s[6][:, :, 0] for s in staged], axis=0).astype(I32)
    run = (counts + ra - 1) // ra * ra
    region = jnp.sum(run, axis=0)
    padded = (region + bm - 1) // bm * bm
    pad_end = jnp.cumsum(padded)
    pad_start = pad_end - padded
    run_start = pad_start[None, :] + jnp.cumsum(run, axis=0) - run
    buf_end = jnp.cumsum(run, axis=1)
    buf_start = buf_end - run
    nq = (buf_end[:, -1] // ra).astype(I32)
    chunk_row = jnp.arange(q_max, dtype=I32) * ra
    chunk_e = jnp.minimum(jnp.sum((buf_end[:, None, :] <= chunk_row[None, :, None]).astype(I32), axis=2),
                          N_EXPERTS - 1)
    e_ids = jnp.arange(N_EXPERTS, dtype=I32)
    shift = run_start - buf_start
    starts = jnp.sum(jnp.where(chunk_e[:, :, None] == e_ids, shift[:, None, :], 0), axis=2) + chunk_row[None, :]
    starts = jnp.where(chunk_row[None, :] < buf_end[:, -1:], starts, 0).reshape(-1).astype(I32)

    idx_all = jnp.concatenate([s[3] for s in staged], axis=1)
    rank_all = jnp.concatenate([s[4] for s in staged], axis=1)
    buf_start_tok = jnp.repeat(buf_start, tm, axis=0).T
    rbuf = jnp.sum(jnp.where(idx_all[None] == e_ids[:, None, None], buf_start_tok[:, None, :], 0), axis=0) + rank_all
    rcol = rbuf.T
    block_row = jnp.arange(n_blocks, dtype=I32) * bm
    block_e = jnp.minimum(jnp.sum((pad_end[None, :] <= block_row[:, None]).astype(I32), axis=1), N_EXPERTS - 1)
    n_used = (pad_end[-1:] // bm).astype(I32)
    meta = jnp.concatenate([jnp.where(region > 0, pad_end - bm, -1), n_used]).astype(I32)

    xs = _dispatch_call(starts, nq, meta, rbuf, staged[0][2], staged[1][2], cap)
    ys = _expert_call(block_e, n_used, xs, w_up[l], b_up[l], w_down[l], b_down[l])

    outs = []
    tok0 = 0
    for x1, mod3, *_, blocking in staged:
        outs.append(_combine_call(starts, nq, x1, mod3, rcol, g_final, ys, tok0, *blocking))
        tok0 += x1.shape[0] * x1.shape[1]

    st_p, st_s = staged[0][5], staged[1][5]
    return (outs[0], outs[1]) + tuple(a[None] for a in st_p) + tuple(a[None] for a in st_s)
```

```python
import functools

import numpy as np
import jax
import jax.numpy as jnp
from jax import lax
from jax.experimental import pallas as pl
from jax.experimental.pallas import tpu as pltpu

F32 = jnp.float32
BF16 = jnp.bfloat16
I32 = jnp.int32

D_MODEL = 1024
PAST_LEN = 16384
RET_HEADS = 4
MLSTM_HEADS = 4
HEAD_DIM = 128
RET_WIDTH = RET_HEADS * HEAD_DIM
MLSTM_WIDTH = MLSTM_HEADS * HEAD_DIM
CONV_WIDTH = 4
CHUNK = 128
ROPE_BASE = 10000.0
N_EXPERTS = 32
TOP_K = 4
D_FF = D_MODEL
SWIGLU_LIMIT = 7.0
SWIGLU_ALPHA = 1.702
N_MOD = 6
EPS = 1e-6
MAIN_COLS = 4 * RET_WIDTH + 2 * MLSTM_WIDTH + 2 * MLSTM_WIDTH
N_GATES = 2 * MLSTM_HEADS
OFF_RQ, OFF_RK, OFF_RV, OFF_RG = 0, RET_WIDTH, 2 * RET_WIDTH, 3 * RET_WIDTH
OFF_MQK = 4 * RET_WIDTH
OFF_MV = OFF_MQK + 2 * MLSTM_WIDTH
OFF_MO = OFF_MV + MLSTM_WIDTH

VMEM_LIMIT_BYTES = 56 * 1024 * 1024
MOE_TILE = 512
EXPERT_BLOCK = 512
MIXER_SEQS_CHUNKED = 4
MIXER_SEQS_SHORT = 16
MIXER_INTERLEAVE_SEQS = 2
RUN_ALIGN = 16
SIDE_ROWS = 4 * TOP_K
SIDE_LANES = 128
TILE_BUF_ROWS = MOE_TILE * TOP_K + N_EXPERTS * RUN_ALIGN
PERM_ROWS = 1280
PERM_COLS = 512
CONV_TAIL_ROWS = 8


def _params(n_axes=1):
    return pltpu.CompilerParams(dimension_semantics=("arbitrary",) * n_axes, vmem_limit_bytes=VMEM_LIMIT_BYTES)


def _dot(a, b):
    return jnp.dot(a, b, preferred_element_type=F32)


def _dot_nt(a, b):
    return lax.dot_general(a, b, (((1,), (1,)), ((), ())), preferred_element_type=F32)


def _dot_tn(a, b):
    return lax.dot_general(a, b, (((0,), (0,)), ((), ())), preferred_element_type=F32)


def _split(a):
    hi = a.astype(BF16)
    lo = (a - hi.astype(F32)).astype(BF16)
    return hi, lo


def _dot3(a, b, dot=_dot):
    ah, al = _split(a)
    bh, bl = _split(b)
    return dot(ah, bh) + (dot(al, bh) + dot(ah, bl))


def _sigmoid(x):
    return 0.5 * (jnp.tanh(0.5 * x) + 1.0)


def _log_sigmoid(x):
    return jnp.minimum(x, 0.0) - jnp.log1p(jnp.exp(-jnp.abs(x)))


def _rms(x, g):
    ms = jnp.mean(x * x, axis=-1, keepdims=True)
    return (x * lax.rsqrt(ms + EPS)) * g


def _layer_norm(x, g):
    mu = jnp.mean(x, axis=-1, keepdims=True)
    xc = x - mu
    var = jnp.mean(xc * xc, axis=-1, keepdims=True)
    return xc * lax.rsqrt(var + EPS) * g


def _cast_kernel(x_ref, o_ref):
    o_ref[...] = x_ref[...].astype(o_ref.dtype)


def _cast_bf16(w):
    rows, cols = w.shape
    slab = 256
    spec = pl.BlockSpec((slab, cols), lambda i: (i, 0))
    return pl.pallas_call(
        _cast_kernel, grid=(rows // slab,), in_specs=[spec], out_specs=spec,
        out_shape=jax.ShapeDtypeStruct((rows, cols), BF16), compiler_params=_params(1), name="cast_bf16",
    )(w)


def _mod_kernel(c_ref, w_ref, b_ref, o_ref):
    c = c_ref[...]
    o_ref[...] = _dot3(c * _sigmoid(c), w_ref[...]) + b_ref[...]


def _mod_call(c_all, w_mod, b_mod):
    rows, d = c_all.shape
    cols = w_mod.shape[1]
    tn = 1024
    return pl.pallas_call(
        _mod_kernel,
        grid=(cols // tn,),
        in_specs=[pl.BlockSpec((rows, d), lambda j: (0, 0)),
                  pl.BlockSpec((d, tn), lambda j: (0, j)),
                  pl.BlockSpec((1, tn), lambda j: (0, j))],
        out_specs=pl.BlockSpec((rows, tn), lambda j: (0, j)),
        out_shape=jax.ShapeDtypeStruct((rows, cols), F32),
        compiler_params=_params(1), name="mod",
    )(c_all, w_mod, b_mod.reshape(1, cols))


def _mixer_kernel(x_ref, sh_ref, sc_ref, gmix_ref, w_ref, wg_ref, wgt_ref,
                  cos_ref, sin_ref, dec_ref, qd_ref, kd_ref, cd_ref,
                  tril_ref, triu_ref, wconv_ref, bconv_ref, gret_ref, gml_ref, bgc_ref, bgr_ref,
                  s0_ref, c0_ref, n0_ref, m0_ref, conv0_ref,
                  y_ref, s_ref, c_ref, n_ref, m_ref, tail_ref, p_ref, xp_ref):
    @pl.when(pl.program_id(1) == 0)
    def _():
        s_ref[...] = s0_ref[...]
        c_ref[...] = c0_ref[...]
        n_ref[...] = n0_ref[...]
        m_ref[...] = m0_ref[...]
        tail_ref[...] = conv0_ref[...]

    x = x_ref[...]
    gs, L, d = x.shape
    h = (_rms(x, gmix_ref[...]) * (1.0 + sc_ref[...]) + sh_ref[...]).reshape(gs * L, d)
    hb = h.astype(BF16)
    for j in range(MAIN_COLS // 1024):
        p_ref[:, j * 1024:(j + 1) * 1024] = _dot(hb, w_ref[:, j * 1024:(j + 1) * 1024])

    chains = []
    for sq in range(gs):
        h_sq = h[sq * L:(sq + 1) * L, :]
        gc = _dot3(h_sq, wg_ref[...])[:, :N_GATES]
        gr = _dot3(wgt_ref[...], h_sq, dot=_dot_nt)
        chains += _mixer_sequence(p_ref.at[pl.ds(sq * L, L)], gc, gr, cos_ref, sin_ref, dec_ref, qd_ref, kd_ref,
                                  cd_ref, tril_ref, triu_ref, wconv_ref, bconv_ref, gret_ref, gml_ref, bgc_ref,
                                  bgr_ref, y_ref.at[sq], s_ref.at[sq], c_ref.at[sq], n_ref.at[sq], m_ref.at[sq],
                                  tail_ref.at[sq], xp_ref.at[sq])
    width = (gs if L == CHUNK else MIXER_INTERLEAVE_SEQS) * (RET_HEADS + MLSTM_HEADS)
    for first in range(0, len(chains), width):
        group = chains[first:first + width]
        while group:
            alive = []
            for chain in group:
                if next(chain, None) is not None:
                    alive.append(chain)
            group = alive


N_MIXER_INPUTS = 26
N_OUTPROJ_INPUTS = 8


def _mixer_outproj_kernel(*refs):
    mixer_in = refs[:N_MIXER_INPUTS]
    outproj_in = refs[N_MIXER_INPUTS:N_MIXER_INPUTS + N_OUTPROJ_INPUTS]
    x1_ref, h2_ref, idx_ref, rank_ref, cnt_ref, s_ref, c_ref, n_ref, m_ref, tail_ref, p_ref, xp_ref, y_ref = refs[
        N_MIXER_INPUTS + N_OUTPROJ_INPUTS:]
    _mixer_kernel(*mixer_in, y_ref, s_ref, c_ref, n_ref, m_ref, tail_ref, p_ref, xp_ref)
    gs, L, d = y_ref.shape
    _outproj_router(y_ref[...].reshape(gs * L, d), mixer_in[0], *outproj_in, x1_ref, h2_ref, idx_ref, rank_ref, cnt_ref)


def _mixer_sequence(p_ref, gc, gr, cos_ref, sin_ref, dec_ref, qd_ref, kd_ref, cd_ref,
                    tril_ref, triu_ref, wconv_ref, bconv_ref, gret_ref, gml_ref, bgc_ref, bgr_ref,
                    y_ref, s_ref, c_ref, n_ref, m_ref, tail_ref, xp_ref):
    L = p_ref.shape[0]
    cos = cos_ref[...]
    sin = sin_ref[...]
    scale = HEAD_DIM ** -0.5

    def rot(x):
        return x * cos + pltpu.roll(x, HEAD_DIM // 2, axis=1) * sin

    def retention_head(h):
        lo = h * HEAD_DIM
        q = rot(p_ref[:, OFF_RQ + lo:OFF_RQ + lo + HEAD_DIM])
        k = rot(p_ref[:, OFF_RK + lo:OFF_RK + lo + HEAD_DIM]) * scale
        v = p_ref[:, OFF_RV + lo:OFF_RV + lo + HEAD_DIM].astype(BF16)
        s_old = s_ref[h]
        yield True
        scores = _dot_nt(q.astype(BF16), k.astype(BF16)) * dec_ref[h]
        yield True
        out = _dot(scores.astype(BF16), v) + _dot((q * qd_ref[h]).astype(BF16), s_old.astype(BF16))
        s_ref[h] = cd_ref[h] * s_old + _dot_tn((k * kd_ref[h]).astype(BF16), v)
        yield True
        g = p_ref[:, OFF_RG + lo:OFF_RG + lo + HEAD_DIM]
        y_ref[:, lo:lo + HEAD_DIM] = ((g * _sigmoid(g)) * _layer_norm(out, gret_ref[:, lo:lo + HEAD_DIM])).astype(y_ref.dtype)

    xp_ref[0:CONV_TAIL_ROWS, :] = tail_ref[...]
    xp_ref[CONV_TAIL_ROWS:CONV_TAIL_ROWS + L, :] = p_ref[:, OFF_MQK:OFF_MQK + 2 * MLSTM_WIDTH]
    acc = bconv_ref[...] + wconv_ref[0:1, :] * xp_ref[CONV_TAIL_ROWS - 3:CONV_TAIL_ROWS - 3 + L, :]
    for j in range(1, CONV_WIDTH):
        acc = acc + wconv_ref[j:j + 1, :] * xp_ref[CONV_TAIL_ROWS - 3 + j:CONV_TAIL_ROWS - 3 + j + L, :]
    tail_ref[...] = xp_ref[L:L + CONV_TAIL_ROWS, :]
    xp_ref[CONV_TAIL_ROWS:CONV_TAIL_ROWS + L, :] = acc * _sigmoid(acc)

    gcol = gc + bgc_ref[...]
    is_f_col = lax.broadcasted_iota(I32, gcol.shape, 1) >= MLSTM_HEADS
    gcol = jnp.where(is_f_col, _log_sigmoid(gcol), gcol)
    grow = gr + bgr_ref[...]
    is_f_row = lax.broadcasted_iota(I32, grow.shape, 0) >= MLSTM_HEADS
    grow = jnp.where(is_f_row, _log_sigmoid(grow), grow)
    bcol_all = _dot3(tril_ref[...], gcol)
    brow_all = _dot3(grow, triu_ref[...])
    causal = lax.broadcasted_iota(I32, (L, L), 0) >= lax.broadcasted_iota(I32, (L, L), 1)

    def mlstm_head(h):
        lo = h * HEAD_DIM
        q = xp_ref[CONV_TAIL_ROWS:CONV_TAIL_ROWS + L, lo:lo + HEAD_DIM]
        k = xp_ref[CONV_TAIL_ROWS:CONV_TAIL_ROWS + L, MLSTM_WIDTH + lo:MLSTM_WIDTH + lo + HEAD_DIM] * scale
        v = p_ref[:, OFF_MV + lo:OFF_MV + lo + HEAD_DIM].astype(BF16)
        ic_col = gcol[:, h:h + 1]
        ic_row = grow[h:h + 1, :]
        b_col = bcol_all[:, MLSTM_HEADS + h:MLSTM_HEADS + h + 1]
        b_row = brow_all[MLSTM_HEADS + h:MLSTM_HEADS + h + 1, :]
        c_old = c_ref[h]
        n_old = n_ref[h]
        m_old = m_ref[h][:, 0:1]

        d_log = jnp.where(causal, b_col - b_row + ic_row, -jnp.inf)
        inter = b_col + m_old
        m_t = jnp.maximum(inter, jnp.max(d_log, axis=1, keepdims=True))
        yield True
        w_intra = jnp.exp(d_log - m_t)
        w_inter = jnp.exp(inter - m_t)
        qb = q.astype(BF16)
        s = _dot_nt(qb, k.astype(BF16)) * w_intra
        yield True
        num = _dot(s.astype(BF16), v) + w_inter * _dot(qb, c_old.astype(BF16))
        den = jnp.sum(s, axis=1, keepdims=True) + w_inter * jnp.sum(q * n_old, axis=1, keepdims=True)
        yield True
        hh = num / jnp.maximum(jnp.abs(den), jnp.exp(-m_t))

        b_last = b_col[L - 1:L, :]
        w_log_col = b_last - b_col + ic_col
        m_new = jnp.maximum(b_last + m_old, jnp.max(w_log_col, axis=0, keepdims=True))
        wk = jnp.exp(w_log_col - m_new) * k
        cdec = jnp.exp(b_last + m_old - m_new)
        yield True
        c_ref[h] = cdec * c_old + _dot_tn(wk.astype(BF16), v)
        n_ref[h] = cdec * n_old + jnp.sum(wk, axis=0, keepdims=True)
        m_ref[h] = jnp.broadcast_to(m_new, (1, HEAD_DIM))
        yield True
        o = p_ref[:, OFF_MO + lo:OFF_MO + lo + HEAD_DIM]
        y_ref[:, RET_WIDTH + lo:RET_WIDTH + lo + HEAD_DIM] = (
            _sigmoid(o) * _layer_norm(hh, gml_ref[:, lo:lo + HEAD_DIM])).astype(y_ref.dtype)

    return [retention_head(h) for h in range(RET_HEADS)] + [mlstm_head(h) for h in range(MLSTM_HEADS)]


def _mixer_consts(L, pos):
    f32 = np.float32
    half = HEAD_DIM // 2
    inv_freq = np.power(f32(ROPE_BASE), -np.arange(half, dtype=f32) / f32(half)).astype(f32)
    ang = (pos.astype(f32)[:, None] * inv_freq[None, :]).astype(f32)
    cos = np.concatenate([np.cos(ang), np.cos(ang)], axis=-1).astype(f32)
    sin = np.concatenate([-np.sin(ang), np.sin(ang)], axis=-1).astype(f32)
    log_gamma = np.log1p(-np.exp2(-5.0 - np.arange(RET_HEADS, dtype=np.float64)))
    idx = np.arange(L, dtype=np.float64)
    rel = idx[:, None] - idx[None, :]
    dec = np.where(rel >= 0, np.exp(log_gamma[:, None, None] * np.maximum(rel, 0.0)), 0.0)
    qd = np.broadcast_to(np.exp(log_gamma[:, None] * (idx + 1.0))[..., None], (RET_HEADS, L, HEAD_DIM))
    kd = np.broadcast_to(np.exp(log_gamma[:, None] * (L - 1.0 - idx))[..., None], (RET_HEADS, L, HEAD_DIM))
    cd = np.broadcast_to(np.exp(log_gamma * L)[:, None, None], (RET_HEADS, 1, HEAD_DIM))
    tril = rel >= 0
    triu = rel <= 0
    return tuple(jnp.asarray(a, F32) for a in (cos, sin, dec, qd, kd, cd, tril, triu))


def _mixer_call(x3, mod3, g_mix, w_main_bf, w_gate, w_gate_t, consts, w_conv, b_conv, g_ret, g_mlstm, bg_col, bg_row,
                s0, c0, n0, m0, conv0, L, gs, outproj=None):
    cos, sin, dec, qd, kd, cd, tril, triu = consts
    groups, seq, d = x3.shape
    nc = seq // L
    hd = HEAD_DIM
    full = lambda *shape: pl.BlockSpec(shape, lambda g, c: (0,) * len(shape))
    mod_spec = lambda k: pl.BlockSpec((gs, 1, d), lambda g, c: (g, 0, k))
    state4 = pl.BlockSpec((gs, RET_HEADS, hd, hd), lambda g, c: (g, 0, 0, 0))
    vec4 = pl.BlockSpec((gs, MLSTM_HEADS, 1, hd), lambda g, c: (g, 0, 0, 0))
    tail3 = pl.BlockSpec((gs, CONV_TAIL_ROWS, 2 * MLSTM_WIDTH), lambda g, c: (g, 0, 0))
    row = lambda g, c: (g, c, 0)
    in_specs = [pl.BlockSpec((gs, L, d), row), mod_spec(0), mod_spec(1),
                full(1, 1, d), full(*w_main_bf.shape), full(d, 128), full(N_GATES, d),
                pl.BlockSpec((L, hd), lambda g, c: (c, 0)),
                pl.BlockSpec((L, hd), lambda g, c: (c, 0)),
                full(RET_HEADS, L, L), full(RET_HEADS, L, hd), full(RET_HEADS, L, hd), full(RET_HEADS, 1, hd),
                full(L, L), full(L, L),
                full(CONV_WIDTH, 2 * MLSTM_WIDTH), full(1, 2 * MLSTM_WIDTH),
                full(1, RET_WIDTH), full(1, MLSTM_WIDTH), full(1, N_GATES), full(N_GATES, 1),
                state4, state4, vec4, vec4, tail3]
    args = [x3, mod3, mod3, g_mix.reshape(1, 1, d), w_main_bf, w_gate, w_gate_t,
            cos, sin, dec, qd, kd, cd, tril, triu, w_conv, b_conv.reshape(1, -1),
            g_ret.reshape(1, -1), g_mlstm.reshape(1, -1), bg_col, bg_row, s0, c0, n0, m0, conv0]
    assert len(in_specs) == len(args) == N_MIXER_INPUTS
    state_specs = [state4, state4, vec4, vec4, tail3]
    state_shapes = [jax.ShapeDtypeStruct((groups, RET_HEADS, hd, hd), F32),
                    jax.ShapeDtypeStruct((groups, MLSTM_HEADS, hd, hd), F32),
                    jax.ShapeDtypeStruct((groups, MLSTM_HEADS, 1, hd), F32),
                    jax.ShapeDtypeStruct((groups, MLSTM_HEADS, 1, hd), F32),
                    jax.ShapeDtypeStruct((groups, CONV_TAIL_ROWS, 2 * MLSTM_WIDTH), F32)]
    scratch = [pltpu.VMEM((gs * L, MAIN_COLS), F32), pltpu.VMEM((gs, CONV_TAIL_ROWS + L, 2 * MLSTM_WIDTH), F32)]
    y_block, y_shape = (gs, L, RET_WIDTH + MLSTM_WIDTH), (groups, seq, RET_WIDTH + MLSTM_WIDTH)
    if outproj is None:
        kernel_fn = _mixer_kernel
        out_specs = [pl.BlockSpec(y_block, row)] + state_specs
        out_shape = [jax.ShapeDtypeStruct(y_shape, F32)] + state_shapes
    else:
        g_ffn, w_out_bf, w_router_t, b_router, ustrict = outproj
        tm = gs * L
        n = groups * seq
        tile = lambda g, c: g * nc + c
        kernel_fn = _mixer_outproj_kernel
        in_specs += [mod_spec(2), mod_spec(3), mod_spec(4),
                     full(1, 1, d), full(d, d), full(N_EXPERTS, d), full(N_EXPERTS, 1), full(tm, tm)]
        args += [mod3, mod3, mod3, g_ffn.reshape(1, 1, d), w_out_bf, w_router_t, b_router.reshape(N_EXPERTS, 1),
                 ustrict]
        assert len(args) == N_MIXER_INPUTS + N_OUTPROJ_INPUTS
        out_specs = [pl.BlockSpec((gs, L, d), row),
                     pl.BlockSpec((tm, d + SIDE_LANES), lambda g, c: (tile(g, c), 0)),
                     pl.BlockSpec((TOP_K, tm), lambda g, c: (0, tile(g, c))),
                     pl.BlockSpec((TOP_K, tm), lambda g, c: (0, tile(g, c))),
                     pl.BlockSpec((1, N_EXPERTS, 128), lambda g, c: (tile(g, c), 0, 0))] + state_specs
        out_shape = [jax.ShapeDtypeStruct((groups, seq, d), F32),
                     jax.ShapeDtypeStruct((n, d + SIDE_LANES), BF16),
                     jax.ShapeDtypeStruct((TOP_K, n), I32),
                     jax.ShapeDtypeStruct((TOP_K, n), I32),
                     jax.ShapeDtypeStruct((n // tm, N_EXPERTS, 128), F32)] + state_shapes
        scratch = scratch + [pltpu.VMEM(y_block, F32)]
    return pl.pallas_call(
        kernel_fn,
        grid=(groups // gs, nc),
        in_specs=in_specs, out_specs=out_specs, out_shape=out_shape, scratch_shapes=scratch,
        compiler_params=_params(2), name="mixer",
    )(*args)


def _outproj_kernel(y_ref, x_ref, gt_ref, sh_ref, sc_ref, g_ref, w_ref, wrt_ref, br_ref, ustrict_ref,
                    x1_ref, h2_ref, idx_ref, rank_ref, cnt_ref):
    _outproj_router(y_ref[...], x_ref, gt_ref, sh_ref, sc_ref, g_ref, w_ref, wrt_ref, br_ref, ustrict_ref,
                    x1_ref, h2_ref, idx_ref, rank_ref, cnt_ref)


def _outproj_router(y, x_ref, gt_ref, sh_ref, sc_ref, g_ref, w_ref, wrt_ref, br_ref, ustrict_ref,
                    x1_ref, h2_ref, idx_ref, rank_ref, cnt_ref):
    x = x_ref[...]
    gb, tt, d = x.shape
    tm = gb * tt
    mixed = _dot(y.astype(BF16), w_ref[...])
    x1 = x + gt_ref[...] * mixed.reshape(gb, tt, d)
    x1_ref[...] = x1
    h2 = (_rms(x1, g_ref[...]) * (1.0 + sc_ref[...]) + sh_ref[...]).reshape(tm, d)
    h2_ref[:, :d] = h2.astype(BF16)

    work = _dot3(wrt_ref[...], h2, dot=_dot_nt) + br_ref[...]
    e_iota = lax.broadcasted_iota(I32, work.shape, 0).astype(F32)
    vals, idxs, sels = [], [], []
    for _ in range(TOP_K):
        mx = jnp.max(work, axis=0, keepdims=True)
        ik = jnp.min(jnp.where(work == mx, e_iota, float(N_EXPERTS)), axis=0, keepdims=True)
        sel = e_iota == ik
        vals.append(mx)
        idxs.append(ik)
        sels.append(sel)
        work = jnp.where(sel, -jnp.inf, work)
    exps = [jnp.exp(v - vals[0]) for v in vals]
    denom = exps[0] + exps[1] + exps[2] + exps[3]
    gates = [e / denom for e in exps]
    idx_ref[...] = jnp.concatenate(idxs, axis=0).astype(I32)

    pieces = []
    for gk in gates:
        p1 = gk.astype(BF16)
        r1 = gk - p1.astype(F32)
        p2 = r1.astype(BF16)
        pieces += [p1, p2, (r1 - p2.astype(F32)).astype(BF16)]
    side = jnp.concatenate(pieces + [ik.astype(BF16) for ik in idxs], axis=0)
    eye = (lax.broadcasted_iota(I32, (SIDE_ROWS, SIDE_LANES), 0)
           == lax.broadcasted_iota(I32, (SIDE_ROWS, SIDE_LANES), 1)).astype(BF16)
    h2_ref[:, d:] = _dot_tn(side, eye).astype(BF16)

    mask = (sels[0] | sels[1] | sels[2] | sels[3]).astype(F32)
    before = _dot(mask.astype(BF16), ustrict_ref[...])
    ranks = [jnp.sum(jnp.where(sel, before, 0.0), axis=0, keepdims=True) for sel in sels]
    rank_ref[...] = jnp.concatenate(ranks, axis=0).astype(I32)
    cnt_ref[0] = jnp.broadcast_to(jnp.sum(mask, axis=1, keepdims=True), cnt_ref.shape[1:])


def _outproj_call(y, x3, mod3, g_ffn, w_out_bf, w_router_t, b_router, ustrict, gb, tt):
    g, t, d = x3.shape
    n = g * t
    tm = gb * tt
    tpg = t // tt
    row = lambda i, j: (i * tpg + j, 0)
    col = lambda i, j: (0, i * tpg + j)
    mod_spec = lambda k: pl.BlockSpec((gb, 1, d), lambda i, j: (i, 0, k))
    const = lambda *shape: pl.BlockSpec(shape, lambda i, j: (0,) * len(shape))
    return pl.pallas_call(
        _outproj_kernel,
        grid=(g // gb, tpg),
        in_specs=[pl.BlockSpec((tm, d), row),
                  pl.BlockSpec((gb, tt, d), lambda i, j: (i, j, 0)),
                  mod_spec(2), mod_spec(3), mod_spec(4),
                  const(1, 1, d), const(d, d), const(N_EXPERTS, d), const(N_EXPERTS, 1),
                  const(tm, tm)],
        out_specs=[pl.BlockSpec((gb, tt, d), lambda i, j: (i, j, 0)),
                   pl.BlockSpec((tm, d + SIDE_LANES), row),
                   pl.BlockSpec((TOP_K, tm), col),
                   pl.BlockSpec((TOP_K, tm), col),
                   pl.BlockSpec((1, N_EXPERTS, 128), lambda i, j: (i * tpg + j, 0, 0))],
        out_shape=[jax.ShapeDtypeStruct((g, t, d), F32),
                   jax.ShapeDtypeStruct((n, d + SIDE_LANES), BF16),
                   jax.ShapeDtypeStruct((TOP_K, n), I32),
                   jax.ShapeDtypeStruct((TOP_K, n), I32),
                   jax.ShapeDtypeStruct((n // tm, N_EXPERTS, 128), F32)],
        compiler_params=_params(2), name="outproj_router",
    )(y, x3, mod3, mod3, mod3, g_ffn.reshape(1, 1, d), w_out_bf, w_router_t, b_router.reshape(N_EXPERTS, 1),
      ustrict)


def _chunk_copy(hbm_ref, hbm_row, buf_ref, chunk, sem, to_hbm):
    hbm = hbm_ref.at[pl.ds(pl.multiple_of(hbm_row, RUN_ALIGN), RUN_ALIGN)]
    buf = buf_ref.at[pl.ds(pl.multiple_of(chunk * RUN_ALIGN, RUN_ALIGN), RUN_ALIGN)]
    return pltpu.make_async_copy(buf, hbm, sem) if to_hbm else pltpu.make_async_copy(hbm, buf, sem)


def _tile_chunks(hbm_ref, bufs_ref, starts_ref, nq_ref, sems, tile, to_hbm, wait):
    slot = tile % 2
    first = tile * (bufs_ref.shape[1] // RUN_ALIGN)

    def body(q, c):
        cp = _chunk_copy(hbm_ref, starts_ref[first + q], bufs_ref.at[slot], q, sems.at[slot], to_hbm)
        cp.wait() if wait else cp.start()
        return c

    lax.fori_loop(0, nq_ref[tile], body, 0)


def _dispatch_kernel(starts_ref, nq_ref, meta_ref, rbuf_ref, h2p_ref, h2s_ref, xs_ref, bufs_ref, zero_ref, sems,
                     *, prompt_tiles):
    i = pl.program_id(0)
    tm = h2p_ref.shape[0]
    bm = zero_ref.shape[0]
    n_blocks = xs_ref.shape[0] // bm
    buf_ref = bufs_ref.at[i % 2]

    def zero_copy(row, sem):
        return pltpu.make_async_copy(zero_ref, xs_ref.at[pl.ds(pl.multiple_of(row, bm), bm)], sem)

    def tails(fn):
        def body(e, c):
            @pl.when(meta_ref[e] >= 0)
            def _():
                fn(zero_copy(meta_ref[e], sems.at[2]))
            return c
        lax.fori_loop(0, N_EXPERTS, body, 0)

    def unused(fn):
        def body(b, c):
            fn(zero_copy(b * bm, sems.at[3]))
            return c
        lax.fori_loop(meta_ref[N_EXPERTS], n_blocks, body, 0)

    @pl.when(i == 0)
    def _():
        zero_ref[...] = jnp.zeros(zero_ref.shape, zero_ref.dtype)
        tails(lambda cp: cp.start())
        unused(lambda cp: cp.start())
        tails(lambda cp: cp.wait())

    @pl.when(i == pl.num_programs(0) - 1)
    def _():
        unused(lambda cp: cp.wait())

    def build(h2_ref):
        h2 = h2_ref[...]
        rb = rbuf_ref[...].astype(jnp.int16)
        for c in range(bufs_ref.shape[1] // PERM_ROWS):
            r = (lax.broadcasted_iota(I32, (PERM_ROWS, tm), 0) + c * PERM_ROWS).astype(jnp.int16)
            hit = (r == rb[0:1, :]) | (r == rb[1:2, :]) | (r == rb[2:3, :]) | (r == rb[3:4, :])
            onehot = jnp.where(hit, jnp.ones((), BF16), jnp.zeros((), BF16))
            buf_ref[c * PERM_ROWS:(c + 1) * PERM_ROWS, :] = _dot(onehot, h2).astype(BF16)

    @pl.when(i < prompt_tiles)
    def _():
        build(h2p_ref)

    @pl.when(i >= prompt_tiles)
    def _():
        build(h2s_ref)

    chunks = functools.partial(_tile_chunks, xs_ref, bufs_ref, starts_ref, nq_ref, sems, to_hbm=True)
    chunks(i, wait=False)

    @pl.when(i > 0)
    def _():
        chunks(i - 1, wait=True)

    @pl.when(i == pl.num_programs(0) - 1)
    def _():
        chunks(i, wait=True)


def _dispatch_call(starts, nq, meta, rbuf, h2_p, h2_s, cap):
    d = h2_p.shape[1]
    tm = MOE_TILE
    pt, st = h2_p.shape[0] // tm, h2_s.shape[0] // tm
    return pl.pallas_call(
        functools.partial(_dispatch_kernel, prompt_tiles=pt),
        grid_spec=pltpu.PrefetchScalarGridSpec(
            num_scalar_prefetch=3,
            grid=(pt + st,),
            in_specs=[pl.BlockSpec((TOP_K, tm), lambda i, *_: (0, i)),
                      pl.BlockSpec((tm, d), lambda i, *_: (jnp.minimum(i, pt - 1), 0)),
                      pl.BlockSpec((tm, d), lambda i, *_: (jnp.maximum(i - pt, 0), 0))],
            out_specs=pl.BlockSpec(memory_space=pl.ANY),
            scratch_shapes=[pltpu.VMEM((2, TILE_BUF_ROWS, d), BF16), pltpu.VMEM((EXPERT_BLOCK, d), BF16),
                            pltpu.SemaphoreType.DMA((4,))]),
        out_shape=jax.ShapeDtypeStruct((cap, d), BF16),
        compiler_params=_params(1), name="dispatch",
    )(starts, nq, meta, rbuf, h2_p, h2_s)


def _expert_kernel(be_ref, nu_ref, xs_ref, wup_ref, bup_ref, wdn_ref, bdn_ref, ys_ref, wup_bf, wdn_bf):
    i = pl.program_id(0)
    prev = be_ref[jnp.maximum(i - 1, 0)]

    @pl.when((i == 0) | (be_ref[i] != prev))
    def _():
        def cast(r, c):
            rows = pl.ds(pl.multiple_of(r * 64, 64), 64)
            wup_bf[rows, :] = wup_ref[0, rows, :].astype(BF16)
            wdn_bf[rows, :] = wdn_ref[0, rows, :].astype(BF16)
            return c

        lax.fori_loop(0, D_MODEL // 64, cast, 0)

    @pl.when(i < nu_ref[0])
    def _():
        hu = _dot(xs_ref[:, :D_MODEL], wup_bf[...]) + bup_ref[0]
        gate = jnp.minimum(hu[:, :D_FF], SWIGLU_LIMIT)
        lin = jnp.clip(hu[:, D_FF:], -SWIGLU_LIMIT, SWIGLU_LIMIT)
        glu = gate * _sigmoid(SWIGLU_ALPHA * gate)
        y = _dot(((lin + 1.0) * glu).astype(BF16), wdn_bf[...]) + bdn_ref[0]
        side = xs_ref[:, D_MODEL:].astype(F32)
        e = be_ref[i].astype(F32)
        weight = jnp.zeros((side.shape[0], 1), F32)
        for k in range(TOP_K):
            g_k = side[:, 3 * k:3 * k + 1] + side[:, 3 * k + 1:3 * k + 2] + side[:, 3 * k + 2:3 * k + 3]
            weight = weight + jnp.where(side[:, SIDE_ROWS - TOP_K + k:SIDE_ROWS - TOP_K + k + 1] == e, g_k, 0.0)
        ys_ref[...] = (weight * y).astype(ys_ref.dtype)

    @pl.when(i >= nu_ref[0])
    def _():
        ys_ref[...] = jnp.zeros(ys_ref.shape, ys_ref.dtype)


def _expert_call(block_e, n_used, xs, w_up, b_up, w_down, b_down):
    cap, dw = xs.shape
    d = w_down.shape[2]
    bm = EXPERT_BLOCK
    blk = lambda i, be, nu: (jnp.minimum(i, nu[0] - 1), 0)
    per_e = lambda i, be, nu: (be[i], 0, 0)
    return pl.pallas_call(
        _expert_kernel,
        grid_spec=pltpu.PrefetchScalarGridSpec(
            num_scalar_prefetch=2,
            grid=(cap // bm,),
            in_specs=[pl.BlockSpec((bm, dw), blk),
                      pl.BlockSpec((1, d, 2 * D_FF), per_e),
                      pl.BlockSpec((1, 1, 2 * D_FF), per_e),
                      pl.BlockSpec((1, D_FF, d), per_e),
                      pl.BlockSpec((1, 1, d), per_e)],
            out_specs=pl.BlockSpec((bm, d), lambda i, be, nu: (i, 0)),
            scratch_shapes=[pltpu.VMEM((d, 2 * D_FF), BF16), pltpu.VMEM((D_FF, d), BF16)]),
        out_shape=jax.ShapeDtypeStruct((cap, d), BF16),
        compiler_params=_params(1), name="experts",
    )(block_e, n_used, xs, w_up, b_up.reshape(N_EXPERTS, 1, -1), w_down, b_down.reshape(N_EXPERTS, 1, -1))


def _combine_kernel(starts_ref, nq_ref, x1_ref, gt_ref, rcol_ref, gfin_ref, ys_ref, o_ref, bufs_ref, sems,
                    *, tile0, tiles_per_group):
    x1 = x1_ref[...]
    gb, tt, d = x1.shape
    tm = gb * tt
    step = pl.program_id(0) * tiles_per_group + pl.program_id(1)
    n_steps = pl.num_programs(0) * tiles_per_group
    tile = tile0 + step
    chunks = functools.partial(_tile_chunks, ys_ref, bufs_ref, starts_ref, nq_ref, sems, to_hbm=False)

    @pl.when(step == 0)
    def _():
        bufs_ref[...] = jnp.zeros(bufs_ref.shape, bufs_ref.dtype)
        chunks(tile, wait=False)

    @pl.when(step + 1 < n_steps)
    def _():
        chunks(tile + 1, wait=False)

    chunks(tile, wait=True)
    buf_ref = bufs_ref.at[tile % 2]

    rows = rcol_ref[...].astype(jnp.int16)
    moe = jnp.zeros((tm, d), F32)
    for c in range(bufs_ref.shape[1] // PERM_COLS):
        r = (lax.broadcasted_iota(I32, (tm, PERM_COLS), 1) + c * PERM_COLS).astype(jnp.int16)
        hit = (r == rows[:, 0:1]) | (r == rows[:, 1:2]) | (r == rows[:, 2:3]) | (r == rows[:, 3:4])
        onehot = jnp.where(hit, jnp.ones((), BF16), jnp.zeros((), BF16))
        moe = moe + _dot(onehot, buf_ref[c * PERM_COLS:(c + 1) * PERM_COLS, :])
    xo = x1 + gt_ref[...] * moe.reshape(gb, tt, d)
    o_ref[...] = _rms(xo, gfin_ref[...])


def _combine_call(starts, nq, x1, mod3, rcol, g_final, ys, tok0, gb, tt):
    g, t, d = x1.shape
    tm = gb * tt
    tpg = t // tt
    t0 = tok0 // tm
    tok = lambda i, j, *_: (t0 + i * tpg + j, 0)
    return pl.pallas_call(
        functools.partial(_combine_kernel, tile0=t0, tiles_per_group=tpg),
        grid_spec=pltpu.PrefetchScalarGridSpec(
            num_scalar_prefetch=2,
            grid=(g // gb, tpg),
            in_specs=[pl.BlockSpec((gb, tt, d), lambda i, j, *_: (i, j, 0)),
                      pl.BlockSpec((gb, 1, d), lambda i, j, *_: (i, 0, 5)),
                      pl.BlockSpec((tm, TOP_K), tok),
                      pl.BlockSpec((1, 1, d), lambda i, j, *_: (0, 0, 0)),
                      pl.BlockSpec(memory_space=pl.ANY)],
            out_specs=pl.BlockSpec((gb, tt, d), lambda i, j, *_: (i, j, 0)),
            scratch_shapes=[pltpu.VMEM((2, TILE_BUF_ROWS, d), BF16), pltpu.SemaphoreType.DMA((2,))]),
        out_shape=jax.ShapeDtypeStruct((g, t, d), F32),
        compiler_params=_params(2), name="combine",
    )(starts, nq, x1, mod3, rcol, g_final.reshape(1, 1, d), ys)


def _group_blocking(groups, seq, tile):
    if seq >= tile:
        return 1, tile
    return tile // seq, seq


def kernel(x_prompt, x_sample, c_prompt, c_sample, state_ret, state_mlstm_c, state_mlstm_n, state_mlstm_m, state_conv, w_mod, b_mod, g_mix, g_ffn, w_in, b_igate, b_fgate, w_conv, b_conv, g_ret, g_mlstm, w_out, w_router, b_router, w_up, b_up, w_down, b_down, g_final):
    depth = w_mod.shape[0]
    assert depth == 1, "single-layer trunk"
    bp, tp, d = x_prompt.shape
    bs, ts, _ = x_sample.shape
    n_p, n_s = bp * tp, bs * ts
    hd = HEAD_DIM
    l = 0

    mod = _mod_call(jnp.concatenate([c_prompt, c_sample], axis=0), w_mod[l], b_mod[l])
    mod_p = mod[:bp].reshape(bp, 1, N_MOD * d)
    mod_s = mod[bp:].reshape(bs, 1, N_MOD * d)

    w_main_bf = _cast_bf16(w_in[l])
    w_gate = jnp.pad(w_in[l][:, MAIN_COLS:], ((0, 0), (0, 128 - N_GATES)))
    w_gate_t = w_in[l][:, MAIN_COLS:].T
    w_out_bf = _cast_bf16(w_out[l])
    bg_col = jnp.concatenate([b_igate[l], b_fgate[l]]).reshape(1, N_GATES)
    bg_row = bg_col.reshape(N_GATES, 1)

    groups = (
        (x_prompt, mod_p, min(CHUNK, tp), np.arange(tp),
         jnp.zeros((bp, RET_HEADS, hd, hd), F32), jnp.zeros((bp, MLSTM_HEADS, hd, hd), F32),
         jnp.zeros((bp, MLSTM_HEADS, hd), F32), jnp.zeros((bp, MLSTM_HEADS), F32),
         jnp.zeros((bp, CONV_WIDTH - 1, 2 * MLSTM_WIDTH), F32)),
        (x_sample, mod_s, min(CHUNK, ts), PAST_LEN + np.arange(ts),
         state_ret[l], state_mlstm_c[l], state_mlstm_n[l], state_mlstm_m[l], state_conv[l]),
    )

    ustrict = jnp.asarray(np.arange(MOE_TILE)[:, None] < np.arange(MOE_TILE)[None, :], BF16)
    staged = []
    for x3, mod3, L, pos, s0, c0, n0, m0, conv0 in groups:
        g, t, _ = x3.shape
        conv0p = jnp.pad(conv0.astype(F32), ((0, 0), (CONV_TAIL_ROWS - (CONV_WIDTH - 1), 0), (0, 0)))
        outproj = (g_ffn[l], w_out_bf, w_router[l].T, b_router[l], ustrict)
        chunked = t > L
        gs = MIXER_SEQS_CHUNKED if chunked else MIXER_SEQS_SHORT
        fused = chunked and gs * L == MOE_TILE
        res = _mixer_call(
            x3, mod3, g_mix[l], w_main_bf, w_gate, w_gate_t, _mixer_consts(L, pos),
            w_conv[l], b_conv[l], g_ret[l], g_mlstm[l],
            bg_col, bg_row, s0.astype(F32), c0.astype(F32), n0.astype(F32).reshape(g, MLSTM_HEADS, 1, hd),
            jnp.broadcast_to(m0.astype(F32)[:, :, None, None], (g, MLSTM_HEADS, 1, hd)), conv0p,
            L, gs, outproj if fused else None)
        s_new, c_new, n_new, m_new, tail = res[-5:]
        states = (s_new, c_new, n_new.reshape(g, MLSTM_HEADS, hd), m_new[:, :, 0, 0],
                  tail[:, CONV_TAIL_ROWS - (CONV_WIDTH - 1):, :])
        if fused:
            x1, h2, idx, rank, cnt = res[:5]
            blocking = (gs, L)
        else:
            blocking = _group_blocking(g, t, MOE_TILE)
            x1, h2, idx, rank, cnt = _outproj_call(res[0].reshape(g * t, d), x3, mod3, *outproj, *blocking)
        staged.append((x1, mod3, h2, idx, rank, states, cnt, blocking))

    n_tok = n_p + n_s
    tm, bm, ra = MOE_TILE, EXPERT_BLOCK, RUN_ALIGN
    n_tiles = n_tok // tm
    q_max = TILE_BUF_ROWS // ra
    n_blocks = -(-(n_tok * TOP_K + n_tiles * N_EXPERTS * (ra - 1)) // bm) + N_EXPERTS
    cap = n_blocks * bm
    counts = jnp.concatenate([s[6][:, :, 0] for s in staged], axis=0).astype(I32)
    run = (counts + ra - 1) // ra * ra
    region = jnp.sum(run, axis=0)
    padded = (region + bm - 1) // bm * bm
    pad_end = jnp.cumsum(padded)
    pad_start = pad_end - padded
    run_start = pad_start[None, :] + jnp.cumsum(run, axis=0) - run
    buf_end = jnp.cumsum(run, axis=1)
    buf_start = buf_end - run
    nq = (buf_end[:, -1] // ra).astype(I32)
    chunk_row = jnp.arange(q_max, dtype=I32) * ra
    chunk_e = jnp.minimum(jnp.sum((buf_end[:, None, :] <= chunk_row[None, :, None]).astype(I32), axis=2),
                          N_EXPERTS - 1)
    e_ids = jnp.arange(N_EXPERTS, dtype=I32)
    shift = run_start - buf_start
    starts = jnp.sum(jnp.where(chunk_e[:, :, None] == e_ids, shift[:, None, :], 0), axis=2) + chunk_row[None, :]
    starts = jnp.where(chunk_row[None, :] < buf_end[:, -1:], starts, 0).reshape(-1).astype(I32)

    idx_all = jnp.concatenate([s[3] for s in staged], axis=1)
    rank_all = jnp.concatenate([s[4] for s in staged], axis=1)
    buf_start_tok = jnp.repeat(buf_start, tm, axis=0).T
    rbuf = jnp.sum(jnp.where(idx_all[None] == e_ids[:, None, None], buf_start_tok[:, None, :], 0), axis=0) + rank_all
    rcol = rbuf.T
    block_row = jnp.arange(n_blocks, dtype=I32) * bm
    block_e = jnp.minimum(jnp.sum((pad_end[None, :] <= block_row[:, None]).astype(I32), axis=1), N_EXPERTS - 1)
    n_used = (pad_end[-1:] // bm).astype(I32)
    meta = jnp.concatenate([jnp.where(region > 0, pad_end - bm, -1), n_used]).astype(I32)

    xs = _dispatch_call(starts, nq, meta, rbuf, staged[0][2], staged[1][2], cap)
    ys = _expert_call(block_e, n_used, xs, w_up[l], b_up[l], w_down[l], b_down[l])

    outs = []
    tok0 = 0
    for x1, mod3, *_, blocking in staged:
        outs.append(_combine_call(starts, nq, x1, mod3, rcol, g_final, ys, tok0, *blocking))
        tok0 += x1.shape[0] * x1.shape[1]

    st_p, st_s = staged[0][5], staged[1][5]
    return (outs[0], outs[1]) + tuple(a[None] for a in st_p) + tuple(a[None] for a in st_s)
```

```python
import functools

import numpy as np
import jax
import jax.numpy as jnp
from jax import lax
from jax.experimental import pallas as pl
from jax.experimental.pallas import tpu as pltpu

F32 = jnp.float32
BF16 = jnp.bfloat16
I32 = jnp.int32

D_MODEL = 1024
PAST_LEN = 16384
RET_HEADS = 4
MLSTM_HEADS = 4
HEAD_DIM = 128
RET_WIDTH = RET_HEADS * HEAD_DIM
MLSTM_WIDTH = MLSTM_HEADS * HEAD_DIM
CONV_WIDTH = 4
CHUNK = 128
ROPE_BASE = 10000.0
N_EXPERTS = 32
TOP_K = 4
D_FF = D_MODEL
SWIGLU_LIMIT = 7.0
SWIGLU_ALPHA = 1.702
N_MOD = 6
EPS = 1e-6
MAIN_COLS = 4 * RET_WIDTH + 2 * MLSTM_WIDTH + 2 * MLSTM_WIDTH
N_GATES = 2 * MLSTM_HEADS
OFF_RQ, OFF_RK, OFF_RV, OFF_RG = 0, RET_WIDTH, 2 * RET_WIDTH, 3 * RET_WIDTH
OFF_MQK = 4 * RET_WIDTH
OFF_MV = OFF_MQK + 2 * MLSTM_WIDTH
OFF_MO = OFF_MV + MLSTM_WIDTH

VMEM_LIMIT_BYTES = 56 * 1024 * 1024
MOE_TILE = 512
EXPERT_BLOCK = 512
MIXER_SEQS_CHUNKED = 4
MIXER_SEQS_SHORT = 16
MIXER_INTERLEAVE_SEQS = 2
RUN_ALIGN = 16
SIDE_ROWS = 4 * TOP_K
SIDE_LANES = 128
TILE_BUF_ROWS = MOE_TILE * TOP_K + N_EXPERTS * RUN_ALIGN
PERM_ROWS = 1280
PERM_COLS = 512
CONV_TAIL_ROWS = 8


def _params(n_axes=1):
    return pltpu.CompilerParams(dimension_semantics=("arbitrary",) * n_axes, vmem_limit_bytes=VMEM_LIMIT_BYTES)


def _dot(a, b):
    return jnp.dot(a, b, preferred_element_type=F32)


def _dot_nt(a, b):
    return lax.dot_general(a, b, (((1,), (1,)), ((), ())), preferred_element_type=F32)


def _dot_tn(a, b):
    return lax.dot_general(a, b, (((0,), (0,)), ((), ())), preferred_element_type=F32)


def _split(a):
    hi = a.astype(BF16)
    lo = (a - hi.astype(F32)).astype(BF16)
    return hi, lo


def _dot3(a, b, dot=_dot):
    ah, al = _split(a)
    bh, bl = _split(b)
    return dot(ah, bh) + (dot(al, bh) + dot(ah, bl))


def _sigmoid(x):
    return 0.5 * (jnp.tanh(0.5 * x) + 1.0)


def _log_sigmoid(x):
    return jnp.minimum(x, 0.0) - jnp.log1p(jnp.exp(-jnp.abs(x)))


def _rms(x, g):
    ms = jnp.mean(x * x, axis=-1, keepdims=True)
    return (x * lax.rsqrt(ms + EPS)) * g


def _layer_norm(x, g):
    mu = jnp.mean(x, axis=-1, keepdims=True)
    xc = x - mu
    var = jnp.mean(xc * xc, axis=-1, keepdims=True)
    return xc * lax.rsqrt(var + EPS) * g


def _mod_kernel(c_ref, w_ref, b_ref, o_ref):
    c = c_ref[...]
    o_ref[...] = _dot3(c * _sigmoid(c), w_ref[...]) + b_ref[...]


def _mod_call(c_all, w_mod, b_mod):
    rows, d = c_all.shape
    cols = w_mod.shape[1]
    tn = 1024
    return pl.pallas_call(
        _mod_kernel,
        grid=(cols // tn,),
        in_specs=[pl.BlockSpec((rows, d), lambda j: (0, 0)),
                  pl.BlockSpec((d, tn), lambda j: (0, j)),
                  pl.BlockSpec((1, tn), lambda j: (0, j))],
        out_specs=pl.BlockSpec((rows, tn), lambda j: (0, j)),
        out_shape=jax.ShapeDtypeStruct((rows, cols), F32),
        compiler_params=_params(1), name="mod",
    )(c_all, w_mod, b_mod.reshape(1, cols))


def _mixer_kernel(x_ref, sh_ref, sc_ref, gmix_ref, w_ref, wg_ref, wgt_ref,
                  cos_ref, sin_ref, dec_ref, qd_ref, kd_ref, cd_ref,
                  tril_ref, triu_ref, wconv_ref, bconv_ref, gret_ref, gml_ref, bgc_ref, bgr_ref,
                  s0_ref, c0_ref, n0_ref, m0_ref, conv0_ref,
                  y_ref, s_ref, c_ref, n_ref, m_ref, tail_ref, p_ref, xp_ref):
    @pl.when(pl.program_id(1) == 0)
    def _():
        s_ref[...] = s0_ref[...]
        c_ref[...] = c0_ref[...]
        n_ref[...] = n0_ref[...]
        m_ref[...] = m0_ref[...]
        tail_ref[...] = conv0_ref[...]

    x = x_ref[...]
    gs, L, d = x.shape
    h = (_rms(x, gmix_ref[...]) * (1.0 + sc_ref[...]) + sh_ref[...]).reshape(gs * L, d)
    hb = h.astype(BF16)
    for j in range(MAIN_COLS // 1024):
        p_ref[:, j * 1024:(j + 1) * 1024] = _dot(hb, w_ref[:, j * 1024:(j + 1) * 1024])

    chains = []
    for sq in range(gs):
        h_sq = h[sq * L:(sq + 1) * L, :]
        gc = _dot3(h_sq, wg_ref[...])[:, :N_GATES]
        gr = _dot3(wgt_ref[...], h_sq, dot=_dot_nt)
        chains += _mixer_sequence(p_ref.at[pl.ds(sq * L, L)], gc, gr, cos_ref, sin_ref, dec_ref, qd_ref, kd_ref,
                                  cd_ref, tril_ref, triu_ref, wconv_ref, bconv_ref, gret_ref, gml_ref, bgc_ref,
                                  bgr_ref, y_ref.at[sq], s_ref.at[sq], c_ref.at[sq], n_ref.at[sq], m_ref.at[sq],
                                  tail_ref.at[sq], xp_ref.at[sq])
    width = (gs if L == CHUNK else MIXER_INTERLEAVE_SEQS) * (RET_HEADS + MLSTM_HEADS)
    for first in range(0, len(chains), width):
        group = chains[first:first + width]
        while group:
            alive = []
            for chain in group:
                if next(chain, None) is not None:
                    alive.append(chain)
            group = alive


N_MIXER_INPUTS = 26
N_OUTPROJ_INPUTS = 8


def _mixer_outproj_kernel(*refs):
    mixer_in = refs[:N_MIXER_INPUTS]
    outproj_in = refs[N_MIXER_INPUTS:N_MIXER_INPUTS + N_OUTPROJ_INPUTS]
    x1_ref, h2_ref, idx_ref, rank_ref, cnt_ref, s_ref, c_ref, n_ref, m_ref, tail_ref, p_ref, xp_ref, y_ref = refs[
        N_MIXER_INPUTS + N_OUTPROJ_INPUTS:]
    _mixer_kernel(*mixer_in, y_ref, s_ref, c_ref, n_ref, m_ref, tail_ref, p_ref, xp_ref)
    gs, L, d = y_ref.shape
    _outproj_router(y_ref[...].reshape(gs * L, d), mixer_in[0], *outproj_in, x1_ref, h2_ref, idx_ref, rank_ref, cnt_ref)


def _mixer_sequence(p_ref, gc, gr, cos_ref, sin_ref, dec_ref, qd_ref, kd_ref, cd_ref,
                    tril_ref, triu_ref, wconv_ref, bconv_ref, gret_ref, gml_ref, bgc_ref, bgr_ref,
                    y_ref, s_ref, c_ref, n_ref, m_ref, tail_ref, xp_ref):
    L = p_ref.shape[0]
    cos = cos_ref[...]
    sin = sin_ref[...]
    scale = HEAD_DIM ** -0.5

    def rot(x):
        return x * cos + pltpu.roll(x, HEAD_DIM // 2, axis=1) * sin

    def retention_head(h):
        lo = h * HEAD_DIM
        q = rot(p_ref[:, OFF_RQ + lo:OFF_RQ + lo + HEAD_DIM])
        k = rot(p_ref[:, OFF_RK + lo:OFF_RK + lo + HEAD_DIM]) * scale
        v = p_ref[:, OFF_RV + lo:OFF_RV + lo + HEAD_DIM].astype(BF16)
        s_old = s_ref[h]
        yield True
        scores = _dot_nt(q.astype(BF16), k.astype(BF16)) * dec_ref[h]
        yield True
        out = _dot(scores.astype(BF16), v) + _dot((q * qd_ref[h]).astype(BF16), s_old.astype(BF16))
        s_ref[h] = cd_ref[h] * s_old + _dot_tn((k * kd_ref[h]).astype(BF16), v)
        yield True
        g = p_ref[:, OFF_RG + lo:OFF_RG + lo + HEAD_DIM]
        y_ref[:, lo:lo + HEAD_DIM] = ((g * _sigmoid(g)) * _layer_norm(out, gret_ref[:, lo:lo + HEAD_DIM])).astype(y_ref.dtype)

    xp_ref[0:CONV_TAIL_ROWS, :] = tail_ref[...]
    xp_ref[CONV_TAIL_ROWS:CONV_TAIL_ROWS + L, :] = p_ref[:, OFF_MQK:OFF_MQK + 2 * MLSTM_WIDTH]
    acc = bconv_ref[...] + wconv_ref[0:1, :] * xp_ref[CONV_TAIL_ROWS - 3:CONV_TAIL_ROWS - 3 + L, :]
    for j in range(1, CONV_WIDTH):
        acc = acc + wconv_ref[j:j + 1, :] * xp_ref[CONV_TAIL_ROWS - 3 + j:CONV_TAIL_ROWS - 3 + j + L, :]
    tail_ref[...] = xp_ref[L:L + CONV_TAIL_ROWS, :]
    xp_ref[CONV_TAIL_ROWS:CONV_TAIL_ROWS + L, :] = acc * _sigmoid(acc)

    gcol = gc + bgc_ref[...]
    is_f_col = lax.broadcasted_iota(I32, gcol.shape, 1) >= MLSTM_HEADS
    gcol = jnp.where(is_f_col, _log_sigmoid(gcol), gcol)
    grow = gr + bgr_ref[...]
    is_f_row = lax.broadcasted_iota(I32, grow.shape, 0) >= MLSTM_HEADS
    grow = jnp.where(is_f_row, _log_sigmoid(grow), grow)
    bcol_all = _dot3(tril_ref[...], gcol)
    brow_all = _dot3(grow, triu_ref[...])
    causal = lax.broadcasted_iota(I32, (L, L), 0) >= lax.broadcasted_iota(I32, (L, L), 1)

    def mlstm_head(h):
        lo = h * HEAD_DIM
        q = xp_ref[CONV_TAIL_ROWS:CONV_TAIL_ROWS + L, lo:lo + HEAD_DIM]
        k = xp_ref[CONV_TAIL_ROWS:CONV_TAIL_ROWS + L, MLSTM_WIDTH + lo:MLSTM_WIDTH + lo + HEAD_DIM] * scale
        v = p_ref[:, OFF_MV + lo:OFF_MV + lo + HEAD_DIM].astype(BF16)
        ic_col = gcol[:, h:h + 1]
        ic_row = grow[h:h + 1, :]
        b_col = bcol_all[:, MLSTM_HEADS + h:MLSTM_HEADS + h + 1]
        b_row = brow_all[MLSTM_HEADS + h:MLSTM_HEADS + h + 1, :]
        c_old = c_ref[h]
        n_old = n_ref[h]
        m_old = m_ref[h][:, 0:1]

        d_log = jnp.where(causal, b_col - b_row + ic_row, -jnp.inf)
        inter = b_col + m_old
        m_t = jnp.maximum(inter, jnp.max(d_log, axis=1, keepdims=True))
        yield True
        w_intra = jnp.exp(d_log - m_t)
        w_inter = jnp.exp(inter - m_t)
        qb = q.astype(BF16)
        s = _dot_nt(qb, k.astype(BF16)) * w_intra
        yield True
        num = _dot(s.astype(BF16), v) + w_inter * _dot(qb, c_old.astype(BF16))
        den = jnp.sum(s, axis=1, keepdims=True) + w_inter * jnp.sum(q * n_old, axis=1, keepdims=True)
        yield True
        hh = num / jnp.maximum(jnp.abs(den), jnp.exp(-m_t))

        b_last = b_col[L - 1:L, :]
        w_log_col = b_last - b_col + ic_col
        m_new = jnp.maximum(b_last + m_old, jnp.max(w_log_col, axis=0, keepdims=True))
        wk = jnp.exp(w_log_col - m_new) * k
        cdec = jnp.exp(b_last + m_old - m_new)
        yield True
        c_ref[h] = cdec * c_old + _dot_tn(wk.astype(BF16), v)
        n_ref[h] = cdec * n_old + jnp.sum(wk, axis=0, keepdims=True)
        m_ref[h] = jnp.broadcast_to(m_new, (1, HEAD_DIM))
        yield True
        o = p_ref[:, OFF_MO + lo:OFF_MO + lo + HEAD_DIM]
        y_ref[:, RET_WIDTH + lo:RET_WIDTH + lo + HEAD_DIM] = (
            _sigmoid(o) * _layer_norm(hh, gml_ref[:, lo:lo + HEAD_DIM])).astype(y_ref.dtype)

    return [retention_head(h) for h in range(RET_HEADS)] + [mlstm_head(h) for h in range(MLSTM_HEADS)]


def _mixer_consts(L, pos):
    f32 = np.float32
    half = HEAD_DIM // 2
    inv_freq = np.power(f32(ROPE_BASE), -np.arange(half, dtype=f32) / f32(half)).astype(f32)
    ang = (pos.astype(f32)[:, None] * inv_freq[None, :]).astype(f32)
    cos = np.concatenate([np.cos(ang), np.cos(ang)], axis=-1).astype(f32)
    sin = np.concatenate([-np.sin(ang), np.sin(ang)], axis=-1).astype(f32)
    log_gamma = np.log1p(-np.exp2(-5.0 - np.arange(RET_HEADS, dtype=np.float64)))
    idx = np.arange(L, dtype=np.float64)
    rel = idx[:, None] - idx[None, :]
    dec = np.where(rel >= 0, np.exp(log_gamma[:, None, None] * np.maximum(rel, 0.0)), 0.0)
    qd = np.broadcast_to(np.exp(log_gamma[:, None] * (idx + 1.0))[..., None], (RET_HEADS, L, HEAD_DIM))
    kd = np.broadcast_to(np.exp(log_gamma[:, None] * (L - 1.0 - idx))[..., None], (RET_HEADS, L, HEAD_DIM))
    cd = np.broadcast_to(np.exp(log_gamma * L)[:, None, None], (RET_HEADS, 1, HEAD_DIM))
    tril = rel >= 0
    triu = rel <= 0
    return tuple(jnp.asarray(a, F32) for a in (cos, sin, dec, qd, kd, cd, tril, triu))


def _mixer_call(x3, mod3, g_mix, w_main_bf, w_gate, w_gate_t, consts, w_conv, b_conv, g_ret, g_mlstm, bg_col, bg_row,
                s0, c0, n0, m0, conv0, L, gs, outproj=None):
    cos, sin, dec, qd, kd, cd, tril, triu = consts
    groups, seq, d = x3.shape
    nc = seq // L
    hd = HEAD_DIM
    full = lambda *shape: pl.BlockSpec(shape, lambda g, c: (0,) * len(shape))
    mod_spec = lambda k: pl.BlockSpec((gs, 1, d), lambda g, c: (g, 0, k))
    state4 = pl.BlockSpec((gs, RET_HEADS, hd, hd), lambda g, c: (g, 0, 0, 0))
    vec4 = pl.BlockSpec((gs, MLSTM_HEADS, 1, hd), lambda g, c: (g, 0, 0, 0))
    tail3 = pl.BlockSpec((gs, CONV_TAIL_ROWS, 2 * MLSTM_WIDTH), lambda g, c: (g, 0, 0))
    row = lambda g, c: (g, c, 0)
    in_specs = [pl.BlockSpec((gs, L, d), row), mod_spec(0), mod_spec(1),
                full(1, 1, d), full(*w_main_bf.shape), full(d, 128), full(N_GATES, d),
                pl.BlockSpec((L, hd), lambda g, c: (c, 0)),
                pl.BlockSpec((L, hd), lambda g, c: (c, 0)),
                full(RET_HEADS, L, L), full(RET_HEADS, L, hd), full(RET_HEADS, L, hd), full(RET_HEADS, 1, hd),
                full(L, L), full(L, L),
                full(CONV_WIDTH, 2 * MLSTM_WIDTH), full(1, 2 * MLSTM_WIDTH),
                full(1, RET_WIDTH), full(1, MLSTM_WIDTH), full(1, N_GATES), full(N_GATES, 1),
                state4, state4, vec4, vec4, tail3]
    args = [x3, mod3, mod3, g_mix.reshape(1, 1, d), w_main_bf, w_gate, w_gate_t,
            cos, sin, dec, qd, kd, cd, tril, triu, w_conv, b_conv.reshape(1, -1),
            g_ret.reshape(1, -1), g_mlstm.reshape(1, -1), bg_col, bg_row, s0, c0, n0, m0, conv0]
    assert len(in_specs) == len(args) == N_MIXER_INPUTS
    state_specs = [state4, state4, vec4, vec4, tail3]
    state_shapes = [jax.ShapeDtypeStruct((groups, RET_HEADS, hd, hd), F32),
                    jax.ShapeDtypeStruct((groups, MLSTM_HEADS, hd, hd), F32),
                    jax.ShapeDtypeStruct((groups, MLSTM_HEADS, 1, hd), F32),
                    jax.ShapeDtypeStruct((groups, MLSTM_HEADS, 1, hd), F32),
                    jax.ShapeDtypeStruct((groups, CONV_TAIL_ROWS, 2 * MLSTM_WIDTH), F32)]
    scratch = [pltpu.VMEM((gs * L, MAIN_COLS), F32), pltpu.VMEM((gs, CONV_TAIL_ROWS + L, 2 * MLSTM_WIDTH), F32)]
    y_block, y_shape = (gs, L, RET_WIDTH + MLSTM_WIDTH), (groups, seq, RET_WIDTH + MLSTM_WIDTH)
    if outproj is None:
        kernel_fn = _mixer_kernel
        out_specs = [pl.BlockSpec(y_block, row)] + state_specs
        out_shape = [jax.ShapeDtypeStruct(y_shape, F32)] + state_shapes
    else:
        g_ffn, w_out_bf, w_router_t, b_router, ustrict = outproj
        tm = gs * L
        n = groups * seq
        tile = lambda g, c: g * nc + c
        kernel_fn = _mixer_outproj_kernel
        in_specs += [mod_spec(2), mod_spec(3), mod_spec(4),
                     full(1, 1, d), full(d, d), full(N_EXPERTS, d), full(N_EXPERTS, 1), full(tm, tm)]
        args += [mod3, mod3, mod3, g_ffn.reshape(1, 1, d), w_out_bf, w_router_t, b_router.reshape(N_EXPERTS, 1),
                 ustrict]
        assert len(args) == N_MIXER_INPUTS + N_OUTPROJ_INPUTS
        out_specs = [pl.BlockSpec((gs, L, d), row),
                     pl.BlockSpec((tm, d + SIDE_LANES), lambda g, c: (tile(g, c), 0)),
                     pl.BlockSpec((TOP_K, tm), lambda g, c: (0, tile(g, c))),
                     pl.BlockSpec((TOP_K, tm), lambda g, c: (0, tile(g, c))),
                     pl.BlockSpec((1, N_EXPERTS, 128), lambda g, c: (tile(g, c), 0, 0))] + state_specs
        out_shape = [jax.ShapeDtypeStruct((groups, seq, d), F32),
                     jax.ShapeDtypeStruct((n, d + SIDE_LANES), BF16),
                     jax.ShapeDtypeStruct((TOP_K, n), I32),
                     jax.ShapeDtypeStruct((TOP_K, n), I32),
                     jax.ShapeDtypeStruct((n // tm, N_EXPERTS, 128), F32)] + state_shapes
        scratch = scratch + [pltpu.VMEM(y_block, F32)]
    return pl.pallas_call(
        kernel_fn,
        grid=(groups // gs, nc),
        in_specs=in_specs, out_specs=out_specs, out_shape=out_shape, scratch_shapes=scratch,
        compiler_params=_params(2), name="mixer",
    )(*args)


def _outproj_kernel(y_ref, x_ref, gt_ref, sh_ref, sc_ref, g_ref, w_ref, wrt_ref, br_ref, ustrict_ref,
                    x1_ref, h2_ref, idx_ref, rank_ref, cnt_ref):
    _outproj_router(y_ref[...], x_ref, gt_ref, sh_ref, sc_ref, g_ref, w_ref, wrt_ref, br_ref, ustrict_ref,
                    x1_ref, h2_ref, idx_ref, rank_ref, cnt_ref)


def _outproj_router(y, x_ref, gt_ref, sh_ref, sc_ref, g_ref, w_ref, wrt_ref, br_ref, ustrict_ref,
                    x1_ref, h2_ref, idx_ref, rank_ref, cnt_ref):
    x = x_ref[...]
    gb, tt, d = x.shape
    tm = gb * tt
    mixed = _dot(y.astype(BF16), w_ref[...])
    x1 = x + gt_ref[...] * mixed.reshape(gb, tt, d)
    x1_ref[...] = x1
    h2 = (_rms(x1, g_ref[...]) * (1.0 + sc_ref[...]) + sh_ref[...]).reshape(tm, d)
    h2_ref[:, :d] = h2.astype(BF16)

    work = _dot3(wrt_ref[...], h2, dot=_dot_nt) + br_ref[...]
    e_iota = lax.broadcasted_iota(I32, work.shape, 0).astype(F32)
    vals, idxs, sels = [], [], []
    for _ in range(TOP_K):
        mx = jnp.max(work, axis=0, keepdims=True)
        ik = jnp.min(jnp.where(work == mx, e_iota, float(N_EXPERTS)), axis=0, keepdims=True)
        sel = e_iota == ik
        vals.append(mx)
        idxs.append(ik)
        sels.append(sel)
        work = jnp.where(sel, -jnp.inf, work)
    exps = [jnp.exp(v - vals[0]) for v in vals]
    denom = exps[0] + exps[1] + exps[2] + exps[3]
    gates = [e / denom for e in exps]
    idx_ref[...] = jnp.concatenate(idxs, axis=0).astype(I32)

    pieces = []
    for gk in gates:
        p1 = gk.astype(BF16)
        r1 = gk - p1.astype(F32)
        p2 = r1.astype(BF16)
        pieces += [p1, p2, (r1 - p2.astype(F32)).astype(BF16)]
    side = jnp.concatenate(pieces + [ik.astype(BF16) for ik in idxs], axis=0)
    eye = (lax.broadcasted_iota(I32, (SIDE_ROWS, SIDE_LANES), 0)
           == lax.broadcasted_iota(I32, (SIDE_ROWS, SIDE_LANES), 1)).astype(BF16)
    h2_ref[:, d:] = _dot_tn(side, eye).astype(BF16)

    mask = (sels[0] | sels[1] | sels[2] | sels[3]).astype(F32)
    before = _dot(mask.astype(BF16), ustrict_ref[...])
    ranks = [jnp.sum(jnp.where(sel, before, 0.0), axis=0, keepdims=True) for sel in sels]
    rank_ref[...] = jnp.concatenate(ranks, axis=0).astype(I32)
    cnt_ref[0] = jnp.broadcast_to(jnp.sum(mask, axis=1, keepdims=True), cnt_ref.shape[1:])


def _outproj_call(y, x3, mod3, g_ffn, w_out_bf, w_router_t, b_router, ustrict, gb, tt):
    g, t, d = x3.shape
    n = g * t
    tm = gb * tt
    tpg = t // tt
    row = lambda i, j: (i * tpg + j, 0)
    col = lambda i, j: (0, i * tpg + j)
    mod_spec = lambda k: pl.BlockSpec((gb, 1, d), lambda i, j: (i, 0, k))
    const = lambda *shape: pl.BlockSpec(shape, lambda i, j: (0,) * len(shape))
    return pl.pallas_call(
        _outproj_kernel,
        grid=(g // gb, tpg),
        in_specs=[pl.BlockSpec((tm, d), row),
                  pl.BlockSpec((gb, tt, d), lambda i, j: (i, j, 0)),
                  mod_spec(2), mod_spec(3), mod_spec(4),
                  const(1, 1, d), const(d, d), const(N_EXPERTS, d), const(N_EXPERTS, 1),
                  const(tm, tm)],
        out_specs=[pl.BlockSpec((gb, tt, d), lambda i, j: (i, j, 0)),
                   pl.BlockSpec((tm, d + SIDE_LANES), row),
                   pl.BlockSpec((TOP_K, tm), col),
                   pl.BlockSpec((TOP_K, tm), col),
                   pl.BlockSpec((1, N_EXPERTS, 128), lambda i, j: (i * tpg + j, 0, 0))],
        out_shape=[jax.ShapeDtypeStruct((g, t, d), F32),
                   jax.ShapeDtypeStruct((n, d + SIDE_LANES), BF16),
                   jax.ShapeDtypeStruct((TOP_K, n), I32),
                   jax.ShapeDtypeStruct((TOP_K, n), I32),
                   jax.ShapeDtypeStruct((n // tm, N_EXPERTS, 128), F32)],
        compiler_params=_params(2), name="outproj_router",
    )(y, x3, mod3, mod3, mod3, g_ffn.reshape(1, 1, d), w_out_bf, w_router_t, b_router.reshape(N_EXPERTS, 1),
      ustrict)


def _chunk_copy(hbm_ref, hbm_row, buf_ref, chunk, sem, to_hbm):
    hbm = hbm_ref.at[pl.ds(pl.multiple_of(hbm_row, RUN_ALIGN), RUN_ALIGN)]
    buf = buf_ref.at[pl.ds(pl.multiple_of(chunk * RUN_ALIGN, RUN_ALIGN), RUN_ALIGN)]
    return pltpu.make_async_copy(buf, hbm, sem) if to_hbm else pltpu.make_async_copy(hbm, buf, sem)


def _tile_chunks(hbm_ref, bufs_ref, starts_ref, nq_ref, sems, tile, to_hbm, wait):
    slot = tile % 2
    first = tile * (bufs_ref.shape[1] // RUN_ALIGN)

    def body(q, c):
        cp = _chunk_copy(hbm_ref, starts_ref[first + q], bufs_ref.at[slot], q, sems.at[slot], to_hbm)
        cp.wait() if wait else cp.start()
        return c

    lax.fori_loop(0, nq_ref[tile], body, 0)


def _dispatch_kernel(starts_ref, nq_ref, meta_ref, rbuf_ref, h2p_ref, h2s_ref, xs_ref, bufs_ref, zero_ref, sems,
                     *, prompt_tiles):
    i = pl.program_id(0)
    tm = h2p_ref.shape[0]
    bm = zero_ref.shape[0]
    n_blocks = xs_ref.shape[0] // bm
    buf_ref = bufs_ref.at[i % 2]

    def zero_copy(row, sem):
        return pltpu.make_async_copy(zero_ref, xs_ref.at[pl.ds(pl.multiple_of(row, bm), bm)], sem)

    def tails(fn):
        def body(e, c):
            @pl.when(meta_ref[e] >= 0)
            def _():
                fn(zero_copy(meta_ref[e], sems.at[2]))
            return c
        lax.fori_loop(0, N_EXPERTS, body, 0)

    def unused(fn):
        def body(b, c):
            fn(zero_copy(b * bm, sems.at[3]))
            return c
        lax.fori_loop(meta_ref[N_EXPERTS], n_blocks, body, 0)

    @pl.when(i == 0)
    def _():
        zero_ref[...] = jnp.zeros(zero_ref.shape, zero_ref.dtype)
        tails(lambda cp: cp.start())
        unused(lambda cp: cp.start())
        tails(lambda cp: cp.wait())

    @pl.when(i == pl.num_programs(0) - 1)
    def _():
        unused(lambda cp: cp.wait())

    def build(h2_ref):
        h2 = h2_ref[...]
        rb = rbuf_ref[...].astype(jnp.int16)
        for c in range(bufs_ref.shape[1] // PERM_ROWS):
            r = (lax.broadcasted_iota(I32, (PERM_ROWS, tm), 0) + c * PERM_ROWS).astype(jnp.int16)
            hit = (r == rb[0:1, :]) | (r == rb[1:2, :]) | (r == rb[2:3, :]) | (r == rb[3:4, :])
            onehot = jnp.where(hit, jnp.ones((), BF16), jnp.zeros((), BF16))
            buf_ref[c * PERM_ROWS:(c + 1) * PERM_ROWS, :] = _dot(onehot, h2).astype(BF16)

    @pl.when(i < prompt_tiles)
    def _():
        build(h2p_ref)

    @pl.when(i >= prompt_tiles)
    def _():
        build(h2s_ref)

    chunks = functools.partial(_tile_chunks, xs_ref, bufs_ref, starts_ref, nq_ref, sems, to_hbm=True)
    chunks(i, wait=False)

    @pl.when(i > 0)
    def _():
        chunks(i - 1, wait=True)

    @pl.when(i == pl.num_programs(0) - 1)
    def _():
        chunks(i, wait=True)


def _dispatch_call(starts, nq, meta, rbuf, h2_p, h2_s, cap):
    d = h2_p.shape[1]
    tm = MOE_TILE
    pt, st = h2_p.shape[0] // tm, h2_s.shape[0] // tm
    return pl.pallas_call(
        functools.partial(_dispatch_kernel, prompt_tiles=pt),
        grid_spec=pltpu.PrefetchScalarGridSpec(
            num_scalar_prefetch=3,
            grid=(pt + st,),
            in_specs=[pl.BlockSpec((TOP_K, tm), lambda i, *_: (0, i)),
                      pl.BlockSpec((tm, d), lambda i, *_: (jnp.minimum(i, pt - 1), 0)),
                      pl.BlockSpec((tm, d), lambda i, *_: (jnp.maximum(i - pt, 0), 0))],
            out_specs=pl.BlockSpec(memory_space=pl.ANY),
            scratch_shapes=[pltpu.VMEM((2, TILE_BUF_ROWS, d), BF16), pltpu.VMEM((EXPERT_BLOCK, d), BF16),
                            pltpu.SemaphoreType.DMA((4,))]),
        out_shape=jax.ShapeDtypeStruct((cap, d), BF16),
        compiler_params=_params(1), name="dispatch",
    )(starts, nq, meta, rbuf, h2_p, h2_s)


def _expert_kernel(be_ref, nu_ref, nxt_ref, xs_ref, wup_hbm, bup_ref, wdn_hbm, bdn_ref, ys_ref,
                   wup_f32, wdn_f32, wup_bf, wdn_bf, sems):
    i = pl.program_id(0)
    prev = be_ref[jnp.maximum(i - 1, 0)]

    def weight_copies(e):
        return (pltpu.make_async_copy(wup_hbm.at[e], wup_f32, sems.at[0]),
                pltpu.make_async_copy(wdn_hbm.at[e], wdn_f32, sems.at[1]))

    @pl.when(i == 0)
    def _():
        for cp in weight_copies(be_ref[0]):
            cp.start()

    @pl.when((i < nu_ref[0]) & ((i == 0) | (be_ref[i] != prev)))
    def _():
        for cp in weight_copies(be_ref[i]):
            cp.wait()

        def cast(r, c):
            rows = pl.ds(pl.multiple_of(r * 64, 64), 64)
            wup_bf[rows, :] = wup_f32[rows, :].astype(BF16)
            wdn_bf[rows, :] = wdn_f32[rows, :].astype(BF16)
            return c

        lax.fori_loop(0, D_MODEL // 64, cast, 0)

        @pl.when(nxt_ref[i] >= 0)
        def _():
            for cp in weight_copies(nxt_ref[i]):
                cp.start()

    @pl.when(i < nu_ref[0])
    def _():
        hu = _dot(xs_ref[:, :D_MODEL], wup_bf[...]) + bup_ref[0]
        gate = jnp.minimum(hu[:, :D_FF], SWIGLU_LIMIT)
        lin = jnp.clip(hu[:, D_FF:], -SWIGLU_LIMIT, SWIGLU_LIMIT)
        glu = gate * _sigmoid(SWIGLU_ALPHA * gate)
        y = _dot(((lin + 1.0) * glu).astype(BF16), wdn_bf[...]) + bdn_ref[0]
        side = xs_ref[:, D_MODEL:].astype(F32)
        e = be_ref[i].astype(F32)
        weight = jnp.zeros((side.shape[0], 1), F32)
        for k in range(TOP_K):
            g_k = side[:, 3 * k:3 * k + 1] + side[:, 3 * k + 1:3 * k + 2] + side[:, 3 * k + 2:3 * k + 3]
            weight = weight + jnp.where(side[:, SIDE_ROWS - TOP_K + k:SIDE_ROWS - TOP_K + k + 1] == e, g_k, 0.0)
        ys_ref[...] = (weight * y).astype(ys_ref.dtype)

    @pl.when(i >= nu_ref[0])
    def _():
        ys_ref[...] = jnp.zeros(ys_ref.shape, ys_ref.dtype)


def _expert_call(block_e, n_used, next_e, xs, w_up, b_up, w_down, b_down):
    cap, dw = xs.shape
    d = w_down.shape[2]
    bm = EXPERT_BLOCK
    blk = lambda i, be, nu, nx: (jnp.minimum(i, nu[0] - 1), 0)
    per_e = lambda i, be, nu, nx: (be[i], 0, 0)
    return pl.pallas_call(
        _expert_kernel,
        grid_spec=pltpu.PrefetchScalarGridSpec(
            num_scalar_prefetch=3,
            grid=(cap // bm,),
            in_specs=[pl.BlockSpec((bm, dw), blk),
                      pl.BlockSpec(memory_space=pl.ANY),
                      pl.BlockSpec((1, 1, 2 * D_FF), per_e),
                      pl.BlockSpec(memory_space=pl.ANY),
                      pl.BlockSpec((1, 1, d), per_e)],
            out_specs=pl.BlockSpec((bm, d), lambda i, be, nu, nx: (i, 0)),
            scratch_shapes=[pltpu.VMEM((d, 2 * D_FF), F32), pltpu.VMEM((D_FF, d), F32),
                            pltpu.VMEM((d, 2 * D_FF), BF16), pltpu.VMEM((D_FF, d), BF16),
                            pltpu.SemaphoreType.DMA((2,))]),
        out_shape=jax.ShapeDtypeStruct((cap, d), BF16),
        compiler_params=_params(1), name="experts",
    )(block_e, n_used, next_e, xs, w_up, b_up.reshape(N_EXPERTS, 1, -1), w_down, b_down.reshape(N_EXPERTS, 1, -1))


def _combine_kernel(starts_ref, nq_ref, x1_ref, gt_ref, rcol_ref, gfin_ref, ys_ref, o_ref, bufs_ref, sems,
                    *, tile0, tiles_per_group):
    x1 = x1_ref[...]
    gb, tt, d = x1.shape
    tm = gb * tt
    step = pl.program_id(0) * tiles_per_group + pl.program_id(1)
    n_steps = pl.num_programs(0) * tiles_per_group
    tile = tile0 + step
    chunks = functools.partial(_tile_chunks, ys_ref, bufs_ref, starts_ref, nq_ref, sems, to_hbm=False)

    @pl.when(step == 0)
    def _():
        bufs_ref[...] = jnp.zeros(bufs_ref.shape, bufs_ref.dtype)
        chunks(tile, wait=False)

    @pl.when(step + 1 < n_steps)
    def _():
        chunks(tile + 1, wait=False)

    chunks(tile, wait=True)
    buf_ref = bufs_ref.at[tile % 2]

    rows = rcol_ref[...].astype(jnp.int16)
    moe = jnp.zeros((tm, d), F32)
    for c in range(bufs_ref.shape[1] // PERM_COLS):
        r = (lax.broadcasted_iota(I32, (tm, PERM_COLS), 1) + c * PERM_COLS).astype(jnp.int16)
        hit = (r == rows[:, 0:1]) | (r == rows[:, 1:2]) | (r == rows[:, 2:3]) | (r == rows[:, 3:4])
        onehot = jnp.where(hit, jnp.ones((), BF16), jnp.zeros((), BF16))
        moe = moe + _dot(onehot, buf_ref[c * PERM_COLS:(c + 1) * PERM_COLS, :])
    xo = x1 + gt_ref[...] * moe.reshape(gb, tt, d)
    o_ref[...] = _rms(xo, gfin_ref[...])


def _combine_call(starts, nq, x1, mod3, rcol, g_final, ys, tok0, gb, tt):
    g, t, d = x1.shape
    tm = gb * tt
    tpg = t // tt
    t0 = tok0 // tm
    tok = lambda i, j, *_: (t0 + i * tpg + j, 0)
    return pl.pallas_call(
        functools.partial(_combine_kernel, tile0=t0, tiles_per_group=tpg),
        grid_spec=pltpu.PrefetchScalarGridSpec(
            num_scalar_prefetch=2,
            grid=(g // gb, tpg),
            in_specs=[pl.BlockSpec((gb, tt, d), lambda i, j, *_: (i, j, 0)),
                      pl.BlockSpec((gb, 1, d), lambda i, j, *_: (i, 0, 5)),
                      pl.BlockSpec((tm, TOP_K), tok),
                      pl.BlockSpec((1, 1, d), lambda i, j, *_: (0, 0, 0)),
                      pl.BlockSpec(memory_space=pl.ANY)],
            out_specs=pl.BlockSpec((gb, tt, d), lambda i, j, *_: (i, j, 0)),
            scratch_shapes=[pltpu.VMEM((2, TILE_BUF_ROWS, d), BF16), pltpu.SemaphoreType.DMA((2,))]),
        out_shape=jax.ShapeDtypeStruct((g, t, d), F32),
        compiler_params=_params(2), name="combine",
    )(starts, nq, x1, mod3, rcol, g_final.reshape(1, 1, d), ys)


def _group_blocking(groups, seq, tile):
    if seq >= tile:
        return 1, tile
    return tile // seq, seq


def kernel(x_prompt, x_sample, c_prompt, c_sample, state_ret, state_mlstm_c, state_mlstm_n, state_mlstm_m, state_conv, w_mod, b_mod, g_mix, g_ffn, w_in, b_igate, b_fgate, w_conv, b_conv, g_ret, g_mlstm, w_out, w_router, b_router, w_up, b_up, w_down, b_down, g_final):
    depth = w_mod.shape[0]
    assert depth == 1, "single-layer trunk"
    bp, tp, d = x_prompt.shape
    bs, ts, _ = x_sample.shape
    n_p, n_s = bp * tp, bs * ts
    hd = HEAD_DIM
    l = 0

    mod = _mod_call(jnp.concatenate([c_prompt, c_sample], axis=0), w_mod[l], b_mod[l])
    mod_p = mod[:bp].reshape(bp, 1, N_MOD * d)
    mod_s = mod[bp:].reshape(bs, 1, N_MOD * d)

    w_main_bf = w_in[l].astype(BF16)
    w_gate = jnp.pad(w_in[l][:, MAIN_COLS:], ((0, 0), (0, 128 - N_GATES)))
    w_gate_t = w_in[l][:, MAIN_COLS:].T
    w_out_bf = w_out[l].astype(BF16)
    bg_col = jnp.concatenate([b_igate[l], b_fgate[l]]).reshape(1, N_GATES)
    bg_row = bg_col.reshape(N_GATES, 1)

    groups = (
        (x_prompt, mod_p, min(CHUNK, tp), np.arange(tp),
         jnp.zeros((bp, RET_HEADS, hd, hd), F32), jnp.zeros((bp, MLSTM_HEADS, hd, hd), F32),
         jnp.zeros((bp, MLSTM_HEADS, hd), F32), jnp.zeros((bp, MLSTM_HEADS), F32),
         jnp.zeros((bp, CONV_WIDTH - 1, 2 * MLSTM_WIDTH), F32)),
        (x_sample, mod_s, min(CHUNK, ts), PAST_LEN + np.arange(ts),
         state_ret[l], state_mlstm_c[l], state_mlstm_n[l], state_mlstm_m[l], state_conv[l]),
    )

    ustrict = jnp.asarray(np.arange(MOE_TILE)[:, None] < np.arange(MOE_TILE)[None, :], BF16)
    staged = []
    for x3, mod3, L, pos, s0, c0, n0, m0, conv0 in groups:
        g, t, _ = x3.shape
        conv0p = jnp.pad(conv0.astype(F32), ((0, 0), (CONV_TAIL_ROWS - (CONV_WIDTH - 1), 0), (0, 0)))
        outproj = (g_ffn[l], w_out_bf, w_router[l].T, b_router[l], ustrict)
        chunked = t > L
        gs = MIXER_SEQS_CHUNKED if chunked else MIXER_SEQS_SHORT
        fused = chunked and gs * L == MOE_TILE
        res = _mixer_call(
            x3, mod3, g_mix[l], w_main_bf, w_gate, w_gate_t, _mixer_consts(L, pos),
            w_conv[l], b_conv[l], g_ret[l], g_mlstm[l],
            bg_col, bg_row, s0.astype(F32), c0.astype(F32), n0.astype(F32).reshape(g, MLSTM_HEADS, 1, hd),
            jnp.broadcast_to(m0.astype(F32)[:, :, None, None], (g, MLSTM_HEADS, 1, hd)), conv0p,
            L, gs, outproj if fused else None)
        s_new, c_new, n_new, m_new, tail = res[-5:]
        states = (s_new, c_new, n_new.reshape(g, MLSTM_HEADS, hd), m_new[:, :, 0, 0],
                  tail[:, CONV_TAIL_ROWS - (CONV_WIDTH - 1):, :])
        if fused:
            x1, h2, idx, rank, cnt = res[:5]
            blocking = (gs, L)
        else:
            blocking = _group_blocking(g, t, MOE_TILE)
            x1, h2, idx, rank, cnt = _outproj_call(res[0].reshape(g * t, d), x3, mod3, *outproj, *blocking)
        staged.append((x1, mod3, h2, idx, rank, states, cnt, blocking))

    n_tok = n_p + n_s
    tm, bm, ra = MOE_TILE, EXPERT_BLOCK, RUN_ALIGN
    n_tiles = n_tok // tm
    q_max = TILE_BUF_ROWS // ra
    n_blocks = -(-(n_tok * TOP_K + n_tiles * N_EXPERTS * (ra - 1)) // bm) + N_EXPERTS
    cap = n_blocks * bm
    counts = jnp.concatenate([s[6][:, :, 0] for s in staged], axis=0).astype(I32)
    run = (counts + ra - 1) // ra * ra
    region = jnp.sum(run, axis=0)
    padded = (region + bm - 1) // bm * bm
    pad_end = jnp.cumsum(padded)
    pad_start = pad_end - padded
    run_start = pad_start[None, :] + jnp.cumsum(run, axis=0) - run
    buf_end = jnp.cumsum(run, axis=1)
    buf_start = buf_end - run
    nq = (buf_end[:, -1] // ra).astype(I32)
    chunk_row = jnp.arange(q_max, dtype=I32) * ra
    chunk_e = jnp.minimum(jnp.sum((buf_end[:, None, :] <= chunk_row[None, :, None]).astype(I32), axis=2),
                          N_EXPERTS - 1)
    e_ids = jnp.arange(N_EXPERTS, dtype=I32)
    shift = run_start - buf_start
    starts = jnp.sum(jnp.where(chunk_e[:, :, None] == e_ids, shift[:, None, :], 0), axis=2) + chunk_row[None, :]
    starts = jnp.where(chunk_row[None, :] < buf_end[:, -1:], starts, 0).reshape(-1).astype(I32)

    idx_all = jnp.concatenate([s[3] for s in staged], axis=1)
    rank_all = jnp.concatenate([s[4] for s in staged], axis=1)
    buf_start_tok = jnp.repeat(buf_start, tm, axis=0).T
    rbuf = jnp.sum(jnp.where(idx_all[None] == e_ids[:, None, None], buf_start_tok[:, None, :], 0), axis=0) + rank_all
    rcol = rbuf.T
    block_row = jnp.arange(n_blocks, dtype=I32) * bm
    block_e = jnp.minimum(jnp.sum((pad_end[None, :] <= block_row[:, None]).astype(I32), axis=1), N_EXPERTS - 1)
    n_used = (pad_end[-1:] // bm).astype(I32)
    meta = jnp.concatenate([jnp.where(region > 0, pad_end - bm, -1), n_used]).astype(I32)

    xs = _dispatch_call(starts, nq, meta, rbuf, staged[0][2], staged[1][2], cap)
    later = (e_ids[None, :] > e_ids[:, None]) & (region[None, :] > 0)
    next_nonempty = jnp.min(jnp.where(later, e_ids[None, :], N_EXPERTS), axis=1)
    next_nonempty = jnp.where(next_nonempty < N_EXPERTS, next_nonempty, -1)
    next_e = jnp.sum(jnp.where(block_e[:, None] == e_ids[None, :], next_nonempty[None, :], 0), axis=1).astype(I32)
    ys = _expert_call(block_e, n_used, next_e, xs, w_up[l], b_up[l], w_down[l], b_down[l])

    outs = []
    tok0 = 0
    for x1, mod3, *_, blocking in staged:
        outs.append(_combine_call(starts, nq, x1, mod3, rcol, g_final, ys, tok0, *blocking))
        tok0 += x1.shape[0] * x1.shape[1]

    st_p, st_s = staged[0][5], staged[1][5]
    return (outs[0], outs[1]) + tuple(a[None] for a in st_p) + tuple(a[None] for a in st_s)
```

```python
import functools

import numpy as np
import jax
import jax.numpy as jnp
from jax import lax
from jax.experimental import pallas as pl
from jax.experimental.pallas import tpu as pltpu

F32 = jnp.float32
BF16 = jnp.bfloat16
I32 = jnp.int32

D_MODEL = 1024
PAST_LEN = 16384
RET_HEADS = 4
MLSTM_HEADS = 4
HEAD_DIM = 128
RET_WIDTH = RET_HEADS * HEAD_DIM
MLSTM_WIDTH = MLSTM_HEADS * HEAD_DIM
CONV_WIDTH = 4
CHUNK = 128
ROPE_BASE = 10000.0
N_EXPERTS = 32
TOP_K = 4
D_FF = D_MODEL
SWIGLU_LIMIT = 7.0
SWIGLU_ALPHA = 1.702
N_MOD = 6
EPS = 1e-6
MAIN_COLS = 4 * RET_WIDTH + 2 * MLSTM_WIDTH + 2 * MLSTM_WIDTH
N_GATES = 2 * MLSTM_HEADS
OFF_RQ, OFF_RK, OFF_RV, OFF_RG = 0, RET_WIDTH, 2 * RET_WIDTH, 3 * RET_WIDTH
OFF_MQK = 4 * RET_WIDTH
OFF_MV = OFF_MQK + 2 * MLSTM_WIDTH
OFF_MO = OFF_MV + MLSTM_WIDTH

VMEM_LIMIT_BYTES = 56 * 1024 * 1024
MOE_TILE = 512
EXPERT_BLOCK = 512
MIXER_SEQS_CHUNKED = 4
MIXER_SEQS_SHORT = 16
MIXER_INTERLEAVE_SEQS = 2
RUN_ALIGN = 16
SIDE_ROWS = 4 * TOP_K
SIDE_LANES = 128
TILE_BUF_ROWS = MOE_TILE * TOP_K + N_EXPERTS * RUN_ALIGN
PERM_ROWS = 1280
PERM_COLS = 512
CONV_TAIL_ROWS = 8


def _params(n_axes=1):
    return pltpu.CompilerParams(dimension_semantics=("arbitrary",) * n_axes, vmem_limit_bytes=VMEM_LIMIT_BYTES)


def _dot(a, b):
    return jnp.dot(a, b, preferred_element_type=F32)


def _dot_nt(a, b):
    return lax.dot_general(a, b, (((1,), (1,)), ((), ())), preferred_element_type=F32)


def _dot_tn(a, b):
    return lax.dot_general(a, b, (((0,), (0,)), ((), ())), preferred_element_type=F32)


def _split(a):
    hi = a.astype(BF16)
    lo = (a - hi.astype(F32)).astype(BF16)
    return hi, lo


def _dot3(a, b, dot=_dot):
    ah, al = _split(a)
    bh, bl = _split(b)
    return dot(ah, bh) + (dot(al, bh) + dot(ah, bl))


def _sigmoid(x):
    return 0.5 * (jnp.tanh(0.5 * x) + 1.0)


def _log_sigmoid(x):
    return jnp.minimum(x, 0.0) - jnp.log1p(jnp.exp(-jnp.abs(x)))


def _rms(x, g):
    ms = jnp.mean(x * x, axis=-1, keepdims=True)
    return (x * lax.rsqrt(ms + EPS)) * g


def _layer_norm(x, g):
    mu = jnp.mean(x, axis=-1, keepdims=True)
    xc = x - mu
    var = jnp.mean(xc * xc, axis=-1, keepdims=True)
    return xc * lax.rsqrt(var + EPS) * g


def _mod_kernel(c_ref, w_ref, b_ref, o_ref):
    c = c_ref[...]
    o_ref[...] = _dot3(c * _sigmoid(c), w_ref[...]) + b_ref[...]


def _mod_call(c_all, w_mod, b_mod):
    rows, d = c_all.shape
    cols = w_mod.shape[1]
    tn = 1024
    return pl.pallas_call(
        _mod_kernel,
        grid=(cols // tn,),
        in_specs=[pl.BlockSpec((rows, d), lambda j: (0, 0)),
                  pl.BlockSpec((d, tn), lambda j: (0, j)),
                  pl.BlockSpec((1, tn), lambda j: (0, j))],
        out_specs=pl.BlockSpec((rows, tn), lambda j: (0, j)),
        out_shape=jax.ShapeDtypeStruct((rows, cols), F32),
        compiler_params=_params(1), name="mod",
    )(c_all, w_mod, b_mod.reshape(1, cols))


def _mixer_kernel(x_ref, sh_ref, sc_ref, gmix_ref, w_ref, wg_ref, wgt_ref,
                  cos_ref, sin_ref, dec_ref, qd_ref, kd_ref, cd_ref,
                  tril_ref, triu_ref, wconv_ref, bconv_ref, gret_ref, gml_ref, bgc_ref, bgr_ref,
                  s0_ref, c0_ref, n0_ref, m0_ref, conv0_ref,
                  y_ref, s_ref, c_ref, n_ref, m_ref, tail_ref, p_ref, xp_ref):
    @pl.when(pl.program_id(1) == 0)
    def _():
        s_ref[...] = s0_ref[...]
        c_ref[...] = c0_ref[...]
        n_ref[...] = n0_ref[...]
        m_ref[...] = m0_ref[...]
        tail_ref[...] = conv0_ref[...]

    x = x_ref[...]
    gs, L, d = x.shape
    h = (_rms(x, gmix_ref[...]) * (1.0 + sc_ref[...]) + sh_ref[...]).reshape(gs * L, d)
    hb = h.astype(BF16)
    for j in range(MAIN_COLS // 1024):
        p_ref[:, j * 1024:(j + 1) * 1024] = _dot(hb, w_ref[:, j * 1024:(j + 1) * 1024])

    chains = []
    for sq in range(gs):
        h_sq = h[sq * L:(sq + 1) * L, :]
        gc = _dot3(h_sq, wg_ref[...])[:, :N_GATES]
        gr = _dot3(wgt_ref[...], h_sq, dot=_dot_nt)
        chains += _mixer_sequence(p_ref.at[pl.ds(sq * L, L)], gc, gr, cos_ref, sin_ref, dec_ref, qd_ref, kd_ref,
                                  cd_ref, tril_ref, triu_ref, wconv_ref, bconv_ref, gret_ref, gml_ref, bgc_ref,
                                  bgr_ref, y_ref.at[sq], s_ref.at[sq], c_ref.at[sq], n_ref.at[sq], m_ref.at[sq],
                                  tail_ref.at[sq], xp_ref.at[sq])
    width = (gs if L == CHUNK else MIXER_INTERLEAVE_SEQS) * (RET_HEADS + MLSTM_HEADS)
    for first in range(0, len(chains), width):
        group = chains[first:first + width]
        while group:
            alive = []
            for chain in group:
                if next(chain, None) is not None:
                    alive.append(chain)
            group = alive


N_MIXER_INPUTS = 26
N_OUTPROJ_INPUTS = 8


def _mixer_outproj_kernel(*refs):
    mixer_in = refs[:N_MIXER_INPUTS]
    outproj_in = refs[N_MIXER_INPUTS:N_MIXER_INPUTS + N_OUTPROJ_INPUTS]
    x1_ref, h2_ref, idx_ref, rank_ref, cnt_ref, s_ref, c_ref, n_ref, m_ref, tail_ref, p_ref, xp_ref, y_ref = refs[
        N_MIXER_INPUTS + N_OUTPROJ_INPUTS:]
    _mixer_kernel(*mixer_in, y_ref, s_ref, c_ref, n_ref, m_ref, tail_ref, p_ref, xp_ref)
    gs, L, d = y_ref.shape
    _outproj_router(y_ref[...].reshape(gs * L, d), mixer_in[0], *outproj_in, x1_ref, h2_ref, idx_ref, rank_ref, cnt_ref)


def _mixer_sequence(p_ref, gc, gr, cos_ref, sin_ref, dec_ref, qd_ref, kd_ref, cd_ref,
                    tril_ref, triu_ref, wconv_ref, bconv_ref, gret_ref, gml_ref, bgc_ref, bgr_ref,
                    y_ref, s_ref, c_ref, n_ref, m_ref, tail_ref, xp_ref):
    L = p_ref.shape[0]
    cos = cos_ref[...]
    sin = sin_ref[...]
    scale = HEAD_DIM ** -0.5

    def rot(x):
        return x * cos + pltpu.roll(x, HEAD_DIM // 2, axis=1) * sin

    def retention_head(h):
        lo = h * HEAD_DIM
        q = rot(p_ref[:, OFF_RQ + lo:OFF_RQ + lo + HEAD_DIM])
        k = rot(p_ref[:, OFF_RK + lo:OFF_RK + lo + HEAD_DIM]) * scale
        v = p_ref[:, OFF_RV + lo:OFF_RV + lo + HEAD_DIM].astype(BF16)
        s_old = s_ref[h]
        yield True
        scores = _dot_nt(q.astype(BF16), k.astype(BF16)) * dec_ref[h]
        yield True
        out = _dot(scores.astype(BF16), v) + _dot((q * qd_ref[h]).astype(BF16), s_old.astype(BF16))
        s_ref[h] = cd_ref[h] * s_old + _dot_tn((k * kd_ref[h]).astype(BF16), v)
        yield True
        g = p_ref[:, OFF_RG + lo:OFF_RG + lo + HEAD_DIM]
        y_ref[:, lo:lo + HEAD_DIM] = ((g * _sigmoid(g)) * _layer_norm(out, gret_ref[:, lo:lo + HEAD_DIM])).astype(y_ref.dtype)

    xp_ref[0:CONV_TAIL_ROWS, :] = tail_ref[...]
    xp_ref[CONV_TAIL_ROWS:CONV_TAIL_ROWS + L, :] = p_ref[:, OFF_MQK:OFF_MQK + 2 * MLSTM_WIDTH]
    acc = bconv_ref[...] + wconv_ref[0:1, :] * xp_ref[CONV_TAIL_ROWS - 3:CONV_TAIL_ROWS - 3 + L, :]
    for j in range(1, CONV_WIDTH):
        acc = acc + wconv_ref[j:j + 1, :] * xp_ref[CONV_TAIL_ROWS - 3 + j:CONV_TAIL_ROWS - 3 + j + L, :]
    tail_ref[...] = xp_ref[L:L + CONV_TAIL_ROWS, :]
    xp_ref[CONV_TAIL_ROWS:CONV_TAIL_ROWS + L, :] = acc * _sigmoid(acc)

    gcol = gc + bgc_ref[...]
    is_f_col = lax.broadcasted_iota(I32, gcol.shape, 1) >= MLSTM_HEADS
    gcol = jnp.where(is_f_col, _log_sigmoid(gcol), gcol)
    grow = gr + bgr_ref[...]
    is_f_row = lax.broadcasted_iota(I32, grow.shape, 0) >= MLSTM_HEADS
    grow = jnp.where(is_f_row, _log_sigmoid(grow), grow)
    bcol_all = _dot3(tril_ref[...], gcol)
    brow_all = _dot3(grow, triu_ref[...])
    causal = lax.broadcasted_iota(I32, (L, L), 0) >= lax.broadcasted_iota(I32, (L, L), 1)

    def mlstm_head(h):
        lo = h * HEAD_DIM
        q = xp_ref[CONV_TAIL_ROWS:CONV_TAIL_ROWS + L, lo:lo + HEAD_DIM]
        k = xp_ref[CONV_TAIL_ROWS:CONV_TAIL_ROWS + L, MLSTM_WIDTH + lo:MLSTM_WIDTH + lo + HEAD_DIM] * scale
        v = p_ref[:, OFF_MV + lo:OFF_MV + lo + HEAD_DIM].astype(BF16)
        ic_col = gcol[:, h:h + 1]
        ic_row = grow[h:h + 1, :]
        b_col = bcol_all[:, MLSTM_HEADS + h:MLSTM_HEADS + h + 1]
        b_row = brow_all[MLSTM_HEADS + h:MLSTM_HEADS + h + 1, :]
        c_old = c_ref[h]
        n_old = n_ref[h]
        m_old = m_ref[h][:, 0:1]

        d_log = jnp.where(causal, b_col - b_row + ic_row, -jnp.inf)
        inter = b_col + m_old
        m_t = jnp.maximum(inter, jnp.max(d_log, axis=1, keepdims=True))
        yield True
        w_intra = jnp.exp(d_log - m_t)
        w_inter = jnp.exp(inter - m_t)
        qb = q.astype(BF16)
        s = _dot_nt(qb, k.astype(BF16)) * w_intra
        yield True
        num = _dot(s.astype(BF16), v) + w_inter * _dot(qb, c_old.astype(BF16))
        den = jnp.sum(s, axis=1, keepdims=True) + w_inter * jnp.sum(q * n_old, axis=1, keepdims=True)
        yield True
        hh = num / jnp.maximum(jnp.abs(den), jnp.exp(-m_t))

        b_last = b_col[L - 1:L, :]
        w_log_col = b_last - b_col + ic_col
        m_new = jnp.maximum(b_last + m_old, jnp.max(w_log_col, axis=0, keepdims=True))
        wk = jnp.exp(w_log_col - m_new) * k
        cdec = jnp.exp(b_last + m_old - m_new)
        yield True
        c_ref[h] = cdec * c_old + _dot_tn(wk.astype(BF16), v)
        n_ref[h] = cdec * n_old + jnp.sum(wk, axis=0, keepdims=True)
        m_ref[h] = jnp.broadcast_to(m_new, (1, HEAD_DIM))
        yield True
        o = p_ref[:, OFF_MO + lo:OFF_MO + lo + HEAD_DIM]
        y_ref[:, RET_WIDTH + lo:RET_WIDTH + lo + HEAD_DIM] = (
            _sigmoid(o) * _layer_norm(hh, gml_ref[:, lo:lo + HEAD_DIM])).astype(y_ref.dtype)

    return [retention_head(h) for h in range(RET_HEADS)] + [mlstm_head(h) for h in range(MLSTM_HEADS)]


def _mixer_consts(L, pos):
    f32 = np.float32
    half = HEAD_DIM // 2
    inv_freq = np.power(f32(ROPE_BASE), -np.arange(half, dtype=f32) / f32(half)).astype(f32)
    ang = (pos.astype(f32)[:, None] * inv_freq[None, :]).astype(f32)
    cos = np.concatenate([np.cos(ang), np.cos(ang)], axis=-1).astype(f32)
    sin = np.concatenate([-np.sin(ang), np.sin(ang)], axis=-1).astype(f32)
    log_gamma = np.log1p(-np.exp2(-5.0 - np.arange(RET_HEADS, dtype=np.float64)))
    idx = np.arange(L, dtype=np.float64)
    rel = idx[:, None] - idx[None, :]
    dec = np.where(rel >= 0, np.exp(log_gamma[:, None, None] * np.maximum(rel, 0.0)), 0.0)
    qd = np.broadcast_to(np.exp(log_gamma[:, None] * (idx + 1.0))[..., None], (RET_HEADS, L, HEAD_DIM))
    kd = np.broadcast_to(np.exp(log_gamma[:, None] * (L - 1.0 - idx))[..., None], (RET_HEADS, L, HEAD_DIM))
    cd = np.broadcast_to(np.exp(log_gamma * L)[:, None, None], (RET_HEADS, 1, HEAD_DIM))
    tril = rel >= 0
    triu = rel <= 0
    return tuple(jnp.asarray(a, F32) for a in (cos, sin, dec, qd, kd, cd, tril, triu))


def _mixer_call(x3, mod3, g_mix, w_main_bf, w_gate, w_gate_t, consts, w_conv, b_conv, g_ret, g_mlstm, bg_col, bg_row,
                s0, c0, n0, m0, conv0, L, gs, outproj=None):
    cos, sin, dec, qd, kd, cd, tril, triu = consts
    groups, seq, d = x3.shape
    nc = seq // L
    hd = HEAD_DIM
    full = lambda *shape: pl.BlockSpec(shape, lambda g, c: (0,) * len(shape))
    mod_spec = lambda k: pl.BlockSpec((gs, 1, d), lambda g, c: (g, 0, k))
    state4 = pl.BlockSpec((gs, RET_HEADS, hd, hd), lambda g, c: (g, 0, 0, 0))
    vec4 = pl.BlockSpec((gs, MLSTM_HEADS, 1, hd), lambda g, c: (g, 0, 0, 0))
    tail3 = pl.BlockSpec((gs, CONV_TAIL_ROWS, 2 * MLSTM_WIDTH), lambda g, c: (g, 0, 0))
    row = lambda g, c: (g, c, 0)
    in_specs = [pl.BlockSpec((gs, L, d), row), mod_spec(0), mod_spec(1),
                full(1, 1, d), full(*w_main_bf.shape), full(d, 128), full(N_GATES, d),
                pl.BlockSpec((L, hd), lambda g, c: (c, 0)),
                pl.BlockSpec((L, hd), lambda g, c: (c, 0)),
                full(RET_HEADS, L, L), full(RET_HEADS, L, hd), full(RET_HEADS, L, hd), full(RET_HEADS, 1, hd),
                full(L, L), full(L, L),
                full(CONV_WIDTH, 2 * MLSTM_WIDTH), full(1, 2 * MLSTM_WIDTH),
                full(1, RET_WIDTH), full(1, MLSTM_WIDTH), full(1, N_GATES), full(N_GATES, 1),
                state4, state4, vec4, vec4, tail3]
    args = [x3, mod3, mod3, g_mix.reshape(1, 1, d), w_main_bf, w_gate, w_gate_t,
            cos, sin, dec, qd, kd, cd, tril, triu, w_conv, b_conv.reshape(1, -1),
            g_ret.reshape(1, -1), g_mlstm.reshape(1, -1), bg_col, bg_row, s0, c0, n0, m0, conv0]
    assert len(in_specs) == len(args) == N_MIXER_INPUTS
    state_specs = [state4, state4, vec4, vec4, tail3]
    state_shapes = [jax.ShapeDtypeStruct((groups, RET_HEADS, hd, hd), F32),
                    jax.ShapeDtypeStruct((groups, MLSTM_HEADS, hd, hd), F32),
                    jax.ShapeDtypeStruct((groups, MLSTM_HEADS, 1, hd), F32),
                    jax.ShapeDtypeStruct((groups, MLSTM_HEADS, 1, hd), F32),
                    jax.ShapeDtypeStruct((groups, CONV_TAIL_ROWS, 2 * MLSTM_WIDTH), F32)]
    scratch = [pltpu.VMEM((gs * L, MAIN_COLS), F32), pltpu.VMEM((gs, CONV_TAIL_ROWS + L, 2 * MLSTM_WIDTH), F32)]
    y_block, y_shape = (gs, L, RET_WIDTH + MLSTM_WIDTH), (groups, seq, RET_WIDTH + MLSTM_WIDTH)
    if outproj is None:
        kernel_fn = _mixer_kernel
        out_specs = [pl.BlockSpec(y_block, row)] + state_specs
        out_shape = [jax.ShapeDtypeStruct(y_shape, F32)] + state_shapes
    else:
        g_ffn, w_out_bf, w_router_t, b_router, ustrict = outproj
        tm = gs * L
        n = groups * seq
        tile = lambda g, c: g * nc + c
        kernel_fn = _mixer_outproj_kernel
        in_specs += [mod_spec(2), mod_spec(3), mod_spec(4),
                     full(1, 1, d), full(d, d), full(N_EXPERTS, d), full(N_EXPERTS, 1), full(tm, tm)]
        args += [mod3, mod3, mod3, g_ffn.reshape(1, 1, d), w_out_bf, w_router_t, b_router.reshape(N_EXPERTS, 1),
                 ustrict]
        assert len(args) == N_MIXER_INPUTS + N_OUTPROJ_INPUTS
        out_specs = [pl.BlockSpec((gs, L, d), row),
                     pl.BlockSpec((tm, d + SIDE_LANES), lambda g, c: (tile(g, c), 0)),
                     pl.BlockSpec((TOP_K, tm), lambda g, c: (0, tile(g, c))),
                     pl.BlockSpec((TOP_K, tm), lambda g, c: (0, tile(g, c))),
                     pl.BlockSpec((1, N_EXPERTS, 128), lambda g, c: (tile(g, c), 0, 0))] + state_specs
        out_shape = [jax.ShapeDtypeStruct((groups, seq, d), F32),
                     jax.ShapeDtypeStruct((n, d + SIDE_LANES), BF16),
                     jax.ShapeDtypeStruct((TOP_K, n), I32),
                     jax.ShapeDtypeStruct((TOP_K, n), I32),
                     jax.ShapeDtypeStruct((n // tm, N_EXPERTS, 128), F32)] + state_shapes
        scratch = scratch + [pltpu.VMEM(y_block, F32)]
    return pl.pallas_call(
        kernel_fn,
        grid=(groups // gs, nc),
        in_specs=in_specs, out_specs=out_specs, out_shape=out_shape, scratch_shapes=scratch,
        compiler_params=_params(2), name="mixer",
    )(*args)


def _outproj_kernel(y_ref, x_ref, gt_ref, sh_ref, sc_ref, g_ref, w_ref, wrt_ref, br_ref, ustrict_ref,
                    x1_ref, h2_ref, idx_ref, rank_ref, cnt_ref):
    _outproj_router(y_ref[...], x_ref, gt_ref, sh_ref, sc_ref, g_ref, w_ref, wrt_ref, br_ref, ustrict_ref,
                    x1_ref, h2_ref, idx_ref, rank_ref, cnt_ref)


def _outproj_router(y, x_ref, gt_ref, sh_ref, sc_ref, g_ref, w_ref, wrt_ref, br_ref, ustrict_ref,
                    x1_ref, h2_ref, idx_ref, rank_ref, cnt_ref):
    x = x_ref[...]
    gb, tt, d = x.shape
    tm = gb * tt
    mixed = _dot(y.astype(BF16), w_ref[...])
    x1 = x + gt_ref[...] * mixed.reshape(gb, tt, d)
    x1_ref[...] = x1
    h2 = (_rms(x1, g_ref[...]) * (1.0 + sc_ref[...]) + sh_ref[...]).reshape(tm, d)
    h2_ref[:, :d] = h2.astype(BF16)

    work = _dot3(wrt_ref[...], h2, dot=_dot_nt) + br_ref[...]
    e_iota = lax.broadcasted_iota(I32, work.shape, 0).astype(F32)
    vals, idxs, sels = [], [], []
    for _ in range(TOP_K):
        mx = jnp.max(work, axis=0, keepdims=True)
        ik = jnp.min(jnp.where(work == mx, e_iota, float(N_EXPERTS)), axis=0, keepdims=True)
        sel = e_iota == ik
        vals.append(mx)
        idxs.append(ik)
        sels.append(sel)
        work = jnp.where(sel, -jnp.inf, work)
    exps = [jnp.exp(v - vals[0]) for v in vals]
    denom = exps[0] + exps[1] + exps[2] + exps[3]
    gates = [e / denom for e in exps]
    idx_ref[...] = jnp.concatenate(idxs, axis=0).astype(I32)

    pieces = []
    for gk in gates:
        p1 = gk.astype(BF16)
        r1 = gk - p1.astype(F32)
        p2 = r1.astype(BF16)
        pieces += [p1, p2, (r1 - p2.astype(F32)).astype(BF16)]
    side = jnp.concatenate(pieces + [ik.astype(BF16) for ik in idxs], axis=0)
    eye = (lax.broadcasted_iota(I32, (SIDE_ROWS, SIDE_LANES), 0)
           == lax.broadcasted_iota(I32, (SIDE_ROWS, SIDE_LANES), 1)).astype(BF16)
    h2_ref[:, d:] = _dot_tn(side, eye).astype(BF16)

    mask = (sels[0] | sels[1] | sels[2] | sels[3]).astype(F32)
    before = _dot(mask.astype(BF16), ustrict_ref[...])
    ranks = [jnp.sum(jnp.where(sel, before, 0.0), axis=0, keepdims=True) for sel in sels]
    rank_ref[...] = jnp.concatenate(ranks, axis=0).astype(I32)
    cnt_ref[0] = jnp.broadcast_to(jnp.sum(mask, axis=1, keepdims=True), cnt_ref.shape[1:])


def _outproj_call(y, x3, mod3, g_ffn, w_out_bf, w_router_t, b_router, ustrict, gb, tt):
    g, t, d = x3.shape
    n = g * t
    tm = gb * tt
    tpg = t // tt
    row = lambda i, j: (i * tpg + j, 0)
    col = lambda i, j: (0, i * tpg + j)
    mod_spec = lambda k: pl.BlockSpec((gb, 1, d), lambda i, j: (i, 0, k))
    const = lambda *shape: pl.BlockSpec(shape, lambda i, j: (0,) * len(shape))
    return pl.pallas_call(
        _outproj_kernel,
        grid=(g // gb, tpg),
        in_specs=[pl.BlockSpec((tm, d), row),
                  pl.BlockSpec((gb, tt, d), lambda i, j: (i, j, 0)),
                  mod_spec(2), mod_spec(3), mod_spec(4),
                  const(1, 1, d), const(d, d), const(N_EXPERTS, d), const(N_EXPERTS, 1),
                  const(tm, tm)],
        out_specs=[pl.BlockSpec((gb, tt, d), lambda i, j: (i, j, 0)),
                   pl.BlockSpec((tm, d + SIDE_LANES), row),
                   pl.BlockSpec((TOP_K, tm), col),
                   pl.BlockSpec((TOP_K, tm), col),
                   pl.BlockSpec((1, N_EXPERTS, 128), lambda i, j: (i * tpg + j, 0, 0))],
        out_shape=[jax.ShapeDtypeStruct((g, t, d), F32),
                   jax.ShapeDtypeStruct((n, d + SIDE_LANES), BF16),
                   jax.ShapeDtypeStruct((TOP_K, n), I32),
                   jax.ShapeDtypeStruct((TOP_K, n), I32),
                   jax.ShapeDtypeStruct((n // tm, N_EXPERTS, 128), F32)],
        compiler_params=_params(2), name="outproj_router",
    )(y, x3, mod3, mod3, mod3, g_ffn.reshape(1, 1, d), w_out_bf, w_router_t, b_router.reshape(N_EXPERTS, 1),
      ustrict)


def _chunk_copy(hbm_ref, hbm_row, buf_ref, chunk, sem, to_hbm):
    hbm = hbm_ref.at[pl.ds(pl.multiple_of(hbm_row, RUN_ALIGN), RUN_ALIGN)]
    buf = buf_ref.at[pl.ds(pl.multiple_of(chunk * RUN_ALIGN, RUN_ALIGN), RUN_ALIGN)]
    return pltpu.make_async_copy(buf, hbm, sem) if to_hbm else pltpu.make_async_copy(hbm, buf, sem)


def _tile_chunks(hbm_ref, bufs_ref, starts_ref, nq_ref, sems, tile, to_hbm, wait):
    slot = tile % 2
    first = tile * (bufs_ref.shape[1] // RUN_ALIGN)
    n = nq_ref[tile]

    def one(q, priority):
        cp = _chunk_copy(hbm_ref, starts_ref[first + q], bufs_ref.at[slot], q, sems.at[slot], to_hbm)
        cp.wait() if wait else cp.start(priority=priority)

    def body(j, c):
        one(2 * j, 0)

        @pl.when(2 * j + 1 < n)
        def _():
            one(2 * j + 1, 1)
        return c

    lax.fori_loop(0, (n + 1) // 2, body, 0)


def _dispatch_kernel(starts_ref, nq_ref, meta_ref, rbuf_ref, h2p_ref, h2s_ref, xs_ref, bufs_ref, zero_ref, sems,
                     *, prompt_tiles):
    i = pl.program_id(0)
    tm = h2p_ref.shape[0]
    bm = zero_ref.shape[0]
    n_blocks = xs_ref.shape[0] // bm
    buf_ref = bufs_ref.at[i % 2]

    def zero_copy(row, sem):
        return pltpu.make_async_copy(zero_ref, xs_ref.at[pl.ds(pl.multiple_of(row, bm), bm)], sem)

    def tails(fn):
        def body(e, c):
            @pl.when(meta_ref[e] >= 0)
            def _():
                fn(zero_copy(meta_ref[e], sems.at[2]))
            return c
        lax.fori_loop(0, N_EXPERTS, body, 0)

    def unused(fn):
        def body(b, c):
            fn(zero_copy(b * bm, sems.at[3]))
            return c
        lax.fori_loop(meta_ref[N_EXPERTS], n_blocks, body, 0)

    @pl.when(i == 0)
    def _():
        zero_ref[...] = jnp.zeros(zero_ref.shape, zero_ref.dtype)
        tails(lambda cp: cp.start())
        unused(lambda cp: cp.start())
        tails(lambda cp: cp.wait())

    @pl.when(i == pl.num_programs(0) - 1)
    def _():
        unused(lambda cp: cp.wait())

    def build(h2_ref):
        h2 = h2_ref[...]
        rb = rbuf_ref[...].astype(jnp.int16)
        for c in range(bufs_ref.shape[1] // PERM_ROWS):
            r = (lax.broadcasted_iota(I32, (PERM_ROWS, tm), 0) + c * PERM_ROWS).astype(jnp.int16)
            hit = (r == rb[0:1, :]) | (r == rb[1:2, :]) | (r == rb[2:3, :]) | (r == rb[3:4, :])
            onehot = jnp.where(hit, jnp.ones((), BF16), jnp.zeros((), BF16))
            buf_ref[c * PERM_ROWS:(c + 1) * PERM_ROWS, :] = _dot(onehot, h2).astype(BF16)

    @pl.when(i < prompt_tiles)
    def _():
        build(h2p_ref)

    @pl.when(i >= prompt_tiles)
    def _():
        build(h2s_ref)

    chunks = functools.partial(_tile_chunks, xs_ref, bufs_ref, starts_ref, nq_ref, sems, to_hbm=True)
    chunks(i, wait=False)

    @pl.when(i > 0)
    def _():
        chunks(i - 1, wait=True)

    @pl.when(i == pl.num_programs(0) - 1)
    def _():
        chunks(i, wait=True)


def _dispatch_call(starts, nq, meta, rbuf, h2_p, h2_s, cap):
    d = h2_p.shape[1]
    tm = MOE_TILE
    pt, st = h2_p.shape[0] // tm, h2_s.shape[0] // tm
    return pl.pallas_call(
        functools.partial(_dispatch_kernel, prompt_tiles=pt),
        grid_spec=pltpu.PrefetchScalarGridSpec(
            num_scalar_prefetch=3,
            grid=(pt + st,),
            in_specs=[pl.BlockSpec((TOP_K, tm), lambda i, *_: (0, i)),
                      pl.BlockSpec((tm, d), lambda i, *_: (jnp.minimum(i, pt - 1), 0)),
                      pl.BlockSpec((tm, d), lambda i, *_: (jnp.maximum(i - pt, 0), 0))],
            out_specs=pl.BlockSpec(memory_space=pl.ANY),
            scratch_shapes=[pltpu.VMEM((2, TILE_BUF_ROWS, d), BF16), pltpu.VMEM((EXPERT_BLOCK, d), BF16),
                            pltpu.SemaphoreType.DMA((4,))]),
        out_shape=jax.ShapeDtypeStruct((cap, d), BF16),
        compiler_params=_params(1), name="dispatch",
    )(starts, nq, meta, rbuf, h2_p, h2_s)


def _expert_kernel(be_ref, nu_ref, nxt_ref, xs_ref, wup_hbm, bup_ref, wdn_hbm, bdn_ref, ys_ref,
                   wup_f32, wdn_f32, wup_bf, wdn_bf, sems):
    i = pl.program_id(0)
    prev = be_ref[jnp.maximum(i - 1, 0)]

    def weight_copies(e):
        return (pltpu.make_async_copy(wup_hbm.at[e], wup_f32, sems.at[0]),
                pltpu.make_async_copy(wdn_hbm.at[e], wdn_f32, sems.at[1]))

    @pl.when(i == 0)
    def _():
        for cp in weight_copies(be_ref[0]):
            cp.start()

    @pl.when((i < nu_ref[0]) & ((i == 0) | (be_ref[i] != prev)))
    def _():
        for cp in weight_copies(be_ref[i]):
            cp.wait()

        def cast(r, c):
            rows = pl.ds(pl.multiple_of(r * 64, 64), 64)
            wup_bf[rows, :] = wup_f32[rows, :].astype(BF16)
            wdn_bf[rows, :] = wdn_f32[rows, :].astype(BF16)
            return c

        lax.fori_loop(0, D_MODEL // 64, cast, 0)

        @pl.when(nxt_ref[i] >= 0)
        def _():
            for cp in weight_copies(nxt_ref[i]):
                cp.start()

    @pl.when(i < nu_ref[0])
    def _():
        hu = _dot(xs_ref[:, :D_MODEL], wup_bf[...]) + bup_ref[0]
        gate = jnp.minimum(hu[:, :D_FF], SWIGLU_LIMIT)
        lin = jnp.clip(hu[:, D_FF:], -SWIGLU_LIMIT, SWIGLU_LIMIT)
        glu = gate * _sigmoid(SWIGLU_ALPHA * gate)
        y = _dot(((lin + 1.0) * glu).astype(BF16), wdn_bf[...]) + bdn_ref[0]
        side = xs_ref[:, D_MODEL:].astype(F32)
        e = be_ref[i].astype(F32)
        weight = jnp.zeros((side.shape[0], 1), F32)
        for k in range(TOP_K):
            g_k = side[:, 3 * k:3 * k + 1] + side[:, 3 * k + 1:3 * k + 2] + side[:, 3 * k + 2:3 * k + 3]
            weight = weight + jnp.where(side[:, SIDE_ROWS - TOP_K + k:SIDE_ROWS - TOP_K + k + 1] == e, g_k, 0.0)
        ys_ref[...] = (weight * y).astype(ys_ref.dtype)

    @pl.when(i >= nu_ref[0])
    def _():
        ys_ref[...] = jnp.zeros(ys_ref.shape, ys_ref.dtype)


def _expert_call(block_e, n_used, next_e, xs, w_up, b_up, w_down, b_down):
    cap, dw = xs.shape
    d = w_down.shape[2]
    bm = EXPERT_BLOCK
    blk = lambda i, be, nu, nx: (jnp.minimum(i, nu[0] - 1), 0)
    per_e = lambda i, be, nu, nx: (be[i], 0, 0)
    return pl.pallas_call(
        _expert_kernel,
        grid_spec=pltpu.PrefetchScalarGridSpec(
            num_scalar_prefetch=3,
            grid=(cap // bm,),
            in_specs=[pl.BlockSpec((bm, dw), blk),
                      pl.BlockSpec(memory_space=pl.ANY),
                      pl.BlockSpec((1, 1, 2 * D_FF), per_e),
                      pl.BlockSpec(memory_space=pl.ANY),
                      pl.BlockSpec((1, 1, d), per_e)],
            out_specs=pl.BlockSpec((bm, d), lambda i, be, nu, nx: (i, 0)),
            scratch_shapes=[pltpu.VMEM((d, 2 * D_FF), F32), pltpu.VMEM((D_FF, d), F32),
                            pltpu.VMEM((d, 2 * D_FF), BF16), pltpu.VMEM((D_FF, d), BF16),
                            pltpu.SemaphoreType.DMA((2,))]),
        out_shape=jax.ShapeDtypeStruct((cap, d), BF16),
        compiler_params=_params(1), name="experts",
    )(block_e, n_used, next_e, xs, w_up, b_up.reshape(N_EXPERTS, 1, -1), w_down, b_down.reshape(N_EXPERTS, 1, -1))


def _combine_kernel(starts_ref, nq_ref, x1_ref, gt_ref, rcol_ref, gfin_ref, ys_ref, o_ref, bufs_ref, sems,
                    *, tile0, tiles_per_group):
    x1 = x1_ref[...]
    gb, tt, d = x1.shape
    tm = gb * tt
    step = pl.program_id(0) * tiles_per_group + pl.program_id(1)
    n_steps = pl.num_programs(0) * tiles_per_group
    tile = tile0 + step
    chunks = functools.partial(_tile_chunks, ys_ref, bufs_ref, starts_ref, nq_ref, sems, to_hbm=False)

    @pl.when(step == 0)
    def _():
        bufs_ref[...] = jnp.zeros(bufs_ref.shape, bufs_ref.dtype)
        chunks(tile, wait=False)

    @pl.when(step + 1 < n_steps)
    def _():
        chunks(tile + 1, wait=False)

    chunks(tile, wait=True)
    buf_ref = bufs_ref.at[tile % 2]

    rows = rcol_ref[...].astype(jnp.int16)
    moe = jnp.zeros((tm, d), F32)
    for c in range(bufs_ref.shape[1] // PERM_COLS):
        r = (lax.broadcasted_iota(I32, (tm, PERM_COLS), 1) + c * PERM_COLS).astype(jnp.int16)
        hit = (r == rows[:, 0:1]) | (r == rows[:, 1:2]) | (r == rows[:, 2:3]) | (r == rows[:, 3:4])
        onehot = jnp.where(hit, jnp.ones((), BF16), jnp.zeros((), BF16))
        moe = moe + _dot(onehot, buf_ref[c * PERM_COLS:(c + 1) * PERM_COLS, :])
    xo = x1 + gt_ref[...] * moe.reshape(gb, tt, d)
    o_ref[...] = _rms(xo, gfin_ref[...])


def _combine_call(starts, nq, x1, mod3, rcol, g_final, ys, tok0, gb, tt):
    g, t, d = x1.shape
    tm = gb * tt
    tpg = t // tt
    t0 = tok0 // tm
    tok = lambda i, j, *_: (t0 + i * tpg + j, 0)
    return pl.pallas_call(
        functools.partial(_combine_kernel, tile0=t0, tiles_per_group=tpg),
        grid_spec=pltpu.PrefetchScalarGridSpec(
            num_scalar_prefetch=2,
            grid=(g // gb, tpg),
            in_specs=[pl.BlockSpec((gb, tt, d), lambda i, j, *_: (i, j, 0)),
                      pl.BlockSpec((gb, 1, d), lambda i, j, *_: (i, 0, 5)),
                      pl.BlockSpec((tm, TOP_K), tok),
                      pl.BlockSpec((1, 1, d), lambda i, j, *_: (0, 0, 0)),
                      pl.BlockSpec(memory_space=pl.ANY)],
            out_specs=pl.BlockSpec((gb, tt, d), lambda i, j, *_: (i, j, 0)),
            scratch_shapes=[pltpu.VMEM((2, TILE_BUF_ROWS, d), BF16), pltpu.SemaphoreType.DMA((2,))]),
        out_shape=jax.ShapeDtypeStruct((g, t, d), F32),
        compiler_params=_params(2), name="combine",
    )(starts, nq, x1, mod3, rcol, g_final.reshape(1, 1, d), ys)


def _group_blocking(groups, seq, tile):
    if seq >= tile:
        return 1, tile
    return tile // seq, seq


def kernel(x_prompt, x_sample, c_prompt, c_sample, state_ret, state_mlstm_c, state_mlstm_n, state_mlstm_m, state_conv, w_mod, b_mod, g_mix, g_ffn, w_in, b_igate, b_fgate, w_conv, b_conv, g_ret, g_mlstm, w_out, w_router, b_router, w_up, b_up, w_down, b_down, g_final):
    depth = w_mod.shape[0]
    assert depth == 1, "single-layer trunk"
    bp, tp, d = x_prompt.shape
    bs, ts, _ = x_sample.shape
    n_p, n_s = bp * tp, bs * ts
    hd = HEAD_DIM
    l = 0

    mod = _mod_call(jnp.concatenate([c_prompt, c_sample], axis=0), w_mod[l], b_mod[l])
    mod_p = mod[:bp].reshape(bp, 1, N_MOD * d)
    mod_s = mod[bp:].reshape(bs, 1, N_MOD * d)

    w_main_bf = w_in[l].astype(BF16)
    w_gate = jnp.pad(w_in[l][:, MAIN_COLS:], ((0, 0), (0, 128 - N_GATES)))
    w_gate_t = w_in[l][:, MAIN_COLS:].T
    w_out_bf = w_out[l].astype(BF16)
    bg_col = jnp.concatenate([b_igate[l], b_fgate[l]]).reshape(1, N_GATES)
    bg_row = bg_col.reshape(N_GATES, 1)

    groups = (
        (x_prompt, mod_p, min(CHUNK, tp), np.arange(tp),
         jnp.zeros((bp, RET_HEADS, hd, hd), F32), jnp.zeros((bp, MLSTM_HEADS, hd, hd), F32),
         jnp.zeros((bp, MLSTM_HEADS, hd), F32), jnp.zeros((bp, MLSTM_HEADS), F32),
         jnp.zeros((bp, CONV_WIDTH - 1, 2 * MLSTM_WIDTH), F32)),
        (x_sample, mod_s, min(CHUNK, ts), PAST_LEN + np.arange(ts),
         state_ret[l], state_mlstm_c[l], state_mlstm_n[l], state_mlstm_m[l], state_conv[l]),
    )

    ustrict = jnp.asarray(np.arange(MOE_TILE)[:, None] < np.arange(MOE_TILE)[None, :], BF16)
    staged = []
    for x3, mod3, L, pos, s0, c0, n0, m0, conv0 in groups:
        g, t, _ = x3.shape
        conv0p = jnp.pad(conv0.astype(F32), ((0, 0), (CONV_TAIL_ROWS - (CONV_WIDTH - 1), 0), (0, 0)))
        outproj = (g_ffn[l], w_out_bf, w_router[l].T, b_router[l], ustrict)
        chunked = t > L
        gs = MIXER_SEQS_CHUNKED if chunked else MIXER_SEQS_SHORT
        fused = chunked and gs * L == MOE_TILE
        res = _mixer_call(
            x3, mod3, g_mix[l], w_main_bf, w_gate, w_gate_t, _mixer_consts(L, pos),
            w_conv[l], b_conv[l], g_ret[l], g_mlstm[l],
            bg_col, bg_row, s0.astype(F32), c0.astype(F32), n0.astype(F32).reshape(g, MLSTM_HEADS, 1, hd),
            jnp.broadcast_to(m0.astype(F32)[:, :, None, None], (g, MLSTM_HEADS, 1, hd)), conv0p,
            L, gs, outproj if fused else None)
        s_new, c_new, n_new, m_new, tail = res[-5:]
        states = (s_new, c_new, n_new.reshape(g, MLSTM_HEADS, hd), m_new[:, :, 0, 0],
                  tail[:, CONV_TAIL_ROWS - (CONV_WIDTH - 1):, :])
        if fused:
            x1, h2, idx, rank, cnt = res[:5]
            blocking = (gs, L)
        else:
            blocking = _group_blocking(g, t, MOE_TILE)
            x1, h2, idx, rank, cnt = _outproj_call(res[0].reshape(g * t, d), x3, mod3, *outproj, *blocking)
        staged.append((x1, mod3, h2, idx, rank, states, cnt, blocking))

    n_tok = n_p + n_s
    tm, bm, ra = MOE_TILE, EXPERT_BLOCK, RUN_ALIGN
    n_tiles = n_tok // tm
    q_max = TILE_BUF_ROWS // ra
    n_blocks = -(-(n_tok * TOP_K + n_tiles * N_EXPERTS * (ra - 1)) // bm) + N_EXPERTS
    cap = n_blocks * bm
    counts = jnp.concatenate([s[6][:, :, 0] for s in staged], axis=0).astype(I32)
    run = (counts + ra - 1) // ra * ra
    region = jnp.sum(run, axis=0)
    padded = (region + bm - 1) // bm * bm
    pad_end = jnp.cumsum(padded)
    pad_start = pad_end - padded
    run_start = pad_start[None, :] + jnp.cumsum(run, axis=0) - run
    buf_end = jnp.cumsum(run, axis=1)
    buf_start = buf_end - run
    nq = (buf_end[:, -1] // ra).astype(I32)
    chunk_row = jnp.arange(q_max, dtype=I32) * ra
    chunk_e = jnp.minimum(jnp.sum((buf_end[:, None, :] <= chunk_row[None, :, None]).astype(I32), axis=2),
                          N_EXPERTS - 1)
    e_ids = jnp.arange(N_EXPERTS, dtype=I32)
    shift = run_start - buf_start
    starts = jnp.sum(jnp.where(chunk_e[:, :, None] == e_ids, shift[:, None, :], 0), axis=2) + chunk_row[None, :]
    starts = jnp.where(chunk_row[None, :] < buf_end[:, -1:], starts, 0).reshape(-1).astype(I32)

    idx_all = jnp.concatenate([s[3] for s in staged], axis=1)
    rank_all = jnp.concatenate([s[4] for s in staged], axis=1)
    buf_start_tok = jnp.repeat(buf_start, tm, axis=0).T
    rbuf = jnp.sum(jnp.where(idx_all[None] == e_ids[:, None, None], buf_start_tok[:, None, :], 0), axis=0) + rank_all
    rcol = rbuf.T
    block_row = jnp.arange(n_blocks, dtype=I32) * bm
    block_e = jnp.minimum(jnp.sum((pad_end[None, :] <= block_row[:, None]).astype(I32), axis=1), N_EXPERTS - 1)
    n_used = (pad_end[-1:] // bm).astype(I32)
    meta = jnp.concatenate([jnp.where(region > 0, pad_end - bm, -1), n_used]).astype(I32)

    xs = _dispatch_call(starts, nq, meta, rbuf, staged[0][2], staged[1][2], cap)
    later = (e_ids[None, :] > e_ids[:, None]) & (region[None, :] > 0)
    next_nonempty = jnp.min(jnp.where(later, e_ids[None, :], N_EXPERTS), axis=1)
    next_nonempty = jnp.where(next_nonempty < N_EXPERTS, next_nonempty, -1)
    next_e = jnp.sum(jnp.where(block_e[:, None] == e_ids[None, :], next_nonempty[None, :], 0), axis=1).astype(I32)
    ys = _expert_call(block_e, n_used, next_e, xs, w_up[l], b_up[l], w_down[l], b_down[l])

    outs = []
    tok0 = 0
    for x1, mod3, *_, blocking in staged:
        outs.append(_combine_call(starts, nq, x1, mod3, rcol, g_final, ys, tok0, *blocking))
        tok0 += x1.shape[0] * x1.shape[1]

    st_p, st_s = staged[0][5], staged[1][5]
    return (outs[0], outs[1]) + tuple(a[None] for a in st_p) + tuple(a[None] for a in st_s)
```
